```python
import jax, jax.numpy as jnp
from jax import lax
import numpy as np

D_MODEL = 1024
BATCH = 8
SEQ = 8192
DEPTH = 4

D_MIX = D_MODEL
N_MIXERS = 4
D_GROUP = D_MIX // N_MIXERS
HEAD_DIM = 64
N_HEADS = D_GROUP // HEAD_DIM
CONF_KERNEL = 31
SHORT_KERNEL = 3
POOL_WINDOWS = (2, 4, 8, 16)
POOL_GROUP = D_GROUP // len(POOL_WINDOWS)
CHUNK = 128
D_FF = 2816
N_IN_PIECES = 8
D_IN = N_IN_PIECES * D_GROUP
FFN_RESIDUAL = 0.5
EPS = 1e-6

kernel_name = "hybrid_macaron_parallel_conv_pool_gmlp"


def rmsnorm(x, g):
    xf = x.astype(jnp.float32)
    y = xf * lax.rsqrt(jnp.mean(xf * xf, axis=-1, keepdims=True) + EPS)
    return (y * g.astype(jnp.float32)).astype(x.dtype)


def layernorm(x, g, b):
    xf = x.astype(jnp.float32)
    mu = jnp.mean(xf, axis=-1, keepdims=True)
    xc = xf - mu
    y = xc * lax.rsqrt(jnp.mean(xc * xc, axis=-1, keepdims=True) + EPS)
    return (y * g.astype(jnp.float32) + b.astype(jnp.float32)).astype(x.dtype)


def causal_depthwise_conv(x, w):
    k, c = w.shape
    return lax.conv_general_dilated(
        x, w[:, None, :].astype(x.dtype), window_strides=(1,), padding=[(k - 1, 0)],
        dimension_numbers=("NWC", "WIO", "NWC"), feature_group_count=c)


def swiglu(h, w1, w3, w2):
    return (jax.nn.silu(h @ w1) * (h @ w3)) @ w2


def conformer_conv(val, gate, conv_w, conv_b, ln_g, ln_b):
    y = val * jax.nn.sigmoid(gate)
    y = causal_depthwise_conv(y, conv_w) + conv_b
    return jax.nn.silu(layernorm(y, ln_g, ln_b))


def short_gated_conv(b_gate, c_gate, xv, conv_w):
    return b_gate * causal_depthwise_conv(c_gate * xv, conv_w)


def multiscale_pool(xp, pool_w, pool_scale):
    bsz, s, _ = xp.shape
    xf = xp.astype(jnp.float32)
    cs = jnp.cumsum(xf, axis=1)
    pos = jnp.arange(1, s + 1, dtype=jnp.float32)[:, None]
    outs = []
    for g, w in enumerate(POOL_WINDOWS):
        sl = slice(g * POOL_GROUP, (g + 1) * POOL_GROUP)
        c = cs[..., sl]
        lagged = jnp.pad(c, ((0, 0), (w, 0), (0, 0)))[:, :s]
        mean = (c - lagged) / jnp.minimum(pos, float(w))
        outs.append(mean - xf[..., sl])
    d = jnp.stack(outs, axis=2).astype(xp.dtype)
    y = jnp.einsum("bsgc,gcd->bsgd", d, pool_w).reshape(bsz, s, D_GROUP)
    return y * pool_scale


def chunked_spatial_gating(u, v, ln_g, ln_b, w_s, b_s):
    bsz, s, _ = v.shape
    v = layernorm(v, ln_g, ln_b)
    vc = v.reshape(bsz, s // CHUNK, CHUNK, N_HEADS, HEAD_DIM)
    mask = jnp.tril(jnp.ones((CHUNK, CHUNK), dtype=bool))
    ws = jnp.where(mask[None], w_s, 0.0).astype(v.dtype)
    mixed = jnp.einsum("hts,bnshc->bnthc", ws, vc) + b_s.T[None, None, :, :, None]
    return u * mixed.reshape(bsz, s, D_GROUP)


def _fwd_setup_inputs(seed: int = 0) -> dict:
    key = jax.random.key(seed)
    ks = iter(jax.random.split(key, 32))

    def nrm(shape, scale):
        return jax.random.normal(next(ks), shape, dtype=jnp.float32) * scale

    def gain(shape):
        return 1.0 + nrm(shape, 0.02)

    L = DEPTH
    return {
        "x": nrm((BATCH, SEQ, D_MODEL), 1.0),
        "ffn1_norm": gain((L, D_MODEL)),
        "ffn1_w1": nrm((L, D_MODEL, D_FF), D_MODEL ** -0.5),
        "ffn1_w3": nrm((L, D_MODEL, D_FF), D_MODEL ** -0.5),
        "ffn1_w2": nrm((L, D_FF, D_MODEL), D_FF ** -0.5),
        "mix_norm": gain((L, D_MODEL)),
        "w_in": nrm((L, D_MODEL, D_IN), D_MODEL ** -0.5),
        "conf_conv_w": nrm((L, CONF_KERNEL, D_GROUP), CONF_KERNEL ** -0.5),
        "conf_conv_b": nrm((L, D_GROUP), 0.02),
        "conf_ln_g": gain((L, D_GROUP)),
        "conf_ln_b": nrm((L, D_GROUP), 0.02),
        "sconv_w": nrm((L, SHORT_KERNEL, D_GROUP), SHORT_KERNEL ** -0.5),
        "pool_w": nrm((L, len(POOL_WINDOWS), POOL_GROUP, POOL_GROUP), POOL_GROUP ** -0.5),
        "pool_scale": 1.0 + nrm((L, D_GROUP), 0.1),
        "gmlp_ln_g": gain((L, D_GROUP)),
        "gmlp_ln_b": nrm((L, D_GROUP), 0.02),
        "gmlp_w_s": nrm((L, N_HEADS, CHUNK, CHUNK), CHUNK ** -0.5),
        "gmlp_b_s": 1.0 + nrm((L, N_HEADS, CHUNK), 0.02),
        "w_out": nrm((L, D_MIX, D_MODEL), D_MIX ** -0.5),
        "ffn2_norm": gain((L, D_MODEL)),
        "ffn2_w1": nrm((L, D_MODEL, D_FF), D_MODEL ** -0.5),
        "ffn2_w3": nrm((L, D_MODEL, D_FF), D_MODEL ** -0.5),
        "ffn2_w2": nrm((L, D_FF, D_MODEL), D_FF ** -0.5),
        "final_norm": gain((D_MODEL,)),
    }


def _fwd_reference(x, ffn1_norm, ffn1_w1, ffn1_w3, ffn1_w2, mix_norm, w_in,
              conf_conv_w, conf_conv_b, conf_ln_g, conf_ln_b, sconv_w,
              pool_w, pool_scale, gmlp_ln_g, gmlp_ln_b, gmlp_w_s, gmlp_b_s,
              w_out, ffn2_norm, ffn2_w1, ffn2_w3, ffn2_w2, final_norm):
    for l in range(DEPTH):
        h = rmsnorm(x, ffn1_norm[l])
        x = x + FFN_RESIDUAL * swiglu(h, ffn1_w1[l], ffn1_w3[l], ffn1_w2[l])

        h = rmsnorm(x, mix_norm[l])
        p = h @ w_in[l]
        a_val, a_gate, s_b, s_c, s_x, pool_in, g_u, g_v = jnp.split(p, N_IN_PIECES, axis=-1)

        y_a = conformer_conv(a_val, a_gate, conf_conv_w[l], conf_conv_b[l],
                             conf_ln_g[l], conf_ln_b[l])
        y_b = short_gated_conv(s_b, s_c, s_x, sconv_w[l])
        y_c = multiscale_pool(pool_in, pool_w[l], pool_scale[l])
        y_d = chunked_spatial_gating(g_u, g_v, gmlp_ln_g[l], gmlp_ln_b[l],
                                     gmlp_w_s[l], gmlp_b_s[l])

        mix = jnp.concatenate([y_a, y_b, y_c, y_d], axis=-1)
        x = x + mix @ w_out[l]

        h = rmsnorm(x, ffn2_norm[l])
        x = x + FFN_RESIDUAL * swiglu(h, ffn2_w1[l], ffn2_w3[l], ffn2_w2[l])

    return rmsnorm(x, final_norm)


import jax as _jax
import jax.numpy as _jnp

TWIN_FORMAT = 'train_step'
FWD_PARAMS = ['x', 'ffn1_norm', 'ffn1_w1', 'ffn1_w3', 'ffn1_w2', 'mix_norm', 'w_in', 'conf_conv_w', 'conf_conv_b', 'conf_ln_g', 'conf_ln_b', 'sconv_w', 'pool_w', 'pool_scale', 'gmlp_ln_g', 'gmlp_ln_b', 'gmlp_w_s', 'gmlp_b_s', 'w_out', 'ffn2_norm', 'ffn2_w1', 'ffn2_w3', 'ffn2_w2', 'final_norm']
TWIN_WEIGHTS = ['ffn1_norm', 'ffn1_w1', 'ffn1_w3', 'ffn1_w2', 'mix_norm', 'w_in', 'conf_conv_w', 'conf_conv_b', 'conf_ln_g', 'conf_ln_b', 'sconv_w', 'pool_w', 'pool_scale', 'gmlp_ln_g', 'gmlp_ln_b', 'gmlp_w_s', 'gmlp_b_s', 'w_out', 'ffn2_norm', 'ffn2_w1', 'ffn2_w3', 'ffn2_w2', 'final_norm']
TWIN_DIFF_INPUT = 'x'
TWIN_INPUTS = ['x', 'ffn1_norm', 'ffn1_w1', 'ffn1_w3', 'ffn1_w2', 'mix_norm', 'w_in', 'conf_conv_w', 'conf_conv_b', 'conf_ln_g', 'conf_ln_b', 'sconv_w', 'pool_w', 'pool_scale', 'gmlp_ln_g', 'gmlp_ln_b', 'gmlp_w_s', 'gmlp_b_s', 'w_out', 'ffn2_norm', 'ffn2_w1', 'ffn2_w3', 'ffn2_w2', 'final_norm', 'loss_target', 'm_ffn1_norm', 'm_ffn1_w1', 'm_ffn1_w3', 'm_ffn1_w2', 'm_mix_norm', 'm_w_in', 'm_conf_conv_w', 'm_conf_conv_b', 'm_conf_ln_g', 'm_conf_ln_b', 'm_sconv_w', 'm_pool_w', 'm_pool_scale', 'm_gmlp_ln_g', 'm_gmlp_ln_b', 'm_gmlp_w_s', 'm_gmlp_b_s', 'm_w_out', 'm_ffn2_norm', 'm_ffn2_w1', 'm_ffn2_w3', 'm_ffn2_w2', 'm_final_norm', 'v_ffn1_norm', 'v_ffn1_w1', 'v_ffn1_w3', 'v_ffn1_w2', 'v_mix_norm', 'v_w_in', 'v_conf_conv_w', 'v_conf_conv_b', 'v_conf_ln_g', 'v_conf_ln_b', 'v_sconv_w', 'v_pool_w', 'v_pool_scale', 'v_gmlp_ln_g', 'v_gmlp_ln_b', 'v_gmlp_w_s', 'v_gmlp_b_s', 'v_w_out', 'v_ffn2_norm', 'v_ffn2_w1', 'v_ffn2_w3', 'v_ffn2_w2', 'v_final_norm']
TWIN_OUTPUTS = ['loss', 'grad_x', 'grad_ffn1_norm', 'grad_ffn1_w1', 'grad_ffn1_w3', 'grad_ffn1_w2', 'grad_mix_norm', 'grad_w_in', 'grad_conf_conv_w', 'grad_conf_conv_b', 'grad_conf_ln_g', 'grad_conf_ln_b', 'grad_sconv_w', 'grad_pool_w', 'grad_pool_scale', 'grad_gmlp_ln_g', 'grad_gmlp_ln_b', 'grad_gmlp_w_s', 'grad_gmlp_b_s', 'grad_w_out', 'grad_ffn2_norm', 'grad_ffn2_w1', 'grad_ffn2_w3', 'grad_ffn2_w2', 'grad_final_norm', 'delta_ffn1_norm', 'delta_ffn1_w1', 'delta_ffn1_w3', 'delta_ffn1_w2', 'delta_mix_norm', 'delta_w_in', 'delta_conf_conv_w', 'delta_conf_conv_b', 'delta_conf_ln_g', 'delta_conf_ln_b', 'delta_sconv_w', 'delta_pool_w', 'delta_pool_scale', 'delta_gmlp_ln_g', 'delta_gmlp_ln_b', 'delta_gmlp_w_s', 'delta_gmlp_b_s', 'delta_w_out', 'delta_ffn2_norm', 'delta_ffn2_w1', 'delta_ffn2_w3', 'delta_ffn2_w2', 'delta_final_norm', 'new_m_ffn1_norm', 'new_m_ffn1_w1', 'new_m_ffn1_w3', 'new_m_ffn1_w2', 'new_m_mix_norm', 'new_m_w_in', 'new_m_conf_conv_w', 'new_m_conf_conv_b', 'new_m_conf_ln_g', 'new_m_conf_ln_b', 'new_m_sconv_w', 'new_m_pool_w', 'new_m_pool_scale', 'new_m_gmlp_ln_g', 'new_m_gmlp_ln_b', 'new_m_gmlp_w_s', 'new_m_gmlp_b_s', 'new_m_w_out', 'new_m_ffn2_norm', 'new_m_ffn2_w1', 'new_m_ffn2_w3', 'new_m_ffn2_w2', 'new_m_final_norm', 'new_v_ffn1_norm', 'new_v_ffn1_w1', 'new_v_ffn1_w3', 'new_v_ffn1_w2', 'new_v_mix_norm', 'new_v_w_in', 'new_v_conf_conv_w', 'new_v_conf_conv_b', 'new_v_conf_ln_g', 'new_v_conf_ln_b', 'new_v_sconv_w', 'new_v_pool_w', 'new_v_pool_scale', 'new_v_gmlp_ln_g', 'new_v_gmlp_ln_b', 'new_v_gmlp_w_s', 'new_v_gmlp_b_s', 'new_v_w_out', 'new_v_ffn2_norm', 'new_v_ffn2_w1', 'new_v_ffn2_w3', 'new_v_ffn2_w2', 'new_v_final_norm']
TWIN_LEAF_KINDS = {'loss': 'loss', 'grad_x': 'grad_x', 'grad_ffn1_norm': 'grad_w', 'grad_ffn1_w1': 'grad_w', 'grad_ffn1_w3': 'grad_w', 'grad_ffn1_w2': 'grad_w', 'grad_mix_norm': 'grad_w', 'grad_w_in': 'grad_w', 'grad_conf_conv_w': 'grad_w', 'grad_conf_conv_b': 'grad_w', 'grad_conf_ln_g': 'grad_w', 'grad_conf_ln_b': 'grad_w', 'grad_sconv_w': 'grad_w', 'grad_pool_w': 'grad_w', 'grad_pool_scale': 'grad_w', 'grad_gmlp_ln_g': 'grad_w', 'grad_gmlp_ln_b': 'grad_w', 'grad_gmlp_w_s': 'grad_w', 'grad_gmlp_b_s': 'grad_w', 'grad_w_out': 'grad_w', 'grad_ffn2_norm': 'grad_w', 'grad_ffn2_w1': 'grad_w', 'grad_ffn2_w3': 'grad_w', 'grad_ffn2_w2': 'grad_w', 'grad_final_norm': 'grad_w', 'delta_ffn1_norm': 'delta_w', 'delta_ffn1_w1': 'delta_w', 'delta_ffn1_w3': 'delta_w', 'delta_ffn1_w2': 'delta_w', 'delta_mix_norm': 'delta_w', 'delta_w_in': 'delta_w', 'delta_conf_conv_w': 'delta_w', 'delta_conf_conv_b': 'delta_w', 'delta_conf_ln_g': 'delta_w', 'delta_conf_ln_b': 'delta_w', 'delta_sconv_w': 'delta_w', 'delta_pool_w': 'delta_w', 'delta_pool_scale': 'delta_w', 'delta_gmlp_ln_g': 'delta_w', 'delta_gmlp_ln_b': 'delta_w', 'delta_gmlp_w_s': 'delta_w', 'delta_gmlp_b_s': 'delta_w', 'delta_w_out': 'delta_w', 'delta_ffn2_norm': 'delta_w', 'delta_ffn2_w1': 'delta_w', 'delta_ffn2_w3': 'delta_w', 'delta_ffn2_w2': 'delta_w', 'delta_final_norm': 'delta_w', 'new_m_ffn1_norm': 'new_m', 'new_m_ffn1_w1': 'new_m', 'new_m_ffn1_w3': 'new_m', 'new_m_ffn1_w2': 'new_m', 'new_m_mix_norm': 'new_m', 'new_m_w_in': 'new_m', 'new_m_conf_conv_w': 'new_m', 'new_m_conf_conv_b': 'new_m', 'new_m_conf_ln_g': 'new_m', 'new_m_conf_ln_b': 'new_m', 'new_m_sconv_w': 'new_m', 'new_m_pool_w': 'new_m', 'new_m_pool_scale': 'new_m', 'new_m_gmlp_ln_g': 'new_m', 'new_m_gmlp_ln_b': 'new_m', 'new_m_gmlp_w_s': 'new_m', 'new_m_gmlp_b_s': 'new_m', 'new_m_w_out': 'new_m', 'new_m_ffn2_norm': 'new_m', 'new_m_ffn2_w1': 'new_m', 'new_m_ffn2_w3': 'new_m', 'new_m_ffn2_w2': 'new_m', 'new_m_final_norm': 'new_m', 'new_v_ffn1_norm': 'new_v', 'new_v_ffn1_w1': 'new_v', 'new_v_ffn1_w3': 'new_v', 'new_v_ffn1_w2': 'new_v', 'new_v_mix_norm': 'new_v', 'new_v_w_in': 'new_v', 'new_v_conf_conv_w': 'new_v', 'new_v_conf_conv_b': 'new_v', 'new_v_conf_ln_g': 'new_v', 'new_v_conf_ln_b': 'new_v', 'new_v_sconv_w': 'new_v', 'new_v_pool_w': 'new_v', 'new_v_pool_scale': 'new_v', 'new_v_gmlp_ln_g': 'new_v', 'new_v_gmlp_ln_b': 'new_v', 'new_v_gmlp_w_s': 'new_v', 'new_v_gmlp_b_s': 'new_v', 'new_v_w_out': 'new_v', 'new_v_ffn2_norm': 'new_v', 'new_v_ffn2_w1': 'new_v', 'new_v_ffn2_w3': 'new_v', 'new_v_ffn2_w2': 'new_v', 'new_v_final_norm': 'new_v'}


def _forward(args):
    return _fwd_reference(*[args[k] for k in FWD_PARAMS])


def _output_shape():
    out = _jax.eval_shape(lambda: _forward(_fwd_setup_inputs(0)))
    return out.shape, out.dtype

N_MICROBATCH = 1
ADAM_LR = 0.001
ADAM_B1 = 0.9
ADAM_B2 = 0.999
ADAM_EPS = 1e-08
ADAM_WD = 0.01
ADAM_STEP = 10
PER_EXAMPLE_BATCH_AXIS = {'x': 0, 'loss_target': 0}
SHARED_INPUTS = []
_WEIGHT_DTYPES = {'ffn1_norm': _jnp.float32, 'ffn1_w1': _jnp.float32, 'ffn1_w3': _jnp.float32, 'ffn1_w2': _jnp.float32, 'mix_norm': _jnp.float32, 'w_in': _jnp.float32, 'conf_conv_w': _jnp.float32, 'conf_conv_b': _jnp.float32, 'conf_ln_g': _jnp.float32, 'conf_ln_b': _jnp.float32, 'sconv_w': _jnp.float32, 'pool_w': _jnp.float32, 'pool_scale': _jnp.float32, 'gmlp_ln_g': _jnp.float32, 'gmlp_ln_b': _jnp.float32, 'gmlp_w_s': _jnp.float32, 'gmlp_b_s': _jnp.float32, 'w_out': _jnp.float32, 'ffn2_norm': _jnp.float32, 'ffn2_w1': _jnp.float32, 'ffn2_w3': _jnp.float32, 'ffn2_w2': _jnp.float32, 'final_norm': _jnp.float32}
MOMENT_SCALE = {'ffn1_norm': 1.077087e-01, 'ffn1_w1': 4.549492e-02, 'ffn1_w3': 4.418253e-02, 'ffn1_w2': 7.330055e-02, 'mix_norm': 2.146882e-01, 'w_in': 1.529714e-01, 'conf_conv_w': 1.081435e-01, 'conf_conv_b': 2.389706e-01, 'conf_ln_g': 1.226619e-01, 'conf_ln_b': 1.013366e-01, 'sconv_w': 1.728914e-01, 'pool_w': 1.525545e-01, 'pool_scale': 1.694180e-01, 'gmlp_ln_g': 1.384408e-01, 'gmlp_ln_b': 1.190819e-01, 'gmlp_w_s': 8.746504e-02, 'gmlp_b_s': 1.243284e-01, 'w_out': 1.648683e-01, 'ffn2_norm': 7.682028e-02, 'ffn2_w1': 3.118164e-02, 'ffn2_w3': 3.021310e-02, 'ffn2_w2': 5.004402e-02, 'final_norm': 6.403396e+01}


def _to_microbatches(a, axis):
    t = _jnp.moveaxis(a, axis, 0)
    t = t.reshape((N_MICROBATCH, t.shape[0] // N_MICROBATCH) + t.shape[1:])
    return _jnp.moveaxis(t, 1, axis + 1)


def setup_inputs(seed: int = 0) -> dict:
    inp = _fwd_setup_inputs(seed)
    key = _jax.random.fold_in(_jax.random.key(seed), 7919)
    shape, _ = _output_shape()
    out = dict(inp)
    out["loss_target"] = _jax.random.normal(_jax.random.fold_in(key, 0), shape, _jnp.float32)
    for i, name in enumerate(TWIN_WEIGHTS):
        w = inp[name].astype(_jnp.float32)
        if MOMENT_SCALE is None:
            s = _jnp.sqrt(_jnp.mean(_jnp.square(w)) + 1e-30)
        else:
            s = MOMENT_SCALE[name]
        km, kv = _jax.random.split(_jax.random.fold_in(key, i + 1))
        out[name] = w
        out["m_" + name] = s * _jax.random.normal(km, w.shape, _jnp.float32)
        out["v_" + name] = (s * s) * _jax.random.uniform(kv, w.shape, _jnp.float32, 0.5, 1.5)
    if N_MICROBATCH > 1:
        for name, axis in PER_EXAMPLE_BATCH_AXIS.items():
            out[name] = _to_microbatches(out[name], axis)
    return {'x': out['x'], 'ffn1_norm': out['ffn1_norm'], 'ffn1_w1': out['ffn1_w1'], 'ffn1_w3': out['ffn1_w3'], 'ffn1_w2': out['ffn1_w2'], 'mix_norm': out['mix_norm'], 'w_in': out['w_in'], 'conf_conv_w': out['conf_conv_w'], 'conf_conv_b': out['conf_conv_b'], 'conf_ln_g': out['conf_ln_g'], 'conf_ln_b': out['conf_ln_b'], 'sconv_w': out['sconv_w'], 'pool_w': out['pool_w'], 'pool_scale': out['pool_scale'], 'gmlp_ln_g': out['gmlp_ln_g'], 'gmlp_ln_b': out['gmlp_ln_b'], 'gmlp_w_s': out['gmlp_w_s'], 'gmlp_b_s': out['gmlp_b_s'], 'w_out': out['w_out'], 'ffn2_norm': out['ffn2_norm'], 'ffn2_w1': out['ffn2_w1'], 'ffn2_w3': out['ffn2_w3'], 'ffn2_w2': out['ffn2_w2'], 'final_norm': out['final_norm'], 'loss_target': out['loss_target'], 'm_ffn1_norm': out['m_ffn1_norm'], 'm_ffn1_w1': out['m_ffn1_w1'], 'm_ffn1_w3': out['m_ffn1_w3'], 'm_ffn1_w2': out['m_ffn1_w2'], 'm_mix_norm': out['m_mix_norm'], 'm_w_in': out['m_w_in'], 'm_conf_conv_w': out['m_conf_conv_w'], 'm_conf_conv_b': out['m_conf_conv_b'], 'm_conf_ln_g': out['m_conf_ln_g'], 'm_conf_ln_b': out['m_conf_ln_b'], 'm_sconv_w': out['m_sconv_w'], 'm_pool_w': out['m_pool_w'], 'm_pool_scale': out['m_pool_scale'], 'm_gmlp_ln_g': out['m_gmlp_ln_g'], 'm_gmlp_ln_b': out['m_gmlp_ln_b'], 'm_gmlp_w_s': out['m_gmlp_w_s'], 'm_gmlp_b_s': out['m_gmlp_b_s'], 'm_w_out': out['m_w_out'], 'm_ffn2_norm': out['m_ffn2_norm'], 'm_ffn2_w1': out['m_ffn2_w1'], 'm_ffn2_w3': out['m_ffn2_w3'], 'm_ffn2_w2': out['m_ffn2_w2'], 'm_final_norm': out['m_final_norm'], 'v_ffn1_norm': out['v_ffn1_norm'], 'v_ffn1_w1': out['v_ffn1_w1'], 'v_ffn1_w3': out['v_ffn1_w3'], 'v_ffn1_w2': out['v_ffn1_w2'], 'v_mix_norm': out['v_mix_norm'], 'v_w_in': out['v_w_in'], 'v_conf_conv_w': out['v_conf_conv_w'], 'v_conf_conv_b': out['v_conf_conv_b'], 'v_conf_ln_g': out['v_conf_ln_g'], 'v_conf_ln_b': out['v_conf_ln_b'], 'v_sconv_w': out['v_sconv_w'], 'v_pool_w': out['v_pool_w'], 'v_pool_scale': out['v_pool_scale'], 'v_gmlp_ln_g': out['v_gmlp_ln_g'], 'v_gmlp_ln_b': out['v_gmlp_ln_b'], 'v_gmlp_w_s': out['v_gmlp_w_s'], 'v_gmlp_b_s': out['v_gmlp_b_s'], 'v_w_out': out['v_w_out'], 'v_ffn2_norm': out['v_ffn2_norm'], 'v_ffn2_w1': out['v_ffn2_w1'], 'v_ffn2_w3': out['v_ffn2_w3'], 'v_ffn2_w2': out['v_ffn2_w2'], 'v_final_norm': out['v_final_norm']}


def _loss(weights, diff, rest, loss_target):
    with _jax.named_scope("forward"):
        args = {**rest, TWIN_DIFF_INPUT: diff, **{k: w.astype(_WEIGHT_DTYPES[k]) for k, w in weights.items()}}
        y = _forward(args)
    with _jax.named_scope("loss_head"):
        err = _jnp.square(y.astype(_jnp.float32) - loss_target)
        return 0.5 * _jnp.sum(_jnp.mean(err, axis=-1)) if err.ndim else 0.5 * err


def _adamw(w, g, m, v):
    m = ADAM_B1 * m + (1.0 - ADAM_B1) * g
    v = ADAM_B2 * v + (1.0 - ADAM_B2) * _jnp.square(g)
    m_hat = m / (1.0 - ADAM_B1 ** ADAM_STEP)
    v_hat = v / (1.0 - ADAM_B2 ** ADAM_STEP)
    delta = -ADAM_LR * (m_hat / (_jnp.sqrt(v_hat) + ADAM_EPS) + ADAM_WD * w)
    return delta, m, v


def reference(x, ffn1_norm, ffn1_w1, ffn1_w3, ffn1_w2, mix_norm, w_in, conf_conv_w, conf_conv_b, conf_ln_g, conf_ln_b, sconv_w, pool_w, pool_scale, gmlp_ln_g, gmlp_ln_b, gmlp_w_s, gmlp_b_s, w_out, ffn2_norm, ffn2_w1, ffn2_w3, ffn2_w2, final_norm, loss_target, m_ffn1_norm, m_ffn1_w1, m_ffn1_w3, m_ffn1_w2, m_mix_norm, m_w_in, m_conf_conv_w, m_conf_conv_b, m_conf_ln_g, m_conf_ln_b, m_sconv_w, m_pool_w, m_pool_scale, m_gmlp_ln_g, m_gmlp_ln_b, m_gmlp_w_s, m_gmlp_b_s, m_w_out, m_ffn2_norm, m_ffn2_w1, m_ffn2_w3, m_ffn2_w2, m_final_norm, v_ffn1_norm, v_ffn1_w1, v_ffn1_w3, v_ffn1_w2, v_mix_norm, v_w_in, v_conf_conv_w, v_conf_conv_b, v_conf_ln_g, v_conf_ln_b, v_sconv_w, v_pool_w, v_pool_scale, v_gmlp_ln_g, v_gmlp_ln_b, v_gmlp_w_s, v_gmlp_b_s, v_w_out, v_ffn2_norm, v_ffn2_w1, v_ffn2_w3, v_ffn2_w2, v_final_norm):
    given = dict(x=x, ffn1_norm=ffn1_norm, ffn1_w1=ffn1_w1, ffn1_w3=ffn1_w3, ffn1_w2=ffn1_w2, mix_norm=mix_norm, w_in=w_in, conf_conv_w=conf_conv_w, conf_conv_b=conf_conv_b, conf_ln_g=conf_ln_g, conf_ln_b=conf_ln_b, sconv_w=sconv_w, pool_w=pool_w, pool_scale=pool_scale, gmlp_ln_g=gmlp_ln_g, gmlp_ln_b=gmlp_ln_b, gmlp_w_s=gmlp_w_s, gmlp_b_s=gmlp_b_s, w_out=w_out, ffn2_norm=ffn2_norm, ffn2_w1=ffn2_w1, ffn2_w3=ffn2_w3, ffn2_w2=ffn2_w2, final_norm=final_norm, loss_target=loss_target, m_ffn1_norm=m_ffn1_norm, m_ffn1_w1=m_ffn1_w1, m_ffn1_w3=m_ffn1_w3, m_ffn1_w2=m_ffn1_w2, m_mix_norm=m_mix_norm, m_w_in=m_w_in, m_conf_conv_w=m_conf_conv_w, m_conf_conv_b=m_conf_conv_b, m_conf_ln_g=m_conf_ln_g, m_conf_ln_b=m_conf_ln_b, m_sconv_w=m_sconv_w, m_pool_w=m_pool_w, m_pool_scale=m_pool_scale, m_gmlp_ln_g=m_gmlp_ln_g, m_gmlp_ln_b=m_gmlp_ln_b, m_gmlp_w_s=m_gmlp_w_s, m_gmlp_b_s=m_gmlp_b_s, m_w_out=m_w_out, m_ffn2_norm=m_ffn2_norm, m_ffn2_w1=m_ffn2_w1, m_ffn2_w3=m_ffn2_w3, m_ffn2_w2=m_ffn2_w2, m_final_norm=m_final_norm, v_ffn1_norm=v_ffn1_norm, v_ffn1_w1=v_ffn1_w1, v_ffn1_w3=v_ffn1_w3, v_ffn1_w2=v_ffn1_w2, v_mix_norm=v_mix_norm, v_w_in=v_w_in, v_conf_conv_w=v_conf_conv_w, v_conf_conv_b=v_conf_conv_b, v_conf_ln_g=v_conf_ln_g, v_conf_ln_b=v_conf_ln_b, v_sconv_w=v_sconv_w, v_pool_w=v_pool_w, v_pool_scale=v_pool_scale, v_gmlp_ln_g=v_gmlp_ln_g, v_gmlp_ln_b=v_gmlp_ln_b, v_gmlp_w_s=v_gmlp_w_s, v_gmlp_b_s=v_gmlp_b_s, v_w_out=v_w_out, v_ffn2_norm=v_ffn2_norm, v_ffn2_w1=v_ffn2_w1, v_ffn2_w3=v_ffn2_w3, v_ffn2_w2=v_ffn2_w2, v_final_norm=v_final_norm)
    weights = {n: given[n] for n in TWIN_WEIGHTS}
    shared = {n: given[n] for n in SHARED_INPUTS}
    per_example = {n: given[n] for n in ['x']}
    grad_fn = _jax.value_and_grad(_loss, argnums=(0, 1))

    def one_microbatch(ex, loss_target):
        ex = dict(ex)
        diff = ex.pop(TWIN_DIFF_INPUT)
        return grad_fn(weights, diff, {**shared, **ex}, loss_target)

    if N_MICROBATCH == 1:
        loss, (grad_w, grad_x) = one_microbatch(per_example, given["loss_target"])
    else:
        def body(carry, xs):
            loss_sum, grad_sum = carry
            l_k, (gw_k, gx_k) = one_microbatch(xs[0], xs[1])
            with _jax.named_scope("update"):
                return (loss_sum + l_k, _jax.tree.map(_jnp.add, grad_sum, gw_k)), gx_k

        init = (_jnp.zeros((), _jnp.float32), _jax.tree.map(_jnp.zeros_like, weights))
        (loss, grad_w), grad_x = _jax.lax.scan(body, init, (per_example, given["loss_target"]))
    with _jax.named_scope("update"):
        delta_w, new_m, new_v = {}, {}, {}
        for n in TWIN_WEIGHTS:
            delta_w[n], new_m[n], new_v[n] = _adamw(weights[n], grad_w[n], given["m_" + n], given["v_" + n])
    return (loss, grad_x, *[grad_w[n] for n in TWIN_WEIGHTS], *[delta_w[n] for n in TWIN_WEIGHTS],
            *[new_m[n] for n in TWIN_WEIGHTS], *[new_v[n] for n in TWIN_WEIGHTS])
```

```python
import functools

import jax
import jax.numpy as jnp
from jax import lax
from jax.experimental import pallas as pl
from jax.experimental.pallas import tpu as pltpu

F32 = jnp.float32
BF16 = jnp.bfloat16
MESH = pl.DeviceIdType.MESH

N_DEV = 8
EPS = 1e-6
FFN_RESIDUAL = 0.5
CONF_KERNEL = 31
SHORT_KERNEL = 3
POOL_WINDOWS = (2, 4, 8, 16)
CHUNK = 128
HEAD_DIM = 64
N_IN_PIECES = 8
HALO = 32
CONV_ROWS = 16
ADAM_LR = 0.001
ADAM_B1 = 0.9
ADAM_B2 = 0.999
ADAM_EPS = 1e-08
ADAM_WD = 0.01
ADAM_STEP = 10
VMEM_LIMIT_V7X = 56 * 1024 * 1024

WEIGHTS = ['ffn1_norm', 'ffn1_w1', 'ffn1_w3', 'ffn1_w2', 'mix_norm', 'w_in', 'conf_conv_w', 'conf_conv_b',
           'conf_ln_g', 'conf_ln_b', 'sconv_w', 'pool_w', 'pool_scale', 'gmlp_ln_g', 'gmlp_ln_b', 'gmlp_w_s',
           'gmlp_b_s', 'w_out', 'ffn2_norm', 'ffn2_w1', 'ffn2_w3', 'ffn2_w2', 'final_norm']
SHARDED_ROWS = ('ffn1_w2', 'ffn2_w2', 'w_out')
SHARDED_COLS = ('ffn1_w1', 'ffn1_w3', 'ffn2_w1', 'ffn2_w3', 'w_in')
SHARDED_CHAN = ('conf_conv_w', 'sconv_w')


def _params(sem=None):
    return pltpu.CompilerParams(dimension_semantics=sem, vmem_limit_bytes=VMEM_LIMIT_V7X)


def _nn(a, b):
    return jnp.dot(a, b, preferred_element_type=F32)


def _nt(a, b):
    return lax.dot_general(a, b, (((1,), (1,)), ((), ())), preferred_element_type=F32)


def _tn(a, b):
    return lax.dot_general(a, b, (((0,), (0,)), ((), ())), preferred_element_type=F32)


def _sigmoid(x):
    return 1.0 / (1.0 + jnp.exp(-x))


def _row_tile(n, pref, mult=8):
    t = min(n, pref)
    while n % t or t % mult:
        t -= 1
    return t


def _chunks(n, size):
    out, s = [], 0
    while s < n:
        out.append((s, min(size, n - s)))
        s += size
    return out


def _const(shape):
    return pl.BlockSpec(shape, lambda i: (0,) * len(shape))


def _resident(shape, index):
    return pl.BlockSpec(shape, lambda i: index, pipeline_mode=pl.Buffered(1))


def _wspec(rows, d, blk):
    return _resident((N_DEV, rows, d), (0, blk, 0))


def _layernorm_stats(x):
    mu = jnp.mean(x, axis=-1, keepdims=True)
    xc = x - mu
    r = lax.rsqrt(jnp.mean(xc * xc, axis=-1, keepdims=True) + EPS)
    return xc * r, r


def _layernorm_bwd(dy, g, xhat, r):
    dxh = dy * g
    return r * (dxh - jnp.mean(dxh, axis=-1, keepdims=True) - xhat * jnp.mean(dxh * xhat, axis=-1, keepdims=True))


def _rmsnorm_stats(x):
    r = lax.rsqrt(jnp.mean(x * x, axis=-1, keepdims=True) + EPS)
    return x * r, r


def _rmsnorm_bwd(dh, g, xhat, r):
    dxh = dh * g
    return r * (dxh - xhat * jnp.mean(dxh * xhat, axis=-1, keepdims=True))


class _Layout:
    def __init__(self, depth, fs, g, wo):
        self.depth, self.fs, self.g, self.wo = depth, fs, g, wo
        self.win_base = depth * 6 * fs
        self.wout_base = self.win_base + depth * g
        self.conv_base = self.wout_base + depth * wo
        self.rows = self.conv_base + CONV_ROWS
        assert self.win_base % g == 0 and self.wout_base % wo == 0 and fs % 16 == 0

    def ffn(self, l, j):
        return l * 6 + j

    def win(self, l):
        return self.win_base // self.g + l

    def wout(self, l):
        return self.wout_base // self.wo + l


def _all_gather(shard, name):
    rows, cols = shard.shape

    def body(x_ref, out_ref, send_sems, recv_sems, local_sem):
        x, y, c = lax.axis_index("x"), lax.axis_index("y"), lax.axis_index("c")
        me, sibling = (x, y, c), (x, y, 1 - c)
        chips = [(1 - x, y), (x, 1 - y), (1 - x, 1 - y)]

        def slot(px, py, pc):
            return out_ref.at[4 * px + 2 * py + pc]

        def copy(k, block, to, src=None):
            return pltpu.make_async_remote_copy(
                src_ref=slot(*block) if src is None else src, dst_ref=slot(*block),
                send_sem=send_sems.at[k], recv_sem=recv_sems.at[k], device_id=to, device_id_type=MESH)

        mine = pltpu.make_async_copy(x_ref, slot(*me), local_sem)
        mine.start()
        first = [copy(0, me, sibling, src=x_ref)]
        first += [copy(1 + j, me, (*chip, c), src=x_ref) for j, chip in enumerate(chips)]
        for cp in first:
            cp.start()
        passed = [copy(4 + j, (*chip, c), sibling) for j, chip in enumerate(chips)]
        for j, chip in enumerate(chips):
            copy(1 + j, (*chip, c), me).wait_recv()
            passed[j].start()
        copy(0, sibling, me).wait_recv()
        for j, chip in enumerate(chips):
            copy(4 + j, (*chip, 1 - c), me).wait_recv()
        for cp in first + passed:
            cp.wait_send()
        mine.wait()

    return pl.pallas_call(
        body, name=name,
        out_shape=jax.ShapeDtypeStruct((N_DEV, rows, cols), shard.dtype),
        in_specs=[pl.BlockSpec(memory_space=pl.ANY)],
        out_specs=pl.BlockSpec(memory_space=pl.ANY),
        scratch_shapes=[pltpu.SemaphoreType.DMA((7,)), pltpu.SemaphoreType.DMA((7,)), pltpu.SemaphoreType.DMA],
    )(shard)


def _exchange(blocks, name):
    _, rows, cols = blocks.shape

    def body(s_ref, r_ref, send_sems, recv_sems, local_sem):
        x, y, c = lax.axis_index("x"), lax.axis_index("y"), lax.axis_index("c")
        me = 4 * x + 2 * y + c
        mine = pltpu.make_async_copy(s_ref.at[me], r_ref.at[me], local_sem)
        mine.start()

        def copy(k):
            fx, fy, fc = (k >> 2) & 1, (k >> 1) & 1, k & 1
            px = (1 - x) if fx else x
            py = (1 - y) if fy else y
            pc = (1 - c) if fc else c
            peer = 4 * px + 2 * py + pc
            send = pltpu.make_async_remote_copy(
                src_ref=s_ref.at[peer], dst_ref=r_ref.at[me], send_sem=send_sems.at[k - 1],
                recv_sem=recv_sems.at[k - 1], device_id=(px, py, pc), device_id_type=MESH)
            recv = pltpu.make_async_remote_copy(
                src_ref=s_ref.at[peer], dst_ref=r_ref.at[peer], send_sem=send_sems.at[k - 1],
                recv_sem=recv_sems.at[k - 1], device_id=(px, py, pc), device_id_type=MESH)
            return send, recv

        copies = [copy(k) for k in range(1, N_DEV)]
        for send, _ in copies:
            send.start()
        for _, recv in copies:
            recv.wait_recv()
        for send, _ in copies:
            send.wait_send()
        mine.wait()

    return pl.pallas_call(
        body, name=name,
        out_shape=jax.ShapeDtypeStruct(blocks.shape, blocks.dtype),
        in_specs=[pl.BlockSpec(memory_space=pl.ANY)],
        out_specs=pl.BlockSpec(memory_space=pl.ANY),
        scratch_shapes=[pltpu.SemaphoreType.DMA((7,)), pltpu.SemaphoreType.DMA((7,)), pltpu.SemaphoreType.DMA],
    )(blocks)


def _sum_slots(slots, name):
    _, rows, cols = slots.shape
    tr = _row_tile(rows, 256, 16)

    def body(s_ref, o_ref):
        acc = s_ref[0].astype(F32)
        for j in range(1, N_DEV):
            acc = acc + s_ref[j].astype(F32)
        o_ref[...] = acc

    return pl.pallas_call(
        body, name=name, grid=(rows // tr,),
        out_shape=jax.ShapeDtypeStruct((rows, cols), F32),
        in_specs=[pl.BlockSpec((N_DEV, tr, cols), lambda i: (0, i, 0))],
        out_specs=pl.BlockSpec((tr, cols), lambda i: (i, 0)),
        compiler_params=_params(("parallel",)),
    )(slots)


def _ffn_fwd(x, gain, wbuf, lay, l, half, name):
    t, d = x.shape
    fs = lay.fs
    f = N_DEV * fs
    tm = _row_tile(t, 256, 128)
    cols = _chunks(f, 1024)

    def body(x_ref, g_ref, w1_ref, w3_ref, w2_ref, y_ref, a_ref, b_ref, u_ref):
        xv = x_ref[...]
        xhat, _ = _rmsnorm_stats(xv)
        h = (xhat * g_ref[...]).astype(BF16)
        w1 = w1_ref[...].reshape(f, d)
        w3 = w3_ref[...].reshape(f, d)
        for s, n in cols:
            a = _nt(h, w1[s:s + n, :])
            b = _nt(h, w3[s:s + n, :])
            a_ref[:, s:s + n] = a.astype(BF16)
            b_ref[:, s:s + n] = b.astype(BF16)
            u_ref[:, s:s + n] = (a * _sigmoid(a) * b).astype(BF16)
        y_ref[...] = xv + FFN_RESIDUAL * _nn(u_ref[...], w2_ref[...].reshape(f, d))

    return pl.pallas_call(
        body, name=name, grid=(t // tm,),
        out_shape=(jax.ShapeDtypeStruct((t, d), F32), jax.ShapeDtypeStruct((t, f), BF16),
                   jax.ShapeDtypeStruct((t, f), BF16)),
        in_specs=[pl.BlockSpec((tm, d), lambda i: (i, 0)), _const((1, d)),
                  _wspec(fs, d, lay.ffn(l, 3 * half)), _wspec(fs, d, lay.ffn(l, 3 * half + 1)),
                  _wspec(fs, d, lay.ffn(l, 3 * half + 2))],
        out_specs=(pl.BlockSpec((tm, d), lambda i: (i, 0)), pl.BlockSpec((tm, f), lambda i: (i, 0)),
                   pl.BlockSpec((tm, f), lambda i: (i, 0))),
        scratch_shapes=[pltpu.VMEM((tm, f), BF16)],
        compiler_params=_params(("parallel",)),
    )(x, gain.reshape(1, d), wbuf, wbuf, wbuf)


def _ffn_bwd(x, dy, a, b, gain, wbuf, lay, l, half, name):
    t, d = x.shape
    fs = lay.fs
    f = N_DEV * fs
    tm = _row_tile(t, 256, 128)
    cols = _chunks(f, 1024)

    def body(x_ref, dy_ref, a_ref, b_ref, g_ref, w1_ref, w3_ref, w2_ref,
             dx_ref, da_ref, db_ref, u_ref, h_ref, dg_ref):
        @pl.when(pl.program_id(0) == 0)
        def _():
            dg_ref[...] = jnp.zeros_like(dg_ref)

        xv, dyv, g = x_ref[...], dy_ref[...], g_ref[...]
        xhat, r = _rmsnorm_stats(xv)
        h_ref[...] = (xhat * g).astype(BF16)
        dyb = (FFN_RESIDUAL * dyv).astype(BF16)
        w1 = w1_ref[...].reshape(f, d)
        w3 = w3_ref[...].reshape(f, d)
        w2 = w2_ref[...].reshape(f, d)
        dh = jnp.zeros((tm, d), F32)
        for s, n in cols:
            du = _nt(dyb, w2[s:s + n, :])
            av = a_ref[:, s:s + n].astype(F32)
            bv = b_ref[:, s:s + n].astype(F32)
            sig = _sigmoid(av)
            sa = av * sig
            u_ref[:, s:s + n] = (sa * bv).astype(BF16)
            da = (du * bv * (sig * (1.0 + av * (1.0 - sig)))).astype(BF16)
            db = (du * sa).astype(BF16)
            da_ref[:, s:s + n] = da
            db_ref[:, s:s + n] = db
            dh = dh + _nn(da, w1[s:s + n, :]) + _nn(db, w3[s:s + n, :])
        dx_ref[...] = dyv + _rmsnorm_bwd(dh, g, xhat, r)
        dg_ref[0:1, :] += jnp.sum(dh * xhat, axis=0, keepdims=True)

    big = jax.ShapeDtypeStruct((t, f), BF16)
    row = lambda w: pl.BlockSpec((tm, w), lambda i: (i, 0))
    return pl.pallas_call(
        body, name=name, grid=(t // tm,),
        out_shape=(jax.ShapeDtypeStruct((t, d), F32), big, big, big, jax.ShapeDtypeStruct((t, d), BF16),
                   jax.ShapeDtypeStruct((8, d), F32)),
        in_specs=[row(d), row(d), row(f), row(f), _const((1, d)),
                  _wspec(fs, d, lay.ffn(l, 3 * half)), _wspec(fs, d, lay.ffn(l, 3 * half + 1)),
                  _wspec(fs, d, lay.ffn(l, 3 * half + 2))],
        out_specs=(row(d), row(f), row(f), row(f), row(d), _const((8, d))),
        compiler_params=_params(("arbitrary",)),
    )(x, dy, a, b, gain.reshape(1, d), wbuf, wbuf, wbuf)


def _wgrad(lhs, rhs, gbuf, rows, blk, scale, name):
    t, m = lhs.shape
    d = rhs.shape[1]
    bt = _row_tile(t, 512, 128)
    nk = t // bt
    assert m == N_DEV * rows

    def body(a_ref, b_ref, g_in, o_ref, acc_ref):
        del g_in
        k = pl.program_id(0)

        @pl.when(k == 0)
        def _():
            acc_ref[...] = jnp.zeros_like(acc_ref)

        acc_ref[...] += _tn(a_ref[...], b_ref[...].astype(BF16))

        @pl.when(k == nk - 1)
        def _():
            o_ref[...] = (scale * acc_ref[...]).astype(BF16).reshape(N_DEV, rows, d)

    return pl.pallas_call(
        body, name=name, grid=(nk,),
        out_shape=jax.ShapeDtypeStruct(gbuf.shape, gbuf.dtype),
        in_specs=[pl.BlockSpec((bt, m), lambda k: (k, 0)), pl.BlockSpec((bt, d), lambda k: (k, 0)),
                  pl.BlockSpec(memory_space=pl.ANY)],
        out_specs=pl.BlockSpec((N_DEV, rows, d), lambda k: (0, blk, 0)),
        scratch_shapes=[pltpu.VMEM((m, d), F32)],
        input_output_aliases={2: 0},
        compiler_params=_params(("arbitrary",)),
    )(lhs, rhs, gbuf)


def _mix_in_fwd(x, gain, wbuf, lay, l, name):
    t, d = x.shape
    g = lay.g
    di = N_DEV * g
    tm = _row_tile(t, 512, 128)

    def body(x_ref, g_ref, w_ref, h_ref, p_ref):
        xhat, _ = _rmsnorm_stats(x_ref[...])
        h = (xhat * g_ref[...]).astype(BF16)
        h_ref[...] = h
        p_ref[...] = _nt(h, w_ref[...].reshape(di, d)).astype(BF16)

    return pl.pallas_call(
        body, name=name, grid=(t // tm,),
        out_shape=(jax.ShapeDtypeStruct((t, d), BF16), jax.ShapeDtypeStruct((t, di), BF16)),
        in_specs=[pl.BlockSpec((tm, d), lambda i: (i, 0)), _const((1, d)), _wspec(g, d, lay.win(l))],
        out_specs=(pl.BlockSpec((tm, d), lambda i: (i, 0)), pl.BlockSpec((tm, di), lambda i: (i, 0))),
        compiler_params=_params(("parallel",)),
    )(x, gain.reshape(1, d), wbuf)


def _mix_in_bwd(x, dx_out, dp, gain, wbuf, lay, l, name):
    t, d = x.shape
    g = lay.g
    di = N_DEV * g
    tm = _row_tile(t, 512, 128)

    def body(x_ref, dxo_ref, dp_ref, g_ref, w_ref, dx_ref, dg_ref):
        @pl.when(pl.program_id(0) == 0)
        def _():
            dg_ref[...] = jnp.zeros_like(dg_ref)

        xhat, r = _rmsnorm_stats(x_ref[...])
        dh = _nn(dp_ref[...], w_ref[...].reshape(di, d))
        dx_ref[...] = dxo_ref[...] + _rmsnorm_bwd(dh, g_ref[...], xhat, r)
        dg_ref[0:1, :] += jnp.sum(dh * xhat, axis=0, keepdims=True)

    row = lambda w: pl.BlockSpec((tm, w), lambda i: (i, 0))
    return pl.pallas_call(
        body, name=name, grid=(t // tm,),
        out_shape=(jax.ShapeDtypeStruct((t, d), F32), jax.ShapeDtypeStruct((8, d), F32)),
        in_specs=[row(d), row(d), row(di), _const((1, d)), _wspec(g, d, lay.win(l))],
        out_specs=(row(d), _const((8, d))),
        compiler_params=_params(("arbitrary",)),
    )(x, dx_out, dp, gain.reshape(1, d), wbuf)


def _head_masks(rows, g):
    lane = lax.broadcasted_iota(jnp.int32, (rows, g), 1)
    return [(lane >= h * HEAD_DIM) & (lane < (h + 1) * HEAD_DIM) for h in range(g // HEAD_DIM)]


def _pool_windows(rows, g):
    lane = lax.broadcasted_iota(jnp.int32, (rows, g), 1)
    pg = g // len(POOL_WINDOWS)
    w = jnp.full((rows, g), float(POOL_WINDOWS[-1]), F32)
    for k in range(len(POOL_WINDOWS) - 2, -1, -1):
        w = jnp.where(lane < (k + 1) * pg, float(POOL_WINDOWS[k]), w)
    return w


def _window_sum(ref, base, rows, wl, direction):
    acc = ref[pl.ds(base, rows), :]
    out = None
    done = 1
    for w in POOL_WINDOWS:
        for j in range(done, w):
            acc = acc + ref[pl.ds(base + direction * j, rows), :]
        done = w
        out = acc if out is None else jnp.where(wl >= float(w), acc, out)
    return out


def _pool_divisor(first_row, rows, g):
    pos1 = (lax.broadcasted_iota(jnp.int32, (rows, g), 0) + first_row + 1).astype(F32)
    return jnp.minimum(pos1, _pool_windows(rows, g))


def _spatial_mix(vq, wcat_ref, masks):
    vstack = jnp.concatenate([jnp.where(m, vq, jnp.zeros_like(vq)) for m in masks], axis=0)
    return _nn(wcat_ref[...], vstack)


def _mix_fwd(p, x, wbuf, lay, l, small, name):
    t, d = x.shape
    g = lay.g
    di = N_DEV * g
    tm = _row_tile(t, 256, CHUNK)
    nh = g // HEAD_DIM
    per = tm // HALO
    conv_w, vecs, sconv_w, pool_bd, wcat, bst = small

    def body(pp_ref, pm_ref, x_ref, wo_ref, cw_ref, vec_ref, sw_ref, pbd_ref, wcat_ref, bst_ref,
             y_ref, mix_ref, ea_ref, eb_ref, ec_ref):
        i = pl.program_id(0)
        seen = (i > 0).astype(F32)

        def prev(j):
            return pp_ref[:, j * g:(j + 1) * g].astype(F32) * seen

        def main(j):
            return pm_ref[:, j * g:(j + 1) * g].astype(F32)

        vec = vec_ref[...]
        ea_ref[0:HALO, :] = prev(0) * _sigmoid(prev(1))
        ea_ref[HALO:, :] = main(0) * _sigmoid(main(1))
        cw = cw_ref[...]
        c = jnp.zeros((tm, g), F32) + vec[0:1, :]
        for k in range(CONF_KERNEL):
            c = c + cw[k:k + 1, :] * ea_ref[pl.ds(HALO - (CONF_KERNEL - 1) + k, tm), :]
        chat, _ = _layernorm_stats(c)
        ln = chat * vec[1:2, :] + vec[2:3, :]
        mix_ref[:, 0:g] = (ln * _sigmoid(ln)).astype(BF16)
        eb_ref[0:HALO, :] = prev(3) * prev(4)
        eb_ref[HALO:, :] = main(3) * main(4)
        sw = sw_ref[...]
        cz = jnp.zeros((tm, g), F32)
        for k in range(SHORT_KERNEL):
            cz = cz + sw[k:k + 1, :] * eb_ref[pl.ds(HALO - (SHORT_KERNEL - 1) + k, tm), :]
        mix_ref[:, g:2 * g] = (main(2) * cz).astype(BF16)
        ec_ref[0:HALO, :] = prev(5)
        xp = main(5)
        ec_ref[HALO:, :] = xp
        wl = _pool_windows(tm, g)
        mean = _window_sum(ec_ref, HALO, tm, wl, -1) / _pool_divisor(i * tm, tm, g)
        yc = _nn((mean - xp).astype(BF16), pbd_ref[...])
        mix_ref[:, 2 * g:3 * g] = (yc * vec[3:4, :]).astype(BF16)
        vhat, _ = _layernorm_stats(main(7))
        v = (vhat * vec[4:5, :] + vec[5:6, :]).astype(BF16)
        gu = main(6)
        masks = _head_masks(CHUNK, g)
        for q in range(tm // CHUNK):
            rs = slice(q * CHUNK, (q + 1) * CHUNK)
            mixed = _spatial_mix(v[rs, :], wcat_ref, masks) + bst_ref[...]
            mix_ref[rs, 3 * g:4 * g] = (gu[rs, :] * mixed).astype(BF16)
        y_ref[...] = x_ref[...] + _nn(mix_ref[...], wo_ref[...].reshape(d, d))

    return pl.pallas_call(
        body, name=name, grid=(t // tm,),
        out_shape=(jax.ShapeDtypeStruct((t, d), F32), jax.ShapeDtypeStruct((t, 4 * g), BF16)),
        in_specs=[pl.BlockSpec((HALO, di), lambda i: (jnp.maximum(i * per - 1, 0), 0)),
                  pl.BlockSpec((tm, di), lambda i: (i, 0)), pl.BlockSpec((tm, d), lambda i: (i, 0)),
                  _wspec(lay.wo, d, lay.wout(l)),
                  _const(conv_w.shape), _const(vecs.shape), _const(sconv_w.shape), _const(pool_bd.shape),
                  _const(wcat.shape), _const(bst.shape)],
        out_specs=(pl.BlockSpec((tm, d), lambda i: (i, 0)), pl.BlockSpec((tm, 4 * g), lambda i: (i, 0))),
        scratch_shapes=[pltpu.VMEM((HALO + tm, g), F32)] * 3,
        compiler_params=_params(("parallel",)),
    )(p, p, x, wbuf, conv_w, vecs, sconv_w, pool_bd, wcat, bst)


def _mix_bwd(p, dy, wbuf, lay, l, small, name):
    t, d = dy.shape
    g = lay.g
    di = N_DEV * g
    tm = _row_tile(t, 256, CHUNK)
    nt = t // tm
    nh = g // HEAD_DIM
    per = tm // HALO
    ext = tm + HALO
    conv_w, vecs, sconv_w, pool_bd, wcat, bst, wcat_t = small

    def body(pp_ref, pm_ref, pn_ref, dy_ref, dyn_ref, wo_ref, cw_ref, vec_ref, sw_ref, pbd_ref, wcat_ref, bst_ref,
             wcatt_ref, dp_ref, dcw_ref, dvec_ref, dsw_ref, dpbd_ref, dws_ref, dbs_ref,
             ea_ref, eb_ref, ec_ref, fa_ref, fb_ref, fc_ref):
        i = pl.program_id(0)
        seen = (i > 0).astype(F32)
        more = (i < nt - 1).astype(F32)

        @pl.when(i == 0)
        def _():
            for ref in (dcw_ref, dvec_ref, dsw_ref, dpbd_ref, dws_ref, dbs_ref):
                ref[...] = jnp.zeros_like(ref)

        def prev(j):
            return pp_ref[:, j * g:(j + 1) * g].astype(F32) * seen

        def main(j):
            return pm_ref[:, j * g:(j + 1) * g].astype(F32)

        def nxt(j):
            return pn_ref[:, j * g:(j + 1) * g].astype(F32)

        def colsum(v):
            return jnp.sum(v, axis=0, keepdims=True)

        vec = vec_ref[...]
        wo = wo_ref[...].reshape(d, d)
        dmix = jnp.concatenate([_nt(dy_ref[...].astype(BF16), wo),
                                _nt((dyn_ref[...] * more).astype(BF16), wo)], axis=0)

        val, gate = main(0), main(1)
        sg = _sigmoid(gate)
        ea_ref[0:HALO, :] = prev(0) * _sigmoid(prev(1))
        ea_ref[HALO:ext, :] = val * sg
        ea_ref[ext:, :] = nxt(0) * _sigmoid(nxt(1))
        cw = cw_ref[...]
        c = jnp.zeros((ext, g), F32) + vec[0:1, :]
        for k in range(CONF_KERNEL):
            c = c + cw[k:k + 1, :] * ea_ref[pl.ds(HALO - (CONF_KERNEL - 1) + k, ext), :]
        chat, r = _layernorm_stats(c)
        ln = chat * vec[1:2, :] + vec[2:3, :]
        sl = _sigmoid(ln)
        dln = dmix[:, 0:g] * (sl * (1.0 + ln * (1.0 - sl)))
        dvec_ref[1:2, :] += colsum(dln[0:tm] * chat[0:tm])
        dvec_ref[2:3, :] += colsum(dln[0:tm])
        dc = _layernorm_bwd(dln, vec[1:2, :], chat, r)
        dvec_ref[0:1, :] += colsum(dc[0:tm])
        fa_ref[...] = dc
        dcm = dc[0:tm]
        dy0 = jnp.zeros((tm, g), F32)
        for k in range(CONF_KERNEL):
            dy0 = dy0 + cw[k:k + 1, :] * fa_ref[pl.ds(CONF_KERNEL - 1 - k, tm), :]
            dcw_ref[k:k + 1, :] += colsum(dcm * ea_ref[pl.ds(HALO - (CONF_KERNEL - 1) + k, tm), :])
        dp_ref[:, 0:g] = (dy0 * sg).astype(BF16)
        dp_ref[:, g:2 * g] = (dy0 * val * sg * (1.0 - sg)).astype(BF16)

        sb, sc, sx = main(2), main(3), main(4)
        eb_ref[0:HALO, :] = prev(3) * prev(4)
        eb_ref[HALO:ext, :] = sc * sx
        sw = sw_ref[...]
        cz = jnp.zeros((tm, g), F32)
        for k in range(SHORT_KERNEL):
            cz = cz + sw[k:k + 1, :] * eb_ref[pl.ds(HALO - (SHORT_KERNEL - 1) + k, tm), :]
        dob = dmix[:, g:2 * g]
        dp_ref[:, 2 * g:3 * g] = (dob[0:tm] * cz).astype(BF16)
        fb_ref[0:tm, :] = dob[0:tm] * sb
        fb_ref[tm:, :] = dob[tm:] * nxt(2)
        dczm = fb_ref[0:tm, :]
        dz = jnp.zeros((tm, g), F32)
        for k in range(SHORT_KERNEL):
            dz = dz + sw[k:k + 1, :] * fb_ref[pl.ds(SHORT_KERNEL - 1 - k, tm), :]
            dsw_ref[k:k + 1, :] += colsum(dczm * eb_ref[pl.ds(HALO - (SHORT_KERNEL - 1) + k, tm), :])
        dp_ref[:, 3 * g:4 * g] = (dz * sx).astype(BF16)
        dp_ref[:, 4 * g:5 * g] = (dz * sc).astype(BF16)

        xp = main(5)
        ec_ref[0:HALO, :] = prev(5)
        ec_ref[HALO:, :] = xp
        wl = _pool_windows(tm, g)
        dpool = ((_window_sum(ec_ref, HALO, tm, wl, -1) / _pool_divisor(i * tm, tm, g)) - xp).astype(BF16)
        pbd = pbd_ref[...]
        yc = _nn(dpool, pbd)
        doc = dmix[:, 2 * g:3 * g]
        dvec_ref[3:4, :] += colsum(doc[0:tm] * yc)
        dyc = (doc * vec[3:4, :]).astype(BF16)
        dpbd_ref[...] += _tn(dpool, dyc[0:tm])
        dd = _nt(dyc, pbd)
        fc_ref[...] = dd / _pool_divisor(i * tm, ext, g)
        dp_ref[:, 5 * g:6 * g] = (_window_sum(fc_ref, 0, tm, wl, 1) - dd[0:tm]).astype(BF16)

        gv = main(7)
        vhat, rv = _layernorm_stats(gv)
        v = (vhat * vec[4:5, :] + vec[5:6, :]).astype(BF16)
        gu = main(6)
        dod = dmix[0:tm, 3 * g:4 * g]
        masks = _head_masks(CHUNK, g)
        tril = (lax.broadcasted_iota(jnp.int32, (CHUNK, CHUNK), 1)
                <= lax.broadcasted_iota(jnp.int32, (CHUNK, CHUNK), 0)).astype(F32)
        tril = jnp.concatenate([tril] * nh, axis=0)
        head_col = lax.broadcasted_iota(jnp.int32, (CHUNK, CHUNK), 1)
        dvs = []
        for q in range(tm // CHUNK):
            rs = slice(q * CHUNK, (q + 1) * CHUNK)
            vq = v[rs, :]
            mixed = _spatial_mix(vq, wcat_ref, masks) + bst_ref[...]
            dp_ref[rs, 6 * g:7 * g] = (dod[rs, :] * mixed).astype(BF16)
            dmixed = dod[rs, :] * gu[rs, :]
            dmb = dmixed.astype(BF16)
            for h, m in enumerate(masks):
                hs = jnp.sum(jnp.where(m, dmixed, 0.0), axis=-1, keepdims=True)
                dbs_ref[...] += jnp.where(head_col == h, hs, 0.0)
            dmstack = jnp.concatenate([jnp.where(m, dmb, jnp.zeros_like(dmb)) for m in masks], axis=0)
            dws_ref[...] += _nt(dmstack, vq) * tril
            back = _nn(wcatt_ref[...], dmb)
            dv = jnp.zeros((CHUNK, g), F32)
            for h, m in enumerate(masks):
                dv = jnp.where(m, back[h * CHUNK:(h + 1) * CHUNK, :], dv)
            dvs.append(dv)
        dv = jnp.concatenate(dvs, axis=0)
        dvec_ref[4:5, :] += colsum(dv * vhat)
        dvec_ref[5:6, :] += colsum(dv)
        dp_ref[:, 7 * g:8 * g] = _layernorm_bwd(dv, vec[4:5, :], vhat, rv).astype(BF16)

    halo_prev = pl.BlockSpec((HALO, di), lambda i: (jnp.maximum(i * per - 1, 0), 0))
    halo_next = lambda w: pl.BlockSpec((HALO, w), lambda i: (jnp.minimum((i + 1) * per, t // HALO - 1), 0))
    row = lambda w: pl.BlockSpec((tm, w), lambda i: (i, 0))
    outs = (jax.ShapeDtypeStruct((t, di), BF16), jax.ShapeDtypeStruct(conv_w.shape, F32),
            jax.ShapeDtypeStruct(vecs.shape, F32), jax.ShapeDtypeStruct(sconv_w.shape, F32),
            jax.ShapeDtypeStruct((g, g), F32), jax.ShapeDtypeStruct((nh * CHUNK, CHUNK), F32),
            jax.ShapeDtypeStruct((CHUNK, CHUNK), F32))
    return pl.pallas_call(
        body, name=name, grid=(nt,),
        out_shape=outs,
        in_specs=[halo_prev, row(di), halo_next(di), row(d), halo_next(d), _wspec(lay.wo, d, lay.wout(l)),
                  _const(conv_w.shape), _const(vecs.shape), _const(sconv_w.shape), _const(pool_bd.shape),
                  _const(wcat.shape), _const(bst.shape), _const(wcat_t.shape)],
        out_specs=(row(di),) + tuple(_const(o.shape) for o in outs[1:]),
        scratch_shapes=[pltpu.VMEM((HALO + tm + HALO, g), F32), pltpu.VMEM((HALO + tm, g), F32),
                        pltpu.VMEM((HALO + tm, g), F32), pltpu.VMEM((ext, g), F32), pltpu.VMEM((ext, g), F32),
                        pltpu.VMEM((ext, g), F32)],
        compiler_params=_params(("arbitrary",)),
    )(p, p, p, dy, dy, wbuf, conv_w, vecs, sconv_w, pool_bd, wcat, bst, wcat_t)


def _loss_head(x, target, gain, name):
    t, d = x.shape
    tm = _row_tile(t, 512, 8)

    def body(x_ref, t_ref, g_ref, dx_ref, loss_ref, dg_ref):
        @pl.when(pl.program_id(0) == 0)
        def _():
            loss_ref[...] = jnp.zeros_like(loss_ref)
            dg_ref[...] = jnp.zeros_like(dg_ref)

        g = g_ref[...]
        xhat, r = _rmsnorm_stats(x_ref[...])
        err = xhat * g - t_ref[...]
        loss_ref[...] += 0.5 * jnp.sum(jnp.mean(err * err, axis=-1, keepdims=True))
        dyv = err * (1.0 / d)
        dx_ref[...] = _rmsnorm_bwd(dyv, g, xhat, r)
        dg_ref[0:1, :] += jnp.sum(dyv * xhat, axis=0, keepdims=True)

    row = pl.BlockSpec((tm, d), lambda i: (i, 0))
    return pl.pallas_call(
        body, name=name, grid=(t // tm,),
        out_shape=(jax.ShapeDtypeStruct((t, d), F32), jax.ShapeDtypeStruct((8, 128), F32),
                   jax.ShapeDtypeStruct((8, d), F32)),
        in_specs=[row, row, _const((1, d))],
        out_specs=(row, _const((8, 128)), _const((8, d))),
        compiler_params=_params(("arbitrary",)),
    )(x, target, gain.reshape(1, d))


def _adamw(w, grad, m, v, name):
    shape = w.shape
    cols = shape[-1]
    rows = w.size // cols
    tr = _row_tile(rows, 512, 8) if rows > 1024 else rows

    def body(w_ref, g_ref, m_ref, v_ref, d_ref, nm_ref, nv_ref):
        gv = g_ref[...]
        nm = ADAM_B1 * m_ref[...] + (1.0 - ADAM_B1) * gv
        nv = ADAM_B2 * v_ref[...] + (1.0 - ADAM_B2) * (gv * gv)
        m_hat = nm / (1.0 - ADAM_B1 ** ADAM_STEP)
        v_hat = nv / (1.0 - ADAM_B2 ** ADAM_STEP)
        d_ref[...] = -ADAM_LR * (m_hat / (jnp.sqrt(v_hat) + ADAM_EPS) + ADAM_WD * w_ref[...])
        nm_ref[...] = nm
        nv_ref[...] = nv

    spec = pl.BlockSpec((tr, cols), lambda i: (i, 0))
    flat = jax.ShapeDtypeStruct((rows, cols), F32)
    outs = pl.pallas_call(
        body, name=name, grid=(rows // tr,),
        out_shape=(flat, flat, flat), in_specs=[spec] * 4, out_specs=(spec,) * 3,
        compiler_params=_params(("parallel",)),
    )(*(a.reshape(rows, cols) for a in (w, grad, m, v)))
    return tuple(o.reshape(shape) for o in outs)


def _f32_rows(parts, d, rows):
    out, offs, at = [], [], 0
    for a in parts:
        n = -(-a.size // d)
        out.append(jnp.pad(a.reshape(-1), (0, n * d - a.size)).reshape(n, d))
        offs.append(at)
        at += n
    total = at if rows is None else rows
    assert at <= total
    if total > at:
        out.append(jnp.zeros((total - at, d), F32))
    return jnp.concatenate(out, axis=0), offs


def _take(buf, off, shape):
    n = 1
    for s in shape:
        n *= s
    d = buf.shape[1]
    return buf[off:off + -(-n // d)].reshape(-1)[:n].reshape(shape)


def _block_diag(blocks):
    n, k, _ = blocks.shape
    out = jnp.zeros((n * k, n * k), blocks.dtype)
    for i in range(n):
        out = lax.dynamic_update_slice(out, blocks[i], (i * k, i * k))
    return out


def kernel(x, ffn1_norm, ffn1_w1, ffn1_w3, ffn1_w2, mix_norm, w_in, conf_conv_w, conf_conv_b, conf_ln_g, conf_ln_b, sconv_w, pool_w, pool_scale, gmlp_ln_g, gmlp_ln_b, gmlp_w_s, gmlp_b_s, w_out, ffn2_norm, ffn2_w1, ffn2_w3, ffn2_w2, final_norm, loss_target, m_ffn1_norm, m_ffn1_w1, m_ffn1_w3, m_ffn1_w2, m_mix_norm, m_w_in, m_conf_conv_w, m_conf_conv_b, m_conf_ln_g, m_conf_ln_b, m_sconv_w, m_pool_w, m_pool_scale, m_gmlp_ln_g, m_gmlp_ln_b, m_gmlp_w_s, m_gmlp_b_s, m_w_out, m_ffn2_norm, m_ffn2_w1, m_ffn2_w3, m_ffn2_w2, m_final_norm, v_ffn1_norm, v_ffn1_w1, v_ffn1_w3, v_ffn1_w2, v_mix_norm, v_w_in, v_conf_conv_w, v_conf_conv_b, v_conf_ln_g, v_conf_ln_b, v_sconv_w, v_pool_w, v_pool_scale, v_gmlp_ln_g, v_gmlp_ln_b, v_gmlp_w_s, v_gmlp_b_s, v_w_out, v_ffn2_norm, v_ffn2_w1, v_ffn2_w3, v_ffn2_w2, v_final_norm):
    w = dict(zip(WEIGHTS, (ffn1_norm, ffn1_w1, ffn1_w3, ffn1_w2, mix_norm, w_in, conf_conv_w, conf_conv_b, conf_ln_g,
                           conf_ln_b, sconv_w, pool_w, pool_scale, gmlp_ln_g, gmlp_ln_b, gmlp_w_s, gmlp_b_s, w_out,
                           ffn2_norm, ffn2_w1, ffn2_w3, ffn2_w2, final_norm)))
    mom1 = dict(zip(WEIGHTS, (m_ffn1_norm, m_ffn1_w1, m_ffn1_w3, m_ffn1_w2, m_mix_norm, m_w_in, m_conf_conv_w,
                              m_conf_conv_b, m_conf_ln_g, m_conf_ln_b, m_sconv_w, m_pool_w, m_pool_scale, m_gmlp_ln_g,
                              m_gmlp_ln_b, m_gmlp_w_s, m_gmlp_b_s, m_w_out, m_ffn2_norm, m_ffn2_w1, m_ffn2_w3,
                              m_ffn2_w2, m_final_norm)))
    mom2 = dict(zip(WEIGHTS, (v_ffn1_norm, v_ffn1_w1, v_ffn1_w3, v_ffn1_w2, v_mix_norm, v_w_in, v_conf_conv_w,
                              v_conf_conv_b, v_conf_ln_g, v_conf_ln_b, v_sconv_w, v_pool_w, v_pool_scale, v_gmlp_ln_g,
                              v_gmlp_ln_b, v_gmlp_w_s, v_gmlp_b_s, v_w_out, v_ffn2_norm, v_ffn2_w1, v_ffn2_w3,
                              v_ffn2_w2, v_final_norm)))
    _, t, d = x.shape
    depth, _, fs = ffn1_w1.shape
    g = w_in.shape[2]
    wo = w_out.shape[1]
    nh = g // HEAD_DIM
    cs = conf_conv_w.shape[2]
    lay = _Layout(depth, fs, g, wo)
    me = 4 * lax.axis_index("x") + 2 * lax.axis_index("y") + lax.axis_index("c")

    mats = []
    for l in range(depth):
        mats += [ffn1_w1[l].T, ffn1_w3[l].T, ffn1_w2[l], ffn2_w1[l].T, ffn2_w3[l].T, ffn2_w2[l]]
    mats += [w_in[l].T for l in range(depth)]
    mats += [w_out[l] for l in range(depth)]
    convs = jnp.concatenate([conf_conv_w, sconv_w], axis=1)
    nconv = convs.size
    conv_bits = jnp.pad(convs.reshape(-1), (0, CONV_ROWS * d // 2 - nconv))
    conv_rows = lax.bitcast_convert_type(conv_bits, BF16).reshape(CONV_ROWS, d)
    shard = jnp.concatenate([m_.astype(BF16) for m_ in mats] + [conv_rows], axis=0)
    wbuf = _all_gather(shard, "gather_weights")

    conv_all = lax.bitcast_convert_type(wbuf[:, lay.conv_base:, :].reshape(N_DEV, CONV_ROWS * d // 2, 2), F32)
    conv_all = conv_all[:, :nconv].reshape(N_DEV, depth, CONF_KERNEL + SHORT_KERNEL, cs)
    conv_all = conv_all.transpose(1, 2, 0, 3).reshape(depth, CONF_KERNEL + SHORT_KERNEL, g)

    def small_inputs(l):
        cw = jnp.pad(conv_all[l, :CONF_KERNEL], ((0, 32 - CONF_KERNEL), (0, 0)))
        sw = jnp.pad(conv_all[l, CONF_KERNEL:], ((0, 8 - SHORT_KERNEL), (0, 0)))
        vecs = jnp.stack([conf_conv_b[l], conf_ln_g[l], conf_ln_b[l], pool_scale[l], gmlp_ln_g[l], gmlp_ln_b[l],
                          jnp.zeros((g,), F32), jnp.zeros((g,), F32)])
        pool_bd = _block_diag(pool_w[l]).astype(BF16)
        low = jnp.tril(jnp.ones((CHUNK, CHUNK), bool))
        ws = jnp.where(low[None], gmlp_w_s[l], 0.0).astype(BF16)
        wcat = ws.transpose(1, 0, 2).reshape(CHUNK, nh * CHUNK)
        wcat_t = ws.transpose(0, 2, 1).reshape(nh * CHUNK, CHUNK)
        bst = jnp.repeat(gmlp_b_s[l].T, HEAD_DIM, axis=1)
        return cw, vecs, sw, pool_bd, wcat, bst, wcat_t

    xs = x[0]
    saved = []
    for l in range(depth):
        sm = small_inputs(l)
        x1, a1, b1 = _ffn_fwd(xs, ffn1_norm[l], wbuf, lay, l, 0, f"ffn1_fwd_{l}")
        h2, p = _mix_in_fwd(x1, mix_norm[l], wbuf, lay, l, f"mix_in_fwd_{l}")
        x2, mix = _mix_fwd(p, x1, wbuf, lay, l, sm[:6], f"mix_fwd_{l}")
        x3, a2, b2 = _ffn_fwd(x2, ffn2_norm[l], wbuf, lay, l, 1, f"ffn2_fwd_{l}")
        saved.append((xs, a1, b1, x1, h2, p, mix, x2, a2, b2, sm))
        xs = x3

    dx, loss_part, dgf = _loss_head(xs, loss_target[0], final_norm, "loss_head")
    loss = lax.psum(loss_part[0, 0], ("x", "y", "c"))

    gbuf = lax.empty((N_DEV, lay.conv_base, d), BF16)
    small = [None] * depth
    for l in reversed(range(depth)):
        x0, a1, b1, x1, h2, p, mix, x2, a2, b2, sm = saved[l]
        dx2, da, db, u, h, dgn2 = _ffn_bwd(x2, dx, a2, b2, ffn2_norm[l], wbuf, lay, l, 1, f"ffn2_bwd_{l}")
        gbuf = _wgrad(da, h, gbuf, fs, lay.ffn(l, 3), 1.0, f"ffn2_dw1_{l}")
        gbuf = _wgrad(db, h, gbuf, fs, lay.ffn(l, 4), 1.0, f"ffn2_dw3_{l}")
        gbuf = _wgrad(u, dx, gbuf, fs, lay.ffn(l, 5), FFN_RESIDUAL, f"ffn2_dw2_{l}")
        dp, dcw, dvec, dsw, dpbd, dws, dbs = _mix_bwd(p, dx2, wbuf, lay, l, sm, f"mix_bwd_{l}")
        gbuf = _wgrad(mix, dx2, gbuf, wo, lay.wout(l), 1.0, f"dw_out_{l}")
        dx1, dgm = _mix_in_bwd(x1, dx2, dp, mix_norm[l], wbuf, lay, l, f"mix_in_bwd_{l}")
        gbuf = _wgrad(dp, h2, gbuf, g, lay.win(l), 1.0, f"dw_in_{l}")
        dx, da, db, u, h, dgn1 = _ffn_bwd(x0, dx1, a1, b1, ffn1_norm[l], wbuf, lay, l, 0, f"ffn1_bwd_{l}")
        gbuf = _wgrad(da, h, gbuf, fs, lay.ffn(l, 0), 1.0, f"ffn1_dw1_{l}")
        gbuf = _wgrad(db, h, gbuf, fs, lay.ffn(l, 1), 1.0, f"ffn1_dw3_{l}")
        gbuf = _wgrad(u, dx1, gbuf, fs, lay.ffn(l, 2), FFN_RESIDUAL, f"ffn1_dw2_{l}")
        pg = g // len(POOL_WINDOWS)
        dpool = jnp.stack([dpbd[k * pg:(k + 1) * pg, k * pg:(k + 1) * pg] for k in range(len(POOL_WINDOWS))])
        small[l] = [dgn1[0], dgm[0], dgn2[0], dvec, dcw, dsw, dpool, dws, dbs[:, :nh].T]
    grad_x = dx[None]

    gsum = _sum_slots(_exchange(gbuf, "exchange_grads"), "sum_grads")
    flat_small = [a for l in range(depth) for a in small[l]] + [dgf[0]]
    sbuf, offs = _f32_rows(flat_small, d, None)
    pad = -sbuf.shape[0] % 16
    sbuf = jnp.pad(sbuf, ((0, pad), (0, 0)))
    ssum = _sum_slots(_all_gather(sbuf, "gather_small_grads"), "sum_small_grads")

    grads = {}

    def rows_of(blk, rows):
        return gsum[blk * rows:(blk + 1) * rows]

    grads['ffn1_w1'] = jnp.stack([rows_of(lay.ffn(l, 0), fs).T for l in range(depth)])
    grads['ffn1_w3'] = jnp.stack([rows_of(lay.ffn(l, 1), fs).T for l in range(depth)])
    grads['ffn1_w2'] = jnp.stack([rows_of(lay.ffn(l, 2), fs) for l in range(depth)])
    grads['ffn2_w1'] = jnp.stack([rows_of(lay.ffn(l, 3), fs).T for l in range(depth)])
    grads['ffn2_w3'] = jnp.stack([rows_of(lay.ffn(l, 4), fs).T for l in range(depth)])
    grads['ffn2_w2'] = jnp.stack([rows_of(lay.ffn(l, 5), fs) for l in range(depth)])
    grads['w_in'] = jnp.stack([rows_of(lay.win(l), g).T for l in range(depth)])
    grads['w_out'] = jnp.stack([rows_of(lay.wout(l), wo) for l in range(depth)])

    per_layer = len(small[0])
    shapes = [(d,), (d,), (d,), (8, g), (32, g), (8, g), (len(POOL_WINDOWS), g // len(POOL_WINDOWS),
              g // len(POOL_WINDOWS)), (nh * CHUNK, CHUNK), (nh, CHUNK)]
    got = [[_take(ssum, offs[l * per_layer + k], shapes[k]) for k in range(per_layer)] for l in range(depth)]
    col = lambda k: jnp.stack([got[l][k] for l in range(depth)])
    grads['ffn1_norm'], grads['mix_norm'], grads['ffn2_norm'] = col(0), col(1), col(2)
    dvec_all = col(3)
    for k, n in enumerate(('conf_conv_b', 'conf_ln_g', 'conf_ln_b', 'pool_scale', 'gmlp_ln_g', 'gmlp_ln_b')):
        grads[n] = dvec_all[:, k]
    grads['conf_conv_w'] = lax.dynamic_slice_in_dim(col(4)[:, :CONF_KERNEL], me * cs, cs, axis=2)
    grads['sconv_w'] = lax.dynamic_slice_in_dim(col(5)[:, :SHORT_KERNEL], me * cs, cs, axis=2)
    grads['pool_w'] = col(6)
    grads['gmlp_w_s'] = col(7).reshape(depth, nh, CHUNK, CHUNK)
    grads['gmlp_b_s'] = col(8)
    grads['final_norm'] = _take(ssum, offs[-1], (d,))

    deltas, new_m, new_v = {}, {}, {}
    for n in WEIGHTS:
        deltas[n], new_m[n], new_v[n] = _adamw(w[n], grads[n], mom1[n], mom2[n], f"adamw_{n}")

    return (loss, grad_x, *[grads[n] for n in WEIGHTS], *[deltas[n] for n in WEIGHTS],
            *[new_m[n] for n in WEIGHTS], *[new_v[n] for n in WEIGHTS])
```

```python
import functools

import jax
import jax.numpy as jnp
from jax import lax
from jax.experimental import pallas as pl
from jax.experimental.pallas import tpu as pltpu

F32 = jnp.float32
BF16 = jnp.bfloat16
MESH = pl.DeviceIdType.MESH

N_DEV = 8
EPS = 1e-6
FFN_RESIDUAL = 0.5
CONF_KERNEL = 31
SHORT_KERNEL = 3
POOL_WINDOWS = (2, 4, 8, 16)
CHUNK = 128
HEAD_DIM = 64
N_IN_PIECES = 8
HALO = 32
CONV_ROWS = 16
ADAM_LR = 0.001
ADAM_B1 = 0.9
ADAM_B2 = 0.999
ADAM_EPS = 1e-08
ADAM_WD = 0.01
ADAM_STEP = 10
VMEM_LIMIT_V7X = 56 * 1024 * 1024

WEIGHTS = ['ffn1_norm', 'ffn1_w1', 'ffn1_w3', 'ffn1_w2', 'mix_norm', 'w_in', 'conf_conv_w', 'conf_conv_b',
           'conf_ln_g', 'conf_ln_b', 'sconv_w', 'pool_w', 'pool_scale', 'gmlp_ln_g', 'gmlp_ln_b', 'gmlp_w_s',
           'gmlp_b_s', 'w_out', 'ffn2_norm', 'ffn2_w1', 'ffn2_w3', 'ffn2_w2', 'final_norm']
SHARDED_ROWS = ('ffn1_w2', 'ffn2_w2', 'w_out')
SHARDED_COLS = ('ffn1_w1', 'ffn1_w3', 'ffn2_w1', 'ffn2_w3', 'w_in')
SHARDED_CHAN = ('conf_conv_w', 'sconv_w')


def _params(sem=None):
    return pltpu.CompilerParams(dimension_semantics=sem, vmem_limit_bytes=VMEM_LIMIT_V7X)


def _nn(a, b):
    return jnp.dot(a, b, preferred_element_type=F32)


def _nt(a, b):
    return lax.dot_general(a, b, (((1,), (1,)), ((), ())), preferred_element_type=F32)


def _tn(a, b):
    return lax.dot_general(a, b, (((0,), (0,)), ((), ())), preferred_element_type=F32)


def _sigmoid(x):
    return 1.0 / (1.0 + jnp.exp(-x))


def _row_tile(n, pref, mult=8):
    t = min(n, pref)
    while n % t or t % mult:
        t -= 1
    return t


def _chunks(n, size):
    out, s = [], 0
    while s < n:
        out.append((s, min(size, n - s)))
        s += size
    return out


def _const(shape):
    return pl.BlockSpec(shape, lambda i: (0,) * len(shape))


def _resident(shape, index):
    return pl.BlockSpec(shape, lambda i: index, pipeline_mode=pl.Buffered(1))


def _wspec(rows, d, blk):
    return _resident((N_DEV, rows, d), (0, blk, 0))


def _layernorm_stats(x):
    mu = jnp.mean(x, axis=-1, keepdims=True)
    xc = x - mu
    r = lax.rsqrt(jnp.mean(xc * xc, axis=-1, keepdims=True) + EPS)
    return xc * r, r


def _layernorm_bwd(dy, g, xhat, r):
    dxh = dy * g
    return r * (dxh - jnp.mean(dxh, axis=-1, keepdims=True) - xhat * jnp.mean(dxh * xhat, axis=-1, keepdims=True))


def _rmsnorm_stats(x):
    r = lax.rsqrt(jnp.mean(x * x, axis=-1, keepdims=True) + EPS)
    return x * r, r


def _rmsnorm_bwd(dh, g, xhat, r):
    dxh = dh * g
    return r * (dxh - xhat * jnp.mean(dxh * xhat, axis=-1, keepdims=True))


class _MixRows:
    def __init__(self, g, wo):
        assert g % wo == 0 and wo % 16 == 0
        self.g, self.wo = g, wo

    def win(self, l):
        return 0

    def wout(self, l):
        return self.g // self.wo


class _Gather:
    def __init__(self, shard):
        self.operands = [shard]
        self.out_shape = [jax.ShapeDtypeStruct((N_DEV,) + shard.shape, shard.dtype)]
        self.scratch = [pltpu.SemaphoreType.DMA((7,)), pltpu.SemaphoreType.DMA((7,)), pltpu.SemaphoreType.DMA]

    def _copies(self, ins, outs, scr, arrivals):
        (x_ref,), (out_ref,), (send_sems, recv_sems, local_sem) = ins, outs, scr
        x, y, c = lax.axis_index("x"), lax.axis_index("y"), lax.axis_index("c")
        me, sibling = (x, y, c), (x, y, 1 - c)
        chips = [(1 - x, y), (x, 1 - y), (1 - x, 1 - y)]

        def slot(px, py, pc):
            return out_ref.at[4 * px + 2 * py + pc]

        def copy(k, block, to, src=None):
            return pltpu.make_async_remote_copy(
                src_ref=slot(*block) if src is None else src, dst_ref=slot(*block),
                send_sem=send_sems.at[k], recv_sem=recv_sems.at[k], device_id=to, device_id_type=MESH)

        mine = pltpu.make_async_copy(x_ref, slot(*me), local_sem)
        first = [copy(0, me, sibling, src=x_ref)]
        first += [copy(1 + j, me, (*chip, c), src=x_ref) for j, chip in enumerate(chips)]
        passed = [copy(4 + j, (*chip, c), sibling) for j, chip in enumerate(chips)]
        if arrivals == "ici":
            landing = [copy(1 + j, (*chip, c), me) for j, chip in enumerate(chips)]
        else:
            landing = [copy(0, sibling, me)] + [copy(4 + j, (*chip, 1 - c), me) for j, chip in enumerate(chips)]
        return mine, first, passed, landing

    def begin(self, i, n, ins, outs, scr):
        mine, first, passed, over_ici = self._copies(ins, outs, scr, "ici")

        @pl.when(i == 0)
        def _():
            mine.start()
            for cp in first:
                cp.start()

        @pl.when(i == n // 2)
        def _():
            for landed, onward in zip(over_ici, passed):
                landed.wait_recv()
                onward.start()

    def end(self, i, n, ins, outs, scr):
        mine, first, passed, over_d2d = self._copies(ins, outs, scr, "d2d")

        @pl.when(i == n - 1)
        def _():
            for cp in over_d2d:
                cp.wait_recv()
            for cp in first + passed:
                cp.wait_send()
            mine.wait()


class _Exchange:
    def __init__(self, blocks):
        self.operands = [blocks]
        self.out_shape = [jax.ShapeDtypeStruct(blocks.shape, blocks.dtype)]
        self.scratch = [pltpu.SemaphoreType.DMA((7,)), pltpu.SemaphoreType.DMA((7,)), pltpu.SemaphoreType.DMA]

    def _copies(self, ins, outs, scr, with_arrivals):
        (s_ref,), (r_ref,), (send_sems, recv_sems, local_sem) = ins, outs, scr
        x, y, c = lax.axis_index("x"), lax.axis_index("y"), lax.axis_index("c")
        me = 4 * x + 2 * y + c
        mine = pltpu.make_async_copy(s_ref.at[me], r_ref.at[me], local_sem)
        sends, recvs = [], []
        for k in range(1, N_DEV):
            px = (1 - x) if (k >> 2) & 1 else x
            py = (1 - y) if (k >> 1) & 1 else y
            pc = (1 - c) if k & 1 else c
            peer = 4 * px + 2 * py + pc
            for dst, group in ((r_ref.at[me], sends), (r_ref.at[peer], recvs)):
                if group is sends or with_arrivals:
                    group.append(pltpu.make_async_remote_copy(
                        src_ref=s_ref.at[peer], dst_ref=dst, send_sem=send_sems.at[k - 1],
                        recv_sem=recv_sems.at[k - 1], device_id=(px, py, pc), device_id_type=MESH))
        return mine, sends, recvs

    def begin(self, i, n, ins, outs, scr):
        mine, sends, _ = self._copies(ins, outs, scr, False)

        @pl.when(i == 0)
        def _():
            mine.start()
            for cp in sends:
                cp.start()

    def end(self, i, n, ins, outs, scr):
        mine, sends, recvs = self._copies(ins, outs, scr, True)

        @pl.when(i == n - 1)
        def _():
            for cp in recvs:
                cp.wait_recv()
            for cp in sends:
                cp.wait_send()
            mine.wait()


_ANY = pl.BlockSpec(memory_space=pl.ANY)


def _call(body, name, steps, operands, in_specs, out_shape, out_specs, scratch=(), riders=(), aliases=None):
    n_in, n_out, n_scr = len(operands), len(out_shape), len(scratch)

    def full(*refs):
        pos = [0]

        def take(k):
            pos[0] += k
            return refs[pos[0] - k:pos[0]]

        h_in, r_in = take(n_in), [take(len(r.operands)) for r in riders]
        h_out, r_out = take(n_out), [take(len(r.out_shape)) for r in riders]
        h_scr, r_scr = take(n_scr), [take(len(r.scratch)) for r in riders]
        i = pl.program_id(0)
        for r, a, b, c in zip(riders, r_in, r_out, r_scr):
            r.begin(i, steps, a, b, c)
        body(*h_in, *h_out, *h_scr)
        for r, a, b, c in zip(riders, r_in, r_out, r_scr):
            r.end(i, steps, a, b, c)

    outs = pl.pallas_call(
        full, name=name, grid=(steps,),
        out_shape=tuple(out_shape) + tuple(s for r in riders for s in r.out_shape),
        in_specs=list(in_specs) + [_ANY for r in riders for _ in r.operands],
        out_specs=tuple(out_specs) + tuple(_ANY for r in riders for _ in r.out_shape),
        scratch_shapes=list(scratch) + [s for r in riders for s in r.scratch],
        input_output_aliases=aliases or {},
        compiler_params=_params(("arbitrary",)),
    )(*operands, *(o for r in riders for o in r.operands))
    host, rest = outs[:n_out], list(outs[n_out:])
    rides = []
    for r in riders:
        rides.append(rest[:len(r.out_shape)])
        rest = rest[len(r.out_shape):]
    return host, rides


def _alone(rider, name):
    return _call(lambda: None, name, 1, [], [], [], [], riders=[rider])[1][0][0]


def _sum_slots(slots, name):
    _, rows, cols = slots.shape
    tr = _row_tile(rows, 256, 16)

    def body(s_ref, o_ref):
        acc = s_ref[0].astype(F32)
        for j in range(1, N_DEV):
            acc = acc + s_ref[j].astype(F32)
        o_ref[...] = acc

    return pl.pallas_call(
        body, name=name, grid=(rows // tr,),
        out_shape=jax.ShapeDtypeStruct((rows, cols), F32),
        in_specs=[pl.BlockSpec((N_DEV, tr, cols), lambda i: (0, i, 0))],
        out_specs=pl.BlockSpec((tr, cols), lambda i: (i, 0)),
        compiler_params=_params(("parallel",)),
    )(slots)


def _ffn_fwd(x, gain, wset, fs, name, riders=()):
    t, d = x.shape
    f = N_DEV * fs
    tm = _row_tile(t, 256, 128)
    cols = _chunks(f, 1024)

    def body(x_ref, g_ref, w1_ref, w3_ref, w2_ref, y_ref, a_ref, b_ref, u_ref):
        xv = x_ref[...]
        xhat, _ = _rmsnorm_stats(xv)
        h = (xhat * g_ref[...]).astype(BF16)
        w1 = w1_ref[...].reshape(f, d)
        w3 = w3_ref[...].reshape(f, d)
        for s, n in cols:
            a = _nt(h, w1[s:s + n, :])
            b = _nt(h, w3[s:s + n, :])
            a_ref[:, s:s + n] = a.astype(BF16)
            b_ref[:, s:s + n] = b.astype(BF16)
            u_ref[:, s:s + n] = (a * _sigmoid(a) * b).astype(BF16)
        y_ref[...] = xv + FFN_RESIDUAL * _nn(u_ref[...], w2_ref[...].reshape(f, d))

    return _call(
        body, name, t // tm, [x, gain.reshape(1, d), wset, wset, wset],
        [pl.BlockSpec((tm, d), lambda i: (i, 0)), _const((1, d)),
         _wspec(fs, d, 0), _wspec(fs, d, 1), _wspec(fs, d, 2)],
        [jax.ShapeDtypeStruct((t, d), F32), jax.ShapeDtypeStruct((t, f), BF16), jax.ShapeDtypeStruct((t, f), BF16)],
        [pl.BlockSpec((tm, d), lambda i: (i, 0)), pl.BlockSpec((tm, f), lambda i: (i, 0)),
         pl.BlockSpec((tm, f), lambda i: (i, 0))],
        scratch=[pltpu.VMEM((tm, f), BF16)], riders=riders)


def _ffn_bwd(x, dy, a, b, gain, wset, fs, name, riders=()):
    t, d = x.shape
    f = N_DEV * fs
    tm = _row_tile(t, 256, 128)
    cols = _chunks(f, 1024)

    def body(x_ref, dy_ref, a_ref, b_ref, g_ref, w1_ref, w3_ref, w2_ref,
             dx_ref, da_ref, db_ref, u_ref, h_ref, dg_ref):
        @pl.when(pl.program_id(0) == 0)
        def _():
            dg_ref[...] = jnp.zeros_like(dg_ref)

        xv, dyv, g = x_ref[...], dy_ref[...], g_ref[...]
        xhat, r = _rmsnorm_stats(xv)
        h_ref[...] = (xhat * g).astype(BF16)
        dyb = (FFN_RESIDUAL * dyv).astype(BF16)
        w1 = w1_ref[...].reshape(f, d)
        w3 = w3_ref[...].reshape(f, d)
        w2 = w2_ref[...].reshape(f, d)
        dh = jnp.zeros((tm, d), F32)
        for s, n in cols:
            du = _nt(dyb, w2[s:s + n, :])
            av = a_ref[:, s:s + n].astype(F32)
            bv = b_ref[:, s:s + n].astype(F32)
            sig = _sigmoid(av)
            sa = av * sig
            u_ref[:, s:s + n] = (sa * bv).astype(BF16)
            da = (du * bv * (sig * (1.0 + av * (1.0 - sig)))).astype(BF16)
            db = (du * sa).astype(BF16)
            da_ref[:, s:s + n] = da
            db_ref[:, s:s + n] = db
            dh = dh + _nn(da, w1[s:s + n, :]) + _nn(db, w3[s:s + n, :])
        dx_ref[...] = dyv + _rmsnorm_bwd(dh, g, xhat, r)
        dg_ref[0:1, :] += jnp.sum(dh * xhat, axis=0, keepdims=True)

    big = jax.ShapeDtypeStruct((t, f), BF16)
    row = lambda w: pl.BlockSpec((tm, w), lambda i: (i, 0))
    return _call(
        body, name, t // tm, [x, dy, a, b, gain.reshape(1, d), wset, wset, wset],
        [row(d), row(d), row(f), row(f), _const((1, d)), _wspec(fs, d, 0), _wspec(fs, d, 1), _wspec(fs, d, 2)],
        [jax.ShapeDtypeStruct((t, d), F32), big, big, big, jax.ShapeDtypeStruct((t, d), BF16),
         jax.ShapeDtypeStruct((8, d), F32)],
        [row(d), row(f), row(f), row(f), row(d), _const((8, d))], riders=riders)


def _wgrad(lhs, rhs, gbuf, rows, blk, scale, name):
    t, m = lhs.shape
    d = rhs.shape[1]
    bt = _row_tile(t, 512, 128)
    nk = t // bt
    assert m == N_DEV * rows

    def body(a_ref, b_ref, g_in, o_ref, acc_ref):
        del g_in
        k = pl.program_id(0)

        @pl.when(k == 0)
        def _():
            acc_ref[...] = jnp.zeros_like(acc_ref)

        acc_ref[...] += _tn(a_ref[...], b_ref[...].astype(BF16))

        @pl.when(k == nk - 1)
        def _():
            o_ref[...] = (scale * acc_ref[...]).astype(BF16).reshape(N_DEV, rows, d)

    return pl.pallas_call(
        body, name=name, grid=(nk,),
        out_shape=jax.ShapeDtypeStruct(gbuf.shape, gbuf.dtype),
        in_specs=[pl.BlockSpec((bt, m), lambda k: (k, 0)), pl.BlockSpec((bt, d), lambda k: (k, 0)),
                  pl.BlockSpec(memory_space=pl.ANY)],
        out_specs=pl.BlockSpec((N_DEV, rows, d), lambda k: (0, blk, 0)),
        scratch_shapes=[pltpu.VMEM((m, d), F32)],
        input_output_aliases={2: 0},
        compiler_params=_params(("arbitrary",)),
    )(lhs, rhs, gbuf)


def _mix_in_fwd(x, gain, wbuf, lay, l, name):
    t, d = x.shape
    g = lay.g
    di = N_DEV * g
    tm = _row_tile(t, 512, 128)

    def body(x_ref, g_ref, w_ref, h_ref, p_ref):
        xhat, _ = _rmsnorm_stats(x_ref[...])
        h = (xhat * g_ref[...]).astype(BF16)
        h_ref[...] = h
        p_ref[...] = _nt(h, w_ref[...].reshape(di, d)).astype(BF16)

    return pl.pallas_call(
        body, name=name, grid=(t // tm,),
        out_shape=(jax.ShapeDtypeStruct((t, d), BF16), jax.ShapeDtypeStruct((t, di), BF16)),
        in_specs=[pl.BlockSpec((tm, d), lambda i: (i, 0)), _const((1, d)), _wspec(g, d, lay.win(l))],
        out_specs=(pl.BlockSpec((tm, d), lambda i: (i, 0)), pl.BlockSpec((tm, di), lambda i: (i, 0))),
        compiler_params=_params(("parallel",)),
    )(x, gain.reshape(1, d), wbuf)


def _mix_in_bwd(x, dx_out, dp, gain, wbuf, lay, l, name):
    t, d = x.shape
    g = lay.g
    di = N_DEV * g
    tm = _row_tile(t, 512, 128)

    def body(x_ref, dxo_ref, dp_ref, g_ref, w_ref, dx_ref, dg_ref):
        @pl.when(pl.program_id(0) == 0)
        def _():
            dg_ref[...] = jnp.zeros_like(dg_ref)

        xhat, r = _rmsnorm_stats(x_ref[...])
        dh = _nn(dp_ref[...], w_ref[...].reshape(di, d))
        dx_ref[...] = dxo_ref[...] + _rmsnorm_bwd(dh, g_ref[...], xhat, r)
        dg_ref[0:1, :] += jnp.sum(dh * xhat, axis=0, keepdims=True)

    row = lambda w: pl.BlockSpec((tm, w), lambda i: (i, 0))
    return pl.pallas_call(
        body, name=name, grid=(t // tm,),
        out_shape=(jax.ShapeDtypeStruct((t, d), F32), jax.ShapeDtypeStruct((8, d), F32)),
        in_specs=[row(d), row(d), row(di), _const((1, d)), _wspec(g, d, lay.win(l))],
        out_specs=(row(d), _const((8, d))),
        compiler_params=_params(("arbitrary",)),
    )(x, dx_out, dp, gain.reshape(1, d), wbuf)


def _head_masks(rows, g):
    lane = lax.broadcasted_iota(jnp.int32, (rows, g), 1)
    return [(lane >= h * HEAD_DIM) & (lane < (h + 1) * HEAD_DIM) for h in range(g // HEAD_DIM)]


def _pool_windows(rows, g):
    lane = lax.broadcasted_iota(jnp.int32, (rows, g), 1)
    pg = g // len(POOL_WINDOWS)
    w = jnp.full((rows, g), float(POOL_WINDOWS[-1]), F32)
    for k in range(len(POOL_WINDOWS) - 2, -1, -1):
        w = jnp.where(lane < (k + 1) * pg, float(POOL_WINDOWS[k]), w)
    return w


def _window_sum(ref, tmp, first, rows, wl, direction):
    total = ref.shape[0]
    src, out, shift = ref, None, 1
    for k, w in enumerate(POOL_WINDOWS):
        assert w == 2 * shift and 8 * (k + 1) <= HALO
        if k + 1 < len(POOL_WINDOWS):
            n = total - 8 * (k + 1)
            lo = 8 * (k + 1) if direction < 0 else 0
            dst = tmp.at[k % 2]
            dst[pl.ds(lo, n), :] = src[pl.ds(lo, n), :] + src[pl.ds(lo + direction * shift, n), :]
            src = dst
            level = dst[pl.ds(first, rows), :]
        else:
            level = src[pl.ds(first, rows), :] + src[pl.ds(first + direction * shift, rows), :]
        out = level if out is None else jnp.where(wl >= float(w), level, out)
        shift = w
    return out


def _shifted_copies(src_ref, dst_ref):
    rows = dst_ref.shape[1]
    for b in range(1, 8):
        dst_ref[b - 1] = src_ref[pl.ds(b, rows), :]


def _window(src_ref, shifted_ref, off, rows):
    b = off % 8
    if b == 0:
        return src_ref[pl.ds(off, rows), :]
    return shifted_ref[b - 1, pl.ds(off - b, rows), :]


def _pool_divisor(first_row, rows, g):
    pos1 = (lax.broadcasted_iota(jnp.int32, (rows, g), 0) + first_row + 1).astype(F32)
    return jnp.minimum(pos1, _pool_windows(rows, g))


def _spatial_mix(vq, wcat_ref, masks):
    vstack = jnp.concatenate([jnp.where(m, vq, jnp.zeros_like(vq)) for m in masks], axis=0)
    return _nn(wcat_ref[...], vstack)


def _mix_fwd(p, x, wbuf, lay, l, small, name):
    t, d = x.shape
    g = lay.g
    di = N_DEV * g
    tm = _row_tile(t, 256, CHUNK)
    nh = g // HEAD_DIM
    per = tm // HALO
    conv_w, vecs, sconv_w, pool_bd, wcat, bst = small

    def body(pp_ref, pm_ref, x_ref, wo_ref, cw_ref, vec_ref, sw_ref, pbd_ref, wcat_ref, bst_ref,
             y_ref, mix_ref, ea_ref, eb_ref, ec_ref, sh_ref, tmp_ref):
        i = pl.program_id(0)
        seen = (i > 0).astype(F32)

        def prev(j):
            return pp_ref[:, j * g:(j + 1) * g].astype(F32) * seen

        def main(j):
            return pm_ref[:, j * g:(j + 1) * g].astype(F32)

        vec = vec_ref[...]
        ea_ref[0:HALO, :] = prev(0) * _sigmoid(prev(1))
        ea_ref[HALO:, :] = main(0) * _sigmoid(main(1))
        _shifted_copies(ea_ref, sh_ref)
        cw = cw_ref[...]
        c = jnp.zeros((tm, g), F32) + vec[0:1, :]
        for k in range(CONF_KERNEL):
            c = c + cw[k:k + 1, :] * _window(ea_ref, sh_ref, HALO - (CONF_KERNEL - 1) + k, tm)
        chat, _ = _layernorm_stats(c)
        ln = chat * vec[1:2, :] + vec[2:3, :]
        mix_ref[:, 0:g] = (ln * _sigmoid(ln)).astype(BF16)
        eb_ref[0:HALO, :] = prev(3) * prev(4)
        eb_ref[HALO:, :] = main(3) * main(4)
        sw = sw_ref[...]
        cz = jnp.zeros((tm, g), F32)
        for k in range(SHORT_KERNEL):
            cz = cz + sw[k:k + 1, :] * eb_ref[pl.ds(HALO - (SHORT_KERNEL - 1) + k, tm), :]
        mix_ref[:, g:2 * g] = (main(2) * cz).astype(BF16)
        ec_ref[0:HALO, :] = prev(5)
        xp = main(5)
        ec_ref[HALO:, :] = xp
        wl = _pool_windows(tm, g)
        mean = _window_sum(ec_ref, tmp_ref, HALO, tm, wl, -1) / _pool_divisor(i * tm, tm, g)
        yc = _nn((mean - xp).astype(BF16), pbd_ref[...])
        mix_ref[:, 2 * g:3 * g] = (yc * vec[3:4, :]).astype(BF16)
        vhat, _ = _layernorm_stats(main(7))
        v = (vhat * vec[4:5, :] + vec[5:6, :]).astype(BF16)
        gu = main(6)
        masks = _head_masks(CHUNK, g)
        for q in range(tm // CHUNK):
            rs = slice(q * CHUNK, (q + 1) * CHUNK)
            mixed = _spatial_mix(v[rs, :], wcat_ref, masks) + bst_ref[...]
            mix_ref[rs, 3 * g:4 * g] = (gu[rs, :] * mixed).astype(BF16)
        y_ref[...] = x_ref[...] + _nn(mix_ref[...], wo_ref[...].reshape(d, d))

    return pl.pallas_call(
        body, name=name, grid=(t // tm,),
        out_shape=(jax.ShapeDtypeStruct((t, d), F32), jax.ShapeDtypeStruct((t, 4 * g), BF16)),
        in_specs=[pl.BlockSpec((HALO, di), lambda i: (jnp.maximum(i * per - 1, 0), 0)),
                  pl.BlockSpec((tm, di), lambda i: (i, 0)), pl.BlockSpec((tm, d), lambda i: (i, 0)),
                  _wspec(lay.wo, d, lay.wout(l)),
                  _const(conv_w.shape), _const(vecs.shape), _const(sconv_w.shape), _const(pool_bd.shape),
                  _const(wcat.shape), _const(bst.shape)],
        out_specs=(pl.BlockSpec((tm, d), lambda i: (i, 0)), pl.BlockSpec((tm, 4 * g), lambda i: (i, 0))),
        scratch_shapes=[pltpu.VMEM((HALO + tm, g), F32)] * 3
        + [pltpu.VMEM((7, HALO + tm - 8, g), F32), pltpu.VMEM((2, HALO + tm, g), F32)],
        compiler_params=_params(("parallel",)),
    )(p, p, x, wbuf, conv_w, vecs, sconv_w, pool_bd, wcat, bst)


def _mix_bwd(p, dy, wbuf, lay, l, small, name, riders=()):
    t, d = dy.shape
    g = lay.g
    di = N_DEV * g
    tm = _row_tile(t, 256, CHUNK)
    nt = t // tm
    nh = g // HEAD_DIM
    per = tm // HALO
    ext = tm + HALO
    conv_w, vecs, sconv_w, pool_bd, wcat, bst, wcat_t = small

    def body(pp_ref, pm_ref, pn_ref, dy_ref, dyn_ref, wo_ref, cw_ref, vec_ref, sw_ref, pbd_ref, wcat_ref, bst_ref,
             wcatt_ref, dp_ref, dcw_ref, dvec_ref, dsw_ref, dpbd_ref, dws_ref, dbs_ref,
             ea_ref, eb_ref, ec_ref, fa_ref, fb_ref, fc_ref, sh_ref, sf_ref, tmp_ref):
        i = pl.program_id(0)
        seen = (i > 0).astype(F32)
        more = (i < nt - 1).astype(F32)

        @pl.when(i == 0)
        def _():
            for ref in (dcw_ref, dvec_ref, dsw_ref, dpbd_ref, dws_ref, dbs_ref):
                ref[...] = jnp.zeros_like(ref)

        def prev(j):
            return pp_ref[:, j * g:(j + 1) * g].astype(F32) * seen

        def main(j):
            return pm_ref[:, j * g:(j + 1) * g].astype(F32)

        def nxt(j):
            return pn_ref[:, j * g:(j + 1) * g].astype(F32)

        def colsum(v):
            return jnp.sum(v, axis=0, keepdims=True)

        vec = vec_ref[...]
        wo = wo_ref[...].reshape(d, d)
        dmix = jnp.concatenate([_nt(dy_ref[...].astype(BF16), wo),
                                _nt((dyn_ref[...] * more).astype(BF16), wo)], axis=0)

        val, gate = main(0), main(1)
        sg = _sigmoid(gate)
        ea_ref[0:HALO, :] = prev(0) * _sigmoid(prev(1))
        ea_ref[HALO:ext, :] = val * sg
        ea_ref[ext:, :] = nxt(0) * _sigmoid(nxt(1))
        _shifted_copies(ea_ref, sh_ref)
        cw = cw_ref[...]
        c = jnp.zeros((ext, g), F32) + vec[0:1, :]
        for k in range(CONF_KERNEL):
            c = c + cw[k:k + 1, :] * _window(ea_ref, sh_ref, HALO - (CONF_KERNEL - 1) + k, ext)
        chat, r = _layernorm_stats(c)
        ln = chat * vec[1:2, :] + vec[2:3, :]
        sl = _sigmoid(ln)
        dln = dmix[:, 0:g] * (sl * (1.0 + ln * (1.0 - sl)))
        dvec_ref[1:2, :] += colsum(dln[0:tm] * chat[0:tm])
        dvec_ref[2:3, :] += colsum(dln[0:tm])
        dc = _layernorm_bwd(dln, vec[1:2, :], chat, r)
        dvec_ref[0:1, :] += colsum(dc[0:tm])
        fa_ref[...] = dc
        _shifted_copies(fa_ref, sf_ref)
        dcm = dc[0:tm]
        dy0 = jnp.zeros((tm, g), F32)
        for k in range(CONF_KERNEL):
            dy0 = dy0 + cw[k:k + 1, :] * _window(fa_ref, sf_ref, CONF_KERNEL - 1 - k, tm)
            dcw_ref[k:k + 1, :] += colsum(dcm * _window(ea_ref, sh_ref, HALO - (CONF_KERNEL - 1) + k, tm))
        dp_ref[:, 0:g] = (dy0 * sg).astype(BF16)
        dp_ref[:, g:2 * g] = (dy0 * val * sg * (1.0 - sg)).astype(BF16)

        sb, sc, sx = main(2), main(3), main(4)
        eb_ref[0:HALO, :] = prev(3) * prev(4)
        eb_ref[HALO:ext, :] = sc * sx
        sw = sw_ref[...]
        cz = jnp.zeros((tm, g), F32)
        for k in range(SHORT_KERNEL):
            cz = cz + sw[k:k + 1, :] * eb_ref[pl.ds(HALO - (SHORT_KERNEL - 1) + k, tm), :]
        dob = dmix[:, g:2 * g]
        dp_ref[:, 2 * g:3 * g] = (dob[0:tm] * cz).astype(BF16)
        fb_ref[0:tm, :] = dob[0:tm] * sb
        fb_ref[tm:, :] = dob[tm:] * nxt(2)
        dczm = fb_ref[0:tm, :]
        dz = jnp.zeros((tm, g), F32)
        for k in range(SHORT_KERNEL):
            dz = dz + sw[k:k + 1, :] * fb_ref[pl.ds(SHORT_KERNEL - 1 - k, tm), :]
            dsw_ref[k:k + 1, :] += colsum(dczm * eb_ref[pl.ds(HALO - (SHORT_KERNEL - 1) + k, tm), :])
        dp_ref[:, 3 * g:4 * g] = (dz * sx).astype(BF16)
        dp_ref[:, 4 * g:5 * g] = (dz * sc).astype(BF16)

        xp = main(5)
        ec_ref[0:HALO, :] = prev(5)
        ec_ref[HALO:, :] = xp
        wl = _pool_windows(tm, g)
        dpool = ((_window_sum(ec_ref, tmp_ref, HALO, tm, wl, -1) / _pool_divisor(i * tm, tm, g)) - xp).astype(BF16)
        pbd = pbd_ref[...]
        yc = _nn(dpool, pbd)
        doc = dmix[:, 2 * g:3 * g]
        dvec_ref[3:4, :] += colsum(doc[0:tm] * yc)
        dyc = (doc * vec[3:4, :]).astype(BF16)
        dpbd_ref[...] += _tn(dpool, dyc[0:tm])
        dd = _nt(dyc, pbd)
        fc_ref[...] = dd / _pool_divisor(i * tm, ext, g)
        dp_ref[:, 5 * g:6 * g] = (_window_sum(fc_ref, tmp_ref, 0, tm, wl, 1) - dd[0:tm]).astype(BF16)

        gv = main(7)
        vhat, rv = _layernorm_stats(gv)
        v = (vhat * vec[4:5, :] + vec[5:6, :]).astype(BF16)
        gu = main(6)
        dod = dmix[0:tm, 3 * g:4 * g]
        masks = _head_masks(CHUNK, g)
        tril = (lax.broadcasted_iota(jnp.int32, (CHUNK, CHUNK), 1)
                <= lax.broadcasted_iota(jnp.int32, (CHUNK, CHUNK), 0)).astype(F32)
        tril = jnp.concatenate([tril] * nh, axis=0)
        head_col = lax.broadcasted_iota(jnp.int32, (CHUNK, CHUNK), 1)
        dvs = []
        for q in range(tm // CHUNK):
            rs = slice(q * CHUNK, (q + 1) * CHUNK)
            vq = v[rs, :]
            mixed = _spatial_mix(vq, wcat_ref, masks) + bst_ref[...]
            dp_ref[rs, 6 * g:7 * g] = (dod[rs, :] * mixed).astype(BF16)
            dmixed = dod[rs, :] * gu[rs, :]
            dmb = dmixed.astype(BF16)
            for h, m in enumerate(masks):
                hs = jnp.sum(jnp.where(m, dmixed, 0.0), axis=-1, keepdims=True)
                dbs_ref[...] += jnp.where(head_col == h, hs, 0.0)
            dmstack = jnp.concatenate([jnp.where(m, dmb, jnp.zeros_like(dmb)) for m in masks], axis=0)
            dws_ref[...] += _nt(dmstack, vq) * tril
            back = _nn(wcatt_ref[...], dmb)
            dv = jnp.zeros((CHUNK, g), F32)
            for h, m in enumerate(masks):
                dv = jnp.where(m, back[h * CHUNK:(h + 1) * CHUNK, :], dv)
            dvs.append(dv)
        dv = jnp.concatenate(dvs, axis=0)
        dvec_ref[4:5, :] += colsum(dv * vhat)
        dvec_ref[5:6, :] += colsum(dv)
        dp_ref[:, 7 * g:8 * g] = _layernorm_bwd(dv, vec[4:5, :], vhat, rv).astype(BF16)

    halo_prev = pl.BlockSpec((HALO, di), lambda i: (jnp.maximum(i * per - 1, 0), 0))
    halo_next = lambda w: pl.BlockSpec((HALO, w), lambda i: (jnp.minimum((i + 1) * per, t // HALO - 1), 0))
    row = lambda w: pl.BlockSpec((tm, w), lambda i: (i, 0))
    outs = (jax.ShapeDtypeStruct((t, di), BF16), jax.ShapeDtypeStruct(conv_w.shape, F32),
            jax.ShapeDtypeStruct(vecs.shape, F32), jax.ShapeDtypeStruct(sconv_w.shape, F32),
            jax.ShapeDtypeStruct((g, g), F32), jax.ShapeDtypeStruct((nh * CHUNK, CHUNK), F32),
            jax.ShapeDtypeStruct((CHUNK, CHUNK), F32))
    return _call(
        body, name, nt, [p, p, p, dy, dy, wbuf, conv_w, vecs, sconv_w, pool_bd, wcat, bst, wcat_t],
        [halo_prev, row(di), halo_next(di), row(d), halo_next(d), _wspec(lay.wo, d, lay.wout(l)),
         _const(conv_w.shape), _const(vecs.shape), _const(sconv_w.shape), _const(pool_bd.shape),
         _const(wcat.shape), _const(bst.shape), _const(wcat_t.shape)],
        list(outs), [row(di)] + [_const(o.shape) for o in outs[1:]],
        scratch=[pltpu.VMEM((HALO + tm + HALO, g), F32), pltpu.VMEM((HALO + tm, g), F32),
                 pltpu.VMEM((HALO + tm, g), F32), pltpu.VMEM((ext, g), F32), pltpu.VMEM((ext, g), F32),
                 pltpu.VMEM((ext, g), F32), pltpu.VMEM((7, HALO + tm + HALO - 8, g), F32),
                 pltpu.VMEM((7, ext - 8, g), F32), pltpu.VMEM((2, ext, g), F32)],
        riders=riders)


def _loss_head(x, target, gain, name):
    t, d = x.shape
    tm = _row_tile(t, 512, 8)

    def body(x_ref, t_ref, g_ref, dx_ref, loss_ref, dg_ref):
        @pl.when(pl.program_id(0) == 0)
        def _():
            loss_ref[...] = jnp.zeros_like(loss_ref)
            dg_ref[...] = jnp.zeros_like(dg_ref)

        g = g_ref[...]
        xhat, r = _rmsnorm_stats(x_ref[...])
        err = xhat * g - t_ref[...]
        loss_ref[...] += 0.5 * jnp.sum(jnp.mean(err * err, axis=-1, keepdims=True))
        dyv = err * (1.0 / d)
        dx_ref[...] = _rmsnorm_bwd(dyv, g, xhat, r)
        dg_ref[0:1, :] += jnp.sum(dyv * xhat, axis=0, keepdims=True)

    row = pl.BlockSpec((tm, d), lambda i: (i, 0))
    return pl.pallas_call(
        body, name=name, grid=(t // tm,),
        out_shape=(jax.ShapeDtypeStruct((t, d), F32), jax.ShapeDtypeStruct((8, 128), F32),
                   jax.ShapeDtypeStruct((8, d), F32)),
        in_specs=[row, row, _const((1, d))],
        out_specs=(row, _const((8, 128)), _const((8, d))),
        compiler_params=_params(("arbitrary",)),
    )(x, target, gain.reshape(1, d))


def _adamw(w, grad, m, v, name):
    shape = w.shape
    cols = shape[-1]
    rows = w.size // cols
    tr = _row_tile(rows, 512, 8) if rows > 1024 else rows

    def body(w_ref, g_ref, m_ref, v_ref, d_ref, nm_ref, nv_ref):
        gv = g_ref[...]
        nm = ADAM_B1 * m_ref[...] + (1.0 - ADAM_B1) * gv
        nv = ADAM_B2 * v_ref[...] + (1.0 - ADAM_B2) * (gv * gv)
        m_hat = nm / (1.0 - ADAM_B1 ** ADAM_STEP)
        v_hat = nv / (1.0 - ADAM_B2 ** ADAM_STEP)
        d_ref[...] = -ADAM_LR * (m_hat / (jnp.sqrt(v_hat) + ADAM_EPS) + ADAM_WD * w_ref[...])
        nm_ref[...] = nm
        nv_ref[...] = nv

    spec = pl.BlockSpec((tr, cols), lambda i: (i, 0))
    flat = jax.ShapeDtypeStruct((rows, cols), F32)
    outs = pl.pallas_call(
        body, name=name, grid=(rows // tr,),
        out_shape=(flat, flat, flat), in_specs=[spec] * 4, out_specs=(spec,) * 3,
        compiler_params=_params(("parallel",)),
    )(*(a.reshape(rows, cols) for a in (w, grad, m, v)))
    return tuple(o.reshape(shape) for o in outs)


def _f32_rows(parts, d, rows):
    out, offs, at = [], [], 0
    for a in parts:
        n = -(-a.size // d)
        out.append(jnp.pad(a.reshape(-1), (0, n * d - a.size)).reshape(n, d))
        offs.append(at)
        at += n
    total = at if rows is None else rows
    assert at <= total
    if total > at:
        out.append(jnp.zeros((total - at, d), F32))
    return jnp.concatenate(out, axis=0), offs


def _take(buf, off, shape):
    n = 1
    for s in shape:
        n *= s
    d = buf.shape[1]
    return buf[off:off + -(-n // d)].reshape(-1)[:n].reshape(shape)


def _block_diag(blocks):
    n, k, _ = blocks.shape
    out = jnp.zeros((n * k, n * k), blocks.dtype)
    for i in range(n):
        out = lax.dynamic_update_slice(out, blocks[i], (i * k, i * k))
    return out


def kernel(x, ffn1_norm, ffn1_w1, ffn1_w3, ffn1_w2, mix_norm, w_in, conf_conv_w, conf_conv_b, conf_ln_g, conf_ln_b, sconv_w, pool_w, pool_scale, gmlp_ln_g, gmlp_ln_b, gmlp_w_s, gmlp_b_s, w_out, ffn2_norm, ffn2_w1, ffn2_w3, ffn2_w2, final_norm, loss_target, m_ffn1_norm, m_ffn1_w1, m_ffn1_w3, m_ffn1_w2, m_mix_norm, m_w_in, m_conf_conv_w, m_conf_conv_b, m_conf_ln_g, m_conf_ln_b, m_sconv_w, m_pool_w, m_pool_scale, m_gmlp_ln_g, m_gmlp_ln_b, m_gmlp_w_s, m_gmlp_b_s, m_w_out, m_ffn2_norm, m_ffn2_w1, m_ffn2_w3, m_ffn2_w2, m_final_norm, v_ffn1_norm, v_ffn1_w1, v_ffn1_w3, v_ffn1_w2, v_mix_norm, v_w_in, v_conf_conv_w, v_conf_conv_b, v_conf_ln_g, v_conf_ln_b, v_sconv_w, v_pool_w, v_pool_scale, v_gmlp_ln_g, v_gmlp_ln_b, v_gmlp_w_s, v_gmlp_b_s, v_w_out, v_ffn2_norm, v_ffn2_w1, v_ffn2_w3, v_ffn2_w2, v_final_norm):
    w = dict(zip(WEIGHTS, (ffn1_norm, ffn1_w1, ffn1_w3, ffn1_w2, mix_norm, w_in, conf_conv_w, conf_conv_b, conf_ln_g,
                           conf_ln_b, sconv_w, pool_w, pool_scale, gmlp_ln_g, gmlp_ln_b, gmlp_w_s, gmlp_b_s, w_out,
                           ffn2_norm, ffn2_w1, ffn2_w3, ffn2_w2, final_norm)))
    mom1 = dict(zip(WEIGHTS, (m_ffn1_norm, m_ffn1_w1, m_ffn1_w3, m_ffn1_w2, m_mix_norm, m_w_in, m_conf_conv_w,
                              m_conf_conv_b, m_conf_ln_g, m_conf_ln_b, m_sconv_w, m_pool_w, m_pool_scale, m_gmlp_ln_g,
                              m_gmlp_ln_b, m_gmlp_w_s, m_gmlp_b_s, m_w_out, m_ffn2_norm, m_ffn2_w1, m_ffn2_w3,
                              m_ffn2_w2, m_final_norm)))
    mom2 = dict(zip(WEIGHTS, (v_ffn1_norm, v_ffn1_w1, v_ffn1_w3, v_ffn1_w2, v_mix_norm, v_w_in, v_conf_conv_w,
                              v_conf_conv_b, v_conf_ln_g, v_conf_ln_b, v_sconv_w, v_pool_w, v_pool_scale, v_gmlp_ln_g,
                              v_gmlp_ln_b, v_gmlp_w_s, v_gmlp_b_s, v_w_out, v_ffn2_norm, v_ffn2_w1, v_ffn2_w3,
                              v_ffn2_w2, v_final_norm)))
    _, t, d = x.shape
    depth, _, fs = ffn1_w1.shape
    g = w_in.shape[2]
    wo = w_out.shape[1]
    nh = g // HEAD_DIM
    cs = conf_conv_w.shape[2]
    lay = _MixRows(g, wo)
    me = 4 * lax.axis_index("x") + 2 * lax.axis_index("y") + lax.axis_index("c")

    def rows_bf16(mats):
        return jnp.concatenate([m_.astype(BF16) for m_ in mats], axis=0)

    set_f1 = [rows_bf16([ffn1_w1[l].T, ffn1_w3[l].T, ffn1_w2[l]]) for l in range(depth)]
    set_f2 = [rows_bf16([ffn2_w1[l].T, ffn2_w3[l].T, ffn2_w2[l]]) for l in range(depth)]
    set_mx = [rows_bf16([w_in[l].T, w_out[l]]) for l in range(depth)]
    convs = jnp.concatenate([conf_conv_w, sconv_w], axis=1)
    nconv = convs.size
    conv_bits = jnp.pad(convs.reshape(-1), (0, CONV_ROWS * d // 2 - nconv))
    conv_rows = lax.bitcast_convert_type(conv_bits, BF16).reshape(CONV_ROWS, d)
    w_f1 = [None] * depth
    w_f1[0] = _alone(_Gather(jnp.concatenate([set_f1[0], conv_rows], axis=0)), "gather_first")

    conv_all = lax.bitcast_convert_type(w_f1[0][:, 3 * fs:, :].reshape(N_DEV, CONV_ROWS * d // 2, 2), F32)
    conv_all = conv_all[:, :nconv].reshape(N_DEV, depth, CONF_KERNEL + SHORT_KERNEL, cs)
    conv_all = conv_all.transpose(1, 2, 0, 3).reshape(depth, CONF_KERNEL + SHORT_KERNEL, g)

    def small_inputs(l):
        cw = jnp.pad(conv_all[l, :CONF_KERNEL], ((0, 32 - CONF_KERNEL), (0, 0)))
        sw = jnp.pad(conv_all[l, CONF_KERNEL:], ((0, 8 - SHORT_KERNEL), (0, 0)))
        vecs = jnp.stack([conf_conv_b[l], conf_ln_g[l], conf_ln_b[l], pool_scale[l], gmlp_ln_g[l], gmlp_ln_b[l],
                          jnp.zeros((g,), F32), jnp.zeros((g,), F32)])
        pool_bd = _block_diag(pool_w[l]).astype(BF16)
        low = jnp.tril(jnp.ones((CHUNK, CHUNK), bool))
        ws = jnp.where(low[None], gmlp_w_s[l], 0.0).astype(BF16)
        wcat = ws.transpose(1, 0, 2).reshape(CHUNK, nh * CHUNK)
        wcat_t = ws.transpose(0, 2, 1).reshape(nh * CHUNK, CHUNK)
        bst = jnp.repeat(gmlp_b_s[l].T, HEAD_DIM, axis=1)
        return cw, vecs, sw, pool_bd, wcat, bst, wcat_t

    xs = x[0]
    saved = []
    for l in range(depth):
        sm = small_inputs(l)
        (x1, a1, b1), ((w_f2,), (w_mx,)) = _ffn_fwd(xs, ffn1_norm[l], w_f1[l], fs, f"ffn1_fwd_{l}",
                                                    riders=[_Gather(set_f2[l]), _Gather(set_mx[l])])
        h2, p = _mix_in_fwd(x1, mix_norm[l], w_mx, lay, l, f"mix_in_fwd_{l}")
        x2, mix = _mix_fwd(p, x1, w_mx, lay, l, sm[:6], f"mix_fwd_{l}")
        nxt = [_Gather(set_f1[l + 1])] if l + 1 < depth else []
        (x3, a2, b2), got = _ffn_fwd(x2, ffn2_norm[l], w_f2, fs, f"ffn2_fwd_{l}", riders=nxt)
        if nxt:
            w_f1[l + 1] = got[0][0]
        saved.append((xs, a1, b1, x1, h2, p, mix, x2, a2, b2, sm, w_f2, w_mx))
        xs = x3

    dx, loss_part, dgf = _loss_head(xs, loss_target[0], final_norm, "loss_head")
    loss = lax.psum(loss_part[0, 0], ("x", "y", "c"))

    small = [None] * depth
    g_f1, g_f2, g_mx = [None] * depth, [None] * depth, [None] * depth
    pending = None
    for l in reversed(range(depth)):
        x0, a1, b1, x1, h2, p, mix, x2, a2, b2, sm, w_f2, w_mx = saved[l]
        ride = [_Exchange(pending)] if pending is not None else []
        (dx2, da, db, u, h, dgn2), got = _ffn_bwd(x2, dx, a2, b2, ffn2_norm[l], w_f2, fs, f"ffn2_bwd_{l}", riders=ride)
        if ride:
            g_f1[l + 1] = got[0][0]
        gset = lax.empty((N_DEV, 3 * fs, d), BF16)
        gset = _wgrad(da, h, gset, fs, 0, 1.0, f"ffn2_dw1_{l}")
        gset = _wgrad(db, h, gset, fs, 1, 1.0, f"ffn2_dw3_{l}")
        gset = _wgrad(u, dx, gset, fs, 2, FFN_RESIDUAL, f"ffn2_dw2_{l}")
        (dp, dcw, dvec, dsw, dpbd, dws, dbs), ((g_f2[l],),) = _mix_bwd(p, dx2, w_mx, lay, l, sm, f"mix_bwd_{l}",
                                                                       riders=[_Exchange(gset)])
        gset = lax.empty((N_DEV, g + wo, d), BF16)
        gset = _wgrad(mix, dx2, gset, wo, lay.wout(l), 1.0, f"dw_out_{l}")
        dx1, dgm = _mix_in_bwd(x1, dx2, dp, mix_norm[l], w_mx, lay, l, f"mix_in_bwd_{l}")
        gset = _wgrad(dp, h2, gset, g, lay.win(l), 1.0, f"dw_in_{l}")
        (dx, da, db, u, h, dgn1), ((g_mx[l],),) = _ffn_bwd(x0, dx1, a1, b1, ffn1_norm[l], w_f1[l], fs,
                                                           f"ffn1_bwd_{l}", riders=[_Exchange(gset)])
        gset = lax.empty((N_DEV, 3 * fs, d), BF16)
        gset = _wgrad(da, h, gset, fs, 0, 1.0, f"ffn1_dw1_{l}")
        gset = _wgrad(db, h, gset, fs, 1, 1.0, f"ffn1_dw3_{l}")
        pending = _wgrad(u, dx1, gset, fs, 2, FFN_RESIDUAL, f"ffn1_dw2_{l}")
        pg = g // len(POOL_WINDOWS)
        dpool = jnp.stack([dpbd[k * pg:(k + 1) * pg, k * pg:(k + 1) * pg] for k in range(len(POOL_WINDOWS))])
        small[l] = [dgn1[0], dgm[0], dgn2[0], dvec, dcw, dsw, dpool, dws, dbs[:, :nh].T]
    grad_x = dx[None]
    g_f1[0] = _alone(_Exchange(pending), "exchange_last")

    flat_small = [a for l in range(depth) for a in small[l]] + [dgf[0]]
    sbuf, offs = _f32_rows(flat_small, d, None)
    pad = -sbuf.shape[0] % 16
    sbuf = jnp.pad(sbuf, ((0, pad), (0, 0)))
    ssum = _sum_slots(_alone(_Gather(sbuf), "gather_small_grads"), "sum_small_grads")

    grads = {}
    s_f1 = [_sum_slots(g_f1[l], f"sum_f1_{l}") for l in range(depth)]
    s_f2 = [_sum_slots(g_f2[l], f"sum_f2_{l}") for l in range(depth)]
    s_mx = [_sum_slots(g_mx[l], f"sum_mx_{l}") for l in range(depth)]
    grads['ffn1_w1'] = jnp.stack([s_f1[l][0:fs].T for l in range(depth)])
    grads['ffn1_w3'] = jnp.stack([s_f1[l][fs:2 * fs].T for l in range(depth)])
    grads['ffn1_w2'] = jnp.stack([s_f1[l][2 * fs:] for l in range(depth)])
    grads['ffn2_w1'] = jnp.stack([s_f2[l][0:fs].T for l in range(depth)])
    grads['ffn2_w3'] = jnp.stack([s_f2[l][fs:2 * fs].T for l in range(depth)])
    grads['ffn2_w2'] = jnp.stack([s_f2[l][2 * fs:] for l in range(depth)])
    grads['w_in'] = jnp.stack([s_mx[l][0:g].T for l in range(depth)])
    grads['w_out'] = jnp.stack([s_mx[l][g:] for l in range(depth)])

    per_layer = len(small[0])
    shapes = [(d,), (d,), (d,), (8, g), (32, g), (8, g), (len(POOL_WINDOWS), g // len(POOL_WINDOWS),
              g // len(POOL_WINDOWS)), (nh * CHUNK, CHUNK), (nh, CHUNK)]
    got = [[_take(ssum, offs[l * per_layer + k], shapes[k]) for k in range(per_layer)] for l in range(depth)]
    col = lambda k: jnp.stack([got[l][k] for l in range(depth)])
    grads['ffn1_norm'], grads['mix_norm'], grads['ffn2_norm'] = col(0), col(1), col(2)
    dvec_all = col(3)
    for k, n in enumerate(('conf_conv_b', 'conf_ln_g', 'conf_ln_b', 'pool_scale', 'gmlp_ln_g', 'gmlp_ln_b')):
        grads[n] = dvec_all[:, k]
    grads['conf_conv_w'] = lax.dynamic_slice_in_dim(col(4)[:, :CONF_KERNEL], me * cs, cs, axis=2)
    grads['sconv_w'] = lax.dynamic_slice_in_dim(col(5)[:, :SHORT_KERNEL], me * cs, cs, axis=2)
    grads['pool_w'] = col(6)
    grads['gmlp_w_s'] = col(7).reshape(depth, nh, CHUNK, CHUNK)
    grads['gmlp_b_s'] = col(8)
    grads['final_norm'] = _take(ssum, offs[-1], (d,))

    deltas, new_m, new_v = {}, {}, {}
    for n in WEIGHTS:
        deltas[n], new_m[n], new_v[n] = _adamw(w[n], grads[n], mom1[n], mom2[n], f"adamw_{n}")

    return (loss, grad_x, *[grads[n] for n in WEIGHTS], *[deltas[n] for n in WEIGHTS],
            *[new_m[n] for n in WEIGHTS], *[new_v[n] for n in WEIGHTS])
```

```python
import functools

import jax
import jax.numpy as jnp
from jax import lax
from jax.experimental import pallas as pl
from jax.experimental.pallas import tpu as pltpu

F32 = jnp.float32
BF16 = jnp.bfloat16
MESH = pl.DeviceIdType.MESH

N_DEV = 8
EPS = 1e-6
FFN_RESIDUAL = 0.5
CONF_KERNEL = 31
SHORT_KERNEL = 3
POOL_WINDOWS = (2, 4, 8, 16)
CHUNK = 128
HEAD_DIM = 64
N_IN_PIECES = 8
HALO = 32
CONV_ROWS = 16
ADAM_LR = 0.001
ADAM_B1 = 0.9
ADAM_B2 = 0.999
ADAM_EPS = 1e-08
ADAM_WD = 0.01
ADAM_STEP = 10
VMEM_LIMIT_V7X = 56 * 1024 * 1024

WEIGHTS = ['ffn1_norm', 'ffn1_w1', 'ffn1_w3', 'ffn1_w2', 'mix_norm', 'w_in', 'conf_conv_w', 'conf_conv_b',
           'conf_ln_g', 'conf_ln_b', 'sconv_w', 'pool_w', 'pool_scale', 'gmlp_ln_g', 'gmlp_ln_b', 'gmlp_w_s',
           'gmlp_b_s', 'w_out', 'ffn2_norm', 'ffn2_w1', 'ffn2_w3', 'ffn2_w2', 'final_norm']
SHARDED_ROWS = ('ffn1_w2', 'ffn2_w2', 'w_out')
SHARDED_COLS = ('ffn1_w1', 'ffn1_w3', 'ffn2_w1', 'ffn2_w3', 'w_in')
SHARDED_CHAN = ('conf_conv_w', 'sconv_w')


def _params(sem=None):
    return pltpu.CompilerParams(dimension_semantics=sem, vmem_limit_bytes=VMEM_LIMIT_V7X)


def _nn(a, b):
    return jnp.dot(a, b, preferred_element_type=F32)


def _nt(a, b):
    return lax.dot_general(a, b, (((1,), (1,)), ((), ())), preferred_element_type=F32)


def _tn(a, b):
    return lax.dot_general(a, b, (((0,), (0,)), ((), ())), preferred_element_type=F32)


def _sigmoid(x):
    return 0.5 * jnp.tanh(0.5 * x) + 0.5


def _row_tile(n, pref, mult=8):
    t = min(n, pref)
    while n % t or t % mult:
        t -= 1
    return t


def _chunks(n, size):
    out, s = [], 0
    while s < n:
        out.append((s, min(size, n - s)))
        s += size
    return out


def _const(shape):
    return pl.BlockSpec(shape, lambda i: (0,) * len(shape))


def _resident(shape, index):
    return pl.BlockSpec(shape, lambda i: index, pipeline_mode=pl.Buffered(1))


def _wspec(rows, d, blk):
    return _resident((N_DEV, rows, d), (0, blk, 0))


def _layernorm_stats(x):
    mu = jnp.mean(x, axis=-1, keepdims=True)
    xc = x - mu
    r = lax.rsqrt(jnp.mean(xc * xc, axis=-1, keepdims=True) + EPS)
    return xc * r, r


def _layernorm_bwd(dy, g, xhat, r):
    dxh = dy * g
    return r * (dxh - jnp.mean(dxh, axis=-1, keepdims=True) - xhat * jnp.mean(dxh * xhat, axis=-1, keepdims=True))


def _rmsnorm_stats(x):
    r = lax.rsqrt(jnp.mean(x * x, axis=-1, keepdims=True) + EPS)
    return x * r, r


def _rmsnorm_bwd(dh, g, xhat, r):
    dxh = dh * g
    return r * (dxh - xhat * jnp.mean(dxh * xhat, axis=-1, keepdims=True))


class _MixRows:
    def __init__(self, g, wo):
        assert g % wo == 0 and wo % 16 == 0
        self.g, self.wo = g, wo

    def win(self, l):
        return 0

    def wout(self, l):
        return self.g // self.wo


class _Gather:
    def __init__(self, shard):
        self.operands = [shard]
        self.out_shape = [jax.ShapeDtypeStruct((N_DEV,) + shard.shape, shard.dtype)]
        self.scratch = [pltpu.SemaphoreType.DMA((7,)), pltpu.SemaphoreType.DMA((7,)), pltpu.SemaphoreType.DMA]

    def _copies(self, ins, outs, scr, arrivals):
        (x_ref,), (out_ref,), (send_sems, recv_sems, local_sem) = ins, outs, scr
        x, y, c = lax.axis_index("x"), lax.axis_index("y"), lax.axis_index("c")
        me, sibling = (x, y, c), (x, y, 1 - c)
        chips = [(1 - x, y), (x, 1 - y), (1 - x, 1 - y)]

        def slot(px, py, pc):
            return out_ref.at[4 * px + 2 * py + pc]

        def copy(k, block, to, src=None):
            return pltpu.make_async_remote_copy(
                src_ref=slot(*block) if src is None else src, dst_ref=slot(*block),
                send_sem=send_sems.at[k], recv_sem=recv_sems.at[k], device_id=to, device_id_type=MESH)

        mine = pltpu.make_async_copy(x_ref, slot(*me), local_sem)
        first = [copy(0, me, sibling, src=x_ref)]
        first += [copy(1 + j, me, (*chip, c), src=x_ref) for j, chip in enumerate(chips)]
        passed = [copy(4 + j, (*chip, c), sibling) for j, chip in enumerate(chips)]
        if arrivals == "ici":
            landing = [copy(1 + j, (*chip, c), me) for j, chip in enumerate(chips)]
        else:
            landing = [copy(0, sibling, me)] + [copy(4 + j, (*chip, 1 - c), me) for j, chip in enumerate(chips)]
        return mine, first, passed, landing

    def begin(self, i, n, ins, outs, scr):
        mine, first, passed, over_ici = self._copies(ins, outs, scr, "ici")

        @pl.when(i == 0)
        def _():
            mine.start()
            for cp in first:
                cp.start()

        @pl.when(i == (n - 3 if n >= 6 else n // 2))
        def _():
            for landed, onward in zip(over_ici, passed):
                landed.wait_recv()
                onward.start()

    def end(self, i, n, ins, outs, scr):
        mine, first, passed, over_d2d = self._copies(ins, outs, scr, "d2d")

        @pl.when(i == n - 1)
        def _():
            for cp in over_d2d:
                cp.wait_recv()
            for cp in first + passed:
                cp.wait_send()
            mine.wait()


class _Exchange:
    def __init__(self, blocks):
        self.operands = [blocks]
        self.out_shape = [jax.ShapeDtypeStruct(blocks.shape, blocks.dtype)]
        self.scratch = [pltpu.SemaphoreType.DMA((7,)), pltpu.SemaphoreType.DMA((7,)), pltpu.SemaphoreType.DMA]

    def _copies(self, ins, outs, scr, with_arrivals):
        (s_ref,), (r_ref,), (send_sems, recv_sems, local_sem) = ins, outs, scr
        x, y, c = lax.axis_index("x"), lax.axis_index("y"), lax.axis_index("c")
        me = 4 * x + 2 * y + c
        mine = pltpu.make_async_copy(s_ref.at[me], r_ref.at[me], local_sem)
        sends, recvs = [], []
        for k in range(1, N_DEV):
            px = (1 - x) if (k >> 2) & 1 else x
            py = (1 - y) if (k >> 1) & 1 else y
            pc = (1 - c) if k & 1 else c
            peer = 4 * px + 2 * py + pc
            for dst, group in ((r_ref.at[me], sends), (r_ref.at[peer], recvs)):
                if group is sends or with_arrivals:
                    group.append(pltpu.make_async_remote_copy(
                        src_ref=s_ref.at[peer], dst_ref=dst, send_sem=send_sems.at[k - 1],
                        recv_sem=recv_sems.at[k - 1], device_id=(px, py, pc), device_id_type=MESH))
        return mine, sends, recvs

    def begin(self, i, n, ins, outs, scr):
        mine, sends, _ = self._copies(ins, outs, scr, False)

        @pl.when(i == 0)
        def _():
            mine.start()
            for cp in sends:
                cp.start()

    def end(self, i, n, ins, outs, scr):
        mine, sends, recvs = self._copies(ins, outs, scr, True)

        @pl.when(i == n - 1)
        def _():
            for cp in recvs:
                cp.wait_recv()
            for cp in sends:
                cp.wait_send()
            mine.wait()


_ANY = pl.BlockSpec(memory_space=pl.ANY)


def _call(body, name, steps, operands, in_specs, out_shape, out_specs, scratch=(), riders=(), aliases=None):
    n_in, n_out, n_scr = len(operands), len(out_shape), len(scratch)

    def full(*refs):
        pos = [0]

        def take(k):
            pos[0] += k
            return refs[pos[0] - k:pos[0]]

        h_in, r_in = take(n_in), [take(len(r.operands)) for r in riders]
        h_out, r_out = take(n_out), [take(len(r.out_shape)) for r in riders]
        h_scr, r_scr = take(n_scr), [take(len(r.scratch)) for r in riders]
        i = pl.program_id(0)
        for r, a, b, c in zip(riders, r_in, r_out, r_scr):
            r.begin(i, steps, a, b, c)
        body(*h_in, *h_out, *h_scr)
        for r, a, b, c in zip(riders, r_in, r_out, r_scr):
            r.end(i, steps, a, b, c)

    outs = pl.pallas_call(
        full, name=name, grid=(steps,),
        out_shape=tuple(out_shape) + tuple(s for r in riders for s in r.out_shape),
        in_specs=list(in_specs) + [_ANY for r in riders for _ in r.operands],
        out_specs=tuple(out_specs) + tuple(_ANY for r in riders for _ in r.out_shape),
        scratch_shapes=list(scratch) + [s for r in riders for s in r.scratch],
        input_output_aliases=aliases or {},
        compiler_params=_params(("arbitrary",)),
    )(*operands, *(o for r in riders for o in r.operands))
    host, rest = outs[:n_out], list(outs[n_out:])
    rides = []
    for r in riders:
        rides.append(rest[:len(r.out_shape)])
        rest = rest[len(r.out_shape):]
    return host, rides


def _alone(rider, name):
    return _call(lambda: None, name, 1, [], [], [], [], riders=[rider])[1][0][0]


def _sum_slots(slots, name):
    _, rows, cols = slots.shape
    tr = _row_tile(rows, 256, 16)

    def body(s_ref, o_ref):
        acc = s_ref[0].astype(F32)
        for j in range(1, N_DEV):
            acc = acc + s_ref[j].astype(F32)
        o_ref[...] = acc

    return pl.pallas_call(
        body, name=name, grid=(rows // tr,),
        out_shape=jax.ShapeDtypeStruct((rows, cols), F32),
        in_specs=[pl.BlockSpec((N_DEV, tr, cols), lambda i: (0, i, 0))],
        out_specs=pl.BlockSpec((tr, cols), lambda i: (i, 0)),
        compiler_params=_params(("parallel",)),
    )(slots)


def _ffn_fwd(x, gain, wset, fs, name, riders=()):
    t, d = x.shape
    f = N_DEV * fs
    tm = _row_tile(t, 512, 128)
    cols = _chunks(f, 1024)

    def body(x_ref, g_ref, w1_ref, w3_ref, w2_ref, y_ref, a_ref, b_ref, u_ref):
        xv = x_ref[...]
        xhat, _ = _rmsnorm_stats(xv)
        h = (xhat * g_ref[...]).astype(BF16)
        w1 = w1_ref[...].reshape(f, d)
        w3 = w3_ref[...].reshape(f, d)
        for s, n in cols:
            a = _nt(h, w1[s:s + n, :])
            b = _nt(h, w3[s:s + n, :])
            a_ref[:, s:s + n] = a.astype(BF16)
            b_ref[:, s:s + n] = b.astype(BF16)
            u_ref[:, s:s + n] = (a * _sigmoid(a) * b).astype(BF16)
        y_ref[...] = xv + FFN_RESIDUAL * _nn(u_ref[...], w2_ref[...].reshape(f, d))

    return _call(
        body, name, t // tm, [x, gain.reshape(1, d), wset, wset, wset],
        [pl.BlockSpec((tm, d), lambda i: (i, 0)), _const((1, d)),
         _wspec(fs, d, 0), _wspec(fs, d, 1), _wspec(fs, d, 2)],
        [jax.ShapeDtypeStruct((t, d), F32), jax.ShapeDtypeStruct((t, f), BF16), jax.ShapeDtypeStruct((t, f), BF16)],
        [pl.BlockSpec((tm, d), lambda i: (i, 0)), pl.BlockSpec((tm, f), lambda i: (i, 0)),
         pl.BlockSpec((tm, f), lambda i: (i, 0))],
        scratch=[pltpu.VMEM((tm, f), BF16)], riders=riders)


def _ffn_bwd(x, dy, a, b, gain, wset, fs, name, riders=()):
    t, d = x.shape
    f = N_DEV * fs
    tm = _row_tile(t, 256, 128)
    cols = _chunks(f, 1024)

    def body(x_ref, dy_ref, a_ref, b_ref, g_ref, w1_ref, w3_ref, w2_ref,
             dx_ref, da_ref, db_ref, u_ref, h_ref, dg_ref):
        @pl.when(pl.program_id(0) == 0)
        def _():
            dg_ref[...] = jnp.zeros_like(dg_ref)

        xv, dyv, g = x_ref[...], dy_ref[...], g_ref[...]
        xhat, r = _rmsnorm_stats(xv)
        h_ref[...] = (xhat * g).astype(BF16)
        dyb = (FFN_RESIDUAL * dyv).astype(BF16)
        w1 = w1_ref[...].reshape(f, d)
        w3 = w3_ref[...].reshape(f, d)
        w2 = w2_ref[...].reshape(f, d)
        dh = jnp.zeros((tm, d), F32)
        for s, n in cols:
            du = _nt(dyb, w2[s:s + n, :])
            av = a_ref[:, s:s + n].astype(F32)
            bv = b_ref[:, s:s + n].astype(F32)
            sig = _sigmoid(av)
            sa = av * sig
            u_ref[:, s:s + n] = (sa * bv).astype(BF16)
            da = (du * bv * (sig + sa * (1.0 - sig))).astype(BF16)
            db = (du * sa).astype(BF16)
            da_ref[:, s:s + n] = da
            db_ref[:, s:s + n] = db
            dh = dh + _nn(da, w1[s:s + n, :]) + _nn(db, w3[s:s + n, :])
        dx_ref[...] = dyv + _rmsnorm_bwd(dh, g, xhat, r)
        dg_ref[0:1, :] += jnp.sum(dh * xhat, axis=0, keepdims=True)

    big = jax.ShapeDtypeStruct((t, f), BF16)
    row = lambda w: pl.BlockSpec((tm, w), lambda i: (i, 0))
    return _call(
        body, name, t // tm, [x, dy, a, b, gain.reshape(1, d), wset, wset, wset],
        [row(d), row(d), row(f), row(f), _const((1, d)), _wspec(fs, d, 0), _wspec(fs, d, 1), _wspec(fs, d, 2)],
        [jax.ShapeDtypeStruct((t, d), F32), big, big, big, jax.ShapeDtypeStruct((t, d), BF16),
         jax.ShapeDtypeStruct((8, d), F32)],
        [row(d), row(f), row(f), row(f), row(d), _const((8, d))], riders=riders)


def _wgrad(lhs, rhs, gbuf, rows, blk, scale, name, riders=()):
    t, m = lhs.shape
    d = rhs.shape[1]
    bt = _row_tile(t, 1024, 128)
    nk = t // bt
    assert m == N_DEV * rows

    def body(a_ref, b_ref, g_in, o_ref, acc_ref):
        del g_in
        k = pl.program_id(0)

        @pl.when(k == 0)
        def _():
            acc_ref[...] = jnp.zeros_like(acc_ref)

        acc_ref[...] += _tn(a_ref[...], b_ref[...].astype(BF16))

        @pl.when(k == nk - 1)
        def _():
            o_ref[...] = (scale * acc_ref[...]).astype(BF16).reshape(N_DEV, rows, d)

    (out,), rides = _call(
        body, name, nk, [lhs, rhs, gbuf],
        [pl.BlockSpec((bt, m), lambda k: (k, 0)), pl.BlockSpec((bt, d), lambda k: (k, 0)), _ANY],
        [jax.ShapeDtypeStruct(gbuf.shape, gbuf.dtype)], [pl.BlockSpec((N_DEV, rows, d), lambda k: (0, blk, 0))],
        scratch=[pltpu.VMEM((m, d), F32)], riders=riders, aliases={2: 0})
    return (out, rides) if riders else out


def _mix_in_fwd(x, gain, wbuf, lay, l, name):
    t, d = x.shape
    g = lay.g
    di = N_DEV * g
    tm = _row_tile(t, 512, 128)

    def body(x_ref, g_ref, w_ref, h_ref, p_ref):
        xhat, _ = _rmsnorm_stats(x_ref[...])
        h = (xhat * g_ref[...]).astype(BF16)
        h_ref[...] = h
        p_ref[...] = _nt(h, w_ref[...].reshape(di, d)).astype(BF16)

    return pl.pallas_call(
        body, name=name, grid=(t // tm,),
        out_shape=(jax.ShapeDtypeStruct((t, d), BF16), jax.ShapeDtypeStruct((t, di), BF16)),
        in_specs=[pl.BlockSpec((tm, d), lambda i: (i, 0)), _const((1, d)), _wspec(g, d, lay.win(l))],
        out_specs=(pl.BlockSpec((tm, d), lambda i: (i, 0)), pl.BlockSpec((tm, di), lambda i: (i, 0))),
        compiler_params=_params(("parallel",)),
    )(x, gain.reshape(1, d), wbuf)


def _mix_in_bwd(x, dx_out, dp, gain, wbuf, lay, l, name):
    t, d = x.shape
    g = lay.g
    di = N_DEV * g
    tm = _row_tile(t, 512, 128)

    def body(x_ref, dxo_ref, dp_ref, g_ref, w_ref, dx_ref, dg_ref):
        @pl.when(pl.program_id(0) == 0)
        def _():
            dg_ref[...] = jnp.zeros_like(dg_ref)

        xhat, r = _rmsnorm_stats(x_ref[...])
        dh = _nn(dp_ref[...], w_ref[...].reshape(di, d))
        dx_ref[...] = dxo_ref[...] + _rmsnorm_bwd(dh, g_ref[...], xhat, r)
        dg_ref[0:1, :] += jnp.sum(dh * xhat, axis=0, keepdims=True)

    row = lambda w: pl.BlockSpec((tm, w), lambda i: (i, 0))
    return pl.pallas_call(
        body, name=name, grid=(t // tm,),
        out_shape=(jax.ShapeDtypeStruct((t, d), F32), jax.ShapeDtypeStruct((8, d), F32)),
        in_specs=[row(d), row(d), row(di), _const((1, d)), _wspec(g, d, lay.win(l))],
        out_specs=(row(d), _const((8, d))),
        compiler_params=_params(("arbitrary",)),
    )(x, dx_out, dp, gain.reshape(1, d), wbuf)


def _head_masks(rows, g):
    lane = lax.broadcasted_iota(jnp.int32, (rows, g), 1)
    return [(lane >= h * HEAD_DIM) & (lane < (h + 1) * HEAD_DIM) for h in range(g // HEAD_DIM)]


def _pool_windows(rows, g):
    lane = lax.broadcasted_iota(jnp.int32, (rows, g), 1)
    pg = g // len(POOL_WINDOWS)
    w = jnp.full((rows, g), float(POOL_WINDOWS[-1]), F32)
    for k in range(len(POOL_WINDOWS) - 2, -1, -1):
        w = jnp.where(lane < (k + 1) * pg, float(POOL_WINDOWS[k]), w)
    return w


def _window_sum(ref, tmp, first, rows, wl, direction):
    total = ref.shape[0]
    src, out, shift = ref, None, 1
    for k, w in enumerate(POOL_WINDOWS):
        assert w == 2 * shift and 8 * (k + 1) <= HALO
        if k + 1 < len(POOL_WINDOWS):
            n = total - 8 * (k + 1)
            lo = 8 * (k + 1) if direction < 0 else 0
            dst = tmp.at[k % 2]
            dst[pl.ds(lo, n), :] = src[pl.ds(lo, n), :] + src[pl.ds(lo + direction * shift, n), :]
            src = dst
            level = dst[pl.ds(first, rows), :]
        else:
            level = src[pl.ds(first, rows), :] + src[pl.ds(first + direction * shift, rows), :]
        out = level if out is None else jnp.where(wl >= float(w), level, out)
        shift = w
    return out


def _shifted_copies(src_ref, dst_ref):
    rows = dst_ref.shape[1]
    for b in range(1, 8):
        dst_ref[b - 1] = src_ref[pl.ds(b, rows), :]


def _window(src_ref, shifted_ref, off, rows):
    b = off % 8
    if b == 0:
        return src_ref[pl.ds(off, rows), :]
    return shifted_ref[b - 1, pl.ds(off - b, rows), :]


def _pool_divisor(first_row, rows, g):
    pos1 = (lax.broadcasted_iota(jnp.int32, (rows, g), 0) + first_row + 1).astype(F32)
    return jnp.minimum(pos1, _pool_windows(rows, g))


def _spatial_mix(vq, wcat_ref, masks):
    vstack = jnp.concatenate([jnp.where(m, vq, jnp.zeros_like(vq)) for m in masks], axis=0)
    return _nn(wcat_ref[...], vstack)


def _mix_fwd(p, x, wbuf, lay, l, small, name):
    t, d = x.shape
    g = lay.g
    di = N_DEV * g
    tm = _row_tile(t, 256, CHUNK)
    nh = g // HEAD_DIM
    per = tm // HALO
    conv_w, vecs, sconv_w, pool_bd, wcat, bst = small

    def body(pp_ref, pm_ref, x_ref, wo_ref, cw_ref, vec_ref, sw_ref, pbd_ref, wcat_ref, bst_ref,
             y_ref, mix_ref, ea_ref, eb_ref, ec_ref, sh_ref, tmp_ref):
        i = pl.program_id(0)
        seen = (i > 0).astype(F32)

        def prev(j):
            return pp_ref[:, j * g:(j + 1) * g].astype(F32) * seen

        def main(j):
            return pm_ref[:, j * g:(j + 1) * g].astype(F32)

        vec = vec_ref[...]
        ea_ref[0:HALO, :] = prev(0) * _sigmoid(prev(1))
        ea_ref[HALO:, :] = main(0) * _sigmoid(main(1))
        _shifted_copies(ea_ref, sh_ref)
        cw = cw_ref[...]
        c = jnp.zeros((tm, g), F32) + vec[0:1, :]
        for k in range(CONF_KERNEL):
            c = c + cw[k:k + 1, :] * _window(ea_ref, sh_ref, HALO - (CONF_KERNEL - 1) + k, tm)
        chat, _ = _layernorm_stats(c)
        ln = chat * vec[1:2, :] + vec[2:3, :]
        mix_ref[:, 0:g] = (ln * _sigmoid(ln)).astype(BF16)
        eb_ref[0:HALO, :] = prev(3) * prev(4)
        eb_ref[HALO:, :] = main(3) * main(4)
        sw = sw_ref[...]
        cz = jnp.zeros((tm, g), F32)
        for k in range(SHORT_KERNEL):
            cz = cz + sw[k:k + 1, :] * eb_ref[pl.ds(HALO - (SHORT_KERNEL - 1) + k, tm), :]
        mix_ref[:, g:2 * g] = (main(2) * cz).astype(BF16)
        ec_ref[0:HALO, :] = prev(5)
        xp = main(5)
        ec_ref[HALO:, :] = xp
        wl = _pool_windows(tm, g)
        mean = _window_sum(ec_ref, tmp_ref, HALO, tm, wl, -1) / _pool_divisor(i * tm, tm, g)
        yc = _nn((mean - xp).astype(BF16), pbd_ref[...])
        mix_ref[:, 2 * g:3 * g] = (yc * vec[3:4, :]).astype(BF16)
        vhat, _ = _layernorm_stats(main(7))
        v = (vhat * vec[4:5, :] + vec[5:6, :]).astype(BF16)
        gu = main(6)
        masks = _head_masks(CHUNK, g)
        for q in range(tm // CHUNK):
            rs = slice(q * CHUNK, (q + 1) * CHUNK)
            mixed = _spatial_mix(v[rs, :], wcat_ref, masks) + bst_ref[...]
            mix_ref[rs, 3 * g:4 * g] = (gu[rs, :] * mixed).astype(BF16)
        y_ref[...] = x_ref[...] + _nn(mix_ref[...], wo_ref[...].reshape(d, d))

    return pl.pallas_call(
        body, name=name, grid=(t // tm,),
        out_shape=(jax.ShapeDtypeStruct((t, d), F32), jax.ShapeDtypeStruct((t, 4 * g), BF16)),
        in_specs=[pl.BlockSpec((HALO, di), lambda i: (jnp.maximum(i * per - 1, 0), 0)),
                  pl.BlockSpec((tm, di), lambda i: (i, 0)), pl.BlockSpec((tm, d), lambda i: (i, 0)),
                  _wspec(lay.wo, d, lay.wout(l)),
                  _const(conv_w.shape), _const(vecs.shape), _const(sconv_w.shape), _const(pool_bd.shape),
                  _const(wcat.shape), _const(bst.shape)],
        out_specs=(pl.BlockSpec((tm, d), lambda i: (i, 0)), pl.BlockSpec((tm, 4 * g), lambda i: (i, 0))),
        scratch_shapes=[pltpu.VMEM((HALO + tm, g), F32)] * 3
        + [pltpu.VMEM((7, HALO + tm - 8, g), F32), pltpu.VMEM((2, HALO + tm, g), F32)],
        compiler_params=_params(("parallel",)),
    )(p, p, x, wbuf, conv_w, vecs, sconv_w, pool_bd, wcat, bst)


def _mix_bwd(p, dy, wbuf, lay, l, small, name, riders=()):
    t, d = dy.shape
    g = lay.g
    di = N_DEV * g
    tm = _row_tile(t, 256, CHUNK)
    nt = t // tm
    nh = g // HEAD_DIM
    per = tm // HALO
    ext = tm + HALO
    conv_w, vecs, sconv_w, pool_bd, wcat, bst, wcat_t = small

    def body(pp_ref, pm_ref, pn_ref, dy_ref, dyn_ref, wo_ref, cw_ref, vec_ref, sw_ref, pbd_ref, wcat_ref, bst_ref,
             wcatt_ref, dp_ref, dcw_ref, dvec_ref, dsw_ref, dpbd_ref, dws_ref, dbs_ref,
             ea_ref, eb_ref, ec_ref, fa_ref, fb_ref, fc_ref, sh_ref, sf_ref, tmp_ref):
        i = pl.program_id(0)
        seen = (i > 0).astype(F32)
        more = (i < nt - 1).astype(F32)

        @pl.when(i == 0)
        def _():
            for ref in (dcw_ref, dvec_ref, dsw_ref, dpbd_ref, dws_ref, dbs_ref):
                ref[...] = jnp.zeros_like(ref)

        def prev(j):
            return pp_ref[:, j * g:(j + 1) * g].astype(F32) * seen

        def main(j):
            return pm_ref[:, j * g:(j + 1) * g].astype(F32)

        def nxt(j):
            return pn_ref[:, j * g:(j + 1) * g].astype(F32)

        def colsum(v):
            return jnp.sum(v, axis=0, keepdims=True)

        vec = vec_ref[...]
        wo = wo_ref[...].reshape(d, d)
        dmix = jnp.concatenate([_nt(dy_ref[...].astype(BF16), wo),
                                _nt((dyn_ref[...] * more).astype(BF16), wo)], axis=0)

        val, gate = main(0), main(1)
        sg = _sigmoid(gate)
        ea_ref[0:HALO, :] = prev(0) * _sigmoid(prev(1))
        ea_ref[HALO:ext, :] = val * sg
        ea_ref[ext:, :] = nxt(0) * _sigmoid(nxt(1))
        _shifted_copies(ea_ref, sh_ref)
        cw = cw_ref[...]
        c = jnp.zeros((ext, g), F32) + vec[0:1, :]
        for k in range(CONF_KERNEL):
            c = c + cw[k:k + 1, :] * _window(ea_ref, sh_ref, HALO - (CONF_KERNEL - 1) + k, ext)
        chat, r = _layernorm_stats(c)
        ln = chat * vec[1:2, :] + vec[2:3, :]
        sl = _sigmoid(ln)
        dln = dmix[:, 0:g] * (sl * (1.0 + ln * (1.0 - sl)))
        dvec_ref[1:2, :] += colsum(dln[0:tm] * chat[0:tm])
        dvec_ref[2:3, :] += colsum(dln[0:tm])
        dc = _layernorm_bwd(dln, vec[1:2, :], chat, r)
        dvec_ref[0:1, :] += colsum(dc[0:tm])
        fa_ref[...] = dc
        _shifted_copies(fa_ref, sf_ref)
        dcm = dc[0:tm]
        dy0 = jnp.zeros((tm, g), F32)
        for k in range(CONF_KERNEL):
            dy0 = dy0 + cw[k:k + 1, :] * _window(fa_ref, sf_ref, CONF_KERNEL - 1 - k, tm)
            dcw_ref[k:k + 1, :] += colsum(dcm * _window(ea_ref, sh_ref, HALO - (CONF_KERNEL - 1) + k, tm))
        dp_ref[:, 0:g] = (dy0 * sg).astype(BF16)
        dp_ref[:, g:2 * g] = (dy0 * val * sg * (1.0 - sg)).astype(BF16)

        sb, sc, sx = main(2), main(3), main(4)
        eb_ref[0:HALO, :] = prev(3) * prev(4)
        eb_ref[HALO:ext, :] = sc * sx
        sw = sw_ref[...]
        cz = jnp.zeros((tm, g), F32)
        for k in range(SHORT_KERNEL):
            cz = cz + sw[k:k + 1, :] * eb_ref[pl.ds(HALO - (SHORT_KERNEL - 1) + k, tm), :]
        dob = dmix[:, g:2 * g]
        dp_ref[:, 2 * g:3 * g] = (dob[0:tm] * cz).astype(BF16)
        fb_ref[0:tm, :] = dob[0:tm] * sb
        fb_ref[tm:, :] = dob[tm:] * nxt(2)
        dczm = fb_ref[0:tm, :]
        dz = jnp.zeros((tm, g), F32)
        for k in range(SHORT_KERNEL):
            dz = dz + sw[k:k + 1, :] * fb_ref[pl.ds(SHORT_KERNEL - 1 - k, tm), :]
            dsw_ref[k:k + 1, :] += colsum(dczm * eb_ref[pl.ds(HALO - (SHORT_KERNEL - 1) + k, tm), :])
        dp_ref[:, 3 * g:4 * g] = (dz * sx).astype(BF16)
        dp_ref[:, 4 * g:5 * g] = (dz * sc).astype(BF16)

        xp = main(5)
        ec_ref[0:HALO, :] = prev(5)
        ec_ref[HALO:, :] = xp
        wl = _pool_windows(tm, g)
        dpool = ((_window_sum(ec_ref, tmp_ref, HALO, tm, wl, -1) / _pool_divisor(i * tm, tm, g)) - xp).astype(BF16)
        pbd = pbd_ref[...]
        yc = _nn(dpool, pbd)
        doc = dmix[:, 2 * g:3 * g]
        dvec_ref[3:4, :] += colsum(doc[0:tm] * yc)
        dyc = (doc * vec[3:4, :]).astype(BF16)
        dpbd_ref[...] += _tn(dpool, dyc[0:tm])
        dd = _nt(dyc, pbd)
        fc_ref[...] = dd / _pool_divisor(i * tm, ext, g)
        dp_ref[:, 5 * g:6 * g] = (_window_sum(fc_ref, tmp_ref, 0, tm, wl, 1) - dd[0:tm]).astype(BF16)

        gv = main(7)
        vhat, rv = _layernorm_stats(gv)
        v = (vhat * vec[4:5, :] + vec[5:6, :]).astype(BF16)
        gu = main(6)
        dod = dmix[0:tm, 3 * g:4 * g]
        masks = _head_masks(CHUNK, g)
        tril = (lax.broadcasted_iota(jnp.int32, (CHUNK, CHUNK), 1)
                <= lax.broadcasted_iota(jnp.int32, (CHUNK, CHUNK), 0)).astype(F32)
        tril = jnp.concatenate([tril] * nh, axis=0)
        head_col = lax.broadcasted_iota(jnp.int32, (CHUNK, CHUNK), 1)
        dvs = []
        for q in range(tm // CHUNK):
            rs = slice(q * CHUNK, (q + 1) * CHUNK)
            vq = v[rs, :]
            mixed = _spatial_mix(vq, wcat_ref, masks) + bst_ref[...]
            dp_ref[rs, 6 * g:7 * g] = (dod[rs, :] * mixed).astype(BF16)
            dmixed = dod[rs, :] * gu[rs, :]
            dmb = dmixed.astype(BF16)
            for h, m in enumerate(masks):
                hs = jnp.sum(jnp.where(m, dmixed, 0.0), axis=-1, keepdims=True)
                dbs_ref[...] += jnp.where(head_col == h, hs, 0.0)
            dmstack = jnp.concatenate([jnp.where(m, dmb, jnp.zeros_like(dmb)) for m in masks], axis=0)
            dws_ref[...] += _nt(dmstack, vq) * tril
            back = _nn(wcatt_ref[...], dmb)
            dv = jnp.zeros((CHUNK, g), F32)
            for h, m in enumerate(masks):
                dv = jnp.where(m, back[h * CHUNK:(h + 1) * CHUNK, :], dv)
            dvs.append(dv)
        dv = jnp.concatenate(dvs, axis=0)
        dvec_ref[4:5, :] += colsum(dv * vhat)
        dvec_ref[5:6, :] += colsum(dv)
        dp_ref[:, 7 * g:8 * g] = _layernorm_bwd(dv, vec[4:5, :], vhat, rv).astype(BF16)

    halo_prev = pl.BlockSpec((HALO, di), lambda i: (jnp.maximum(i * per - 1, 0), 0))
    halo_next = lambda w: pl.BlockSpec((HALO, w), lambda i: (jnp.minimum((i + 1) * per, t // HALO - 1), 0))
    row = lambda w: pl.BlockSpec((tm, w), lambda i: (i, 0))
    outs = (jax.ShapeDtypeStruct((t, di), BF16), jax.ShapeDtypeStruct(conv_w.shape, F32),
            jax.ShapeDtypeStruct(vecs.shape, F32), jax.ShapeDtypeStruct(sconv_w.shape, F32),
            jax.ShapeDtypeStruct((g, g), F32), jax.ShapeDtypeStruct((nh * CHUNK, CHUNK), F32),
            jax.ShapeDtypeStruct((CHUNK, CHUNK), F32))
    return _call(
        body, name, nt, [p, p, p, dy, dy, wbuf, conv_w, vecs, sconv_w, pool_bd, wcat, bst, wcat_t],
        [halo_prev, row(di), halo_next(di), row(d), halo_next(d), _wspec(lay.wo, d, lay.wout(l)),
         _const(conv_w.shape), _const(vecs.shape), _const(sconv_w.shape), _const(pool_bd.shape),
         _const(wcat.shape), _const(bst.shape), _const(wcat_t.shape)],
        list(outs), [row(di)] + [_const(o.shape) for o in outs[1:]],
        scratch=[pltpu.VMEM((HALO + tm + HALO, g), F32), pltpu.VMEM((HALO + tm, g), F32),
                 pltpu.VMEM((HALO + tm, g), F32), pltpu.VMEM((ext, g), F32), pltpu.VMEM((ext, g), F32),
                 pltpu.VMEM((ext, g), F32), pltpu.VMEM((7, HALO + tm + HALO - 8, g), F32),
                 pltpu.VMEM((7, ext - 8, g), F32), pltpu.VMEM((2, ext, g), F32)],
        riders=riders)


def _loss_head(x, target, gain, name):
    t, d = x.shape
    tm = _row_tile(t, 512, 8)

    def body(x_ref, t_ref, g_ref, dx_ref, loss_ref, dg_ref):
        @pl.when(pl.program_id(0) == 0)
        def _():
            loss_ref[...] = jnp.zeros_like(loss_ref)
            dg_ref[...] = jnp.zeros_like(dg_ref)

        g = g_ref[...]
        xhat, r = _rmsnorm_stats(x_ref[...])
        err = xhat * g - t_ref[...]
        loss_ref[...] += 0.5 * jnp.sum(jnp.mean(err * err, axis=-1, keepdims=True))
        dyv = err * (1.0 / d)
        dx_ref[...] = _rmsnorm_bwd(dyv, g, xhat, r)
        dg_ref[0:1, :] += jnp.sum(dyv * xhat, axis=0, keepdims=True)

    row = pl.BlockSpec((tm, d), lambda i: (i, 0))
    return pl.pallas_call(
        body, name=name, grid=(t // tm,),
        out_shape=(jax.ShapeDtypeStruct((t, d), F32), jax.ShapeDtypeStruct((8, 128), F32),
                   jax.ShapeDtypeStruct((8, d), F32)),
        in_specs=[row, row, _const((1, d))],
        out_specs=(row, _const((8, 128)), _const((8, d))),
        compiler_params=_params(("arbitrary",)),
    )(x, target, gain.reshape(1, d))


def _adamw(w, grad, m, v, name):
    shape = w.shape
    cols = shape[-1]
    rows = w.size // cols
    tr = _row_tile(rows, 512, 8) if rows > 1024 else rows

    def body(w_ref, g_ref, m_ref, v_ref, d_ref, nm_ref, nv_ref):
        gv = g_ref[...]
        nm = ADAM_B1 * m_ref[...] + (1.0 - ADAM_B1) * gv
        nv = ADAM_B2 * v_ref[...] + (1.0 - ADAM_B2) * (gv * gv)
        m_hat = nm / (1.0 - ADAM_B1 ** ADAM_STEP)
        v_hat = nv / (1.0 - ADAM_B2 ** ADAM_STEP)
        d_ref[...] = -ADAM_LR * (m_hat / (jnp.sqrt(v_hat) + ADAM_EPS) + ADAM_WD * w_ref[...])
        nm_ref[...] = nm
        nv_ref[...] = nv

    spec = pl.BlockSpec((tr, cols), lambda i: (i, 0))
    flat = jax.ShapeDtypeStruct((rows, cols), F32)
    outs = pl.pallas_call(
        body, name=name, grid=(rows // tr,),
        out_shape=(flat, flat, flat), in_specs=[spec] * 4, out_specs=(spec,) * 3,
        compiler_params=_params(("parallel",)),
    )(*(a.reshape(rows, cols) for a in (w, grad, m, v)))
    return tuple(o.reshape(shape) for o in outs)


def _f32_rows(parts, d, rows):
    out, offs, at = [], [], 0
    for a in parts:
        n = -(-a.size // d)
        out.append(jnp.pad(a.reshape(-1), (0, n * d - a.size)).reshape(n, d))
        offs.append(at)
        at += n
    total = at if rows is None else rows
    assert at <= total
    if total > at:
        out.append(jnp.zeros((total - at, d), F32))
    return jnp.concatenate(out, axis=0), offs


def _take(buf, off, shape):
    n = 1
    for s in shape:
        n *= s
    d = buf.shape[1]
    return buf[off:off + -(-n // d)].reshape(-1)[:n].reshape(shape)


def _block_diag(blocks):
    n, k, _ = blocks.shape
    out = jnp.zeros((n * k, n * k), blocks.dtype)
    for i in range(n):
        out = lax.dynamic_update_slice(out, blocks[i], (i * k, i * k))
    return out


def kernel(x, ffn1_norm, ffn1_w1, ffn1_w3, ffn1_w2, mix_norm, w_in, conf_conv_w, conf_conv_b, conf_ln_g, conf_ln_b, sconv_w, pool_w, pool_scale, gmlp_ln_g, gmlp_ln_b, gmlp_w_s, gmlp_b_s, w_out, ffn2_norm, ffn2_w1, ffn2_w3, ffn2_w2, final_norm, loss_target, m_ffn1_norm, m_ffn1_w1, m_ffn1_w3, m_ffn1_w2, m_mix_norm, m_w_in, m_conf_conv_w, m_conf_conv_b, m_conf_ln_g, m_conf_ln_b, m_sconv_w, m_pool_w, m_pool_scale, m_gmlp_ln_g, m_gmlp_ln_b, m_gmlp_w_s, m_gmlp_b_s, m_w_out, m_ffn2_norm, m_ffn2_w1, m_ffn2_w3, m_ffn2_w2, m_final_norm, v_ffn1_norm, v_ffn1_w1, v_ffn1_w3, v_ffn1_w2, v_mix_norm, v_w_in, v_conf_conv_w, v_conf_conv_b, v_conf_ln_g, v_conf_ln_b, v_sconv_w, v_pool_w, v_pool_scale, v_gmlp_ln_g, v_gmlp_ln_b, v_gmlp_w_s, v_gmlp_b_s, v_w_out, v_ffn2_norm, v_ffn2_w1, v_ffn2_w3, v_ffn2_w2, v_final_norm):
    w = dict(zip(WEIGHTS, (ffn1_norm, ffn1_w1, ffn1_w3, ffn1_w2, mix_norm, w_in, conf_conv_w, conf_conv_b, conf_ln_g,
                           conf_ln_b, sconv_w, pool_w, pool_scale, gmlp_ln_g, gmlp_ln_b, gmlp_w_s, gmlp_b_s, w_out,
                           ffn2_norm, ffn2_w1, ffn2_w3, ffn2_w2, final_norm)))
    mom1 = dict(zip(WEIGHTS, (m_ffn1_norm, m_ffn1_w1, m_ffn1_w3, m_ffn1_w2, m_mix_norm, m_w_in, m_conf_conv_w,
                              m_conf_conv_b, m_conf_ln_g, m_conf_ln_b, m_sconv_w, m_pool_w, m_pool_scale, m_gmlp_ln_g,
                              m_gmlp_ln_b, m_gmlp_w_s, m_gmlp_b_s, m_w_out, m_ffn2_norm, m_ffn2_w1, m_ffn2_w3,
                              m_ffn2_w2, m_final_norm)))
    mom2 = dict(zip(WEIGHTS, (v_ffn1_norm, v_ffn1_w1, v_ffn1_w3, v_ffn1_w2, v_mix_norm, v_w_in, v_conf_conv_w,
                              v_conf_conv_b, v_conf_ln_g, v_conf_ln_b, v_sconv_w, v_pool_w, v_pool_scale, v_gmlp_ln_g,
                              v_gmlp_ln_b, v_gmlp_w_s, v_gmlp_b_s, v_w_out, v_ffn2_norm, v_ffn2_w1, v_ffn2_w3,
                              v_ffn2_w2, v_final_norm)))
    _, t, d = x.shape
    depth, _, fs = ffn1_w1.shape
    g = w_in.shape[2]
    wo = w_out.shape[1]
    nh = g // HEAD_DIM
    cs = conf_conv_w.shape[2]
    lay = _MixRows(g, wo)
    me = 4 * lax.axis_index("x") + 2 * lax.axis_index("y") + lax.axis_index("c")

    def rows_bf16(mats):
        return jnp.concatenate([m_.astype(BF16) for m_ in mats], axis=0)

    set_f1 = [rows_bf16([ffn1_w1[l].T, ffn1_w3[l].T, ffn1_w2[l]]) for l in range(depth)]
    set_f2 = [rows_bf16([ffn2_w1[l].T, ffn2_w3[l].T, ffn2_w2[l]]) for l in range(depth)]
    set_mx = [rows_bf16([w_in[l].T, w_out[l]]) for l in range(depth)]
    convs = jnp.concatenate([conf_conv_w, sconv_w], axis=1)
    nconv = convs.size
    conv_bits = jnp.pad(convs.reshape(-1), (0, CONV_ROWS * d // 2 - nconv))
    conv_rows = lax.bitcast_convert_type(conv_bits, BF16).reshape(CONV_ROWS, d)
    w_f1 = [None] * depth
    w_f1[0] = _alone(_Gather(jnp.concatenate([set_f1[0], conv_rows], axis=0)), "gather_first")

    conv_all = lax.bitcast_convert_type(w_f1[0][:, 3 * fs:, :].reshape(N_DEV, CONV_ROWS * d // 2, 2), F32)
    conv_all = conv_all[:, :nconv].reshape(N_DEV, depth, CONF_KERNEL + SHORT_KERNEL, cs)
    conv_all = conv_all.transpose(1, 2, 0, 3).reshape(depth, CONF_KERNEL + SHORT_KERNEL, g)

    def small_inputs(l):
        cw = jnp.pad(conv_all[l, :CONF_KERNEL], ((0, 32 - CONF_KERNEL), (0, 0)))
        sw = jnp.pad(conv_all[l, CONF_KERNEL:], ((0, 8 - SHORT_KERNEL), (0, 0)))
        vecs = jnp.stack([conf_conv_b[l], conf_ln_g[l], conf_ln_b[l], pool_scale[l], gmlp_ln_g[l], gmlp_ln_b[l],
                          jnp.zeros((g,), F32), jnp.zeros((g,), F32)])
        pool_bd = _block_diag(pool_w[l]).astype(BF16)
        low = jnp.tril(jnp.ones((CHUNK, CHUNK), bool))
        ws = jnp.where(low[None], gmlp_w_s[l], 0.0).astype(BF16)
        wcat = ws.transpose(1, 0, 2).reshape(CHUNK, nh * CHUNK)
        wcat_t = ws.transpose(0, 2, 1).reshape(nh * CHUNK, CHUNK)
        bst = jnp.repeat(gmlp_b_s[l].T, HEAD_DIM, axis=1)
        return cw, vecs, sw, pool_bd, wcat, bst, wcat_t

    xs = x[0]
    saved = []
    for l in range(depth):
        sm = small_inputs(l)
        (x1, a1, b1), ((w_f2,), (w_mx,)) = _ffn_fwd(xs, ffn1_norm[l], w_f1[l], fs, f"ffn1_fwd_{l}",
                                                    riders=[_Gather(set_f2[l]), _Gather(set_mx[l])])
        h2, p = _mix_in_fwd(x1, mix_norm[l], w_mx, lay, l, f"mix_in_fwd_{l}")
        x2, mix = _mix_fwd(p, x1, w_mx, lay, l, sm[:6], f"mix_fwd_{l}")
        nxt = [_Gather(set_f1[l + 1])] if l + 1 < depth else []
        (x3, a2, b2), got = _ffn_fwd(x2, ffn2_norm[l], w_f2, fs, f"ffn2_fwd_{l}", riders=nxt)
        if nxt:
            w_f1[l + 1] = got[0][0]
        saved.append((xs, a1, b1, x1, h2, p, mix, x2, a2, b2, sm, w_f2, w_mx))
        xs = x3

    dx, loss_part, dgf = _loss_head(xs, loss_target[0], final_norm, "loss_head")
    loss = lax.psum(loss_part[0, 0], ("x", "y", "c"))

    def small_rows(parts):
        buf, at = _f32_rows(parts, d, None)
        return jnp.pad(buf, ((0, -buf.shape[0] % 16), (0, 0))), at

    small = [None] * depth
    g_f1, g_f2, g_mx = [None] * depth, [None] * depth, [None] * depth
    pending = None
    for l in reversed(range(depth)):
        x0, a1, b1, x1, h2, p, mix, x2, a2, b2, sm, w_f2, w_mx = saved[l]
        ride = [_Exchange(pending)] if pending is not None else []
        (dx2, da, db, u, h, dgn2), got = _ffn_bwd(x2, dx, a2, b2, ffn2_norm[l], w_f2, fs, f"ffn2_bwd_{l}", riders=ride)
        if ride:
            g_f1[l + 1] = got[0][0]
        gset = lax.empty((N_DEV, 3 * fs, d), BF16)
        gset = _wgrad(da, h, gset, fs, 0, 1.0, f"ffn2_dw1_{l}")
        gset = _wgrad(db, h, gset, fs, 1, 1.0, f"ffn2_dw3_{l}")
        gset = _wgrad(u, dx, gset, fs, 2, FFN_RESIDUAL, f"ffn2_dw2_{l}")
        (dp, dcw, dvec, dsw, dpbd, dws, dbs), ((g_f2[l],),) = _mix_bwd(p, dx2, w_mx, lay, l, sm, f"mix_bwd_{l}",
                                                                       riders=[_Exchange(gset)])
        gset = lax.empty((N_DEV, g + wo, d), BF16)
        gset = _wgrad(mix, dx2, gset, wo, lay.wout(l), 1.0, f"dw_out_{l}")
        dx1, dgm = _mix_in_bwd(x1, dx2, dp, mix_norm[l], w_mx, lay, l, f"mix_in_bwd_{l}")
        gset = _wgrad(dp, h2, gset, g, lay.win(l), 1.0, f"dw_in_{l}")
        ride = [_Exchange(gset)]
        if l == 0 and depth > 1:
            early, early_offs = small_rows([a for k in range(1, depth) for a in small[k]] + [dgf[0]])
            ride.append(_Gather(early))
        (dx, da, db, u, h, dgn1), got = _ffn_bwd(x0, dx1, a1, b1, ffn1_norm[l], w_f1[l], fs,
                                                 f"ffn1_bwd_{l}", riders=ride)
        g_mx[l] = got[0][0]
        pg = g // len(POOL_WINDOWS)
        dpool = jnp.stack([dpbd[k * pg:(k + 1) * pg, k * pg:(k + 1) * pg] for k in range(len(POOL_WINDOWS))])
        small[l] = [dgn1[0], dgm[0], dgn2[0], dvec, dcw, dsw, dpool, dws, dbs[:, :nh].T]
        if l > 0:
            gset = lax.empty((N_DEV, 3 * fs, d), BF16)
            gset = _wgrad(da, h, gset, fs, 0, 1.0, f"ffn1_dw1_{l}")
            gset = _wgrad(db, h, gset, fs, 1, 1.0, f"ffn1_dw3_{l}")
            pending = _wgrad(u, dx1, gset, fs, 2, FFN_RESIDUAL, f"ffn1_dw2_{l}")
    grad_x = dx[None]
    early_all = got[1][0] if depth > 1 else None
    late, late_offs = small_rows(small[0] if depth > 1 else small[0] + [dgf[0]])
    one = lambda: lax.empty((N_DEV, fs, d), BF16)
    g_a, ((late_all,),) = _wgrad(da, h, one(), fs, 0, 1.0, "ffn1_dw1_0", riders=[_Gather(late)])
    g_b, ((x_a,),) = _wgrad(db, h, one(), fs, 0, 1.0, "ffn1_dw3_0", riders=[_Exchange(g_a)])
    g_c, ((x_b,),) = _wgrad(u, dx1, one(), fs, 0, FFN_RESIDUAL, "ffn1_dw2_0", riders=[_Exchange(g_b)])
    x_c = _alone(_Exchange(g_c), "exchange_last")

    grads = {}
    s_f1 = [jnp.concatenate([_sum_slots(x_a, "sum_f1a_0"), _sum_slots(x_b, "sum_f1b_0"),
                             _sum_slots(x_c, "sum_f1c_0")], axis=0)]
    s_f1 += [_sum_slots(g_f1[l], f"sum_f1_{l}") for l in range(1, depth)]
    s_f2 = [_sum_slots(g_f2[l], f"sum_f2_{l}") for l in range(depth)]
    s_mx = [_sum_slots(g_mx[l], f"sum_mx_{l}") for l in range(depth)]
    grads['ffn1_w1'] = jnp.stack([s_f1[l][0:fs].T for l in range(depth)])
    grads['ffn1_w3'] = jnp.stack([s_f1[l][fs:2 * fs].T for l in range(depth)])
    grads['ffn1_w2'] = jnp.stack([s_f1[l][2 * fs:] for l in range(depth)])
    grads['ffn2_w1'] = jnp.stack([s_f2[l][0:fs].T for l in range(depth)])
    grads['ffn2_w3'] = jnp.stack([s_f2[l][fs:2 * fs].T for l in range(depth)])
    grads['ffn2_w2'] = jnp.stack([s_f2[l][2 * fs:] for l in range(depth)])
    grads['w_in'] = jnp.stack([s_mx[l][0:g].T for l in range(depth)])
    grads['w_out'] = jnp.stack([s_mx[l][g:] for l in range(depth)])

    per_layer = len(small[0])
    shapes = [(d,), (d,), (d,), (8, g), (32, g), (8, g), (len(POOL_WINDOWS), g // len(POOL_WINDOWS),
              g // len(POOL_WINDOWS)), (nh * CHUNK, CHUNK), (nh, CHUNK)]
    late_sum = _sum_slots(late_all, "sum_small_late")
    early_sum = _sum_slots(early_all, "sum_small_early") if depth > 1 else None
    got = [[_take(late_sum, late_offs[k], shapes[k]) for k in range(per_layer)]]
    got += [[_take(early_sum, early_offs[(l - 1) * per_layer + k], shapes[k]) for k in range(per_layer)]
            for l in range(1, depth)]
    col = lambda k: jnp.stack([got[l][k] for l in range(depth)])
    grads['ffn1_norm'], grads['mix_norm'], grads['ffn2_norm'] = col(0), col(1), col(2)
    dvec_all = col(3)
    for k, n in enumerate(('conf_conv_b', 'conf_ln_g', 'conf_ln_b', 'pool_scale', 'gmlp_ln_g', 'gmlp_ln_b')):
        grads[n] = dvec_all[:, k]
    grads['conf_conv_w'] = lax.dynamic_slice_in_dim(col(4)[:, :CONF_KERNEL], me * cs, cs, axis=2)
    grads['sconv_w'] = lax.dynamic_slice_in_dim(col(5)[:, :SHORT_KERNEL], me * cs, cs, axis=2)
    grads['pool_w'] = col(6)
    grads['gmlp_w_s'] = col(7).reshape(depth, nh, CHUNK, CHUNK)
    grads['gmlp_b_s'] = col(8)
    grads['final_norm'] = (_take(early_sum, early_offs[-1], (d,)) if depth > 1
                           else _take(late_sum, late_offs[-1], (d,)))

    deltas, new_m, new_v = {}, {}, {}
    for n in WEIGHTS:
        deltas[n], new_m[n], new_v[n] = _adamw(w[n], grads[n], mom1[n], mom2[n], f"adamw_{n}")

    return (loss, grad_x, *[grads[n] for n in WEIGHTS], *[deltas[n] for n in WEIGHTS],
            *[new_m[n] for n in WEIGHTS], *[new_v[n] for n in WEIGHTS])
```

```python
import functools

import jax
import jax.numpy as jnp
from jax import lax
from jax.experimental import pallas as pl
from jax.experimental.pallas import tpu as pltpu

F32 = jnp.float32
BF16 = jnp.bfloat16
MESH = pl.DeviceIdType.MESH

N_DEV = 8
EPS = 1e-6
FFN_RESIDUAL = 0.5
CONF_KERNEL = 31
SHORT_KERNEL = 3
POOL_WINDOWS = (2, 4, 8, 16)
CHUNK = 128
HEAD_DIM = 64
N_IN_PIECES = 8
HALO = 32
CONV_ROWS = 16
ADAM_LR = 0.001
ADAM_B1 = 0.9
ADAM_B2 = 0.999
ADAM_EPS = 1e-08
ADAM_WD = 0.01
ADAM_STEP = 10
VMEM_LIMIT_V7X = 56 * 1024 * 1024

WEIGHTS = ['ffn1_norm', 'ffn1_w1', 'ffn1_w3', 'ffn1_w2', 'mix_norm', 'w_in', 'conf_conv_w', 'conf_conv_b',
           'conf_ln_g', 'conf_ln_b', 'sconv_w', 'pool_w', 'pool_scale', 'gmlp_ln_g', 'gmlp_ln_b', 'gmlp_w_s',
           'gmlp_b_s', 'w_out', 'ffn2_norm', 'ffn2_w1', 'ffn2_w3', 'ffn2_w2', 'final_norm']
SHARDED_ROWS = ('ffn1_w2', 'ffn2_w2', 'w_out')
SHARDED_COLS = ('ffn1_w1', 'ffn1_w3', 'ffn2_w1', 'ffn2_w3', 'w_in')
SHARDED_CHAN = ('conf_conv_w', 'sconv_w')


def _params(sem=None):
    return pltpu.CompilerParams(dimension_semantics=sem, vmem_limit_bytes=VMEM_LIMIT_V7X)


def _nn(a, b):
    return jnp.dot(a, b, preferred_element_type=F32)


def _nt(a, b):
    return lax.dot_general(a, b, (((1,), (1,)), ((), ())), preferred_element_type=F32)


def _tn(a, b):
    return lax.dot_general(a, b, (((0,), (0,)), ((), ())), preferred_element_type=F32)


def _sigmoid(x):
    return 0.5 * jnp.tanh(0.5 * x) + 0.5


def _row_tile(n, pref, mult=8):
    t = min(n, pref)
    while n % t or t % mult:
        t -= 1
    return t


def _chunks(n, size):
    out, s = [], 0
    while s < n:
        out.append((s, min(size, n - s)))
        s += size
    return out


def _const(shape):
    return pl.BlockSpec(shape, lambda i: (0,) * len(shape))


def _resident(shape, index):
    return pl.BlockSpec(shape, lambda i: index, pipeline_mode=pl.Buffered(1))


def _wspec(rows, d, blk):
    return _resident((N_DEV, rows, d), (0, blk, 0))


def _layernorm_stats(x):
    mu = jnp.mean(x, axis=-1, keepdims=True)
    xc = x - mu
    r = lax.rsqrt(jnp.mean(xc * xc, axis=-1, keepdims=True) + EPS)
    return xc * r, r


def _layernorm_bwd(dy, g, xhat, r):
    dxh = dy * g
    return r * (dxh - jnp.mean(dxh, axis=-1, keepdims=True) - xhat * jnp.mean(dxh * xhat, axis=-1, keepdims=True))


def _rmsnorm_stats(x):
    r = lax.rsqrt(jnp.mean(x * x, axis=-1, keepdims=True) + EPS)
    return x * r, r


def _rmsnorm_bwd(dh, g, xhat, r):
    dxh = dh * g
    return r * (dxh - xhat * jnp.mean(dxh * xhat, axis=-1, keepdims=True))


class _MixRows:
    def __init__(self, g, wo):
        assert g % wo == 0 and wo % 16 == 0
        self.g, self.wo = g, wo

    def win(self, l):
        return 0

    def wout(self, l):
        return self.g // self.wo


class _Gather:
    def __init__(self, shard):
        self.operands = [shard]
        self.out_shape = [jax.ShapeDtypeStruct((N_DEV,) + shard.shape, shard.dtype)]
        self.scratch = [pltpu.SemaphoreType.DMA((7,)), pltpu.SemaphoreType.DMA((7,)), pltpu.SemaphoreType.DMA]

    def _copies(self, ins, outs, scr, arrivals):
        (x_ref,), (out_ref,), (send_sems, recv_sems, local_sem) = ins, outs, scr
        x, y, c = lax.axis_index("x"), lax.axis_index("y"), lax.axis_index("c")
        me, sibling = (x, y, c), (x, y, 1 - c)
        chips = [(1 - x, y), (x, 1 - y), (1 - x, 1 - y)]

        def slot(px, py, pc):
            return out_ref.at[4 * px + 2 * py + pc]

        def copy(k, block, to, src=None):
            return pltpu.make_async_remote_copy(
                src_ref=slot(*block) if src is None else src, dst_ref=slot(*block),
                send_sem=send_sems.at[k], recv_sem=recv_sems.at[k], device_id=to, device_id_type=MESH)

        mine = pltpu.make_async_copy(x_ref, slot(*me), local_sem)
        first = [copy(0, me, sibling, src=x_ref)]
        first += [copy(1 + j, me, (*chip, c), src=x_ref) for j, chip in enumerate(chips)]
        passed = [copy(4 + j, (*chip, c), sibling) for j, chip in enumerate(chips)]
        if arrivals == "ici":
            landing = [copy(1 + j, (*chip, c), me) for j, chip in enumerate(chips)]
        else:
            landing = [copy(0, sibling, me)] + [copy(4 + j, (*chip, 1 - c), me) for j, chip in enumerate(chips)]
        return mine, first, passed, landing

    def begin(self, i, n, ins, outs, scr):
        mine, first, passed, over_ici = self._copies(ins, outs, scr, "ici")

        @pl.when(i == 0)
        def _():
            mine.start()
            for cp in first:
                cp.start()

        @pl.when(i == (n - 3 if n >= 6 else n // 2))
        def _():
            for landed, onward in zip(over_ici, passed):
                landed.wait_recv()
                onward.start()

    def end(self, i, n, ins, outs, scr):
        mine, first, passed, over_d2d = self._copies(ins, outs, scr, "d2d")

        @pl.when(i == n - 1)
        def _():
            for cp in over_d2d:
                cp.wait_recv()
            for cp in first + passed:
                cp.wait_send()
            mine.wait()


class _Exchange:
    def __init__(self, blocks):
        self.operands = [blocks]
        self.out_shape = [jax.ShapeDtypeStruct(blocks.shape, blocks.dtype)]
        self.scratch = [pltpu.SemaphoreType.DMA((7,)), pltpu.SemaphoreType.DMA((7,)), pltpu.SemaphoreType.DMA]

    def _copies(self, ins, outs, scr, with_arrivals):
        (s_ref,), (r_ref,), (send_sems, recv_sems, local_sem) = ins, outs, scr
        x, y, c = lax.axis_index("x"), lax.axis_index("y"), lax.axis_index("c")
        me = 4 * x + 2 * y + c
        mine = pltpu.make_async_copy(s_ref.at[me], r_ref.at[me], local_sem)
        sends, recvs = [], []
        for k in range(1, N_DEV):
            px = (1 - x) if (k >> 2) & 1 else x
            py = (1 - y) if (k >> 1) & 1 else y
            pc = (1 - c) if k & 1 else c
            peer = 4 * px + 2 * py + pc
            for dst, group in ((r_ref.at[me], sends), (r_ref.at[peer], recvs)):
                if group is sends or with_arrivals:
                    group.append(pltpu.make_async_remote_copy(
                        src_ref=s_ref.at[peer], dst_ref=dst, send_sem=send_sems.at[k - 1],
                        recv_sem=recv_sems.at[k - 1], device_id=(px, py, pc), device_id_type=MESH))
        return mine, sends, recvs

    def begin(self, i, n, ins, outs, scr):
        mine, sends, _ = self._copies(ins, outs, scr, False)

        @pl.when(i == 0)
        def _():
            mine.start()
            for cp in sends:
                cp.start()

    def end(self, i, n, ins, outs, scr):
        mine, sends, recvs = self._copies(ins, outs, scr, True)

        @pl.when(i == n - 1)
        def _():
            for cp in recvs:
                cp.wait_recv()
            for cp in sends:
                cp.wait_send()
            mine.wait()


_ANY = pl.BlockSpec(memory_space=pl.ANY)


def _call(body, name, steps, operands, in_specs, out_shape, out_specs, scratch=(), riders=(), aliases=None):
    n_in, n_out, n_scr = len(operands), len(out_shape), len(scratch)

    def full(*refs):
        pos = [0]

        def take(k):
            pos[0] += k
            return refs[pos[0] - k:pos[0]]

        h_in, r_in = take(n_in), [take(len(r.operands)) for r in riders]
        h_out, r_out = take(n_out), [take(len(r.out_shape)) for r in riders]
        h_scr, r_scr = take(n_scr), [take(len(r.scratch)) for r in riders]
        i = pl.program_id(0)
        for r, a, b, c in zip(riders, r_in, r_out, r_scr):
            r.begin(i, steps, a, b, c)
        body(*h_in, *h_out, *h_scr)
        for r, a, b, c in zip(riders, r_in, r_out, r_scr):
            r.end(i, steps, a, b, c)

    outs = pl.pallas_call(
        full, name=name, grid=(steps,),
        out_shape=tuple(out_shape) + tuple(s for r in riders for s in r.out_shape),
        in_specs=list(in_specs) + [_ANY for r in riders for _ in r.operands],
        out_specs=tuple(out_specs) + tuple(_ANY for r in riders for _ in r.out_shape),
        scratch_shapes=list(scratch) + [s for r in riders for s in r.scratch],
        input_output_aliases=aliases or {},
        compiler_params=_params(("arbitrary",)),
    )(*operands, *(o for r in riders for o in r.operands))
    host, rest = outs[:n_out], list(outs[n_out:])
    rides = []
    for r in riders:
        rides.append(rest[:len(r.out_shape)])
        rest = rest[len(r.out_shape):]
    return host, rides


def _alone(rider, name):
    return _call(lambda: None, name, 1, [], [], [], [], riders=[rider])[1][0][0]


def _sum_slots(slots, name):
    _, rows, cols = slots.shape
    tr = _row_tile(rows, 256, 16)

    def body(s_ref, o_ref):
        acc = s_ref[0].astype(F32)
        for j in range(1, N_DEV):
            acc = acc + s_ref[j].astype(F32)
        o_ref[...] = acc

    return pl.pallas_call(
        body, name=name, grid=(rows // tr,),
        out_shape=jax.ShapeDtypeStruct((rows, cols), F32),
        in_specs=[pl.BlockSpec((N_DEV, tr, cols), lambda i: (0, i, 0))],
        out_specs=pl.BlockSpec((tr, cols), lambda i: (i, 0)),
        compiler_params=_params(("parallel",)),
    )(slots)


def _ffn_fwd(x, gain, wset, fs, name, riders=()):
    t, d = x.shape
    f = N_DEV * fs
    tm = _row_tile(t, 512, 128)
    cols = _chunks(f, 1024)

    def body(x_ref, g_ref, w1_ref, w3_ref, w2_ref, y_ref, a_ref, b_ref, u_ref):
        xv = x_ref[...]
        xhat, _ = _rmsnorm_stats(xv)
        h = (xhat * g_ref[...]).astype(BF16)
        w1 = w1_ref[...].reshape(f, d)
        w3 = w3_ref[...].reshape(f, d)
        for s, n in cols:
            a = _nt(h, w1[s:s + n, :])
            b = _nt(h, w3[s:s + n, :])
            a_ref[:, s:s + n] = a.astype(BF16)
            b_ref[:, s:s + n] = b.astype(BF16)
            u_ref[:, s:s + n] = (a * _sigmoid(a) * b).astype(BF16)
        y_ref[...] = xv + FFN_RESIDUAL * _nn(u_ref[...], w2_ref[...].reshape(f, d))

    return _call(
        body, name, t // tm, [x, gain.reshape(1, d), wset, wset, wset],
        [pl.BlockSpec((tm, d), lambda i: (i, 0)), _const((1, d)),
         _wspec(fs, d, 0), _wspec(fs, d, 1), _wspec(fs, d, 2)],
        [jax.ShapeDtypeStruct((t, d), F32), jax.ShapeDtypeStruct((t, f), BF16), jax.ShapeDtypeStruct((t, f), BF16)],
        [pl.BlockSpec((tm, d), lambda i: (i, 0)), pl.BlockSpec((tm, f), lambda i: (i, 0)),
         pl.BlockSpec((tm, f), lambda i: (i, 0))],
        scratch=[pltpu.VMEM((tm, f), BF16)], riders=riders)


def _ffn_bwd(x, dy, a, b, gain, wset, fs, name, riders=()):
    t, d = x.shape
    f = N_DEV * fs
    tm = _row_tile(t, 256, 128)
    cols = _chunks(f, 1024)

    def body(x_ref, dy_ref, a_ref, b_ref, g_ref, w1_ref, w3_ref, w2_ref,
             dx_ref, da_ref, db_ref, u_ref, h_ref, dg_ref):
        @pl.when(pl.program_id(0) == 0)
        def _():
            dg_ref[...] = jnp.zeros_like(dg_ref)

        xv, dyv, g = x_ref[...], dy_ref[...], g_ref[...]
        xhat, r = _rmsnorm_stats(xv)
        h_ref[...] = (xhat * g).astype(BF16)
        dyb = (FFN_RESIDUAL * dyv).astype(BF16)
        w1 = w1_ref[...].reshape(f, d)
        w3 = w3_ref[...].reshape(f, d)
        w2 = w2_ref[...].reshape(f, d)
        dh = jnp.zeros((tm, d), F32)
        for s, n in cols:
            du = _nt(dyb, w2[s:s + n, :])
            av = a_ref[:, s:s + n].astype(F32)
            bv = b_ref[:, s:s + n].astype(F32)
            sig = _sigmoid(av)
            sa = av * sig
            u_ref[:, s:s + n] = (sa * bv).astype(BF16)
            da = (du * bv * (sig + sa * (1.0 - sig))).astype(BF16)
            db = (du * sa).astype(BF16)
            da_ref[:, s:s + n] = da
            db_ref[:, s:s + n] = db
            dh = dh + _nn(da, w1[s:s + n, :]) + _nn(db, w3[s:s + n, :])
        dx_ref[...] = dyv + _rmsnorm_bwd(dh, g, xhat, r)
        dg_ref[0:1, :] += jnp.sum(dh * xhat, axis=0, keepdims=True)

    big = jax.ShapeDtypeStruct((t, f), BF16)
    row = lambda w: pl.BlockSpec((tm, w), lambda i: (i, 0))
    return _call(
        body, name, t // tm, [x, dy, a, b, gain.reshape(1, d), wset, wset, wset],
        [row(d), row(d), row(f), row(f), _const((1, d)), _wspec(fs, d, 0), _wspec(fs, d, 1), _wspec(fs, d, 2)],
        [jax.ShapeDtypeStruct((t, d), F32), big, big, big, jax.ShapeDtypeStruct((t, d), BF16),
         jax.ShapeDtypeStruct((8, d), F32)],
        [row(d), row(f), row(f), row(f), row(d), _const((8, d))], riders=riders)


def _wgrad(lhs, rhs, gbuf, rows, blk, scale, name, riders=()):
    t, m = lhs.shape
    d = rhs.shape[1]
    bt = _row_tile(t, 1024, 128)
    nk = t // bt
    assert m == N_DEV * rows

    def body(a_ref, b_ref, g_in, o_ref, acc_ref):
        del g_in
        k = pl.program_id(0)

        @pl.when(k == 0)
        def _():
            acc_ref[...] = jnp.zeros_like(acc_ref)

        acc_ref[...] += _tn(a_ref[...], b_ref[...].astype(BF16))

        @pl.when(k == nk - 1)
        def _():
            o_ref[...] = (scale * acc_ref[...]).astype(BF16).reshape(N_DEV, rows, d)

    (out,), rides = _call(
        body, name, nk, [lhs, rhs, gbuf],
        [pl.BlockSpec((bt, m), lambda k: (k, 0)), pl.BlockSpec((bt, d), lambda k: (k, 0)), _ANY],
        [jax.ShapeDtypeStruct(gbuf.shape, gbuf.dtype)], [pl.BlockSpec((N_DEV, rows, d), lambda k: (0, blk, 0))],
        scratch=[pltpu.VMEM((m, d), F32)], riders=riders, aliases={2: 0})
    return (out, rides) if riders else out


def _head_masks(rows, g):
    lane = lax.broadcasted_iota(jnp.int32, (rows, g), 1)
    return [(lane >= h * HEAD_DIM) & (lane < (h + 1) * HEAD_DIM) for h in range(g // HEAD_DIM)]


def _pool_windows(rows, g):
    lane = lax.broadcasted_iota(jnp.int32, (rows, g), 1)
    pg = g // len(POOL_WINDOWS)
    w = jnp.full((rows, g), float(POOL_WINDOWS[-1]), F32)
    for k in range(len(POOL_WINDOWS) - 2, -1, -1):
        w = jnp.where(lane < (k + 1) * pg, float(POOL_WINDOWS[k]), w)
    return w


def _window_sum(ref, tmp, first, rows, wl, direction):
    total = ref.shape[0]
    src, out, shift = ref, None, 1
    for k, w in enumerate(POOL_WINDOWS):
        assert w == 2 * shift and 8 * (k + 1) <= HALO
        if k + 1 < len(POOL_WINDOWS):
            n = total - 8 * (k + 1)
            lo = 8 * (k + 1) if direction < 0 else 0
            dst = tmp.at[k % 2]
            dst[pl.ds(lo, n), :] = src[pl.ds(lo, n), :] + src[pl.ds(lo + direction * shift, n), :]
            src = dst
            level = dst[pl.ds(first, rows), :]
        else:
            level = src[pl.ds(first, rows), :] + src[pl.ds(first + direction * shift, rows), :]
        out = level if out is None else jnp.where(wl >= float(w), level, out)
        shift = w
    return out


def _shifted_copies(src_ref, dst_ref):
    rows = dst_ref.shape[1]
    for b in range(1, 8):
        dst_ref[b - 1] = src_ref[pl.ds(b, rows), :]


def _window(src_ref, shifted_ref, off, rows):
    b = off % 8
    if b == 0:
        return src_ref[pl.ds(off, rows), :]
    return shifted_ref[b - 1, pl.ds(off - b, rows), :]


def _pool_divisor(first_row, rows, g):
    pos1 = (lax.broadcasted_iota(jnp.int32, (rows, g), 0) + first_row + 1).astype(F32)
    return jnp.minimum(pos1, _pool_windows(rows, g))


def _spatial_mix(vq, wcat_ref, masks):
    vstack = jnp.concatenate([jnp.where(m, vq, jnp.zeros_like(vq)) for m in masks], axis=0)
    return _nn(wcat_ref[...], vstack)


def _mix_fwd(x, gain, wbuf, lay, l, small, name):
    t, d = x.shape
    g = lay.g
    di = N_DEV * g
    tm = _row_tile(t, 256, CHUNK)
    conv_w, vecs, sconv_w, pool_bd, wcat, bst = small

    def body(x_ref, g_ref, wi_ref, wo_ref, cw_ref, vec_ref, sw_ref, pbd_ref, wcat_ref, bst_ref,
             y_ref, mix_ref, h_ref, pm_ref, pp_ref, ea_ref, eb_ref, ec_ref, sh_ref, tmp_ref):
        i = pl.program_id(0)

        @pl.when(i == 0)
        def _():
            pp_ref[...] = jnp.zeros_like(pp_ref)

        xhat, _ = _rmsnorm_stats(x_ref[...])
        h = (xhat * g_ref[...]).astype(BF16)
        h_ref[...] = h
        pm_ref[...] = _nt(h, wi_ref[...].reshape(di, d)).astype(BF16)

        def prev(j):
            return pp_ref[:, j * g:(j + 1) * g].astype(F32)

        def main(j):
            return pm_ref[:, j * g:(j + 1) * g].astype(F32)

        vec = vec_ref[...]
        ea_ref[0:HALO, :] = prev(0) * _sigmoid(prev(1))
        ea_ref[HALO:, :] = main(0) * _sigmoid(main(1))
        _shifted_copies(ea_ref, sh_ref)
        cw = cw_ref[...]
        c = jnp.zeros((tm, g), F32) + vec[0:1, :]
        for k in range(CONF_KERNEL):
            c = c + cw[k:k + 1, :] * _window(ea_ref, sh_ref, HALO - (CONF_KERNEL - 1) + k, tm)
        chat, _ = _layernorm_stats(c)
        ln = chat * vec[1:2, :] + vec[2:3, :]
        mix_ref[:, 0:g] = (ln * _sigmoid(ln)).astype(BF16)
        eb_ref[0:HALO, :] = prev(3) * prev(4)
        eb_ref[HALO:, :] = main(3) * main(4)
        sw = sw_ref[...]
        cz = jnp.zeros((tm, g), F32)
        for k in range(SHORT_KERNEL):
            cz = cz + sw[k:k + 1, :] * eb_ref[pl.ds(HALO - (SHORT_KERNEL - 1) + k, tm), :]
        mix_ref[:, g:2 * g] = (main(2) * cz).astype(BF16)
        ec_ref[0:HALO, :] = prev(5)
        xp = main(5)
        ec_ref[HALO:, :] = xp
        wl = _pool_windows(tm, g)
        mean = _window_sum(ec_ref, tmp_ref, HALO, tm, wl, -1) / _pool_divisor(i * tm, tm, g)
        yc = _nn((mean - xp).astype(BF16), pbd_ref[...])
        mix_ref[:, 2 * g:3 * g] = (yc * vec[3:4, :]).astype(BF16)
        vhat, _ = _layernorm_stats(main(7))
        v = (vhat * vec[4:5, :] + vec[5:6, :]).astype(BF16)
        gu = main(6)
        masks = _head_masks(CHUNK, g)
        for q in range(tm // CHUNK):
            rs = slice(q * CHUNK, (q + 1) * CHUNK)
            mixed = _spatial_mix(v[rs, :], wcat_ref, masks) + bst_ref[...]
            mix_ref[rs, 3 * g:4 * g] = (gu[rs, :] * mixed).astype(BF16)
        y_ref[...] = x_ref[...] + _nn(mix_ref[...], wo_ref[...].reshape(d, d))
        pp_ref[...] = pm_ref[tm - HALO:, :]

    row = lambda w: pl.BlockSpec((tm, w), lambda i: (i, 0))
    return pl.pallas_call(
        body, name=name, grid=(t // tm,),
        out_shape=(jax.ShapeDtypeStruct((t, d), F32), jax.ShapeDtypeStruct((t, 4 * g), BF16),
                   jax.ShapeDtypeStruct((t, d), BF16), jax.ShapeDtypeStruct((t, di), BF16)),
        in_specs=[row(d), _const((1, d)), _wspec(g, d, lay.win(l)), _wspec(lay.wo, d, lay.wout(l)),
                  _const(conv_w.shape), _const(vecs.shape), _const(sconv_w.shape), _const(pool_bd.shape),
                  _const(wcat.shape), _const(bst.shape)],
        out_specs=(row(d), row(4 * g), row(d), row(di)),
        scratch_shapes=[pltpu.VMEM((HALO, di), BF16)] + [pltpu.VMEM((HALO + tm, g), F32)] * 3
        + [pltpu.VMEM((7, HALO + tm - 8, g), F32), pltpu.VMEM((2, HALO + tm, g), F32)],
        compiler_params=_params(("arbitrary",)),
    )(x, gain.reshape(1, d), wbuf, wbuf, conv_w, vecs, sconv_w, pool_bd, wcat, bst)


def _mix_bwd(p, dy, x, gain, wbuf, lay, l, small, name, riders=()):
    t, d = dy.shape
    g = lay.g
    di = N_DEV * g
    tm = _row_tile(t, 256, CHUNK)
    nt = t // tm
    nh = g // HEAD_DIM
    per = tm // HALO
    ext = tm + HALO
    conv_w, vecs, sconv_w, pool_bd, wcat, bst, wcat_t = small

    def body(pp_ref, pm_ref, pn_ref, dy_ref, dyn_ref, x_ref, g_ref, wi_ref, wo_ref, cw_ref, vec_ref, sw_ref, pbd_ref,
             wcat_ref, bst_ref, wcatt_ref, dp_ref, dcw_ref, dvec_ref, dsw_ref, dpbd_ref, dws_ref, dbs_ref,
             dx_ref, dg_ref, ea_ref, eb_ref, ec_ref, fa_ref, fb_ref, fc_ref, sh_ref, sf_ref, tmp_ref):
        i = pl.program_id(0)
        seen = (i > 0).astype(F32)
        more = (i < nt - 1).astype(F32)

        @pl.when(i == 0)
        def _():
            for ref in (dcw_ref, dvec_ref, dsw_ref, dpbd_ref, dws_ref, dbs_ref, dg_ref):
                ref[...] = jnp.zeros_like(ref)

        def prev(j):
            return pp_ref[:, j * g:(j + 1) * g].astype(F32) * seen

        def main(j):
            return pm_ref[:, j * g:(j + 1) * g].astype(F32)

        def nxt(j):
            return pn_ref[:, j * g:(j + 1) * g].astype(F32)

        def colsum(v):
            return jnp.sum(v, axis=0, keepdims=True)

        vec = vec_ref[...]
        wo = wo_ref[...].reshape(d, d)
        dmix = jnp.concatenate([_nt(dy_ref[...].astype(BF16), wo),
                                _nt((dyn_ref[...] * more).astype(BF16), wo)], axis=0)

        val, gate = main(0), main(1)
        sg = _sigmoid(gate)
        ea_ref[0:HALO, :] = prev(0) * _sigmoid(prev(1))
        ea_ref[HALO:ext, :] = val * sg
        ea_ref[ext:, :] = nxt(0) * _sigmoid(nxt(1))
        _shifted_copies(ea_ref, sh_ref)
        cw = cw_ref[...]
        c = jnp.zeros((ext, g), F32) + vec[0:1, :]
        for k in range(CONF_KERNEL):
            c = c + cw[k:k + 1, :] * _window(ea_ref, sh_ref, HALO - (CONF_KERNEL - 1) + k, ext)
        chat, r = _layernorm_stats(c)
        ln = chat * vec[1:2, :] + vec[2:3, :]
        sl = _sigmoid(ln)
        dln = dmix[:, 0:g] * (sl * (1.0 + ln * (1.0 - sl)))
        dvec_ref[1:2, :] += colsum(dln[0:tm] * chat[0:tm])
        dvec_ref[2:3, :] += colsum(dln[0:tm])
        dc = _layernorm_bwd(dln, vec[1:2, :], chat, r)
        dvec_ref[0:1, :] += colsum(dc[0:tm])
        fa_ref[...] = dc
        _shifted_copies(fa_ref, sf_ref)
        dcm = dc[0:tm]
        dy0 = jnp.zeros((tm, g), F32)
        for k in range(CONF_KERNEL):
            dy0 = dy0 + cw[k:k + 1, :] * _window(fa_ref, sf_ref, CONF_KERNEL - 1 - k, tm)
            dcw_ref[k:k + 1, :] += colsum(dcm * _window(ea_ref, sh_ref, HALO - (CONF_KERNEL - 1) + k, tm))
        dp_ref[:, 0:g] = (dy0 * sg).astype(BF16)
        dp_ref[:, g:2 * g] = (dy0 * val * sg * (1.0 - sg)).astype(BF16)

        sb, sc, sx = main(2), main(3), main(4)
        eb_ref[0:HALO, :] = prev(3) * prev(4)
        eb_ref[HALO:ext, :] = sc * sx
        sw = sw_ref[...]
        cz = jnp.zeros((tm, g), F32)
        for k in range(SHORT_KERNEL):
            cz = cz + sw[k:k + 1, :] * eb_ref[pl.ds(HALO - (SHORT_KERNEL - 1) + k, tm), :]
        dob = dmix[:, g:2 * g]
        dp_ref[:, 2 * g:3 * g] = (dob[0:tm] * cz).astype(BF16)
        fb_ref[0:tm, :] = dob[0:tm] * sb
        fb_ref[tm:, :] = dob[tm:] * nxt(2)
        dczm = fb_ref[0:tm, :]
        dz = jnp.zeros((tm, g), F32)
        for k in range(SHORT_KERNEL):
            dz = dz + sw[k:k + 1, :] * fb_ref[pl.ds(SHORT_KERNEL - 1 - k, tm), :]
            dsw_ref[k:k + 1, :] += colsum(dczm * eb_ref[pl.ds(HALO - (SHORT_KERNEL - 1) + k, tm), :])
        dp_ref[:, 3 * g:4 * g] = (dz * sx).astype(BF16)
        dp_ref[:, 4 * g:5 * g] = (dz * sc).astype(BF16)

        xp = main(5)
        ec_ref[0:HALO, :] = prev(5)
        ec_ref[HALO:, :] = xp
        wl = _pool_windows(tm, g)
        dpool = ((_window_sum(ec_ref, tmp_ref, HALO, tm, wl, -1) / _pool_divisor(i * tm, tm, g)) - xp).astype(BF16)
        pbd = pbd_ref[...]
        yc = _nn(dpool, pbd)
        doc = dmix[:, 2 * g:3 * g]
        dvec_ref[3:4, :] += colsum(doc[0:tm] * yc)
        dyc = (doc * vec[3:4, :]).astype(BF16)
        dpbd_ref[...] += _tn(dpool, dyc[0:tm])
        dd = _nt(dyc, pbd)
        fc_ref[...] = dd / _pool_divisor(i * tm, ext, g)
        dp_ref[:, 5 * g:6 * g] = (_window_sum(fc_ref, tmp_ref, 0, tm, wl, 1) - dd[0:tm]).astype(BF16)

        gv = main(7)
        vhat, rv = _layernorm_stats(gv)
        v = (vhat * vec[4:5, :] + vec[5:6, :]).astype(BF16)
        gu = main(6)
        dod = dmix[0:tm, 3 * g:4 * g]
        masks = _head_masks(CHUNK, g)
        tril = (lax.broadcasted_iota(jnp.int32, (CHUNK, CHUNK), 1)
                <= lax.broadcasted_iota(jnp.int32, (CHUNK, CHUNK), 0)).astype(F32)
        tril = jnp.concatenate([tril] * nh, axis=0)
        head_col = lax.broadcasted_iota(jnp.int32, (CHUNK, CHUNK), 1)
        dvs = []
        for q in range(tm // CHUNK):
            rs = slice(q * CHUNK, (q + 1) * CHUNK)
            vq = v[rs, :]
            mixed = _spatial_mix(vq, wcat_ref, masks) + bst_ref[...]
            dp_ref[rs, 6 * g:7 * g] = (dod[rs, :] * mixed).astype(BF16)
            dmixed = dod[rs, :] * gu[rs, :]
            dmb = dmixed.astype(BF16)
            for h, m in enumerate(masks):
                hs = jnp.sum(jnp.where(m, dmixed, 0.0), axis=-1, keepdims=True)
                dbs_ref[...] += jnp.where(head_col == h, hs, 0.0)
            dmstack = jnp.concatenate([jnp.where(m, dmb, jnp.zeros_like(dmb)) for m in masks], axis=0)
            dws_ref[...] += _nt(dmstack, vq) * tril
            back = _nn(wcatt_ref[...], dmb)
            dv = jnp.zeros((CHUNK, g), F32)
            for h, m in enumerate(masks):
                dv = jnp.where(m, back[h * CHUNK:(h + 1) * CHUNK, :], dv)
            dvs.append(dv)
        dv = jnp.concatenate(dvs, axis=0)
        dvec_ref[4:5, :] += colsum(dv * vhat)
        dvec_ref[5:6, :] += colsum(dv)
        dp_ref[:, 7 * g:8 * g] = _layernorm_bwd(dv, vec[4:5, :], vhat, rv).astype(BF16)

        xhat, rx = _rmsnorm_stats(x_ref[...])
        dh = _nn(dp_ref[...], wi_ref[...].reshape(di, d))
        dx_ref[...] = dy_ref[...] + _rmsnorm_bwd(dh, g_ref[...], xhat, rx)
        dg_ref[0:1, :] += colsum(dh * xhat)

    halo_prev = pl.BlockSpec((HALO, di), lambda i: (jnp.maximum(i * per - 1, 0), 0))
    halo_next = lambda w: pl.BlockSpec((HALO, w), lambda i: (jnp.minimum((i + 1) * per, t // HALO - 1), 0))
    row = lambda w: pl.BlockSpec((tm, w), lambda i: (i, 0))
    outs = (jax.ShapeDtypeStruct((t, di), BF16), jax.ShapeDtypeStruct(conv_w.shape, F32),
            jax.ShapeDtypeStruct(vecs.shape, F32), jax.ShapeDtypeStruct(sconv_w.shape, F32),
            jax.ShapeDtypeStruct((g, g), F32), jax.ShapeDtypeStruct((nh * CHUNK, CHUNK), F32),
            jax.ShapeDtypeStruct((CHUNK, CHUNK), F32), jax.ShapeDtypeStruct((t, d), F32),
            jax.ShapeDtypeStruct((8, d), F32))
    return _call(
        body, name, nt, [p, p, p, dy, dy, x, gain.reshape(1, d), wbuf, wbuf, conv_w, vecs, sconv_w, pool_bd, wcat,
                         bst, wcat_t],
        [halo_prev, row(di), halo_next(di), row(d), halo_next(d), row(d), _const((1, d)),
         _wspec(g, d, lay.win(l)), _wspec(lay.wo, d, lay.wout(l)),
         _const(conv_w.shape), _const(vecs.shape), _const(sconv_w.shape), _const(pool_bd.shape),
         _const(wcat.shape), _const(bst.shape), _const(wcat_t.shape)],
        list(outs), [row(di)] + [_const(o.shape) for o in outs[1:7]] + [row(d), _const((8, d))],
        scratch=[pltpu.VMEM((HALO + tm + HALO, g), F32), pltpu.VMEM((HALO + tm, g), F32),
                 pltpu.VMEM((HALO + tm, g), F32), pltpu.VMEM((ext, g), F32), pltpu.VMEM((ext, g), F32),
                 pltpu.VMEM((ext, g), F32), pltpu.VMEM((7, HALO + tm + HALO - 8, g), F32),
                 pltpu.VMEM((7, ext - 8, g), F32), pltpu.VMEM((2, ext, g), F32)],
        riders=riders)


def _loss_head(x, target, gain, name):
    t, d = x.shape
    tm = _row_tile(t, 512, 8)

    def body(x_ref, t_ref, g_ref, dx_ref, loss_ref, dg_ref):
        @pl.when(pl.program_id(0) == 0)
        def _():
            loss_ref[...] = jnp.zeros_like(loss_ref)
            dg_ref[...] = jnp.zeros_like(dg_ref)

        g = g_ref[...]
        xhat, r = _rmsnorm_stats(x_ref[...])
        err = xhat * g - t_ref[...]
        loss_ref[...] += 0.5 * jnp.sum(jnp.mean(err * err, axis=-1, keepdims=True))
        dyv = err * (1.0 / d)
        dx_ref[...] = _rmsnorm_bwd(dyv, g, xhat, r)
        dg_ref[0:1, :] += jnp.sum(dyv * xhat, axis=0, keepdims=True)

    row = pl.BlockSpec((tm, d), lambda i: (i, 0))
    return pl.pallas_call(
        body, name=name, grid=(t // tm,),
        out_shape=(jax.ShapeDtypeStruct((t, d), F32), jax.ShapeDtypeStruct((8, 128), F32),
                   jax.ShapeDtypeStruct((8, d), F32)),
        in_specs=[row, row, _const((1, d))],
        out_specs=(row, _const((8, 128)), _const((8, d))),
        compiler_params=_params(("arbitrary",)),
    )(x, target, gain.reshape(1, d))


def _adamw(w, grad, m, v, name):
    shape = w.shape
    cols = shape[-1]
    rows = w.size // cols
    tr = _row_tile(rows, 512, 8) if rows > 1024 else rows

    def body(w_ref, g_ref, m_ref, v_ref, d_ref, nm_ref, nv_ref):
        gv = g_ref[...]
        nm = ADAM_B1 * m_ref[...] + (1.0 - ADAM_B1) * gv
        nv = ADAM_B2 * v_ref[...] + (1.0 - ADAM_B2) * (gv * gv)
        m_hat = nm / (1.0 - ADAM_B1 ** ADAM_STEP)
        v_hat = nv / (1.0 - ADAM_B2 ** ADAM_STEP)
        d_ref[...] = -ADAM_LR * (m_hat / (jnp.sqrt(v_hat) + ADAM_EPS) + ADAM_WD * w_ref[...])
        nm_ref[...] = nm
        nv_ref[...] = nv

    spec = pl.BlockSpec((tr, cols), lambda i: (i, 0))
    flat = jax.ShapeDtypeStruct((rows, cols), F32)
    outs = pl.pallas_call(
        body, name=name, grid=(rows // tr,),
        out_shape=(flat, flat, flat), in_specs=[spec] * 4, out_specs=(spec,) * 3,
        compiler_params=_params(("parallel",)),
    )(*(a.reshape(rows, cols) for a in (w, grad, m, v)))
    return tuple(o.reshape(shape) for o in outs)


def _f32_rows(parts, d, rows):
    out, offs, at = [], [], 0
    for a in parts:
        n = -(-a.size // d)
        out.append(jnp.pad(a.reshape(-1), (0, n * d - a.size)).reshape(n, d))
        offs.append(at)
        at += n
    total = at if rows is None else rows
    assert at <= total
    if total > at:
        out.append(jnp.zeros((total - at, d), F32))
    return jnp.concatenate(out, axis=0), offs


def _take(buf, off, shape):
    n = 1
    for s in shape:
        n *= s
    d = buf.shape[1]
    return buf[off:off + -(-n // d)].reshape(-1)[:n].reshape(shape)


def _block_diag(blocks):
    n, k, _ = blocks.shape
    out = jnp.zeros((n * k, n * k), blocks.dtype)
    for i in range(n):
        out = lax.dynamic_update_slice(out, blocks[i], (i * k, i * k))
    return out


def kernel(x, ffn1_norm, ffn1_w1, ffn1_w3, ffn1_w2, mix_norm, w_in, conf_conv_w, conf_conv_b, conf_ln_g, conf_ln_b, sconv_w, pool_w, pool_scale, gmlp_ln_g, gmlp_ln_b, gmlp_w_s, gmlp_b_s, w_out, ffn2_norm, ffn2_w1, ffn2_w3, ffn2_w2, final_norm, loss_target, m_ffn1_norm, m_ffn1_w1, m_ffn1_w3, m_ffn1_w2, m_mix_norm, m_w_in, m_conf_conv_w, m_conf_conv_b, m_conf_ln_g, m_conf_ln_b, m_sconv_w, m_pool_w, m_pool_scale, m_gmlp_ln_g, m_gmlp_ln_b, m_gmlp_w_s, m_gmlp_b_s, m_w_out, m_ffn2_norm, m_ffn2_w1, m_ffn2_w3, m_ffn2_w2, m_final_norm, v_ffn1_norm, v_ffn1_w1, v_ffn1_w3, v_ffn1_w2, v_mix_norm, v_w_in, v_conf_conv_w, v_conf_conv_b, v_conf_ln_g, v_conf_ln_b, v_sconv_w, v_pool_w, v_pool_scale, v_gmlp_ln_g, v_gmlp_ln_b, v_gmlp_w_s, v_gmlp_b_s, v_w_out, v_ffn2_norm, v_ffn2_w1, v_ffn2_w3, v_ffn2_w2, v_final_norm):
    w = dict(zip(WEIGHTS, (ffn1_norm, ffn1_w1, ffn1_w3, ffn1_w2, mix_norm, w_in, conf_conv_w, conf_conv_b, conf_ln_g,
                           conf_ln_b, sconv_w, pool_w, pool_scale, gmlp_ln_g, gmlp_ln_b, gmlp_w_s, gmlp_b_s, w_out,
                           ffn2_norm, ffn2_w1, ffn2_w3, ffn2_w2, final_norm)))
    mom1 = dict(zip(WEIGHTS, (m_ffn1_norm, m_ffn1_w1, m_ffn1_w3, m_ffn1_w2, m_mix_norm, m_w_in, m_conf_conv_w,
                              m_conf_conv_b, m_conf_ln_g, m_conf_ln_b, m_sconv_w, m_pool_w, m_pool_scale, m_gmlp_ln_g,
                              m_gmlp_ln_b, m_gmlp_w_s, m_gmlp_b_s, m_w_out, m_ffn2_norm, m_ffn2_w1, m_ffn2_w3,
                              m_ffn2_w2, m_final_norm)))
    mom2 = dict(zip(WEIGHTS, (v_ffn1_norm, v_ffn1_w1, v_ffn1_w3, v_ffn1_w2, v_mix_norm, v_w_in, v_conf_conv_w,
                              v_conf_conv_b, v_conf_ln_g, v_conf_ln_b, v_sconv_w, v_pool_w, v_pool_scale, v_gmlp_ln_g,
                              v_gmlp_ln_b, v_gmlp_w_s, v_gmlp_b_s, v_w_out, v_ffn2_norm, v_ffn2_w1, v_ffn2_w3,
                              v_ffn2_w2, v_final_norm)))
    _, t, d = x.shape
    depth, _, fs = ffn1_w1.shape
    g = w_in.shape[2]
    wo = w_out.shape[1]
    nh = g // HEAD_DIM
    cs = conf_conv_w.shape[2]
    lay = _MixRows(g, wo)
    me = 4 * lax.axis_index("x") + 2 * lax.axis_index("y") + lax.axis_index("c")

    def rows_bf16(mats):
        return jnp.concatenate([m_.astype(BF16) for m_ in mats], axis=0)

    set_f1 = [rows_bf16([ffn1_w1[l].T, ffn1_w3[l].T, ffn1_w2[l]]) for l in range(depth)]
    set_f2 = [rows_bf16([ffn2_w1[l].T, ffn2_w3[l].T, ffn2_w2[l]]) for l in range(depth)]
    set_mx = [rows_bf16([w_in[l].T, w_out[l]]) for l in range(depth)]
    convs = jnp.concatenate([conf_conv_w, sconv_w], axis=1)
    nconv = convs.size
    conv_bits = jnp.pad(convs.reshape(-1), (0, CONV_ROWS * d // 2 - nconv))
    conv_rows = lax.bitcast_convert_type(conv_bits, BF16).reshape(CONV_ROWS, d)
    w_f1 = [None] * depth
    w_f1[0] = _alone(_Gather(jnp.concatenate([set_f1[0], conv_rows], axis=0)), "gather_first")

    conv_all = lax.bitcast_convert_type(w_f1[0][:, 3 * fs:, :].reshape(N_DEV, CONV_ROWS * d // 2, 2), F32)
    conv_all = conv_all[:, :nconv].reshape(N_DEV, depth, CONF_KERNEL + SHORT_KERNEL, cs)
    conv_all = conv_all.transpose(1, 2, 0, 3).reshape(depth, CONF_KERNEL + SHORT_KERNEL, g)

    def small_inputs(l):
        cw = jnp.pad(conv_all[l, :CONF_KERNEL], ((0, 32 - CONF_KERNEL), (0, 0)))
        sw = jnp.pad(conv_all[l, CONF_KERNEL:], ((0, 8 - SHORT_KERNEL), (0, 0)))
        vecs = jnp.stack([conf_conv_b[l], conf_ln_g[l], conf_ln_b[l], pool_scale[l], gmlp_ln_g[l], gmlp_ln_b[l],
                          jnp.zeros((g,), F32), jnp.zeros((g,), F32)])
        pool_bd = _block_diag(pool_w[l]).astype(BF16)
        low = jnp.tril(jnp.ones((CHUNK, CHUNK), bool))
        ws = jnp.where(low[None], gmlp_w_s[l], 0.0).astype(BF16)
        wcat = ws.transpose(1, 0, 2).reshape(CHUNK, nh * CHUNK)
        wcat_t = ws.transpose(0, 2, 1).reshape(nh * CHUNK, CHUNK)
        bst = jnp.repeat(gmlp_b_s[l].T, HEAD_DIM, axis=1)
        return cw, vecs, sw, pool_bd, wcat, bst, wcat_t

    xs = x[0]
    saved = []
    for l in range(depth):
        sm = small_inputs(l)
        (x1, a1, b1), ((w_f2,), (w_mx,)) = _ffn_fwd(xs, ffn1_norm[l], w_f1[l], fs, f"ffn1_fwd_{l}",
                                                    riders=[_Gather(set_f2[l]), _Gather(set_mx[l])])
        x2, mix, h2, p = _mix_fwd(x1, mix_norm[l], w_mx, lay, l, sm[:6], f"mix_fwd_{l}")
        nxt = [_Gather(set_f1[l + 1])] if l + 1 < depth else []
        (x3, a2, b2), got = _ffn_fwd(x2, ffn2_norm[l], w_f2, fs, f"ffn2_fwd_{l}", riders=nxt)
        if nxt:
            w_f1[l + 1] = got[0][0]
        saved.append((xs, a1, b1, x1, h2, p, mix, x2, a2, b2, sm, w_f2, w_mx))
        xs = x3

    dx, loss_part, dgf = _loss_head(xs, loss_target[0], final_norm, "loss_head")
    loss = lax.psum(loss_part[0, 0], ("x", "y", "c"))

    def small_rows(parts):
        buf, at = _f32_rows(parts, d, None)
        return jnp.pad(buf, ((0, -buf.shape[0] % 16), (0, 0))), at

    small = [None] * depth
    g_f1, g_f2, g_mx = [None] * depth, [None] * depth, [None] * depth
    pending = None
    for l in reversed(range(depth)):
        x0, a1, b1, x1, h2, p, mix, x2, a2, b2, sm, w_f2, w_mx = saved[l]
        ride = [_Exchange(pending)] if pending is not None else []
        (dx2, da, db, u, h, dgn2), got = _ffn_bwd(x2, dx, a2, b2, ffn2_norm[l], w_f2, fs, f"ffn2_bwd_{l}", riders=ride)
        if ride:
            g_f1[l + 1] = got[0][0]
        gset = lax.empty((N_DEV, 3 * fs, d), BF16)
        gset = _wgrad(da, h, gset, fs, 0, 1.0, f"ffn2_dw1_{l}")
        gset = _wgrad(db, h, gset, fs, 1, 1.0, f"ffn2_dw3_{l}")
        gset = _wgrad(u, dx, gset, fs, 2, FFN_RESIDUAL, f"ffn2_dw2_{l}")
        (dp, dcw, dvec, dsw, dpbd, dws, dbs, dx1, dgm), ((g_f2[l],),) = _mix_bwd(
            p, dx2, x1, mix_norm[l], w_mx, lay, l, sm, f"mix_bwd_{l}", riders=[_Exchange(gset)])
        gset = lax.empty((N_DEV, g + wo, d), BF16)
        gset = _wgrad(mix, dx2, gset, wo, lay.wout(l), 1.0, f"dw_out_{l}")
        gset = _wgrad(dp, h2, gset, g, lay.win(l), 1.0, f"dw_in_{l}")
        ride = [_Exchange(gset)]
        if l == 0 and depth > 1:
            early, early_offs = small_rows([a for k in range(1, depth) for a in small[k]] + [dgf[0]])
            ride.append(_Gather(early))
        (dx, da, db, u, h, dgn1), got = _ffn_bwd(x0, dx1, a1, b1, ffn1_norm[l], w_f1[l], fs,
                                                 f"ffn1_bwd_{l}", riders=ride)
        g_mx[l] = got[0][0]
        pg = g // len(POOL_WINDOWS)
        dpool = jnp.stack([dpbd[k * pg:(k + 1) * pg, k * pg:(k + 1) * pg] for k in range(len(POOL_WINDOWS))])
        small[l] = [dgn1[0], dgm[0], dgn2[0], dvec, dcw, dsw, dpool, dws, dbs[:, :nh].T]
        if l > 0:
            gset = lax.empty((N_DEV, 3 * fs, d), BF16)
            gset = _wgrad(da, h, gset, fs, 0, 1.0, f"ffn1_dw1_{l}")
            gset = _wgrad(db, h, gset, fs, 1, 1.0, f"ffn1_dw3_{l}")
            pending = _wgrad(u, dx1, gset, fs, 2, FFN_RESIDUAL, f"ffn1_dw2_{l}")
    grad_x = dx[None]
    early_all = got[1][0] if depth > 1 else None
    late, late_offs = small_rows(small[0] if depth > 1 else small[0] + [dgf[0]])
    one = lambda: lax.empty((N_DEV, fs, d), BF16)
    g_a, ((late_all,),) = _wgrad(da, h, one(), fs, 0, 1.0, "ffn1_dw1_0", riders=[_Gather(late)])
    g_b, ((x_a,),) = _wgrad(db, h, one(), fs, 0, 1.0, "ffn1_dw3_0", riders=[_Exchange(g_a)])
    g_c, ((x_b,),) = _wgrad(u, dx1, one(), fs, 0, FFN_RESIDUAL, "ffn1_dw2_0", riders=[_Exchange(g_b)])
    x_c = _alone(_Exchange(g_c), "exchange_last")

    grads = {}
    s_f1 = [jnp.concatenate([_sum_slots(x_a, "sum_f1a_0"), _sum_slots(x_b, "sum_f1b_0"),
                             _sum_slots(x_c, "sum_f1c_0")], axis=0)]
    s_f1 += [_sum_slots(g_f1[l], f"sum_f1_{l}") for l in range(1, depth)]
    s_f2 = [_sum_slots(g_f2[l], f"sum_f2_{l}") for l in range(depth)]
    s_mx = [_sum_slots(g_mx[l], f"sum_mx_{l}") for l in range(depth)]
    grads['ffn1_w1'] = jnp.stack([s_f1[l][0:fs].T for l in range(depth)])
    grads['ffn1_w3'] = jnp.stack([s_f1[l][fs:2 * fs].T for l in range(depth)])
    grads['ffn1_w2'] = jnp.stack([s_f1[l][2 * fs:] for l in range(depth)])
    grads['ffn2_w1'] = jnp.stack([s_f2[l][0:fs].T for l in range(depth)])
    grads['ffn2_w3'] = jnp.stack([s_f2[l][fs:2 * fs].T for l in range(depth)])
    grads['ffn2_w2'] = jnp.stack([s_f2[l][2 * fs:] for l in range(depth)])
    grads['w_in'] = jnp.stack([s_mx[l][0:g].T for l in range(depth)])
    grads['w_out'] = jnp.stack([s_mx[l][g:] for l in range(depth)])

    per_layer = len(small[0])
    shapes = [(d,), (d,), (d,), (8, g), (32, g), (8, g), (len(POOL_WINDOWS), g // len(POOL_WINDOWS),
              g // len(POOL_WINDOWS)), (nh * CHUNK, CHUNK), (nh, CHUNK)]
    late_sum = _sum_slots(late_all, "sum_small_late")
    early_sum = _sum_slots(early_all, "sum_small_early") if depth > 1 else None
    got = [[_take(late_sum, late_offs[k], shapes[k]) for k in range(per_layer)]]
    got += [[_take(early_sum, early_offs[(l - 1) * per_layer + k], shapes[k]) for k in range(per_layer)]
            for l in range(1, depth)]
    col = lambda k: jnp.stack([got[l][k] for l in range(depth)])
    grads['ffn1_norm'], grads['mix_norm'], grads['ffn2_norm'] = col(0), col(1), col(2)
    dvec_all = col(3)
    for k, n in enumerate(('conf_conv_b', 'conf_ln_g', 'conf_ln_b', 'pool_scale', 'gmlp_ln_g', 'gmlp_ln_b')):
        grads[n] = dvec_all[:, k]
    grads['conf_conv_w'] = lax.dynamic_slice_in_dim(col(4)[:, :CONF_KERNEL], me * cs, cs, axis=2)
    grads['sconv_w'] = lax.dynamic_slice_in_dim(col(5)[:, :SHORT_KERNEL], me * cs, cs, axis=2)
    grads['pool_w'] = col(6)
    grads['gmlp_w_s'] = col(7).reshape(depth, nh, CHUNK, CHUNK)
    grads['gmlp_b_s'] = col(8)
    grads['final_norm'] = (_take(early_sum, early_offs[-1], (d,)) if depth > 1
                           else _take(late_sum, late_offs[-1], (d,)))

    deltas, new_m, new_v = {}, {}, {}
    for n in WEIGHTS:
        deltas[n], new_m[n], new_v[n] = _adamw(w[n], grads[n], mom1[n], mom2[n], f"adamw_{n}")

    return (loss, grad_x, *[grads[n] for n in WEIGHTS], *[deltas[n] for n in WEIGHTS],
            *[new_m[n] for n in WEIGHTS], *[new_v[n] for n in WEIGHTS])
```

```python
import functools

import jax
import jax.numpy as jnp
from jax import lax
from jax.experimental import pallas as pl
from jax.experimental.pallas import tpu as pltpu

F32 = jnp.float32
BF16 = jnp.bfloat16
MESH = pl.DeviceIdType.MESH

N_DEV = 8
EPS = 1e-6
FFN_RESIDUAL = 0.5
CONF_KERNEL = 31
SHORT_KERNEL = 3
POOL_WINDOWS = (2, 4, 8, 16)
CHUNK = 128
HEAD_DIM = 64
N_IN_PIECES = 8
HALO = 32
CONV_ROWS = 16
ADAM_LR = 0.001
ADAM_B1 = 0.9
ADAM_B2 = 0.999
ADAM_EPS = 1e-08
ADAM_WD = 0.01
ADAM_STEP = 10
VMEM_LIMIT_V7X = 56 * 1024 * 1024

WEIGHTS = ['ffn1_norm', 'ffn1_w1', 'ffn1_w3', 'ffn1_w2', 'mix_norm', 'w_in', 'conf_conv_w', 'conf_conv_b',
           'conf_ln_g', 'conf_ln_b', 'sconv_w', 'pool_w', 'pool_scale', 'gmlp_ln_g', 'gmlp_ln_b', 'gmlp_w_s',
           'gmlp_b_s', 'w_out', 'ffn2_norm', 'ffn2_w1', 'ffn2_w3', 'ffn2_w2', 'final_norm']
SHARDED_ROWS = ('ffn1_w2', 'ffn2_w2', 'w_out')
SHARDED_COLS = ('ffn1_w1', 'ffn1_w3', 'ffn2_w1', 'ffn2_w3', 'w_in')
SHARDED_CHAN = ('conf_conv_w', 'sconv_w')


def _params(sem=None):
    return pltpu.CompilerParams(dimension_semantics=sem, vmem_limit_bytes=VMEM_LIMIT_V7X)


def _nn(a, b):
    return jnp.dot(a, b, preferred_element_type=F32)


def _nt(a, b):
    return lax.dot_general(a, b, (((1,), (1,)), ((), ())), preferred_element_type=F32)


def _tn(a, b):
    return lax.dot_general(a, b, (((0,), (0,)), ((), ())), preferred_element_type=F32)


def _sigmoid(x):
    return 0.5 * jnp.tanh(0.5 * x) + 0.5


def _row_tile(n, pref, mult=8):
    t = min(n, pref)
    while n % t or t % mult:
        t -= 1
    return t


def _chunks(n, size):
    out, s = [], 0
    while s < n:
        out.append((s, min(size, n - s)))
        s += size
    return out


def _const(shape):
    return pl.BlockSpec(shape, lambda i: (0,) * len(shape))


def _resident(shape, index):
    return pl.BlockSpec(shape, lambda i: index, pipeline_mode=pl.Buffered(1))


def _wspec(rows, d, blk):
    return _resident((N_DEV, rows, d), (0, blk, 0))


def _layernorm_stats(x):
    mu = jnp.mean(x, axis=-1, keepdims=True)
    xc = x - mu
    r = lax.rsqrt(jnp.mean(xc * xc, axis=-1, keepdims=True) + EPS)
    return xc * r, r


def _layernorm_bwd(dy, g, xhat, r):
    dxh = dy * g
    return r * (dxh - jnp.mean(dxh, axis=-1, keepdims=True) - xhat * jnp.mean(dxh * xhat, axis=-1, keepdims=True))


def _rmsnorm_stats(x):
    r = lax.rsqrt(jnp.mean(x * x, axis=-1, keepdims=True) + EPS)
    return x * r, r


def _rmsnorm_bwd(dh, g, xhat, r):
    dxh = dh * g
    return r * (dxh - xhat * jnp.mean(dxh * xhat, axis=-1, keepdims=True))


class _MixRows:
    def __init__(self, g, wo):
        assert g % wo == 0 and wo % 16 == 0
        self.g, self.wo = g, wo

    def win(self, l):
        return 0

    def wout(self, l):
        return self.g // self.wo


class _Gather:
    def __init__(self, shard):
        self.operands = [shard]
        self.out_shape = [jax.ShapeDtypeStruct((N_DEV,) + shard.shape, shard.dtype)]
        self.scratch = [pltpu.SemaphoreType.DMA((7,)), pltpu.SemaphoreType.DMA((7,)), pltpu.SemaphoreType.DMA]

    def _copies(self, ins, outs, scr, arrivals):
        (x_ref,), (out_ref,), (send_sems, recv_sems, local_sem) = ins, outs, scr
        x, y, c = lax.axis_index("x"), lax.axis_index("y"), lax.axis_index("c")
        me, sibling = (x, y, c), (x, y, 1 - c)
        chips = [(1 - x, y), (x, 1 - y), (1 - x, 1 - y)]

        def slot(px, py, pc):
            return out_ref.at[4 * px + 2 * py + pc]

        def copy(k, block, to, src=None):
            return pltpu.make_async_remote_copy(
                src_ref=slot(*block) if src is None else src, dst_ref=slot(*block),
                send_sem=send_sems.at[k], recv_sem=recv_sems.at[k], device_id=to, device_id_type=MESH)

        mine = pltpu.make_async_copy(x_ref, slot(*me), local_sem)
        first = [copy(0, me, sibling, src=x_ref)]
        first += [copy(1 + j, me, (*chip, c), src=x_ref) for j, chip in enumerate(chips)]
        passed = [copy(4 + j, (*chip, c), sibling) for j, chip in enumerate(chips)]
        if arrivals == "ici":
            landing = [copy(1 + j, (*chip, c), me) for j, chip in enumerate(chips)]
        else:
            landing = [copy(0, sibling, me)] + [copy(4 + j, (*chip, 1 - c), me) for j, chip in enumerate(chips)]
        return mine, first, passed, landing

    def begin(self, i, n, ins, outs, scr):
        mine, first, passed, over_ici = self._copies(ins, outs, scr, "ici")

        @pl.when(i == 0)
        def _():
            mine.start()
            for cp in first:
                cp.start()

        @pl.when(i == (n - 3 if n >= 6 else n // 2))
        def _():
            for landed, onward in zip(over_ici, passed):
                landed.wait_recv()
                onward.start()

    def end(self, i, n, ins, outs, scr):
        mine, first, passed, over_d2d = self._copies(ins, outs, scr, "d2d")

        @pl.when(i == n - 1)
        def _():
            for cp in over_d2d:
                cp.wait_recv()
            for cp in first + passed:
                cp.wait_send()
            mine.wait()


class _Exchange:
    def __init__(self, blocks):
        self.operands = [blocks]
        self.out_shape = [jax.ShapeDtypeStruct(blocks.shape, blocks.dtype)]
        self.scratch = [pltpu.SemaphoreType.DMA((7,)), pltpu.SemaphoreType.DMA((7,)), pltpu.SemaphoreType.DMA]

    def _copies(self, ins, outs, scr, with_arrivals):
        (s_ref,), (r_ref,), (send_sems, recv_sems, local_sem) = ins, outs, scr
        x, y, c = lax.axis_index("x"), lax.axis_index("y"), lax.axis_index("c")
        me = 4 * x + 2 * y + c
        mine = pltpu.make_async_copy(s_ref.at[me], r_ref.at[me], local_sem)
        sends, recvs = [], []
        for k in range(1, N_DEV):
            px = (1 - x) if (k >> 2) & 1 else x
            py = (1 - y) if (k >> 1) & 1 else y
            pc = (1 - c) if k & 1 else c
            peer = 4 * px + 2 * py + pc
            for dst, group in ((r_ref.at[me], sends), (r_ref.at[peer], recvs)):
                if group is sends or with_arrivals:
                    group.append(pltpu.make_async_remote_copy(
                        src_ref=s_ref.at[peer], dst_ref=dst, send_sem=send_sems.at[k - 1],
                        recv_sem=recv_sems.at[k - 1], device_id=(px, py, pc), device_id_type=MESH))
        return mine, sends, recvs

    def begin(self, i, n, ins, outs, scr):
        mine, sends, _ = self._copies(ins, outs, scr, False)

        @pl.when(i == 0)
        def _():
            mine.start()
            for cp in sends:
                cp.start()

    def end(self, i, n, ins, outs, scr):
        mine, sends, recvs = self._copies(ins, outs, scr, True)

        @pl.when(i == n - 1)
        def _():
            for cp in recvs:
                cp.wait_recv()
            for cp in sends:
                cp.wait_send()
            mine.wait()


_ANY = pl.BlockSpec(memory_space=pl.ANY)


def _call(body, name, steps, operands, in_specs, out_shape, out_specs, scratch=(), riders=(), aliases=None):
    n_in, n_out, n_scr = len(operands), len(out_shape), len(scratch)

    def full(*refs):
        pos = [0]

        def take(k):
            pos[0] += k
            return refs[pos[0] - k:pos[0]]

        h_in, r_in = take(n_in), [take(len(r.operands)) for r in riders]
        h_out, r_out = take(n_out), [take(len(r.out_shape)) for r in riders]
        h_scr, r_scr = take(n_scr), [take(len(r.scratch)) for r in riders]
        i = pl.program_id(0)
        for r, a, b, c in zip(riders, r_in, r_out, r_scr):
            r.begin(i, steps, a, b, c)
        body(*h_in, *h_out, *h_scr)
        for r, a, b, c in zip(riders, r_in, r_out, r_scr):
            r.end(i, steps, a, b, c)

    outs = pl.pallas_call(
        full, name=name, grid=(steps,),
        out_shape=tuple(out_shape) + tuple(s for r in riders for s in r.out_shape),
        in_specs=list(in_specs) + [_ANY for r in riders for _ in r.operands],
        out_specs=tuple(out_specs) + tuple(_ANY for r in riders for _ in r.out_shape),
        scratch_shapes=list(scratch) + [s for r in riders for s in r.scratch],
        input_output_aliases=aliases or {},
        compiler_params=_params(("arbitrary",)),
    )(*operands, *(o for r in riders for o in r.operands))
    host, rest = outs[:n_out], list(outs[n_out:])
    rides = []
    for r in riders:
        rides.append(rest[:len(r.out_shape)])
        rest = rest[len(r.out_shape):]
    return host, rides


def _alone(rider, name):
    return _call(lambda: None, name, 1, [], [], [], [], riders=[rider])[1][0][0]


def _sum_slots(slots, name):
    _, rows, cols = slots.shape
    tr = _row_tile(rows, 256, 16)

    def body(s_ref, o_ref):
        acc = s_ref[0].astype(F32)
        for j in range(1, N_DEV):
            acc = acc + s_ref[j].astype(F32)
        o_ref[...] = acc

    return pl.pallas_call(
        body, name=name, grid=(rows // tr,),
        out_shape=jax.ShapeDtypeStruct((rows, cols), F32),
        in_specs=[pl.BlockSpec((N_DEV, tr, cols), lambda i: (0, i, 0))],
        out_specs=pl.BlockSpec((tr, cols), lambda i: (i, 0)),
        compiler_params=_params(("parallel",)),
    )(slots)


def _ffn_fwd(x, gain, wset, fs, name, riders=()):
    t, d = x.shape
    f = N_DEV * fs
    tm = _row_tile(t, 512, 128)
    cols = _chunks(f, 1024)

    def body(x_ref, g_ref, w1_ref, w3_ref, w2_ref, y_ref, a_ref, b_ref, u_ref):
        xv = x_ref[...]
        xhat, _ = _rmsnorm_stats(xv)
        h = (xhat * g_ref[...]).astype(BF16)
        w1 = w1_ref[...].reshape(f, d)
        w3 = w3_ref[...].reshape(f, d)
        for s, n in cols:
            a = _nt(h, w1[s:s + n, :])
            b = _nt(h, w3[s:s + n, :])
            a_ref[:, s:s + n] = a.astype(BF16)
            b_ref[:, s:s + n] = b.astype(BF16)
            u_ref[:, s:s + n] = (a * _sigmoid(a) * b).astype(BF16)
        y_ref[...] = xv + FFN_RESIDUAL * _nn(u_ref[...], w2_ref[...].reshape(f, d))

    return _call(
        body, name, t // tm, [x, gain.reshape(1, d), wset, wset, wset],
        [pl.BlockSpec((tm, d), lambda i: (i, 0)), _const((1, d)),
         _wspec(fs, d, 0), _wspec(fs, d, 1), _wspec(fs, d, 2)],
        [jax.ShapeDtypeStruct((t, d), F32), jax.ShapeDtypeStruct((t, f), BF16), jax.ShapeDtypeStruct((t, f), BF16)],
        [pl.BlockSpec((tm, d), lambda i: (i, 0)), pl.BlockSpec((tm, f), lambda i: (i, 0)),
         pl.BlockSpec((tm, f), lambda i: (i, 0))],
        scratch=[pltpu.VMEM((tm, f), BF16)], riders=riders)


def _ffn_bwd(x, dy, a, b, gain, wset, fs, name, riders=()):
    t, d = x.shape
    f = N_DEV * fs
    tm = _row_tile(t, 256, 128)
    cols = _chunks(f, 1024)

    def body(x_ref, dy_ref, a_ref, b_ref, g_ref, w1_ref, w3_ref, w2_ref,
             dx_ref, da_ref, db_ref, u_ref, h_ref, dg_ref):
        @pl.when(pl.program_id(0) == 0)
        def _():
            dg_ref[...] = jnp.zeros_like(dg_ref)

        xv, dyv, g = x_ref[...], dy_ref[...], g_ref[...]
        xhat, r = _rmsnorm_stats(xv)
        h_ref[...] = (xhat * g).astype(BF16)
        dyb = (FFN_RESIDUAL * dyv).astype(BF16)
        w1 = w1_ref[...].reshape(f, d)
        w3 = w3_ref[...].reshape(f, d)
        w2 = w2_ref[...].reshape(f, d)
        dh = jnp.zeros((tm, d), F32)
        for s, n in cols:
            du = _nt(dyb, w2[s:s + n, :])
            av = a_ref[:, s:s + n].astype(F32)
            bv = b_ref[:, s:s + n].astype(F32)
            sig = _sigmoid(av)
            sa = av * sig
            u_ref[:, s:s + n] = (sa * bv).astype(BF16)
            da = (du * bv * (sig + sa * (1.0 - sig))).astype(BF16)
            db = (du * sa).astype(BF16)
            da_ref[:, s:s + n] = da
            db_ref[:, s:s + n] = db
            dh = dh + _nn(da, w1[s:s + n, :]) + _nn(db, w3[s:s + n, :])
        dx_ref[...] = dyv + _rmsnorm_bwd(dh, g, xhat, r)
        dg_ref[0:1, :] += jnp.sum(dh * xhat, axis=0, keepdims=True)

    big = jax.ShapeDtypeStruct((t, f), BF16)
    row = lambda w: pl.BlockSpec((tm, w), lambda i: (i, 0))
    return _call(
        body, name, t // tm, [x, dy, a, b, gain.reshape(1, d), wset, wset, wset],
        [row(d), row(d), row(f), row(f), _const((1, d)), _wspec(fs, d, 0), _wspec(fs, d, 1), _wspec(fs, d, 2)],
        [jax.ShapeDtypeStruct((t, d), F32), big, big, big, jax.ShapeDtypeStruct((t, d), BF16),
         jax.ShapeDtypeStruct((8, d), F32)],
        [row(d), row(f), row(f), row(f), row(d), _const((8, d))], riders=riders)


def _wgrad(lhs, rhs, scale, by_rows, name, riders=()):
    t, m = lhs.shape
    n = rhs.shape[1]
    bt = _row_tile(t, 1024, 128)
    nk = t // bt
    out = (N_DEV, m // N_DEV, n) if by_rows else (N_DEV, m, n // N_DEV)

    def body(a_ref, b_ref, o_ref, acc_ref):
        k = pl.program_id(0)

        @pl.when(k == 0)
        def _():
            acc_ref[...] = jnp.zeros_like(acc_ref)

        acc_ref[...] += _tn(a_ref[...].astype(BF16), b_ref[...].astype(BF16))

        @pl.when(k == nk - 1)
        def _():
            if by_rows:
                o_ref[...] = (scale * acc_ref[...]).astype(BF16).reshape(out)
            else:
                w = n // N_DEV
                for j in range(N_DEV):
                    o_ref[j] = (scale * acc_ref[:, j * w:(j + 1) * w]).astype(BF16)

    (blocks,), rides = _call(
        body, name, nk, [lhs, rhs],
        [pl.BlockSpec((bt, m), lambda k: (k, 0)), pl.BlockSpec((bt, n), lambda k: (k, 0))],
        [jax.ShapeDtypeStruct(out, BF16)], [_const(out)],
        scratch=[pltpu.VMEM((m, n), F32)], riders=riders)
    return (blocks, rides) if riders else blocks


def _head_masks(rows, g):
    lane = lax.broadcasted_iota(jnp.int32, (rows, g), 1)
    return [(lane >= h * HEAD_DIM) & (lane < (h + 1) * HEAD_DIM) for h in range(g // HEAD_DIM)]


def _pool_windows(rows, g):
    lane = lax.broadcasted_iota(jnp.int32, (rows, g), 1)
    pg = g // len(POOL_WINDOWS)
    w = jnp.full((rows, g), float(POOL_WINDOWS[-1]), F32)
    for k in range(len(POOL_WINDOWS) - 2, -1, -1):
        w = jnp.where(lane < (k + 1) * pg, float(POOL_WINDOWS[k]), w)
    return w


def _window_sum(ref, tmp, first, rows, wl, direction):
    total = ref.shape[0]
    src, out, shift = ref, None, 1
    for k, w in enumerate(POOL_WINDOWS):
        assert w == 2 * shift and 8 * (k + 1) <= HALO
        if k + 1 < len(POOL_WINDOWS):
            n = total - 8 * (k + 1)
            lo = 8 * (k + 1) if direction < 0 else 0
            dst = tmp.at[k % 2]
            dst[pl.ds(lo, n), :] = src[pl.ds(lo, n), :] + src[pl.ds(lo + direction * shift, n), :]
            src = dst
            level = dst[pl.ds(first, rows), :]
        else:
            level = src[pl.ds(first, rows), :] + src[pl.ds(first + direction * shift, rows), :]
        out = level if out is None else jnp.where(wl >= float(w), level, out)
        shift = w
    return out


def _shifted_copies(src_ref, dst_ref):
    rows = dst_ref.shape[1]
    for b in range(1, 8):
        dst_ref[b - 1] = src_ref[pl.ds(b, rows), :]


def _window(src_ref, shifted_ref, off, rows):
    b = off % 8
    if b == 0:
        return src_ref[pl.ds(off, rows), :]
    return shifted_ref[b - 1, pl.ds(off - b, rows), :]


def _pool_divisor(first_row, rows, g):
    pos1 = (lax.broadcasted_iota(jnp.int32, (rows, g), 0) + first_row + 1).astype(F32)
    return jnp.minimum(pos1, _pool_windows(rows, g))


def _spatial_mix(vq, wcat_ref, masks):
    vstack = jnp.concatenate([jnp.where(m, vq, jnp.zeros_like(vq)) for m in masks], axis=0)
    return _nn(wcat_ref[...], vstack)


def _mix_fwd(x, gain, wbuf, lay, l, small, name):
    t, d = x.shape
    g = lay.g
    di = N_DEV * g
    tm = _row_tile(t, 256, CHUNK)
    conv_w, vecs, sconv_w, pool_bd, wcat, bst = small

    def body(x_ref, g_ref, wi_ref, wo_ref, cw_ref, vec_ref, sw_ref, pbd_ref, wcat_ref, bst_ref,
             y_ref, mix_ref, h_ref, pm_ref, pp_ref, ea_ref, eb_ref, ec_ref, sh_ref, tmp_ref):
        i = pl.program_id(0)

        @pl.when(i == 0)
        def _():
            pp_ref[...] = jnp.zeros_like(pp_ref)

        xhat, _ = _rmsnorm_stats(x_ref[...])
        h = (xhat * g_ref[...]).astype(BF16)
        h_ref[...] = h
        pm_ref[...] = _nt(h, wi_ref[...].reshape(di, d)).astype(BF16)

        def prev(j):
            return pp_ref[:, j * g:(j + 1) * g].astype(F32)

        def main(j):
            return pm_ref[:, j * g:(j + 1) * g].astype(F32)

        vec = vec_ref[...]
        ea_ref[0:HALO, :] = prev(0) * _sigmoid(prev(1))
        ea_ref[HALO:, :] = main(0) * _sigmoid(main(1))
        _shifted_copies(ea_ref, sh_ref)
        cw = cw_ref[...]
        c = jnp.zeros((tm, g), F32) + vec[0:1, :]
        for k in range(CONF_KERNEL):
            c = c + cw[k:k + 1, :] * _window(ea_ref, sh_ref, HALO - (CONF_KERNEL - 1) + k, tm)
        chat, _ = _layernorm_stats(c)
        ln = chat * vec[1:2, :] + vec[2:3, :]
        mix_ref[:, 0:g] = (ln * _sigmoid(ln)).astype(BF16)
        eb_ref[0:HALO, :] = prev(3) * prev(4)
        eb_ref[HALO:, :] = main(3) * main(4)
        sw = sw_ref[...]
        cz = jnp.zeros((tm, g), F32)
        for k in range(SHORT_KERNEL):
            cz = cz + sw[k:k + 1, :] * eb_ref[pl.ds(HALO - (SHORT_KERNEL - 1) + k, tm), :]
        mix_ref[:, g:2 * g] = (main(2) * cz).astype(BF16)
        ec_ref[0:HALO, :] = prev(5)
        xp = main(5)
        ec_ref[HALO:, :] = xp
        wl = _pool_windows(tm, g)
        mean = _window_sum(ec_ref, tmp_ref, HALO, tm, wl, -1) / _pool_divisor(i * tm, tm, g)
        yc = _nn((mean - xp).astype(BF16), pbd_ref[...])
        mix_ref[:, 2 * g:3 * g] = (yc * vec[3:4, :]).astype(BF16)
        vhat, _ = _layernorm_stats(main(7))
        v = (vhat * vec[4:5, :] + vec[5:6, :]).astype(BF16)
        gu = main(6)
        masks = _head_masks(CHUNK, g)
        for q in range(tm // CHUNK):
            rs = slice(q * CHUNK, (q + 1) * CHUNK)
            mixed = _spatial_mix(v[rs, :], wcat_ref, masks) + bst_ref[...]
            mix_ref[rs, 3 * g:4 * g] = (gu[rs, :] * mixed).astype(BF16)
        y_ref[...] = x_ref[...] + _nn(mix_ref[...], wo_ref[...].reshape(d, d))
        pp_ref[...] = pm_ref[tm - HALO:, :]

    row = lambda w: pl.BlockSpec((tm, w), lambda i: (i, 0))
    return pl.pallas_call(
        body, name=name, grid=(t // tm,),
        out_shape=(jax.ShapeDtypeStruct((t, d), F32), jax.ShapeDtypeStruct((t, 4 * g), BF16),
                   jax.ShapeDtypeStruct((t, d), BF16), jax.ShapeDtypeStruct((t, di), BF16)),
        in_specs=[row(d), _const((1, d)), _wspec(g, d, lay.win(l)), _wspec(lay.wo, d, lay.wout(l)),
                  _const(conv_w.shape), _const(vecs.shape), _const(sconv_w.shape), _const(pool_bd.shape),
                  _const(wcat.shape), _const(bst.shape)],
        out_specs=(row(d), row(4 * g), row(d), row(di)),
        scratch_shapes=[pltpu.VMEM((HALO, di), BF16)] + [pltpu.VMEM((HALO + tm, g), F32)] * 3
        + [pltpu.VMEM((7, HALO + tm - 8, g), F32), pltpu.VMEM((2, HALO + tm, g), F32)],
        compiler_params=_params(("arbitrary",)),
    )(x, gain.reshape(1, d), wbuf, wbuf, conv_w, vecs, sconv_w, pool_bd, wcat, bst)


def _mix_bwd(p, dy, x, gain, wbuf, lay, l, small, name, riders=()):
    t, d = dy.shape
    g = lay.g
    di = N_DEV * g
    tm = _row_tile(t, 256, CHUNK)
    nt = t // tm
    nh = g // HEAD_DIM
    per = tm // HALO
    ext = tm + HALO
    conv_w, vecs, sconv_w, pool_bd, wcat, bst, wcat_t = small

    def body(pp_ref, pm_ref, pn_ref, dy_ref, dyn_ref, x_ref, g_ref, wi_ref, wo_ref, cw_ref, vec_ref, sw_ref, pbd_ref,
             wcat_ref, bst_ref, wcatt_ref, dp_ref, dcw_ref, dvec_ref, dsw_ref, dpbd_ref, dws_ref, dbs_ref,
             dx_ref, dg_ref, ea_ref, eb_ref, ec_ref, fa_ref, fb_ref, fc_ref, sh_ref, sf_ref, tmp_ref):
        i = pl.program_id(0)
        seen = (i > 0).astype(F32)
        more = (i < nt - 1).astype(F32)

        @pl.when(i == 0)
        def _():
            for ref in (dcw_ref, dvec_ref, dsw_ref, dpbd_ref, dws_ref, dbs_ref, dg_ref):
                ref[...] = jnp.zeros_like(ref)

        def prev(j):
            return pp_ref[:, j * g:(j + 1) * g].astype(F32) * seen

        def main(j):
            return pm_ref[:, j * g:(j + 1) * g].astype(F32)

        def nxt(j):
            return pn_ref[:, j * g:(j + 1) * g].astype(F32)

        def colsum(v):
            return jnp.sum(v, axis=0, keepdims=True)

        vec = vec_ref[...]
        wo = wo_ref[...].reshape(d, d)
        dmix = jnp.concatenate([_nt(dy_ref[...].astype(BF16), wo),
                                _nt((dyn_ref[...] * more).astype(BF16), wo)], axis=0)

        val, gate = main(0), main(1)
        sg = _sigmoid(gate)
        ea_ref[0:HALO, :] = prev(0) * _sigmoid(prev(1))
        ea_ref[HALO:ext, :] = val * sg
        ea_ref[ext:, :] = nxt(0) * _sigmoid(nxt(1))
        _shifted_copies(ea_ref, sh_ref)
        cw = cw_ref[...]
        c = jnp.zeros((ext, g), F32) + vec[0:1, :]
        for k in range(CONF_KERNEL):
            c = c + cw[k:k + 1, :] * _window(ea_ref, sh_ref, HALO - (CONF_KERNEL - 1) + k, ext)
        chat, r = _layernorm_stats(c)
        ln = chat * vec[1:2, :] + vec[2:3, :]
        sl = _sigmoid(ln)
        dln = dmix[:, 0:g] * (sl * (1.0 + ln * (1.0 - sl)))
        dvec_ref[1:2, :] += colsum(dln[0:tm] * chat[0:tm])
        dvec_ref[2:3, :] += colsum(dln[0:tm])
        dc = _layernorm_bwd(dln, vec[1:2, :], chat, r)
        dvec_ref[0:1, :] += colsum(dc[0:tm])
        fa_ref[...] = dc
        _shifted_copies(fa_ref, sf_ref)
        dcm = dc[0:tm]
        dy0 = jnp.zeros((tm, g), F32)
        for k in range(CONF_KERNEL):
            dy0 = dy0 + cw[k:k + 1, :] * _window(fa_ref, sf_ref, CONF_KERNEL - 1 - k, tm)
            dcw_ref[k:k + 1, :] += colsum(dcm * _window(ea_ref, sh_ref, HALO - (CONF_KERNEL - 1) + k, tm))
        dp_ref[:, 0:g] = (dy0 * sg).astype(BF16)
        dp_ref[:, g:2 * g] = (dy0 * val * sg * (1.0 - sg)).astype(BF16)

        sb, sc, sx = main(2), main(3), main(4)
        eb_ref[0:HALO, :] = prev(3) * prev(4)
        eb_ref[HALO:ext, :] = sc * sx
        sw = sw_ref[...]
        cz = jnp.zeros((tm, g), F32)
        for k in range(SHORT_KERNEL):
            cz = cz + sw[k:k + 1, :] * eb_ref[pl.ds(HALO - (SHORT_KERNEL - 1) + k, tm), :]
        dob = dmix[:, g:2 * g]
        dp_ref[:, 2 * g:3 * g] = (dob[0:tm] * cz).astype(BF16)
        fb_ref[0:tm, :] = dob[0:tm] * sb
        fb_ref[tm:, :] = dob[tm:] * nxt(2)
        dczm = fb_ref[0:tm, :]
        dz = jnp.zeros((tm, g), F32)
        for k in range(SHORT_KERNEL):
            dz = dz + sw[k:k + 1, :] * fb_ref[pl.ds(SHORT_KERNEL - 1 - k, tm), :]
            dsw_ref[k:k + 1, :] += colsum(dczm * eb_ref[pl.ds(HALO - (SHORT_KERNEL - 1) + k, tm), :])
        dp_ref[:, 3 * g:4 * g] = (dz * sx).astype(BF16)
        dp_ref[:, 4 * g:5 * g] = (dz * sc).astype(BF16)

        xp = main(5)
        ec_ref[0:HALO, :] = prev(5)
        ec_ref[HALO:, :] = xp
        wl = _pool_windows(tm, g)
        dpool = ((_window_sum(ec_ref, tmp_ref, HALO, tm, wl, -1) / _pool_divisor(i * tm, tm, g)) - xp).astype(BF16)
        pbd = pbd_ref[...]
        yc = _nn(dpool, pbd)
        doc = dmix[:, 2 * g:3 * g]
        dvec_ref[3:4, :] += colsum(doc[0:tm] * yc)
        dyc = (doc * vec[3:4, :]).astype(BF16)
        dpbd_ref[...] += _tn(dpool, dyc[0:tm])
        dd = _nt(dyc, pbd)
        fc_ref[...] = dd / _pool_divisor(i * tm, ext, g)
        dp_ref[:, 5 * g:6 * g] = (_window_sum(fc_ref, tmp_ref, 0, tm, wl, 1) - dd[0:tm]).astype(BF16)

        gv = main(7)
        vhat, rv = _layernorm_stats(gv)
        v = (vhat * vec[4:5, :] + vec[5:6, :]).astype(BF16)
        gu = main(6)
        dod = dmix[0:tm, 3 * g:4 * g]
        masks = _head_masks(CHUNK, g)
        tril = (lax.broadcasted_iota(jnp.int32, (CHUNK, CHUNK), 1)
                <= lax.broadcasted_iota(jnp.int32, (CHUNK, CHUNK), 0)).astype(F32)
        tril = jnp.concatenate([tril] * nh, axis=0)
        head_col = lax.broadcasted_iota(jnp.int32, (CHUNK, CHUNK), 1)
        dvs = []
        for q in range(tm // CHUNK):
            rs = slice(q * CHUNK, (q + 1) * CHUNK)
            vq = v[rs, :]
            mixed = _spatial_mix(vq, wcat_ref, masks) + bst_ref[...]
            dp_ref[rs, 6 * g:7 * g] = (dod[rs, :] * mixed).astype(BF16)
            dmixed = dod[rs, :] * gu[rs, :]
            dmb = dmixed.astype(BF16)
            for h, m in enumerate(masks):
                hs = jnp.sum(jnp.where(m, dmixed, 0.0), axis=-1, keepdims=True)
                dbs_ref[...] += jnp.where(head_col == h, hs, 0.0)
            dmstack = jnp.concatenate([jnp.where(m, dmb, jnp.zeros_like(dmb)) for m in masks], axis=0)
            dws_ref[...] += _nt(dmstack, vq) * tril
            back = _nn(wcatt_ref[...], dmb)
            dv = jnp.zeros((CHUNK, g), F32)
            for h, m in enumerate(masks):
                dv = jnp.where(m, back[h * CHUNK:(h + 1) * CHUNK, :], dv)
            dvs.append(dv)
        dv = jnp.concatenate(dvs, axis=0)
        dvec_ref[4:5, :] += colsum(dv * vhat)
        dvec_ref[5:6, :] += colsum(dv)
        dp_ref[:, 7 * g:8 * g] = _layernorm_bwd(dv, vec[4:5, :], vhat, rv).astype(BF16)

        xhat, rx = _rmsnorm_stats(x_ref[...])
        dh = _nn(dp_ref[...], wi_ref[...].reshape(di, d))
        dx_ref[...] = dy_ref[...] + _rmsnorm_bwd(dh, g_ref[...], xhat, rx)
        dg_ref[0:1, :] += colsum(dh * xhat)

    halo_prev = pl.BlockSpec((HALO, di), lambda i: (jnp.maximum(i * per - 1, 0), 0))
    halo_next = lambda w: pl.BlockSpec((HALO, w), lambda i: (jnp.minimum((i + 1) * per, t // HALO - 1), 0))
    row = lambda w: pl.BlockSpec((tm, w), lambda i: (i, 0))
    outs = (jax.ShapeDtypeStruct((t, di), BF16), jax.ShapeDtypeStruct(conv_w.shape, F32),
            jax.ShapeDtypeStruct(vecs.shape, F32), jax.ShapeDtypeStruct(sconv_w.shape, F32),
            jax.ShapeDtypeStruct((g, g), F32), jax.ShapeDtypeStruct((nh * CHUNK, CHUNK), F32),
            jax.ShapeDtypeStruct((CHUNK, CHUNK), F32), jax.ShapeDtypeStruct((t, d), F32),
            jax.ShapeDtypeStruct((8, d), F32))
    return _call(
        body, name, nt, [p, p, p, dy, dy, x, gain.reshape(1, d), wbuf, wbuf, conv_w, vecs, sconv_w, pool_bd, wcat,
                         bst, wcat_t],
        [halo_prev, row(di), halo_next(di), row(d), halo_next(d), row(d), _const((1, d)),
         _wspec(g, d, lay.win(l)), _wspec(lay.wo, d, lay.wout(l)),
         _const(conv_w.shape), _const(vecs.shape), _const(sconv_w.shape), _const(pool_bd.shape),
         _const(wcat.shape), _const(bst.shape), _const(wcat_t.shape)],
        list(outs), [row(di)] + [_const(o.shape) for o in outs[1:7]] + [row(d), _const((8, d))],
        scratch=[pltpu.VMEM((HALO + tm + HALO, g), F32), pltpu.VMEM((HALO + tm, g), F32),
                 pltpu.VMEM((HALO + tm, g), F32), pltpu.VMEM((ext, g), F32), pltpu.VMEM((ext, g), F32),
                 pltpu.VMEM((ext, g), F32), pltpu.VMEM((7, HALO + tm + HALO - 8, g), F32),
                 pltpu.VMEM((7, ext - 8, g), F32), pltpu.VMEM((2, ext, g), F32)],
        riders=riders)


def _loss_head(x, target, gain, name):
    t, d = x.shape
    tm = _row_tile(t, 512, 8)

    def body(x_ref, t_ref, g_ref, dx_ref, loss_ref, dg_ref):
        @pl.when(pl.program_id(0) == 0)
        def _():
            loss_ref[...] = jnp.zeros_like(loss_ref)
            dg_ref[...] = jnp.zeros_like(dg_ref)

        g = g_ref[...]
        xhat, r = _rmsnorm_stats(x_ref[...])
        err = xhat * g - t_ref[...]
        loss_ref[...] += 0.5 * jnp.sum(jnp.mean(err * err, axis=-1, keepdims=True))
        dyv = err * (1.0 / d)
        dx_ref[...] = _rmsnorm_bwd(dyv, g, xhat, r)
        dg_ref[0:1, :] += jnp.sum(dyv * xhat, axis=0, keepdims=True)

    row = pl.BlockSpec((tm, d), lambda i: (i, 0))
    return pl.pallas_call(
        body, name=name, grid=(t // tm,),
        out_shape=(jax.ShapeDtypeStruct((t, d), F32), jax.ShapeDtypeStruct((8, 128), F32),
                   jax.ShapeDtypeStruct((8, d), F32)),
        in_specs=[row, row, _const((1, d))],
        out_specs=(row, _const((8, 128)), _const((8, d))),
        compiler_params=_params(("arbitrary",)),
    )(x, target, gain.reshape(1, d))


def _adam_step(w, gv, m, v):
    nm = ADAM_B1 * m + (1.0 - ADAM_B1) * gv
    nv = ADAM_B2 * v + (1.0 - ADAM_B2) * (gv * gv)
    m_hat = nm / (1.0 - ADAM_B1 ** ADAM_STEP)
    v_hat = nv / (1.0 - ADAM_B2 ** ADAM_STEP)
    return -ADAM_LR * (m_hat / (jnp.sqrt(v_hat) + ADAM_EPS) + ADAM_WD * w), nm, nv


def _reduce_adamw(slots, w, m, v, name):
    depth, rows, cols = w.shape
    tr = _row_tile(rows, 256, 16)
    nt = rows // tr

    def body(*refs):
        slot_refs = refs[:depth]
        w_ref, m_ref, v_ref, g_ref, d_ref, nm_ref, nv_ref = refs[depth:]
        layer = pl.program_id(0) // nt
        for l in range(depth):
            @pl.when(layer == l)
            def _(s_ref=slot_refs[l]):
                gv = s_ref[0].astype(F32)
                for j in range(1, N_DEV):
                    gv = gv + s_ref[j].astype(F32)
                g_ref[0] = gv
                d_ref[0], nm_ref[0], nv_ref[0] = _adam_step(w_ref[0], gv, m_ref[0], v_ref[0])

    def slot_spec(l):
        return pl.BlockSpec((N_DEV, tr, cols), lambda i: (0, jnp.clip(i - l * nt, 0, nt - 1), 0))

    spec = pl.BlockSpec((1, tr, cols), lambda i: (i // nt, i % nt, 0))
    shape = jax.ShapeDtypeStruct(w.shape, F32)
    return pl.pallas_call(
        body, name=name, grid=(depth * nt,),
        out_shape=(shape,) * 4, in_specs=[slot_spec(l) for l in range(depth)] + [spec] * 3, out_specs=(spec,) * 4,
        compiler_params=_params(("arbitrary",)),
    )(*slots, w, m, v)


def _adamw(w, grad, m, v, name):
    shape = w.shape
    cols = shape[-1]
    rows = w.size // cols
    tr = _row_tile(rows, 512, 8) if rows > 1024 else rows

    def body(w_ref, g_ref, m_ref, v_ref, d_ref, nm_ref, nv_ref):
        d_ref[...], nm_ref[...], nv_ref[...] = _adam_step(w_ref[...], g_ref[...], m_ref[...], v_ref[...])

    spec = pl.BlockSpec((tr, cols), lambda i: (i, 0))
    flat = jax.ShapeDtypeStruct((rows, cols), F32)
    outs = pl.pallas_call(
        body, name=name, grid=(rows // tr,),
        out_shape=(flat, flat, flat), in_specs=[spec] * 4, out_specs=(spec,) * 3,
        compiler_params=_params(("parallel",)),
    )(*(a.reshape(rows, cols) for a in (w, grad, m, v)))
    return tuple(o.reshape(shape) for o in outs)


def _f32_rows(parts, d, rows):
    out, offs, at = [], [], 0
    for a in parts:
        n = -(-a.size // d)
        out.append(jnp.pad(a.reshape(-1), (0, n * d - a.size)).reshape(n, d))
        offs.append(at)
        at += n
    total = at if rows is None else rows
    assert at <= total
    if total > at:
        out.append(jnp.zeros((total - at, d), F32))
    return jnp.concatenate(out, axis=0), offs


def _take(buf, off, shape):
    n = 1
    for s in shape:
        n *= s
    d = buf.shape[1]
    return buf[off:off + -(-n // d)].reshape(-1)[:n].reshape(shape)


def _block_diag(blocks):
    n, k, _ = blocks.shape
    out = jnp.zeros((n * k, n * k), blocks.dtype)
    for i in range(n):
        out = lax.dynamic_update_slice(out, blocks[i], (i * k, i * k))
    return out


def kernel(x, ffn1_norm, ffn1_w1, ffn1_w3, ffn1_w2, mix_norm, w_in, conf_conv_w, conf_conv_b, conf_ln_g, conf_ln_b, sconv_w, pool_w, pool_scale, gmlp_ln_g, gmlp_ln_b, gmlp_w_s, gmlp_b_s, w_out, ffn2_norm, ffn2_w1, ffn2_w3, ffn2_w2, final_norm, loss_target, m_ffn1_norm, m_ffn1_w1, m_ffn1_w3, m_ffn1_w2, m_mix_norm, m_w_in, m_conf_conv_w, m_conf_conv_b, m_conf_ln_g, m_conf_ln_b, m_sconv_w, m_pool_w, m_pool_scale, m_gmlp_ln_g, m_gmlp_ln_b, m_gmlp_w_s, m_gmlp_b_s, m_w_out, m_ffn2_norm, m_ffn2_w1, m_ffn2_w3, m_ffn2_w2, m_final_norm, v_ffn1_norm, v_ffn1_w1, v_ffn1_w3, v_ffn1_w2, v_mix_norm, v_w_in, v_conf_conv_w, v_conf_conv_b, v_conf_ln_g, v_conf_ln_b, v_sconv_w, v_pool_w, v_pool_scale, v_gmlp_ln_g, v_gmlp_ln_b, v_gmlp_w_s, v_gmlp_b_s, v_w_out, v_ffn2_norm, v_ffn2_w1, v_ffn2_w3, v_ffn2_w2, v_final_norm):
    w = dict(zip(WEIGHTS, (ffn1_norm, ffn1_w1, ffn1_w3, ffn1_w2, mix_norm, w_in, conf_conv_w, conf_conv_b, conf_ln_g,
                           conf_ln_b, sconv_w, pool_w, pool_scale, gmlp_ln_g, gmlp_ln_b, gmlp_w_s, gmlp_b_s, w_out,
                           ffn2_norm, ffn2_w1, ffn2_w3, ffn2_w2, final_norm)))
    mom1 = dict(zip(WEIGHTS, (m_ffn1_norm, m_ffn1_w1, m_ffn1_w3, m_ffn1_w2, m_mix_norm, m_w_in, m_conf_conv_w,
                              m_conf_conv_b, m_conf_ln_g, m_conf_ln_b, m_sconv_w, m_pool_w, m_pool_scale, m_gmlp_ln_g,
                              m_gmlp_ln_b, m_gmlp_w_s, m_gmlp_b_s, m_w_out, m_ffn2_norm, m_ffn2_w1, m_ffn2_w3,
                              m_ffn2_w2, m_final_norm)))
    mom2 = dict(zip(WEIGHTS, (v_ffn1_norm, v_ffn1_w1, v_ffn1_w3, v_ffn1_w2, v_mix_norm, v_w_in, v_conf_conv_w,
                              v_conf_conv_b, v_conf_ln_g, v_conf_ln_b, v_sconv_w, v_pool_w, v_pool_scale, v_gmlp_ln_g,
                              v_gmlp_ln_b, v_gmlp_w_s, v_gmlp_b_s, v_w_out, v_ffn2_norm, v_ffn2_w1, v_ffn2_w3,
                              v_ffn2_w2, v_final_norm)))
    _, t, d = x.shape
    depth, _, fs = ffn1_w1.shape
    g = w_in.shape[2]
    wo = w_out.shape[1]
    nh = g // HEAD_DIM
    cs = conf_conv_w.shape[2]
    lay = _MixRows(g, wo)
    me = 4 * lax.axis_index("x") + 2 * lax.axis_index("y") + lax.axis_index("c")

    def rows_bf16(mats):
        return jnp.concatenate([m_.astype(BF16) for m_ in mats], axis=0)

    set_f1 = [rows_bf16([ffn1_w1[l].T, ffn1_w3[l].T, ffn1_w2[l]]) for l in range(depth)]
    set_f2 = [rows_bf16([ffn2_w1[l].T, ffn2_w3[l].T, ffn2_w2[l]]) for l in range(depth)]
    set_mx = [rows_bf16([w_in[l].T, w_out[l]]) for l in range(depth)]
    convs = jnp.concatenate([conf_conv_w, sconv_w], axis=1)
    nconv = convs.size
    conv_bits = jnp.pad(convs.reshape(-1), (0, CONV_ROWS * d // 2 - nconv))
    conv_rows = lax.bitcast_convert_type(conv_bits, BF16).reshape(CONV_ROWS, d)
    w_f1 = [None] * depth
    w_f1[0] = _alone(_Gather(jnp.concatenate([set_f1[0], conv_rows], axis=0)), "gather_first")

    conv_all = lax.bitcast_convert_type(w_f1[0][:, 3 * fs:, :].reshape(N_DEV, CONV_ROWS * d // 2, 2), F32)
    conv_all = conv_all[:, :nconv].reshape(N_DEV, depth, CONF_KERNEL + SHORT_KERNEL, cs)
    conv_all = conv_all.transpose(1, 2, 0, 3).reshape(depth, CONF_KERNEL + SHORT_KERNEL, g)

    def small_inputs(l):
        cw = jnp.pad(conv_all[l, :CONF_KERNEL], ((0, 32 - CONF_KERNEL), (0, 0)))
        sw = jnp.pad(conv_all[l, CONF_KERNEL:], ((0, 8 - SHORT_KERNEL), (0, 0)))
        vecs = jnp.stack([conf_conv_b[l], conf_ln_g[l], conf_ln_b[l], pool_scale[l], gmlp_ln_g[l], gmlp_ln_b[l],
                          jnp.zeros((g,), F32), jnp.zeros((g,), F32)])
        pool_bd = _block_diag(pool_w[l]).astype(BF16)
        low = jnp.tril(jnp.ones((CHUNK, CHUNK), bool))
        ws = jnp.where(low[None], gmlp_w_s[l], 0.0).astype(BF16)
        wcat = ws.transpose(1, 0, 2).reshape(CHUNK, nh * CHUNK)
        wcat_t = ws.transpose(0, 2, 1).reshape(nh * CHUNK, CHUNK)
        bst = jnp.repeat(gmlp_b_s[l].T, HEAD_DIM, axis=1)
        return cw, vecs, sw, pool_bd, wcat, bst, wcat_t

    xs = x[0]
    saved = []
    for l in range(depth):
        sm = small_inputs(l)
        (x1, a1, b1), ((w_f2,), (w_mx,)) = _ffn_fwd(xs, ffn1_norm[l], w_f1[l], fs, f"ffn1_fwd_{l}",
                                                    riders=[_Gather(set_f2[l]), _Gather(set_mx[l])])
        x2, mix, h2, p = _mix_fwd(x1, mix_norm[l], w_mx, lay, l, sm[:6], f"mix_fwd_{l}")
        nxt = [_Gather(set_f1[l + 1])] if l + 1 < depth else []
        (x3, a2, b2), got = _ffn_fwd(x2, ffn2_norm[l], w_f2, fs, f"ffn2_fwd_{l}", riders=nxt)
        if nxt:
            w_f1[l + 1] = got[0][0]
        saved.append((xs, a1, b1, x1, h2, p, mix, x2, a2, b2, sm, w_f2, w_mx))
        xs = x3

    dx, loss_part, dgf = _loss_head(xs, loss_target[0], final_norm, "loss_head")
    loss = lax.psum(loss_part[0, 0], ("x", "y", "c"))

    def small_rows(parts):
        buf, at = _f32_rows(parts, d, None)
        return jnp.pad(buf, ((0, -buf.shape[0] % 16), (0, 0))), at

    small = [None] * depth
    big = ('ffn1_w1', 'ffn1_w3', 'ffn1_w2', 'ffn2_w1', 'ffn2_w3', 'ffn2_w2', 'w_in', 'w_out')
    slots = {n: [None] * depth for n in big}
    f1_names, f2_names, mx_names = big[0:3], big[3:6], big[6:8]

    def exchanges(blocks):
        return [_Exchange(b_) for b_ in blocks]

    def landed(names, l, rides):
        for n, (got_,) in zip(names, rides):
            slots[n][l] = got_

    pending = None
    for l in reversed(range(depth)):
        x0, a1, b1, x1, h2, p, mix, x2, a2, b2, sm, w_f2, w_mx = saved[l]
        ride = exchanges(pending) if pending is not None else []
        (dx2, da, db, u, h, dgn2), got = _ffn_bwd(x2, dx, a2, b2, ffn2_norm[l], w_f2, fs, f"ffn2_bwd_{l}", riders=ride)
        if ride:
            landed(f1_names, l + 1, got)
        blocks = [_wgrad(h, da, 1.0, False, f"ffn2_dw1_{l}"), _wgrad(h, db, 1.0, False, f"ffn2_dw3_{l}"),
                  _wgrad(u, dx, FFN_RESIDUAL, True, f"ffn2_dw2_{l}")]
        ride = exchanges(blocks)
        if l == 0 and depth > 1:
            early, early_offs = small_rows([a for k in range(1, depth) for a in small[k]] + [dgf[0]])
            ride.append(_Gather(early))
        (dp, dcw, dvec, dsw, dpbd, dws, dbs, dx1, dgm), got = _mix_bwd(
            p, dx2, x1, mix_norm[l], w_mx, lay, l, sm, f"mix_bwd_{l}", riders=ride)
        landed(f2_names, l, got[:3])
        early_all = got[3][0] if l == 0 and depth > 1 else None
        blocks = [_wgrad(h2, dp, 1.0, False, f"dw_in_{l}"), _wgrad(mix, dx2, 1.0, True, f"dw_out_{l}")]
        ride = exchanges(blocks) if l > 0 else []
        (dx, da, db, u, h, dgn1), got = _ffn_bwd(x0, dx1, a1, b1, ffn1_norm[l], w_f1[l], fs,
                                                 f"ffn1_bwd_{l}", riders=ride)
        if l > 0:
            landed(mx_names, l, got)
        pg = g // len(POOL_WINDOWS)
        dpool = jnp.stack([dpbd[k * pg:(k + 1) * pg, k * pg:(k + 1) * pg] for k in range(len(POOL_WINDOWS))])
        small[l] = [dgn1[0], dgm[0], dgn2[0], dvec, dcw, dsw, dpool, dws, dbs[:, :nh].T]
        if l > 0:
            pending = [_wgrad(h, da, 1.0, False, f"ffn1_dw1_{l}"), _wgrad(h, db, 1.0, False, f"ffn1_dw3_{l}"),
                       _wgrad(u, dx1, FFN_RESIDUAL, True, f"ffn1_dw2_{l}")]
    grad_x = dx[None]
    late, late_offs = small_rows(small[0] if depth > 1 else small[0] + [dgf[0]])
    g_a, got = _wgrad(h, da, 1.0, False, "ffn1_dw1_0", riders=exchanges(blocks))
    landed(mx_names, 0, got)
    g_b, got = _wgrad(h, db, 1.0, False, "ffn1_dw3_0", riders=[_Exchange(g_a), _Gather(late)])
    slots['ffn1_w1'][0], late_all = got[0][0], got[1][0]
    g_c, got = _wgrad(u, dx1, FFN_RESIDUAL, True, "ffn1_dw2_0", riders=[_Exchange(g_b)])
    slots['ffn1_w3'][0] = got[0][0]
    slots['ffn1_w2'][0] = _alone(_Exchange(g_c), "exchange_last")

    grads, deltas, new_m, new_v = {}, {}, {}, {}
    for n in big:
        grads[n], deltas[n], new_m[n], new_v[n] = _reduce_adamw(slots[n], w[n], mom1[n], mom2[n], f"reduce_adamw_{n}")

    per_layer = len(small[0])
    shapes = [(d,), (d,), (d,), (8, g), (32, g), (8, g), (len(POOL_WINDOWS), g // len(POOL_WINDOWS),
              g // len(POOL_WINDOWS)), (nh * CHUNK, CHUNK), (nh, CHUNK)]
    late_sum = _sum_slots(late_all, "sum_small_late")
    early_sum = _sum_slots(early_all, "sum_small_early") if depth > 1 else None
    got = [[_take(late_sum, late_offs[k], shapes[k]) for k in range(per_layer)]]
    got += [[_take(early_sum, early_offs[(l - 1) * per_layer + k], shapes[k]) for k in range(per_layer)]
            for l in range(1, depth)]
    col = lambda k: jnp.stack([got[l][k] for l in range(depth)])
    grads['ffn1_norm'], grads['mix_norm'], grads['ffn2_norm'] = col(0), col(1), col(2)
    dvec_all = col(3)
    for k, n in enumerate(('conf_conv_b', 'conf_ln_g', 'conf_ln_b', 'pool_scale', 'gmlp_ln_g', 'gmlp_ln_b')):
        grads[n] = dvec_all[:, k]
    grads['conf_conv_w'] = lax.dynamic_slice_in_dim(col(4)[:, :CONF_KERNEL], me * cs, cs, axis=2)
    grads['sconv_w'] = lax.dynamic_slice_in_dim(col(5)[:, :SHORT_KERNEL], me * cs, cs, axis=2)
    grads['pool_w'] = col(6)
    grads['gmlp_w_s'] = col(7).reshape(depth, nh, CHUNK, CHUNK)
    grads['gmlp_b_s'] = col(8)
    grads['final_norm'] = (_take(early_sum, early_offs[-1], (d,)) if depth > 1
                           else _take(late_sum, late_offs[-1], (d,)))

    for n in WEIGHTS:
        if n not in big:
            deltas[n], new_m[n], new_v[n] = _adamw(w[n], grads[n], mom1[n], mom2[n], f"adamw_{n}")

    return (loss, grad_x, *[grads[n] for n in WEIGHTS], *[deltas[n] for n in WEIGHTS],
            *[new_m[n] for n in WEIGHTS], *[new_v[n] for n in WEIGHTS])
```

```python
import functools

import jax
import jax.numpy as jnp
from jax import lax
from jax.experimental import pallas as pl
from jax.experimental.pallas import tpu as pltpu

F32 = jnp.float32
BF16 = jnp.bfloat16
MESH = pl.DeviceIdType.MESH

N_DEV = 8
EPS = 1e-6
FFN_RESIDUAL = 0.5
CONF_KERNEL = 31
SHORT_KERNEL = 3
POOL_WINDOWS = (2, 4, 8, 16)
CHUNK = 128
HEAD_DIM = 64
N_IN_PIECES = 8
HALO = 32
CONV_ROWS = 16
ADAM_LR = 0.001
ADAM_B1 = 0.9
ADAM_B2 = 0.999
ADAM_EPS = 1e-08
ADAM_WD = 0.01
ADAM_STEP = 10
VMEM_LIMIT_V7X = 56 * 1024 * 1024

WEIGHTS = ['ffn1_norm', 'ffn1_w1', 'ffn1_w3', 'ffn1_w2', 'mix_norm', 'w_in', 'conf_conv_w', 'conf_conv_b',
           'conf_ln_g', 'conf_ln_b', 'sconv_w', 'pool_w', 'pool_scale', 'gmlp_ln_g', 'gmlp_ln_b', 'gmlp_w_s',
           'gmlp_b_s', 'w_out', 'ffn2_norm', 'ffn2_w1', 'ffn2_w3', 'ffn2_w2', 'final_norm']
SHARDED_ROWS = ('ffn1_w2', 'ffn2_w2', 'w_out')
SHARDED_COLS = ('ffn1_w1', 'ffn1_w3', 'ffn2_w1', 'ffn2_w3', 'w_in')
SHARDED_CHAN = ('conf_conv_w', 'sconv_w')


def _params(sem=None):
    return pltpu.CompilerParams(dimension_semantics=sem, vmem_limit_bytes=VMEM_LIMIT_V7X)


def _nn(a, b):
    return jnp.dot(a, b, preferred_element_type=F32)


def _nt(a, b):
    return lax.dot_general(a, b, (((1,), (1,)), ((), ())), preferred_element_type=F32)


def _tn(a, b):
    return lax.dot_general(a, b, (((0,), (0,)), ((), ())), preferred_element_type=F32)


def _sigmoid(x):
    return 0.5 * jnp.tanh(0.5 * x) + 0.5


def _row_tile(n, pref, mult=8):
    t = min(n, pref)
    while n % t or t % mult:
        t -= 1
    return t


def _chunks(n, size):
    out, s = [], 0
    while s < n:
        out.append((s, min(size, n - s)))
        s += size
    return out


def _const(shape):
    return pl.BlockSpec(shape, lambda i: (0,) * len(shape))


def _resident(shape, index):
    return pl.BlockSpec(shape, lambda i: index, pipeline_mode=pl.Buffered(1))


def _wspec(rows, d, blk):
    return _resident((N_DEV, rows, d), (0, blk, 0))


def _layernorm_stats(x):
    mu = jnp.mean(x, axis=-1, keepdims=True)
    xc = x - mu
    r = lax.rsqrt(jnp.mean(xc * xc, axis=-1, keepdims=True) + EPS)
    return xc * r, r


def _layernorm_bwd(dy, g, xhat, r):
    dxh = dy * g
    return r * (dxh - jnp.mean(dxh, axis=-1, keepdims=True) - xhat * jnp.mean(dxh * xhat, axis=-1, keepdims=True))


def _rmsnorm_stats(x):
    r = lax.rsqrt(jnp.mean(x * x, axis=-1, keepdims=True) + EPS)
    return x * r, r


def _rmsnorm_bwd(dh, g, xhat, r):
    dxh = dh * g
    return r * (dxh - xhat * jnp.mean(dxh * xhat, axis=-1, keepdims=True))


class _MixRows:
    def __init__(self, g, wo):
        assert g % wo == 0 and wo % 16 == 0
        self.g, self.wo = g, wo

    def win(self, l):
        return 0

    def wout(self, l):
        return self.g // self.wo


class _Gather:
    def __init__(self, shard):
        self.operands = [shard]
        self.out_shape = [jax.ShapeDtypeStruct((N_DEV,) + shard.shape, shard.dtype)]
        self.scratch = [pltpu.SemaphoreType.DMA((7,)), pltpu.SemaphoreType.DMA((7,)), pltpu.SemaphoreType.DMA]

    def _copies(self, ins, outs, scr, arrivals):
        (x_ref,), (out_ref,), (send_sems, recv_sems, local_sem) = ins, outs, scr
        x, y, c = lax.axis_index("x"), lax.axis_index("y"), lax.axis_index("c")
        me, sibling = (x, y, c), (x, y, 1 - c)
        chips = [(1 - x, y), (x, 1 - y), (1 - x, 1 - y)]

        def slot(px, py, pc):
            return out_ref.at[4 * px + 2 * py + pc]

        def copy(k, block, to, src=None):
            return pltpu.make_async_remote_copy(
                src_ref=slot(*block) if src is None else src, dst_ref=slot(*block),
                send_sem=send_sems.at[k], recv_sem=recv_sems.at[k], device_id=to, device_id_type=MESH)

        mine = pltpu.make_async_copy(x_ref, slot(*me), local_sem)
        first = [copy(0, me, sibling, src=x_ref)]
        first += [copy(1 + j, me, (*chip, c), src=x_ref) for j, chip in enumerate(chips)]
        passed = [copy(4 + j, (*chip, c), sibling) for j, chip in enumerate(chips)]
        if arrivals == "ici":
            landing = [copy(1 + j, (*chip, c), me) for j, chip in enumerate(chips)]
        else:
            landing = [copy(0, sibling, me)] + [copy(4 + j, (*chip, 1 - c), me) for j, chip in enumerate(chips)]
        return mine, first, passed, landing

    def begin(self, i, n, ins, outs, scr):
        mine, first, passed, over_ici = self._copies(ins, outs, scr, "ici")

        @pl.when(i == 0)
        def _():
            mine.start()
            for cp in first:
                cp.start()

        @pl.when(i == (n - 3 if n >= 6 else n // 2))
        def _():
            for landed, onward in zip(over_ici, passed):
                landed.wait_recv()
                onward.start()

    def end(self, i, n, ins, outs, scr):
        mine, first, passed, over_d2d = self._copies(ins, outs, scr, "d2d")

        @pl.when(i == n - 1)
        def _():
            for cp in over_d2d:
                cp.wait_recv()
            for cp in first + passed:
                cp.wait_send()
            mine.wait()


class _Exchange:
    def __init__(self, blocks):
        self.operands = [blocks]
        self.out_shape = [jax.ShapeDtypeStruct(blocks.shape, blocks.dtype)]
        self.scratch = [pltpu.SemaphoreType.DMA((7,)), pltpu.SemaphoreType.DMA((7,)), pltpu.SemaphoreType.DMA]

    def _copies(self, ins, outs, scr, with_arrivals):
        (s_ref,), (r_ref,), (send_sems, recv_sems, local_sem) = ins, outs, scr
        x, y, c = lax.axis_index("x"), lax.axis_index("y"), lax.axis_index("c")
        me = 4 * x + 2 * y + c
        mine = pltpu.make_async_copy(s_ref.at[me], r_ref.at[me], local_sem)
        sends, recvs = [], []
        for k in range(1, N_DEV):
            px = (1 - x) if (k >> 2) & 1 else x
            py = (1 - y) if (k >> 1) & 1 else y
            pc = (1 - c) if k & 1 else c
            peer = 4 * px + 2 * py + pc
            for dst, group in ((r_ref.at[me], sends), (r_ref.at[peer], recvs)):
                if group is sends or with_arrivals:
                    group.append(pltpu.make_async_remote_copy(
                        src_ref=s_ref.at[peer], dst_ref=dst, send_sem=send_sems.at[k - 1],
                        recv_sem=recv_sems.at[k - 1], device_id=(px, py, pc), device_id_type=MESH))
        return mine, sends, recvs

    def begin(self, i, n, ins, outs, scr):
        mine, sends, _ = self._copies(ins, outs, scr, False)

        @pl.when(i == 0)
        def _():
            mine.start()
            for cp in sends:
                cp.start()

    def end(self, i, n, ins, outs, scr):
        mine, sends, recvs = self._copies(ins, outs, scr, True)

        @pl.when(i == n - 1)
        def _():
            for cp in recvs:
                cp.wait_recv()
            for cp in sends:
                cp.wait_send()
            mine.wait()


_ANY = pl.BlockSpec(memory_space=pl.ANY)


def _call(body, name, steps, operands, in_specs, out_shape, out_specs, scratch=(), riders=(), aliases=None):
    n_in, n_out, n_scr = len(operands), len(out_shape), len(scratch)

    def full(*refs):
        pos = [0]

        def take(k):
            pos[0] += k
            return refs[pos[0] - k:pos[0]]

        h_in, r_in = take(n_in), [take(len(r.operands)) for r in riders]
        h_out, r_out = take(n_out), [take(len(r.out_shape)) for r in riders]
        h_scr, r_scr = take(n_scr), [take(len(r.scratch)) for r in riders]
        i = pl.program_id(0)
        for r, a, b, c in zip(riders, r_in, r_out, r_scr):
            r.begin(i, steps, a, b, c)
        body(*h_in, *h_out, *h_scr)
        for r, a, b, c in zip(riders, r_in, r_out, r_scr):
            r.end(i, steps, a, b, c)

    outs = pl.pallas_call(
        full, name=name, grid=(steps,),
        out_shape=tuple(out_shape) + tuple(s for r in riders for s in r.out_shape),
        in_specs=list(in_specs) + [_ANY for r in riders for _ in r.operands],
        out_specs=tuple(out_specs) + tuple(_ANY for r in riders for _ in r.out_shape),
        scratch_shapes=list(scratch) + [s for r in riders for s in r.scratch],
        input_output_aliases=aliases or {},
        compiler_params=_params(("arbitrary",)),
    )(*operands, *(o for r in riders for o in r.operands))
    host, rest = outs[:n_out], list(outs[n_out:])
    rides = []
    for r in riders:
        rides.append(rest[:len(r.out_shape)])
        rest = rest[len(r.out_shape):]
    return host, rides


def _alone(rider, name):
    return _call(lambda: None, name, 1, [], [], [], [], riders=[rider])[1][0][0]


def _sum_slots(slots, name):
    _, rows, cols = slots.shape
    tr = _row_tile(rows, 256, 16)

    def body(s_ref, o_ref):
        acc = s_ref[0].astype(F32)
        for j in range(1, N_DEV):
            acc = acc + s_ref[j].astype(F32)
        o_ref[...] = acc

    return pl.pallas_call(
        body, name=name, grid=(rows // tr,),
        out_shape=jax.ShapeDtypeStruct((rows, cols), F32),
        in_specs=[pl.BlockSpec((N_DEV, tr, cols), lambda i: (0, i, 0))],
        out_specs=pl.BlockSpec((tr, cols), lambda i: (i, 0)),
        compiler_params=_params(("parallel",)),
    )(slots)


def _ffn_fwd(x, gain, wset, fs, name, riders=()):
    t, d = x.shape
    f = N_DEV * fs
    tm = _row_tile(t, 512, 128)
    cols = _chunks(f, 1024)

    def body(x_ref, g_ref, w1_ref, w3_ref, w2_ref, y_ref, a_ref, b_ref, u_ref):
        xv = x_ref[...]
        xhat, _ = _rmsnorm_stats(xv)
        h = (xhat * g_ref[...]).astype(BF16)
        w1 = w1_ref[...].reshape(f, d)
        w3 = w3_ref[...].reshape(f, d)
        for s, n in cols:
            a = _nt(h, w1[s:s + n, :])
            b = _nt(h, w3[s:s + n, :])
            a_ref[:, s:s + n] = a.astype(BF16)
            b_ref[:, s:s + n] = b.astype(BF16)
            u_ref[:, s:s + n] = (a * _sigmoid(a) * b).astype(BF16)
        y_ref[...] = xv + FFN_RESIDUAL * _nn(u_ref[...], w2_ref[...].reshape(f, d))

    return _call(
        body, name, t // tm, [x, gain.reshape(1, d), wset, wset, wset],
        [pl.BlockSpec((tm, d), lambda i: (i, 0)), _const((1, d)),
         _wspec(fs, d, 0), _wspec(fs, d, 1), _wspec(fs, d, 2)],
        [jax.ShapeDtypeStruct((t, d), F32), jax.ShapeDtypeStruct((t, f), BF16), jax.ShapeDtypeStruct((t, f), BF16)],
        [pl.BlockSpec((tm, d), lambda i: (i, 0)), pl.BlockSpec((tm, f), lambda i: (i, 0)),
         pl.BlockSpec((tm, f), lambda i: (i, 0))],
        scratch=[pltpu.VMEM((tm, f), BF16)], riders=riders)


def _ffn_bwd(x, dy, a, b, gain, wset, fs, name, riders=()):
    t, d = x.shape
    f = N_DEV * fs
    tm = _row_tile(t, 256, 128)
    cols = _chunks(f, 1024)

    def body(x_ref, dy_ref, a_ref, b_ref, g_ref, w1_ref, w3_ref, w2_ref,
             dx_ref, da_ref, db_ref, u_ref, h_ref, dg_ref):
        @pl.when(pl.program_id(0) == 0)
        def _():
            dg_ref[...] = jnp.zeros_like(dg_ref)

        xv, dyv, g = x_ref[...], dy_ref[...], g_ref[...]
        xhat, r = _rmsnorm_stats(xv)
        h_ref[...] = (xhat * g).astype(BF16)
        dyb = (FFN_RESIDUAL * dyv).astype(BF16)
        w1 = w1_ref[...].reshape(f, d)
        w3 = w3_ref[...].reshape(f, d)
        w2 = w2_ref[...].reshape(f, d)
        dh = jnp.zeros((tm, d), F32)
        for s, n in cols:
            du = _nt(dyb, w2[s:s + n, :])
            av = a_ref[:, s:s + n].astype(F32)
            bv = b_ref[:, s:s + n].astype(F32)
            sig = _sigmoid(av)
            sa = av * sig
            u_ref[:, s:s + n] = (sa * bv).astype(BF16)
            da = (du * bv * (sig + sa * (1.0 - sig))).astype(BF16)
            db = (du * sa).astype(BF16)
            da_ref[:, s:s + n] = da
            db_ref[:, s:s + n] = db
            dh = dh + _nn(da, w1[s:s + n, :]) + _nn(db, w3[s:s + n, :])
        dx_ref[...] = dyv + _rmsnorm_bwd(dh, g, xhat, r)
        dg_ref[0:1, :] += jnp.sum(dh * xhat, axis=0, keepdims=True)

    big = jax.ShapeDtypeStruct((t, f), BF16)
    row = lambda w: pl.BlockSpec((tm, w), lambda i: (i, 0))
    return _call(
        body, name, t // tm, [x, dy, a, b, gain.reshape(1, d), wset, wset, wset],
        [row(d), row(d), row(f), row(f), _const((1, d)), _wspec(fs, d, 0), _wspec(fs, d, 1), _wspec(fs, d, 2)],
        [jax.ShapeDtypeStruct((t, d), F32), big, big, big, jax.ShapeDtypeStruct((t, d), BF16),
         jax.ShapeDtypeStruct((8, d), F32)],
        [row(d), row(f), row(f), row(f), row(d), _const((8, d))], riders=riders)


def _wgrad(lhs, rhs, gbuf, rows, blk, scale, name, riders=()):
    t, m = lhs.shape
    d = rhs.shape[1]
    bt = _row_tile(t, 1024, 128)
    nk = t // bt
    assert m == N_DEV * rows

    def body(a_ref, b_ref, g_in, o_ref, acc_ref):
        del g_in
        k = pl.program_id(0)

        @pl.when(k == 0)
        def _():
            acc_ref[...] = jnp.zeros_like(acc_ref)

        acc_ref[...] += _tn(a_ref[...], b_ref[...].astype(BF16))

        @pl.when(k == nk - 1)
        def _():
            o_ref[...] = (scale * acc_ref[...]).astype(BF16).reshape(N_DEV, rows, d)

    (out,), rides = _call(
        body, name, nk, [lhs, rhs, gbuf],
        [pl.BlockSpec((bt, m), lambda k: (k, 0)), pl.BlockSpec((bt, d), lambda k: (k, 0)), _ANY],
        [jax.ShapeDtypeStruct(gbuf.shape, gbuf.dtype)], [pl.BlockSpec((N_DEV, rows, d), lambda k: (0, blk, 0))],
        scratch=[pltpu.VMEM((m, d), F32)], riders=riders, aliases={2: 0})
    return (out, rides) if riders else out


def _head_masks(rows, g):
    lane = lax.broadcasted_iota(jnp.int32, (rows, g), 1)
    return [(lane >= h * HEAD_DIM) & (lane < (h + 1) * HEAD_DIM) for h in range(g // HEAD_DIM)]


def _pool_windows(rows, g):
    lane = lax.broadcasted_iota(jnp.int32, (rows, g), 1)
    pg = g // len(POOL_WINDOWS)
    w = jnp.full((rows, g), float(POOL_WINDOWS[-1]), F32)
    for k in range(len(POOL_WINDOWS) - 2, -1, -1):
        w = jnp.where(lane < (k + 1) * pg, float(POOL_WINDOWS[k]), w)
    return w


def _window_sum(ref, tmp, first, rows, wl, direction):
    total = ref.shape[0]
    src, out, shift = ref, None, 1
    for k, w in enumerate(POOL_WINDOWS):
        assert w == 2 * shift and 8 * (k + 1) <= HALO
        if k + 1 < len(POOL_WINDOWS):
            n = total - 8 * (k + 1)
            lo = 8 * (k + 1) if direction < 0 else 0
            dst = tmp.at[k % 2]
            dst[pl.ds(lo, n), :] = src[pl.ds(lo, n), :] + src[pl.ds(lo + direction * shift, n), :]
            src = dst
            level = dst[pl.ds(first, rows), :]
        else:
            level = src[pl.ds(first, rows), :] + src[pl.ds(first + direction * shift, rows), :]
        out = level if out is None else jnp.where(wl >= float(w), level, out)
        shift = w
    return out


def _shifted_copies(src_ref, dst_ref):
    rows = dst_ref.shape[1]
    for b in range(1, 8):
        dst_ref[b - 1] = src_ref[pl.ds(b, rows), :]


def _window(src_ref, shifted_ref, off, rows):
    b = off % 8
    if b == 0:
        return src_ref[pl.ds(off, rows), :]
    return shifted_ref[b - 1, pl.ds(off - b, rows), :]


def _pool_divisor(first_row, rows, g):
    pos1 = (lax.broadcasted_iota(jnp.int32, (rows, g), 0) + first_row + 1).astype(F32)
    return jnp.minimum(pos1, _pool_windows(rows, g))


def _spatial_mix(vq, wcat_ref, masks):
    vstack = jnp.concatenate([jnp.where(m, vq, jnp.zeros_like(vq)) for m in masks], axis=0)
    return _nn(wcat_ref[...], vstack)


def _mix_fwd(x, gain, wbuf, lay, l, small, name):
    t, d = x.shape
    g = lay.g
    di = N_DEV * g
    tm = _row_tile(t, 256, CHUNK)
    conv_w, vecs, sconv_w, pool_bd, wcat, bst = small

    def body(x_ref, g_ref, wi_ref, wo_ref, cw_ref, vec_ref, sw_ref, pbd_ref, wcat_ref, bst_ref,
             y_ref, mix_ref, h_ref, pm_ref, pp_ref, ea_ref, eb_ref, ec_ref, sh_ref, tmp_ref):
        i = pl.program_id(0)

        @pl.when(i == 0)
        def _():
            pp_ref[...] = jnp.zeros_like(pp_ref)

        xhat, _ = _rmsnorm_stats(x_ref[...])
        h = (xhat * g_ref[...]).astype(BF16)
        h_ref[...] = h
        pm_ref[...] = _nt(h, wi_ref[...].reshape(di, d)).astype(BF16)

        def prev(j):
            return pp_ref[:, j * g:(j + 1) * g].astype(F32)

        def main(j):
            return pm_ref[:, j * g:(j + 1) * g].astype(F32)

        vec = vec_ref[...]
        ea_ref[0:HALO, :] = prev(0) * _sigmoid(prev(1))
        ea_ref[HALO:, :] = main(0) * _sigmoid(main(1))
        _shifted_copies(ea_ref, sh_ref)
        cw = cw_ref[...]
        c = jnp.zeros((tm, g), F32) + vec[0:1, :]
        for k in range(CONF_KERNEL):
            c = c + cw[k:k + 1, :] * _window(ea_ref, sh_ref, HALO - (CONF_KERNEL - 1) + k, tm)
        chat, _ = _layernorm_stats(c)
        ln = chat * vec[1:2, :] + vec[2:3, :]
        mix_ref[:, 0:g] = (ln * _sigmoid(ln)).astype(BF16)
        eb_ref[0:HALO, :] = prev(3) * prev(4)
        eb_ref[HALO:, :] = main(3) * main(4)
        sw = sw_ref[...]
        cz = jnp.zeros((tm, g), F32)
        for k in range(SHORT_KERNEL):
            cz = cz + sw[k:k + 1, :] * eb_ref[pl.ds(HALO - (SHORT_KERNEL - 1) + k, tm), :]
        mix_ref[:, g:2 * g] = (main(2) * cz).astype(BF16)
        ec_ref[0:HALO, :] = prev(5)
        xp = main(5)
        ec_ref[HALO:, :] = xp
        wl = _pool_windows(tm, g)
        mean = _window_sum(ec_ref, tmp_ref, HALO, tm, wl, -1) / _pool_divisor(i * tm, tm, g)
        yc = _nn((mean - xp).astype(BF16), pbd_ref[...])
        mix_ref[:, 2 * g:3 * g] = (yc * vec[3:4, :]).astype(BF16)
        vhat, _ = _layernorm_stats(main(7))
        v = (vhat * vec[4:5, :] + vec[5:6, :]).astype(BF16)
        gu = main(6)
        masks = _head_masks(CHUNK, g)
        for q in range(tm // CHUNK):
            rs = slice(q * CHUNK, (q + 1) * CHUNK)
            mixed = _spatial_mix(v[rs, :], wcat_ref, masks) + bst_ref[...]
            mix_ref[rs, 3 * g:4 * g] = (gu[rs, :] * mixed).astype(BF16)
        y_ref[...] = x_ref[...] + _nn(mix_ref[...], wo_ref[...].reshape(d, d))
        pp_ref[...] = pm_ref[tm - HALO:, :]

    row = lambda w: pl.BlockSpec((tm, w), lambda i: (i, 0))
    return pl.pallas_call(
        body, name=name, grid=(t // tm,),
        out_shape=(jax.ShapeDtypeStruct((t, d), F32), jax.ShapeDtypeStruct((t, 4 * g), BF16),
                   jax.ShapeDtypeStruct((t, d), BF16), jax.ShapeDtypeStruct((t, di), BF16)),
        in_specs=[row(d), _const((1, d)), _wspec(g, d, lay.win(l)), _wspec(lay.wo, d, lay.wout(l)),
                  _const(conv_w.shape), _const(vecs.shape), _const(sconv_w.shape), _const(pool_bd.shape),
                  _const(wcat.shape), _const(bst.shape)],
        out_specs=(row(d), row(4 * g), row(d), row(di)),
        scratch_shapes=[pltpu.VMEM((HALO, di), BF16)] + [pltpu.VMEM((HALO + tm, g), F32)] * 3
        + [pltpu.VMEM((7, HALO + tm - 8, g), F32), pltpu.VMEM((2, HALO + tm, g), F32)],
        compiler_params=_params(("arbitrary",)),
    )(x, gain.reshape(1, d), wbuf, wbuf, conv_w, vecs, sconv_w, pool_bd, wcat, bst)


def _mix_bwd(p, dy, x, gain, wbuf, lay, l, small, name, riders=()):
    t, d = dy.shape
    g = lay.g
    di = N_DEV * g
    tm = _row_tile(t, 256, CHUNK)
    nt = t // tm
    nh = g // HEAD_DIM
    per = tm // HALO
    ext = tm + HALO
    conv_w, vecs, sconv_w, pool_bd, wcat, bst, wcat_t = small

    def body(pp_ref, pm_ref, pn_ref, dy_ref, dyn_ref, x_ref, g_ref, wi_ref, wo_ref, cw_ref, vec_ref, sw_ref, pbd_ref,
             wcat_ref, bst_ref, wcatt_ref, dp_ref, dcw_ref, dvec_ref, dsw_ref, dpbd_ref, dws_ref, dbs_ref,
             dx_ref, dg_ref, ea_ref, eb_ref, ec_ref, fa_ref, fb_ref, fc_ref, sh_ref, sf_ref, tmp_ref):
        i = pl.program_id(0)
        seen = (i > 0).astype(F32)
        more = (i < nt - 1).astype(F32)

        @pl.when(i == 0)
        def _():
            for ref in (dcw_ref, dvec_ref, dsw_ref, dpbd_ref, dws_ref, dbs_ref, dg_ref):
                ref[...] = jnp.zeros_like(ref)

        def prev(j):
            return pp_ref[:, j * g:(j + 1) * g].astype(F32) * seen

        def main(j):
            return pm_ref[:, j * g:(j + 1) * g].astype(F32)

        def nxt(j):
            return pn_ref[:, j * g:(j + 1) * g].astype(F32)

        def colsum(v):
            return jnp.sum(v, axis=0, keepdims=True)

        vec = vec_ref[...]
        wo = wo_ref[...].reshape(d, d)
        dmix = jnp.concatenate([_nt(dy_ref[...].astype(BF16), wo),
                                _nt((dyn_ref[...] * more).astype(BF16), wo)], axis=0)

        val, gate = main(0), main(1)
        sg = _sigmoid(gate)
        ea_ref[0:HALO, :] = prev(0) * _sigmoid(prev(1))
        ea_ref[HALO:ext, :] = val * sg
        ea_ref[ext:, :] = nxt(0) * _sigmoid(nxt(1))
        _shifted_copies(ea_ref, sh_ref)
        cw = cw_ref[...]
        c = jnp.zeros((ext, g), F32) + vec[0:1, :]
        for k in range(CONF_KERNEL):
            c = c + cw[k:k + 1, :] * _window(ea_ref, sh_ref, HALO - (CONF_KERNEL - 1) + k, ext)
        chat, r = _layernorm_stats(c)
        ln = chat * vec[1:2, :] + vec[2:3, :]
        sl = _sigmoid(ln)
        dln = dmix[:, 0:g] * (sl * (1.0 + ln * (1.0 - sl)))
        dvec_ref[1:2, :] += colsum(dln[0:tm] * chat[0:tm])
        dvec_ref[2:3, :] += colsum(dln[0:tm])
        dc = _layernorm_bwd(dln, vec[1:2, :], chat, r)
        dvec_ref[0:1, :] += colsum(dc[0:tm])
        fa_ref[...] = dc
        _shifted_copies(fa_ref, sf_ref)
        dcm = dc[0:tm]
        dy0 = jnp.zeros((tm, g), F32)
        for k in range(CONF_KERNEL):
            dy0 = dy0 + cw[k:k + 1, :] * _window(fa_ref, sf_ref, CONF_KERNEL - 1 - k, tm)
            dcw_ref[k:k + 1, :] += colsum(dcm * _window(ea_ref, sh_ref, HALO - (CONF_KERNEL - 1) + k, tm))
        dp_ref[:, 0:g] = (dy0 * sg).astype(BF16)
        dp_ref[:, g:2 * g] = (dy0 * val * sg * (1.0 - sg)).astype(BF16)

        sb, sc, sx = main(2), main(3), main(4)
        eb_ref[0:HALO, :] = prev(3) * prev(4)
        eb_ref[HALO:ext, :] = sc * sx
        sw = sw_ref[...]
        cz = jnp.zeros((tm, g), F32)
        for k in range(SHORT_KERNEL):
            cz = cz + sw[k:k + 1, :] * eb_ref[pl.ds(HALO - (SHORT_KERNEL - 1) + k, tm), :]
        dob = dmix[:, g:2 * g]
        dp_ref[:, 2 * g:3 * g] = (dob[0:tm] * cz).astype(BF16)
        fb_ref[0:tm, :] = dob[0:tm] * sb
        fb_ref[tm:, :] = dob[tm:] * nxt(2)
        dczm = fb_ref[0:tm, :]
        dz = jnp.zeros((tm, g), F32)
        for k in range(SHORT_KERNEL):
            dz = dz + sw[k:k + 1, :] * fb_ref[pl.ds(SHORT_KERNEL - 1 - k, tm), :]
            dsw_ref[k:k + 1, :] += colsum(dczm * eb_ref[pl.ds(HALO - (SHORT_KERNEL - 1) + k, tm), :])
        dp_ref[:, 3 * g:4 * g] = (dz * sx).astype(BF16)
        dp_ref[:, 4 * g:5 * g] = (dz * sc).astype(BF16)

        xp = main(5)
        ec_ref[0:HALO, :] = prev(5)
        ec_ref[HALO:, :] = xp
        wl = _pool_windows(tm, g)
        dpool = ((_window_sum(ec_ref, tmp_ref, HALO, tm, wl, -1) / _pool_divisor(i * tm, tm, g)) - xp).astype(BF16)
        pbd = pbd_ref[...]
        yc = _nn(dpool, pbd)
        doc = dmix[:, 2 * g:3 * g]
        dvec_ref[3:4, :] += colsum(doc[0:tm] * yc)
        dyc = (doc * vec[3:4, :]).astype(BF16)
        dpbd_ref[...] += _tn(dpool, dyc[0:tm])
        dd = _nt(dyc, pbd)
        fc_ref[...] = dd / _pool_divisor(i * tm, ext, g)
        dp_ref[:, 5 * g:6 * g] = (_window_sum(fc_ref, tmp_ref, 0, tm, wl, 1) - dd[0:tm]).astype(BF16)

        gv = main(7)
        vhat, rv = _layernorm_stats(gv)
        v = (vhat * vec[4:5, :] + vec[5:6, :]).astype(BF16)
        gu = main(6)
        dod = dmix[0:tm, 3 * g:4 * g]
        masks = _head_masks(CHUNK, g)
        tril = (lax.broadcasted_iota(jnp.int32, (CHUNK, CHUNK), 1)
                <= lax.broadcasted_iota(jnp.int32, (CHUNK, CHUNK), 0)).astype(F32)
        tril = jnp.concatenate([tril] * nh, axis=0)
        head_col = lax.broadcasted_iota(jnp.int32, (CHUNK, CHUNK), 1)
        dvs = []
        for q in range(tm // CHUNK):
            rs = slice(q * CHUNK, (q + 1) * CHUNK)
            vq = v[rs, :]
            mixed = _spatial_mix(vq, wcat_ref, masks) + bst_ref[...]
            dp_ref[rs, 6 * g:7 * g] = (dod[rs, :] * mixed).astype(BF16)
            dmixed = dod[rs, :] * gu[rs, :]
            dmb = dmixed.astype(BF16)
            for h, m in enumerate(masks):
                hs = jnp.sum(jnp.where(m, dmixed, 0.0), axis=-1, keepdims=True)
                dbs_ref[...] += jnp.where(head_col == h, hs, 0.0)
            dmstack = jnp.concatenate([jnp.where(m, dmb, jnp.zeros_like(dmb)) for m in masks], axis=0)
            dws_ref[...] += _nt(dmstack, vq) * tril
            back = _nn(wcatt_ref[...], dmb)
            dv = jnp.zeros((CHUNK, g), F32)
            for h, m in enumerate(masks):
                dv = jnp.where(m, back[h * CHUNK:(h + 1) * CHUNK, :], dv)
            dvs.append(dv)
        dv = jnp.concatenate(dvs, axis=0)
        dvec_ref[4:5, :] += colsum(dv * vhat)
        dvec_ref[5:6, :] += colsum(dv)
        dp_ref[:, 7 * g:8 * g] = _layernorm_bwd(dv, vec[4:5, :], vhat, rv).astype(BF16)

        xhat, rx = _rmsnorm_stats(x_ref[...])
        dh = _nn(dp_ref[...], wi_ref[...].reshape(di, d))
        dx_ref[...] = dy_ref[...] + _rmsnorm_bwd(dh, g_ref[...], xhat, rx)
        dg_ref[0:1, :] += colsum(dh * xhat)

    halo_prev = pl.BlockSpec((HALO, di), lambda i: (jnp.maximum(i * per - 1, 0), 0))
    halo_next = lambda w: pl.BlockSpec((HALO, w), lambda i: (jnp.minimum((i + 1) * per, t // HALO - 1), 0))
    row = lambda w: pl.BlockSpec((tm, w), lambda i: (i, 0))
    outs = (jax.ShapeDtypeStruct((t, di), BF16), jax.ShapeDtypeStruct(conv_w.shape, F32),
            jax.ShapeDtypeStruct(vecs.shape, F32), jax.ShapeDtypeStruct(sconv_w.shape, F32),
            jax.ShapeDtypeStruct((g, g), F32), jax.ShapeDtypeStruct((nh * CHUNK, CHUNK), F32),
            jax.ShapeDtypeStruct((CHUNK, CHUNK), F32), jax.ShapeDtypeStruct((t, d), F32),
            jax.ShapeDtypeStruct((8, d), F32))
    return _call(
        body, name, nt, [p, p, p, dy, dy, x, gain.reshape(1, d), wbuf, wbuf, conv_w, vecs, sconv_w, pool_bd, wcat,
                         bst, wcat_t],
        [halo_prev, row(di), halo_next(di), row(d), halo_next(d), row(d), _const((1, d)),
         _wspec(g, d, lay.win(l)), _wspec(lay.wo, d, lay.wout(l)),
         _const(conv_w.shape), _const(vecs.shape), _const(sconv_w.shape), _const(pool_bd.shape),
         _const(wcat.shape), _const(bst.shape), _const(wcat_t.shape)],
        list(outs), [row(di)] + [_const(o.shape) for o in outs[1:7]] + [row(d), _const((8, d))],
        scratch=[pltpu.VMEM((HALO + tm + HALO, g), F32), pltpu.VMEM((HALO + tm, g), F32),
                 pltpu.VMEM((HALO + tm, g), F32), pltpu.VMEM((ext, g), F32), pltpu.VMEM((ext, g), F32),
                 pltpu.VMEM((ext, g), F32), pltpu.VMEM((7, HALO + tm + HALO - 8, g), F32),
                 pltpu.VMEM((7, ext - 8, g), F32), pltpu.VMEM((2, ext, g), F32)],
        riders=riders)


def _loss_head(x, target, gain, name):
    t, d = x.shape
    tm = _row_tile(t, 512, 8)

    def body(x_ref, t_ref, g_ref, dx_ref, loss_ref, dg_ref):
        @pl.when(pl.program_id(0) == 0)
        def _():
            loss_ref[...] = jnp.zeros_like(loss_ref)
            dg_ref[...] = jnp.zeros_like(dg_ref)

        g = g_ref[...]
        xhat, r = _rmsnorm_stats(x_ref[...])
        err = xhat * g - t_ref[...]
        loss_ref[...] += 0.5 * jnp.sum(jnp.mean(err * err, axis=-1, keepdims=True))
        dyv = err * (1.0 / d)
        dx_ref[...] = _rmsnorm_bwd(dyv, g, xhat, r)
        dg_ref[0:1, :] += jnp.sum(dyv * xhat, axis=0, keepdims=True)

    row = pl.BlockSpec((tm, d), lambda i: (i, 0))
    return pl.pallas_call(
        body, name=name, grid=(t // tm,),
        out_shape=(jax.ShapeDtypeStruct((t, d), F32), jax.ShapeDtypeStruct((8, 128), F32),
                   jax.ShapeDtypeStruct((8, d), F32)),
        in_specs=[row, row, _const((1, d))],
        out_specs=(row, _const((8, 128)), _const((8, d))),
        compiler_params=_params(("arbitrary",)),
    )(x, target, gain.reshape(1, d))


def _adam_step(w, gv, m, v):
    nm = ADAM_B1 * m + (1.0 - ADAM_B1) * gv
    nv = ADAM_B2 * v + (1.0 - ADAM_B2) * (gv * gv)
    m_hat = nm / (1.0 - ADAM_B1 ** ADAM_STEP)
    v_hat = nv / (1.0 - ADAM_B2 ** ADAM_STEP)
    return -ADAM_LR * (m_hat / (jnp.sqrt(v_hat) + ADAM_EPS) + ADAM_WD * w), nm, nv


def _reduce_adamw(slots, blks, transposed, w, m, v, name):
    depth, rows, cols = w.shape
    tr = _row_tile(rows, 256, 128 if transposed else 16)
    nt = rows // tr
    pad = -cols % 128

    def body(*refs):
        slot_refs = refs[:depth]
        w_ref, m_ref, v_ref, g_ref, d_ref, nm_ref, nv_ref = refs[depth:]
        layer = pl.program_id(0) // nt
        for l in range(depth):
            @pl.when(layer == l)
            def _(s_ref=slot_refs[l]):
                gv = s_ref[0].astype(F32)
                for j in range(1, N_DEV):
                    gv = gv + s_ref[j].astype(F32)
                if transposed:
                    if pad:
                        gv = jnp.concatenate([gv, jnp.zeros((pad, tr), F32)], axis=0)
                    gv = gv.T[:, :cols]
                g_ref[0] = gv
                d_ref[0], nm_ref[0], nv_ref[0] = _adam_step(w_ref[0], gv, m_ref[0], v_ref[0])

    def slot_spec(l):
        tile = lambda i: jnp.clip(i - l * nt, 0, nt - 1)
        if transposed:
            return pl.BlockSpec((N_DEV, cols, tr), lambda i: (0, blks[l], tile(i)))
        return pl.BlockSpec((N_DEV, tr, cols), lambda i: (0, blks[l] * nt + tile(i), 0))

    spec = pl.BlockSpec((1, tr, cols), lambda i: (i // nt, i % nt, 0))
    shape = jax.ShapeDtypeStruct(w.shape, F32)
    return pl.pallas_call(
        body, name=name, grid=(depth * nt,),
        out_shape=(shape,) * 4, in_specs=[slot_spec(l) for l in range(depth)] + [spec] * 3, out_specs=(spec,) * 4,
        compiler_params=_params(("arbitrary",)),
    )(*slots, w, m, v)


def _adamw(w, grad, m, v, name):
    shape = w.shape
    cols = shape[-1]
    rows = w.size // cols
    tr = _row_tile(rows, 512, 8) if rows > 1024 else rows

    def body(w_ref, g_ref, m_ref, v_ref, d_ref, nm_ref, nv_ref):
        d_ref[...], nm_ref[...], nv_ref[...] = _adam_step(w_ref[...], g_ref[...], m_ref[...], v_ref[...])

    spec = pl.BlockSpec((tr, cols), lambda i: (i, 0))
    flat = jax.ShapeDtypeStruct((rows, cols), F32)
    outs = pl.pallas_call(
        body, name=name, grid=(rows // tr,),
        out_shape=(flat, flat, flat), in_specs=[spec] * 4, out_specs=(spec,) * 3,
        compiler_params=_params(("parallel",)),
    )(*(a.reshape(rows, cols) for a in (w, grad, m, v)))
    return tuple(o.reshape(shape) for o in outs)


def _f32_rows(parts, d, rows):
    out, offs, at = [], [], 0
    for a in parts:
        n = -(-a.size // d)
        out.append(jnp.pad(a.reshape(-1), (0, n * d - a.size)).reshape(n, d))
        offs.append(at)
        at += n
    total = at if rows is None else rows
    assert at <= total
    if total > at:
        out.append(jnp.zeros((total - at, d), F32))
    return jnp.concatenate(out, axis=0), offs


def _take(buf, off, shape):
    n = 1
    for s in shape:
        n *= s
    d = buf.shape[1]
    return buf[off:off + -(-n // d)].reshape(-1)[:n].reshape(shape)


def _block_diag(blocks):
    n, k, _ = blocks.shape
    out = jnp.zeros((n * k, n * k), blocks.dtype)
    for i in range(n):
        out = lax.dynamic_update_slice(out, blocks[i], (i * k, i * k))
    return out


def kernel(x, ffn1_norm, ffn1_w1, ffn1_w3, ffn1_w2, mix_norm, w_in, conf_conv_w, conf_conv_b, conf_ln_g, conf_ln_b, sconv_w, pool_w, pool_scale, gmlp_ln_g, gmlp_ln_b, gmlp_w_s, gmlp_b_s, w_out, ffn2_norm, ffn2_w1, ffn2_w3, ffn2_w2, final_norm, loss_target, m_ffn1_norm, m_ffn1_w1, m_ffn1_w3, m_ffn1_w2, m_mix_norm, m_w_in, m_conf_conv_w, m_conf_conv_b, m_conf_ln_g, m_conf_ln_b, m_sconv_w, m_pool_w, m_pool_scale, m_gmlp_ln_g, m_gmlp_ln_b, m_gmlp_w_s, m_gmlp_b_s, m_w_out, m_ffn2_norm, m_ffn2_w1, m_ffn2_w3, m_ffn2_w2, m_final_norm, v_ffn1_norm, v_ffn1_w1, v_ffn1_w3, v_ffn1_w2, v_mix_norm, v_w_in, v_conf_conv_w, v_conf_conv_b, v_conf_ln_g, v_conf_ln_b, v_sconv_w, v_pool_w, v_pool_scale, v_gmlp_ln_g, v_gmlp_ln_b, v_gmlp_w_s, v_gmlp_b_s, v_w_out, v_ffn2_norm, v_ffn2_w1, v_ffn2_w3, v_ffn2_w2, v_final_norm):
    w = dict(zip(WEIGHTS, (ffn1_norm, ffn1_w1, ffn1_w3, ffn1_w2, mix_norm, w_in, conf_conv_w, conf_conv_b, conf_ln_g,
                           conf_ln_b, sconv_w, pool_w, pool_scale, gmlp_ln_g, gmlp_ln_b, gmlp_w_s, gmlp_b_s, w_out,
                           ffn2_norm, ffn2_w1, ffn2_w3, ffn2_w2, final_norm)))
    mom1 = dict(zip(WEIGHTS, (m_ffn1_norm, m_ffn1_w1, m_ffn1_w3, m_ffn1_w2, m_mix_norm, m_w_in, m_conf_conv_w,
                              m_conf_conv_b, m_conf_ln_g, m_conf_ln_b, m_sconv_w, m_pool_w, m_pool_scale, m_gmlp_ln_g,
                              m_gmlp_ln_b, m_gmlp_w_s, m_gmlp_b_s, m_w_out, m_ffn2_norm, m_ffn2_w1, m_ffn2_w3,
                              m_ffn2_w2, m_final_norm)))
    mom2 = dict(zip(WEIGHTS, (v_ffn1_norm, v_ffn1_w1, v_ffn1_w3, v_ffn1_w2, v_mix_norm, v_w_in, v_conf_conv_w,
                              v_conf_conv_b, v_conf_ln_g, v_conf_ln_b, v_sconv_w, v_pool_w, v_pool_scale, v_gmlp_ln_g,
                              v_gmlp_ln_b, v_gmlp_w_s, v_gmlp_b_s, v_w_out, v_ffn2_norm, v_ffn2_w1, v_ffn2_w3,
                              v_ffn2_w2, v_final_norm)))
    _, t, d = x.shape
    depth, _, fs = ffn1_w1.shape
    g = w_in.shape[2]
    wo = w_out.shape[1]
    nh = g // HEAD_DIM
    cs = conf_conv_w.shape[2]
    lay = _MixRows(g, wo)
    me = 4 * lax.axis_index("x") + 2 * lax.axis_index("y") + lax.axis_index("c")

    def rows_bf16(mats):
        return jnp.concatenate([m_.astype(BF16) for m_ in mats], axis=0)

    set_f1 = [rows_bf16([ffn1_w1[l].T, ffn1_w3[l].T, ffn1_w2[l]]) for l in range(depth)]
    set_f2 = [rows_bf16([ffn2_w1[l].T, ffn2_w3[l].T, ffn2_w2[l]]) for l in range(depth)]
    set_mx = [rows_bf16([w_in[l].T, w_out[l]]) for l in range(depth)]
    convs = jnp.concatenate([conf_conv_w, sconv_w], axis=1)
    nconv = convs.size
    conv_bits = jnp.pad(convs.reshape(-1), (0, CONV_ROWS * d // 2 - nconv))
    conv_rows = lax.bitcast_convert_type(conv_bits, BF16).reshape(CONV_ROWS, d)
    w_f1 = [None] * depth
    w_f1[0] = _alone(_Gather(jnp.concatenate([set_f1[0], conv_rows], axis=0)), "gather_first")

    conv_all = lax.bitcast_convert_type(w_f1[0][:, 3 * fs:, :].reshape(N_DEV, CONV_ROWS * d // 2, 2), F32)
    conv_all = conv_all[:, :nconv].reshape(N_DEV, depth, CONF_KERNEL + SHORT_KERNEL, cs)
    conv_all = conv_all.transpose(1, 2, 0, 3).reshape(depth, CONF_KERNEL + SHORT_KERNEL, g)

    def small_inputs(l):
        cw = jnp.pad(conv_all[l, :CONF_KERNEL], ((0, 32 - CONF_KERNEL), (0, 0)))
        sw = jnp.pad(conv_all[l, CONF_KERNEL:], ((0, 8 - SHORT_KERNEL), (0, 0)))
        vecs = jnp.stack([conf_conv_b[l], conf_ln_g[l], conf_ln_b[l], pool_scale[l], gmlp_ln_g[l], gmlp_ln_b[l],
                          jnp.zeros((g,), F32), jnp.zeros((g,), F32)])
        pool_bd = _block_diag(pool_w[l]).astype(BF16)
        low = jnp.tril(jnp.ones((CHUNK, CHUNK), bool))
        ws = jnp.where(low[None], gmlp_w_s[l], 0.0).astype(BF16)
        wcat = ws.transpose(1, 0, 2).reshape(CHUNK, nh * CHUNK)
        wcat_t = ws.transpose(0, 2, 1).reshape(nh * CHUNK, CHUNK)
        bst = jnp.repeat(gmlp_b_s[l].T, HEAD_DIM, axis=1)
        return cw, vecs, sw, pool_bd, wcat, bst, wcat_t

    xs = x[0]
    saved = []
    for l in range(depth):
        sm = small_inputs(l)
        (x1, a1, b1), ((w_f2,), (w_mx,)) = _ffn_fwd(xs, ffn1_norm[l], w_f1[l], fs, f"ffn1_fwd_{l}",
                                                    riders=[_Gather(set_f2[l]), _Gather(set_mx[l])])
        x2, mix, h2, p = _mix_fwd(x1, mix_norm[l], w_mx, lay, l, sm[:6], f"mix_fwd_{l}")
        nxt = [_Gather(set_f1[l + 1])] if l + 1 < depth else []
        (x3, a2, b2), got = _ffn_fwd(x2, ffn2_norm[l], w_f2, fs, f"ffn2_fwd_{l}", riders=nxt)
        if nxt:
            w_f1[l + 1] = got[0][0]
        saved.append((xs, a1, b1, x1, h2, p, mix, x2, a2, b2, sm, w_f2, w_mx))
        xs = x3

    dx, loss_part, dgf = _loss_head(xs, loss_target[0], final_norm, "loss_head")
    loss = lax.psum(loss_part[0, 0], ("x", "y", "c"))

    def small_rows(parts):
        buf, at = _f32_rows(parts, d, None)
        return jnp.pad(buf, ((0, -buf.shape[0] % 16), (0, 0))), at

    small = [None] * depth
    big = ('ffn1_w1', 'ffn1_w3', 'ffn1_w2', 'ffn2_w1', 'ffn2_w3', 'ffn2_w2', 'w_in', 'w_out')
    slots = {n: [None] * depth for n in big}
    blk_of = {'ffn1_w1': 0, 'ffn1_w3': 1, 'ffn1_w2': 2, 'ffn2_w1': 0, 'ffn2_w3': 1, 'ffn2_w2': 2,
              'w_in': lay.win(0), 'w_out': lay.wout(0)}
    blks = {n: [blk_of[n]] * depth for n in big}

    def landed(names, l, got_):
        for n in names:
            slots[n][l] = got_

    def ffn_grads(da, db, u, h, dy, tag):
        gset = lax.empty((N_DEV, 3 * fs, d), BF16)
        gset = _wgrad(da, h, gset, fs, 0, 1.0, f"{tag}_dw1")
        gset = _wgrad(db, h, gset, fs, 1, 1.0, f"{tag}_dw3")
        return _wgrad(u, dy, gset, fs, 2, FFN_RESIDUAL, f"{tag}_dw2")

    pending = None
    for l in reversed(range(depth)):
        x0, a1, b1, x1, h2, p, mix, x2, a2, b2, sm, w_f2, w_mx = saved[l]
        ride = [_Exchange(pending)] if pending is not None else []
        (dx2, da, db, u, h, dgn2), got = _ffn_bwd(x2, dx, a2, b2, ffn2_norm[l], w_f2, fs, f"ffn2_bwd_{l}", riders=ride)
        if ride:
            landed(big[0:3], l + 1, got[0][0])
        ride = [_Exchange(ffn_grads(da, db, u, h, dx, f"ffn2_{l}"))]
        if l == 0 and depth > 1:
            early, early_offs = small_rows([a for k in range(1, depth) for a in small[k]] + [dgf[0]])
            ride.append(_Gather(early))
        (dp, dcw, dvec, dsw, dpbd, dws, dbs, dx1, dgm), got = _mix_bwd(
            p, dx2, x1, mix_norm[l], w_mx, lay, l, sm, f"mix_bwd_{l}", riders=ride)
        landed(big[3:6], l, got[0][0])
        early_all = got[1][0] if l == 0 and depth > 1 else None
        gmx = lax.empty((N_DEV, g + wo, d), BF16)
        gmx = _wgrad(mix, dx2, gmx, wo, lay.wout(l), 1.0, f"dw_out_{l}")
        gmx = _wgrad(dp, h2, gmx, g, lay.win(l), 1.0, f"dw_in_{l}")
        ride = [_Exchange(gmx)] if l > 0 else []
        (dx, da, db, u, h, dgn1), got = _ffn_bwd(x0, dx1, a1, b1, ffn1_norm[l], w_f1[l], fs,
                                                 f"ffn1_bwd_{l}", riders=ride)
        if l > 0:
            landed(big[6:8], l, got[0][0])
        pg = g // len(POOL_WINDOWS)
        dpool = jnp.stack([dpbd[k * pg:(k + 1) * pg, k * pg:(k + 1) * pg] for k in range(len(POOL_WINDOWS))])
        small[l] = [dgn1[0], dgm[0], dgn2[0], dvec, dcw, dsw, dpool, dws, dbs[:, :nh].T]
        if l > 0:
            pending = ffn_grads(da, db, u, h, dx1, f"ffn1_{l}")
    grad_x = dx[None]
    late, late_offs = small_rows(small[0] if depth > 1 else small[0] + [dgf[0]])
    one = lambda: lax.empty((N_DEV, fs, d), BF16)
    g_a, got = _wgrad(da, h, one(), fs, 0, 1.0, "ffn1_0_dw1", riders=[_Exchange(gmx)])
    landed(big[6:8], 0, got[0][0])
    g_b, got = _wgrad(db, h, one(), fs, 0, 1.0, "ffn1_0_dw3", riders=[_Exchange(g_a), _Gather(late)])
    slots['ffn1_w1'][0], late_all = got[0][0], got[1][0]
    g_c, got = _wgrad(u, dx1, one(), fs, 0, FFN_RESIDUAL, "ffn1_0_dw2", riders=[_Exchange(g_b)])
    slots['ffn1_w3'][0] = got[0][0]
    slots['ffn1_w2'][0] = _alone(_Exchange(g_c), "exchange_last")
    for n in big[0:3]:
        blks[n][0] = 0

    grads, deltas, new_m, new_v = {}, {}, {}, {}
    for n in big:
        grads[n], deltas[n], new_m[n], new_v[n] = _reduce_adamw(
            slots[n], blks[n], n in SHARDED_COLS, w[n], mom1[n], mom2[n], f"reduce_adamw_{n}")

    per_layer = len(small[0])
    shapes = [(d,), (d,), (d,), (8, g), (32, g), (8, g), (len(POOL_WINDOWS), g // len(POOL_WINDOWS),
              g // len(POOL_WINDOWS)), (nh * CHUNK, CHUNK), (nh, CHUNK)]
    late_sum = _sum_slots(late_all, "sum_small_late")
    early_sum = _sum_slots(early_all, "sum_small_early") if depth > 1 else None
    got = [[_take(late_sum, late_offs[k], shapes[k]) for k in range(per_layer)]]
    got += [[_take(early_sum, early_offs[(l - 1) * per_layer + k], shapes[k]) for k in range(per_layer)]
            for l in range(1, depth)]
    col = lambda k: jnp.stack([got[l][k] for l in range(depth)])
    grads['ffn1_norm'], grads['mix_norm'], grads['ffn2_norm'] = col(0), col(1), col(2)
    dvec_all = col(3)
    for k, n in enumerate(('conf_conv_b', 'conf_ln_g', 'conf_ln_b', 'pool_scale', 'gmlp_ln_g', 'gmlp_ln_b')):
        grads[n] = dvec_all[:, k]
    grads['conf_conv_w'] = lax.dynamic_slice_in_dim(col(4)[:, :CONF_KERNEL], me * cs, cs, axis=2)
    grads['sconv_w'] = lax.dynamic_slice_in_dim(col(5)[:, :SHORT_KERNEL], me * cs, cs, axis=2)
    grads['pool_w'] = col(6)
    grads['gmlp_w_s'] = col(7).reshape(depth, nh, CHUNK, CHUNK)
    grads['gmlp_b_s'] = col(8)
    grads['final_norm'] = (_take(early_sum, early_offs[-1], (d,)) if depth > 1
                           else _take(late_sum, late_offs[-1], (d,)))

    for n in WEIGHTS:
        if n not in big:
            deltas[n], new_m[n], new_v[n] = _adamw(w[n], grads[n], mom1[n], mom2[n], f"adamw_{n}")

    return (loss, grad_x, *[grads[n] for n in WEIGHTS], *[deltas[n] for n in WEIGHTS],
            *[new_m[n] for n in WEIGHTS], *[new_v[n] for n in WEIGHTS])
```

```python
import functools

import jax
import jax.numpy as jnp
from jax import lax
from jax.experimental import pallas as pl
from jax.experimental.pallas import tpu as pltpu

F32 = jnp.float32
BF16 = jnp.bfloat16
MESH = pl.DeviceIdType.MESH

N_DEV = 8
EPS = 1e-6
FFN_RESIDUAL = 0.5
CONF_KERNEL = 31
SHORT_KERNEL = 3
POOL_WINDOWS = (2, 4, 8, 16)
CHUNK = 128
HEAD_DIM = 64
N_IN_PIECES = 8
HALO = 32
CONV_ROWS = 16
ADAM_LR = 0.001
ADAM_B1 = 0.9
ADAM_B2 = 0.999
ADAM_EPS = 1e-08
ADAM_WD = 0.01
ADAM_STEP = 10
VMEM_LIMIT_V7X = 56 * 1024 * 1024

WEIGHTS = ['ffn1_norm', 'ffn1_w1', 'ffn1_w3', 'ffn1_w2', 'mix_norm', 'w_in', 'conf_conv_w', 'conf_conv_b',
           'conf_ln_g', 'conf_ln_b', 'sconv_w', 'pool_w', 'pool_scale', 'gmlp_ln_g', 'gmlp_ln_b', 'gmlp_w_s',
           'gmlp_b_s', 'w_out', 'ffn2_norm', 'ffn2_w1', 'ffn2_w3', 'ffn2_w2', 'final_norm']
SHARDED_ROWS = ('ffn1_w2', 'ffn2_w2', 'w_out')
SHARDED_COLS = ('ffn1_w1', 'ffn1_w3', 'ffn2_w1', 'ffn2_w3', 'w_in')
SHARDED_CHAN = ('conf_conv_w', 'sconv_w')


def _params(sem=None):
    return pltpu.CompilerParams(dimension_semantics=sem, vmem_limit_bytes=VMEM_LIMIT_V7X)


def _nn(a, b):
    return jnp.dot(a, b, preferred_element_type=F32)


def _nt(a, b):
    return lax.dot_general(a, b, (((1,), (1,)), ((), ())), preferred_element_type=F32)


def _tn(a, b):
    return lax.dot_general(a, b, (((0,), (0,)), ((), ())), preferred_element_type=F32)


def _sigmoid(x):
    return 0.5 * jnp.tanh(0.5 * x) + 0.5


def _row_tile(n, pref, mult=8):
    t = min(n, pref)
    while n % t or t % mult:
        t -= 1
    return t


def _chunks(n, size):
    out, s = [], 0
    while s < n:
        out.append((s, min(size, n - s)))
        s += size
    return out


def _const(shape):
    return pl.BlockSpec(shape, lambda i: (0,) * len(shape))


def _resident(shape, index):
    return pl.BlockSpec(shape, lambda i: index, pipeline_mode=pl.Buffered(1))


def _wspec(rows, d, blk):
    return _resident((N_DEV, rows, d), (0, blk, 0))


def _layernorm_stats(x):
    mu = jnp.mean(x, axis=-1, keepdims=True)
    xc = x - mu
    r = lax.rsqrt(jnp.mean(xc * xc, axis=-1, keepdims=True) + EPS)
    return xc * r, r


def _layernorm_bwd(dy, g, xhat, r):
    dxh = dy * g
    return r * (dxh - jnp.mean(dxh, axis=-1, keepdims=True) - xhat * jnp.mean(dxh * xhat, axis=-1, keepdims=True))


def _rmsnorm_stats(x):
    r = lax.rsqrt(jnp.mean(x * x, axis=-1, keepdims=True) + EPS)
    return x * r, r


def _rmsnorm_bwd(dh, g, xhat, r):
    dxh = dh * g
    return r * (dxh - xhat * jnp.mean(dxh * xhat, axis=-1, keepdims=True))


class _MixRows:
    def __init__(self, g, wo):
        assert g % wo == 0 and wo % 16 == 0
        self.g, self.wo = g, wo

    def win(self, l):
        return 0

    def wout(self, l):
        return self.g // self.wo


class _Gather:
    def __init__(self, shard):
        self.operands = [shard]
        self.out_shape = [jax.ShapeDtypeStruct((N_DEV,) + shard.shape, shard.dtype)]
        self.scratch = [pltpu.SemaphoreType.DMA((7,)), pltpu.SemaphoreType.DMA((7,)), pltpu.SemaphoreType.DMA]

    def _copies(self, ins, outs, scr, arrivals):
        (x_ref,), (out_ref,), (send_sems, recv_sems, local_sem) = ins, outs, scr
        x, y, c = lax.axis_index("x"), lax.axis_index("y"), lax.axis_index("c")
        me, sibling = (x, y, c), (x, y, 1 - c)
        chips = [(1 - x, y), (x, 1 - y), (1 - x, 1 - y)]

        def slot(px, py, pc):
            return out_ref.at[4 * px + 2 * py + pc]

        def copy(k, block, to, src=None):
            return pltpu.make_async_remote_copy(
                src_ref=slot(*block) if src is None else src, dst_ref=slot(*block),
                send_sem=send_sems.at[k], recv_sem=recv_sems.at[k], device_id=to, device_id_type=MESH)

        mine = pltpu.make_async_copy(x_ref, slot(*me), local_sem)
        first = [copy(0, me, sibling, src=x_ref)]
        first += [copy(1 + j, me, (*chip, c), src=x_ref) for j, chip in enumerate(chips)]
        passed = [copy(4 + j, (*chip, c), sibling) for j, chip in enumerate(chips)]
        if arrivals == "ici":
            landing = [copy(1 + j, (*chip, c), me) for j, chip in enumerate(chips)]
        else:
            landing = [copy(0, sibling, me)] + [copy(4 + j, (*chip, 1 - c), me) for j, chip in enumerate(chips)]
        return mine, first, passed, landing

    def begin(self, i, n, ins, outs, scr):
        mine, first, passed, over_ici = self._copies(ins, outs, scr, "ici")

        @pl.when(i == 0)
        def _():
            mine.start()
            for cp in first:
                cp.start()

        @pl.when(i == (n - 3 if n >= 6 else n // 2))
        def _():
            for landed, onward in zip(over_ici, passed):
                landed.wait_recv()
                onward.start()

    def end(self, i, n, ins, outs, scr):
        mine, first, passed, over_d2d = self._copies(ins, outs, scr, "d2d")

        @pl.when(i == n - 1)
        def _():
            for cp in over_d2d:
                cp.wait_recv()
            for cp in first + passed:
                cp.wait_send()
            mine.wait()


class _Exchange:
    def __init__(self, blocks):
        self.operands = [blocks]
        self.out_shape = [jax.ShapeDtypeStruct(blocks.shape, blocks.dtype)]
        self.scratch = [pltpu.SemaphoreType.DMA((7,)), pltpu.SemaphoreType.DMA((7,)), pltpu.SemaphoreType.DMA]

    def _copies(self, ins, outs, scr, with_arrivals):
        (s_ref,), (r_ref,), (send_sems, recv_sems, local_sem) = ins, outs, scr
        x, y, c = lax.axis_index("x"), lax.axis_index("y"), lax.axis_index("c")
        me = 4 * x + 2 * y + c
        mine = pltpu.make_async_copy(s_ref.at[me], r_ref.at[me], local_sem)
        sends, recvs = [], []
        for k in range(1, N_DEV):
            px = (1 - x) if (k >> 2) & 1 else x
            py = (1 - y) if (k >> 1) & 1 else y
            pc = (1 - c) if k & 1 else c
            peer = 4 * px + 2 * py + pc
            for dst, group in ((r_ref.at[me], sends), (r_ref.at[peer], recvs)):
                if group is sends or with_arrivals:
                    group.append(pltpu.make_async_remote_copy(
                        src_ref=s_ref.at[peer], dst_ref=dst, send_sem=send_sems.at[k - 1],
                        recv_sem=recv_sems.at[k - 1], device_id=(px, py, pc), device_id_type=MESH))
        return mine, sends, recvs

    def begin(self, i, n, ins, outs, scr):
        mine, sends, _ = self._copies(ins, outs, scr, False)

        @pl.when(i == 0)
        def _():
            mine.start()
            for cp in sends:
                cp.start()

    def end(self, i, n, ins, outs, scr):
        mine, sends, recvs = self._copies(ins, outs, scr, True)

        @pl.when(i == n - 1)
        def _():
            for cp in recvs:
                cp.wait_recv()
            for cp in sends:
                cp.wait_send()
            mine.wait()


_ANY = pl.BlockSpec(memory_space=pl.ANY)


def _call(body, name, steps, operands, in_specs, out_shape, out_specs, scratch=(), riders=(), aliases=None):
    n_in, n_out, n_scr = len(operands), len(out_shape), len(scratch)

    def full(*refs):
        pos = [0]

        def take(k):
            pos[0] += k
            return refs[pos[0] - k:pos[0]]

        h_in, r_in = take(n_in), [take(len(r.operands)) for r in riders]
        h_out, r_out = take(n_out), [take(len(r.out_shape)) for r in riders]
        h_scr, r_scr = take(n_scr), [take(len(r.scratch)) for r in riders]
        i = pl.program_id(0)
        for r, a, b, c in zip(riders, r_in, r_out, r_scr):
            r.begin(i, steps, a, b, c)
        body(*h_in, *h_out, *h_scr)
        for r, a, b, c in zip(riders, r_in, r_out, r_scr):
            r.end(i, steps, a, b, c)

    outs = pl.pallas_call(
        full, name=name, grid=(steps,),
        out_shape=tuple(out_shape) + tuple(s for r in riders for s in r.out_shape),
        in_specs=list(in_specs) + [_ANY for r in riders for _ in r.operands],
        out_specs=tuple(out_specs) + tuple(_ANY for r in riders for _ in r.out_shape),
        scratch_shapes=list(scratch) + [s for r in riders for s in r.scratch],
        input_output_aliases=aliases or {},
        compiler_params=_params(("arbitrary",)),
    )(*operands, *(o for r in riders for o in r.operands))
    host, rest = outs[:n_out], list(outs[n_out:])
    rides = []
    for r in riders:
        rides.append(rest[:len(r.out_shape)])
        rest = rest[len(r.out_shape):]
    return host, rides


def _alone(rider, name):
    return _call(lambda: None, name, 1, [], [], [], [], riders=[rider])[1][0][0]


def _sum_slots(slots, name):
    _, rows, cols = slots.shape
    tr = _row_tile(rows, 256, 16)

    def body(s_ref, o_ref):
        acc = s_ref[0].astype(F32)
        for j in range(1, N_DEV):
            acc = acc + s_ref[j].astype(F32)
        o_ref[...] = acc

    return pl.pallas_call(
        body, name=name, grid=(rows // tr,),
        out_shape=jax.ShapeDtypeStruct((rows, cols), F32),
        in_specs=[pl.BlockSpec((N_DEV, tr, cols), lambda i: (0, i, 0))],
        out_specs=pl.BlockSpec((tr, cols), lambda i: (i, 0)),
        compiler_params=_params(("parallel",)),
    )(slots)


def _ffn_fwd(x, gain, wset, fs, name, riders=()):
    t, d = x.shape
    f = N_DEV * fs
    tm = _row_tile(t, 512, 128)
    cols = _chunks(f, 1024)

    def body(x_ref, g_ref, w1_ref, w3_ref, w2_ref, y_ref, a_ref, b_ref, u_ref):
        xv = x_ref[...]
        xhat, _ = _rmsnorm_stats(xv)
        h = (xhat * g_ref[...]).astype(BF16)
        w1 = w1_ref[...].reshape(f, d)
        w3 = w3_ref[...].reshape(f, d)
        for s, n in cols:
            a = _nt(h, w1[s:s + n, :])
            b = _nt(h, w3[s:s + n, :])
            a_ref[:, s:s + n] = a.astype(BF16)
            b_ref[:, s:s + n] = b.astype(BF16)
            u_ref[:, s:s + n] = (a * _sigmoid(a) * b).astype(BF16)
        y_ref[...] = xv + FFN_RESIDUAL * _nn(u_ref[...], w2_ref[...].reshape(f, d))

    return _call(
        body, name, t // tm, [x, gain.reshape(1, d), wset, wset, wset],
        [pl.BlockSpec((tm, d), lambda i: (i, 0)), _const((1, d)),
         _wspec(fs, d, 0), _wspec(fs, d, 1), _wspec(fs, d, 2)],
        [jax.ShapeDtypeStruct((t, d), F32), jax.ShapeDtypeStruct((t, f), BF16), jax.ShapeDtypeStruct((t, f), BF16)],
        [pl.BlockSpec((tm, d), lambda i: (i, 0)), pl.BlockSpec((tm, f), lambda i: (i, 0)),
         pl.BlockSpec((tm, f), lambda i: (i, 0))],
        scratch=[pltpu.VMEM((tm, f), BF16)], riders=riders)


def _ffn_bwd(x, dy, a, b, gain, wset, fs, name, riders=()):
    t, d = x.shape
    f = N_DEV * fs
    tm = _row_tile(t, 256, 128)
    cols = _chunks(f, 1024)

    def body(x_ref, dy_ref, a_ref, b_ref, g_ref, w1_ref, w3_ref, w2_ref,
             dx_ref, da_ref, db_ref, u_ref, h_ref, dg_ref):
        @pl.when(pl.program_id(0) == 0)
        def _():
            dg_ref[...] = jnp.zeros_like(dg_ref)

        xv, dyv, g = x_ref[...], dy_ref[...], g_ref[...]
        xhat, r = _rmsnorm_stats(xv)
        h_ref[...] = (xhat * g).astype(BF16)
        dyb = (FFN_RESIDUAL * dyv).astype(BF16)
        w1 = w1_ref[...].reshape(f, d)
        w3 = w3_ref[...].reshape(f, d)
        w2 = w2_ref[...].reshape(f, d)
        dh = jnp.zeros((tm, d), F32)
        for s, n in cols:
            du = _nt(dyb, w2[s:s + n, :])
            av = a_ref[:, s:s + n].astype(F32)
            bv = b_ref[:, s:s + n].astype(F32)
            sig = _sigmoid(av)
            sa = av * sig
            u_ref[:, s:s + n] = (sa * bv).astype(BF16)
            da = (du * bv * (sig + sa * (1.0 - sig))).astype(BF16)
            db = (du * sa).astype(BF16)
            da_ref[:, s:s + n] = da
            db_ref[:, s:s + n] = db
            dh = dh + _nn(da, w1[s:s + n, :]) + _nn(db, w3[s:s + n, :])
        dx_ref[...] = dyv + _rmsnorm_bwd(dh, g, xhat, r)
        dg_ref[0:1, :] += jnp.sum(dh * xhat, axis=0, keepdims=True)

    big = jax.ShapeDtypeStruct((t, f), BF16)
    row = lambda w: pl.BlockSpec((tm, w), lambda i: (i, 0))
    return _call(
        body, name, t // tm, [x, dy, a, b, gain.reshape(1, d), wset, wset, wset],
        [row(d), row(d), row(f), row(f), _const((1, d)), _wspec(fs, d, 0), _wspec(fs, d, 1), _wspec(fs, d, 2)],
        [jax.ShapeDtypeStruct((t, d), F32), big, big, big, jax.ShapeDtypeStruct((t, d), BF16),
         jax.ShapeDtypeStruct((8, d), F32)],
        [row(d), row(f), row(f), row(f), row(d), _const((8, d))], riders=riders)


def _hidden_major(wcols, name):
    depth, d, c = wcols.shape

    def body(w_ref, o_ref):
        eye = lax.broadcasted_iota(jnp.int32, (d, d), 0) == lax.broadcasted_iota(jnp.int32, (d, d), 1)
        o_ref[0] = _tn(w_ref[0].astype(BF16), eye.astype(BF16)).astype(BF16)

    return pl.pallas_call(
        body, name=name, grid=(depth,),
        out_shape=jax.ShapeDtypeStruct((depth, c, d), BF16),
        in_specs=[pl.BlockSpec((1, d, c), lambda l: (l, 0, 0))],
        out_specs=pl.BlockSpec((1, c, d), lambda l: (l, 0, 0)),
        compiler_params=_params(("parallel",)),
    )(wcols)


def _wgrad(lhs, rhs, gbuf, rows, blk, scale, name, riders=()):
    t, m = lhs.shape
    d = rhs.shape[1]
    bt = _row_tile(t, 1024, 128)
    nk = t // bt
    assert m == N_DEV * rows

    def body(a_ref, b_ref, g_in, o_ref, acc_ref):
        del g_in
        k = pl.program_id(0)

        @pl.when(k == 0)
        def _():
            acc_ref[...] = jnp.zeros_like(acc_ref)

        acc_ref[...] += _tn(a_ref[...], b_ref[...].astype(BF16))

        @pl.when(k == nk - 1)
        def _():
            o_ref[...] = (scale * acc_ref[...]).astype(BF16).reshape(N_DEV, rows, d)

    (out,), rides = _call(
        body, name, nk, [lhs, rhs, gbuf],
        [pl.BlockSpec((bt, m), lambda k: (k, 0)), pl.BlockSpec((bt, d), lambda k: (k, 0)), _ANY],
        [jax.ShapeDtypeStruct(gbuf.shape, gbuf.dtype)], [pl.BlockSpec((N_DEV, rows, d), lambda k: (0, blk, 0))],
        scratch=[pltpu.VMEM((m, d), F32)], riders=riders, aliases={2: 0})
    return (out, rides) if riders else out


def _head_masks(rows, g):
    lane = lax.broadcasted_iota(jnp.int32, (rows, g), 1)
    return [(lane >= h * HEAD_DIM) & (lane < (h + 1) * HEAD_DIM) for h in range(g // HEAD_DIM)]


def _pool_windows(rows, g):
    lane = lax.broadcasted_iota(jnp.int32, (rows, g), 1)
    pg = g // len(POOL_WINDOWS)
    w = jnp.full((rows, g), float(POOL_WINDOWS[-1]), F32)
    for k in range(len(POOL_WINDOWS) - 2, -1, -1):
        w = jnp.where(lane < (k + 1) * pg, float(POOL_WINDOWS[k]), w)
    return w


def _window_sum(ref, tmp, first, rows, wl, direction):
    total = ref.shape[0]
    src, out, shift = ref, None, 1
    for k, w in enumerate(POOL_WINDOWS):
        assert w == 2 * shift and 8 * (k + 1) <= HALO
        if k + 1 < len(POOL_WINDOWS):
            n = total - 8 * (k + 1)
            lo = 8 * (k + 1) if direction < 0 else 0
            dst = tmp.at[k % 2]
            dst[pl.ds(lo, n), :] = src[pl.ds(lo, n), :] + src[pl.ds(lo + direction * shift, n), :]
            src = dst
            level = dst[pl.ds(first, rows), :]
        else:
            level = src[pl.ds(first, rows), :] + src[pl.ds(first + direction * shift, rows), :]
        out = level if out is None else jnp.where(wl >= float(w), level, out)
        shift = w
    return out


def _shifted_copies(src_ref, dst_ref):
    rows = dst_ref.shape[1]
    for b in range(1, 8):
        dst_ref[b - 1] = src_ref[pl.ds(b, rows), :]


def _window(src_ref, shifted_ref, off, rows):
    b = off % 8
    if b == 0:
        return src_ref[pl.ds(off, rows), :]
    return shifted_ref[b - 1, pl.ds(off - b, rows), :]


def _pool_divisor(first_row, rows, g):
    pos1 = (lax.broadcasted_iota(jnp.int32, (rows, g), 0) + first_row + 1).astype(F32)
    return jnp.minimum(pos1, _pool_windows(rows, g))


def _spatial_mix(vq, wcat_ref, masks):
    vstack = jnp.concatenate([jnp.where(m, vq, jnp.zeros_like(vq)) for m in masks], axis=0)
    return _nn(wcat_ref[...], vstack)


def _mix_fwd(x, gain, wbuf, lay, l, small, name):
    t, d = x.shape
    g = lay.g
    di = N_DEV * g
    tm = _row_tile(t, 256, CHUNK)
    conv_w, vecs, sconv_w, pool_bd, wcat, bst = small

    def body(x_ref, g_ref, wi_ref, wo_ref, cw_ref, vec_ref, sw_ref, pbd_ref, wcat_ref, bst_ref,
             y_ref, mix_ref, h_ref, pm_ref, pp_ref, ea_ref, eb_ref, ec_ref, sh_ref, tmp_ref):
        i = pl.program_id(0)

        @pl.when(i == 0)
        def _():
            pp_ref[...] = jnp.zeros_like(pp_ref)

        xhat, _ = _rmsnorm_stats(x_ref[...])
        h = (xhat * g_ref[...]).astype(BF16)
        h_ref[...] = h
        pm_ref[...] = _nt(h, wi_ref[...].reshape(di, d)).astype(BF16)

        def prev(j):
            return pp_ref[:, j * g:(j + 1) * g].astype(F32)

        def main(j):
            return pm_ref[:, j * g:(j + 1) * g].astype(F32)

        vec = vec_ref[...]
        ea_ref[0:HALO, :] = prev(0) * _sigmoid(prev(1))
        ea_ref[HALO:, :] = main(0) * _sigmoid(main(1))
        _shifted_copies(ea_ref, sh_ref)
        cw = cw_ref[...]
        c = jnp.zeros((tm, g), F32) + vec[0:1, :]
        for k in range(CONF_KERNEL):
            c = c + cw[k:k + 1, :] * _window(ea_ref, sh_ref, HALO - (CONF_KERNEL - 1) + k, tm)
        chat, _ = _layernorm_stats(c)
        ln = chat * vec[1:2, :] + vec[2:3, :]
        mix_ref[:, 0:g] = (ln * _sigmoid(ln)).astype(BF16)
        eb_ref[0:HALO, :] = prev(3) * prev(4)
        eb_ref[HALO:, :] = main(3) * main(4)
        sw = sw_ref[...]
        cz = jnp.zeros((tm, g), F32)
        for k in range(SHORT_KERNEL):
            cz = cz + sw[k:k + 1, :] * eb_ref[pl.ds(HALO - (SHORT_KERNEL - 1) + k, tm), :]
        mix_ref[:, g:2 * g] = (main(2) * cz).astype(BF16)
        ec_ref[0:HALO, :] = prev(5)
        xp = main(5)
        ec_ref[HALO:, :] = xp
        wl = _pool_windows(tm, g)
        mean = _window_sum(ec_ref, tmp_ref, HALO, tm, wl, -1) / _pool_divisor(i * tm, tm, g)
        yc = _nn((mean - xp).astype(BF16), pbd_ref[...])
        mix_ref[:, 2 * g:3 * g] = (yc * vec[3:4, :]).astype(BF16)
        vhat, _ = _layernorm_stats(main(7))
        v = (vhat * vec[4:5, :] + vec[5:6, :]).astype(BF16)
        gu = main(6)
        masks = _head_masks(CHUNK, g)
        for q in range(tm // CHUNK):
            rs = slice(q * CHUNK, (q + 1) * CHUNK)
            mixed = _spatial_mix(v[rs, :], wcat_ref, masks) + bst_ref[...]
            mix_ref[rs, 3 * g:4 * g] = (gu[rs, :] * mixed).astype(BF16)
        y_ref[...] = x_ref[...] + _nn(mix_ref[...], wo_ref[...].reshape(d, d))
        pp_ref[...] = pm_ref[tm - HALO:, :]

    row = lambda w: pl.BlockSpec((tm, w), lambda i: (i, 0))
    return pl.pallas_call(
        body, name=name, grid=(t // tm,),
        out_shape=(jax.ShapeDtypeStruct((t, d), F32), jax.ShapeDtypeStruct((t, 4 * g), BF16),
                   jax.ShapeDtypeStruct((t, d), BF16), jax.ShapeDtypeStruct((t, di), BF16)),
        in_specs=[row(d), _const((1, d)), _wspec(g, d, lay.win(l)), _wspec(lay.wo, d, lay.wout(l)),
                  _const(conv_w.shape), _const(vecs.shape), _const(sconv_w.shape), _const(pool_bd.shape),
                  _const(wcat.shape), _const(bst.shape)],
        out_specs=(row(d), row(4 * g), row(d), row(di)),
        scratch_shapes=[pltpu.VMEM((HALO, di), BF16)] + [pltpu.VMEM((HALO + tm, g), F32)] * 3
        + [pltpu.VMEM((7, HALO + tm - 8, g), F32), pltpu.VMEM((2, HALO + tm, g), F32)],
        compiler_params=_params(("arbitrary",)),
    )(x, gain.reshape(1, d), wbuf, wbuf, conv_w, vecs, sconv_w, pool_bd, wcat, bst)


def _mix_bwd(p, dy, x, gain, wbuf, lay, l, small, name, riders=()):
    t, d = dy.shape
    g = lay.g
    di = N_DEV * g
    tm = _row_tile(t, 256, CHUNK)
    nt = t // tm
    nh = g // HEAD_DIM
    per = tm // HALO
    ext = tm + HALO
    conv_w, vecs, sconv_w, pool_bd, wcat, bst, wcat_t = small

    def body(pp_ref, pm_ref, pn_ref, dy_ref, dyn_ref, x_ref, g_ref, wi_ref, wo_ref, cw_ref, vec_ref, sw_ref, pbd_ref,
             wcat_ref, bst_ref, wcatt_ref, dp_ref, dcw_ref, dvec_ref, dsw_ref, dpbd_ref, dws_ref, dbs_ref,
             dx_ref, dg_ref, ea_ref, eb_ref, ec_ref, fa_ref, fb_ref, fc_ref, sh_ref, sf_ref, tmp_ref):
        i = pl.program_id(0)
        seen = (i > 0).astype(F32)
        more = (i < nt - 1).astype(F32)

        @pl.when(i == 0)
        def _():
            for ref in (dcw_ref, dvec_ref, dsw_ref, dpbd_ref, dws_ref, dbs_ref, dg_ref):
                ref[...] = jnp.zeros_like(ref)

        def prev(j):
            return pp_ref[:, j * g:(j + 1) * g].astype(F32) * seen

        def main(j):
            return pm_ref[:, j * g:(j + 1) * g].astype(F32)

        def nxt(j):
            return pn_ref[:, j * g:(j + 1) * g].astype(F32)

        def colsum(v):
            return jnp.sum(v, axis=0, keepdims=True)

        vec = vec_ref[...]
        wo = wo_ref[...].reshape(d, d)
        dmix = jnp.concatenate([_nt(dy_ref[...].astype(BF16), wo),
                                _nt((dyn_ref[...] * more).astype(BF16), wo)], axis=0)

        val, gate = main(0), main(1)
        sg = _sigmoid(gate)
        ea_ref[0:HALO, :] = prev(0) * _sigmoid(prev(1))
        ea_ref[HALO:ext, :] = val * sg
        ea_ref[ext:, :] = nxt(0) * _sigmoid(nxt(1))
        _shifted_copies(ea_ref, sh_ref)
        cw = cw_ref[...]
        c = jnp.zeros((ext, g), F32) + vec[0:1, :]
        for k in range(CONF_KERNEL):
            c = c + cw[k:k + 1, :] * _window(ea_ref, sh_ref, HALO - (CONF_KERNEL - 1) + k, ext)
        chat, r = _layernorm_stats(c)
        ln = chat * vec[1:2, :] + vec[2:3, :]
        sl = _sigmoid(ln)
        dln = dmix[:, 0:g] * (sl * (1.0 + ln * (1.0 - sl)))
        dvec_ref[1:2, :] += colsum(dln[0:tm] * chat[0:tm])
        dvec_ref[2:3, :] += colsum(dln[0:tm])
        dc = _layernorm_bwd(dln, vec[1:2, :], chat, r)
        dvec_ref[0:1, :] += colsum(dc[0:tm])
        fa_ref[...] = dc
        _shifted_copies(fa_ref, sf_ref)
        dcm = dc[0:tm]
        dy0 = jnp.zeros((tm, g), F32)
        for k in range(CONF_KERNEL):
            dy0 = dy0 + cw[k:k + 1, :] * _window(fa_ref, sf_ref, CONF_KERNEL - 1 - k, tm)
            dcw_ref[k:k + 1, :] += colsum(dcm * _window(ea_ref, sh_ref, HALO - (CONF_KERNEL - 1) + k, tm))
        dp_ref[:, 0:g] = (dy0 * sg).astype(BF16)
        dp_ref[:, g:2 * g] = (dy0 * val * sg * (1.0 - sg)).astype(BF16)

        sb, sc, sx = main(2), main(3), main(4)
        eb_ref[0:HALO, :] = prev(3) * prev(4)
        eb_ref[HALO:ext, :] = sc * sx
        sw = sw_ref[...]
        cz = jnp.zeros((tm, g), F32)
        for k in range(SHORT_KERNEL):
            cz = cz + sw[k:k + 1, :] * eb_ref[pl.ds(HALO - (SHORT_KERNEL - 1) + k, tm), :]
        dob = dmix[:, g:2 * g]
        dp_ref[:, 2 * g:3 * g] = (dob[0:tm] * cz).astype(BF16)
        fb_ref[0:tm, :] = dob[0:tm] * sb
        fb_ref[tm:, :] = dob[tm:] * nxt(2)
        dczm = fb_ref[0:tm, :]
        dz = jnp.zeros((tm, g), F32)
        for k in range(SHORT_KERNEL):
            dz = dz + sw[k:k + 1, :] * fb_ref[pl.ds(SHORT_KERNEL - 1 - k, tm), :]
            dsw_ref[k:k + 1, :] += colsum(dczm * eb_ref[pl.ds(HALO - (SHORT_KERNEL - 1) + k, tm), :])
        dp_ref[:, 3 * g:4 * g] = (dz * sx).astype(BF16)
        dp_ref[:, 4 * g:5 * g] = (dz * sc).astype(BF16)

        xp = main(5)
        ec_ref[0:HALO, :] = prev(5)
        ec_ref[HALO:, :] = xp
        wl = _pool_windows(tm, g)
        dpool = ((_window_sum(ec_ref, tmp_ref, HALO, tm, wl, -1) / _pool_divisor(i * tm, tm, g)) - xp).astype(BF16)
        pbd = pbd_ref[...]
        yc = _nn(dpool, pbd)
        doc = dmix[:, 2 * g:3 * g]
        dvec_ref[3:4, :] += colsum(doc[0:tm] * yc)
        dyc = (doc * vec[3:4, :]).astype(BF16)
        dpbd_ref[...] += _tn(dpool, dyc[0:tm])
        dd = _nt(dyc, pbd)
        fc_ref[...] = dd / _pool_divisor(i * tm, ext, g)
        dp_ref[:, 5 * g:6 * g] = (_window_sum(fc_ref, tmp_ref, 0, tm, wl, 1) - dd[0:tm]).astype(BF16)

        gv = main(7)
        vhat, rv = _layernorm_stats(gv)
        v = (vhat * vec[4:5, :] + vec[5:6, :]).astype(BF16)
        gu = main(6)
        dod = dmix[0:tm, 3 * g:4 * g]
        masks = _head_masks(CHUNK, g)
        tril = (lax.broadcasted_iota(jnp.int32, (CHUNK, CHUNK), 1)
                <= lax.broadcasted_iota(jnp.int32, (CHUNK, CHUNK), 0)).astype(F32)
        tril = jnp.concatenate([tril] * nh, axis=0)
        head_col = lax.broadcasted_iota(jnp.int32, (CHUNK, CHUNK), 1)
        dvs = []
        for q in range(tm // CHUNK):
            rs = slice(q * CHUNK, (q + 1) * CHUNK)
            vq = v[rs, :]
            mixed = _spatial_mix(vq, wcat_ref, masks) + bst_ref[...]
            dp_ref[rs, 6 * g:7 * g] = (dod[rs, :] * mixed).astype(BF16)
            dmixed = dod[rs, :] * gu[rs, :]
            dmb = dmixed.astype(BF16)
            for h, m in enumerate(masks):
                hs = jnp.sum(jnp.where(m, dmixed, 0.0), axis=-1, keepdims=True)
                dbs_ref[...] += jnp.where(head_col == h, hs, 0.0)
            dmstack = jnp.concatenate([jnp.where(m, dmb, jnp.zeros_like(dmb)) for m in masks], axis=0)
            dws_ref[...] += _nt(dmstack, vq) * tril
            back = _nn(wcatt_ref[...], dmb)
            dv = jnp.zeros((CHUNK, g), F32)
            for h, m in enumerate(masks):
                dv = jnp.where(m, back[h * CHUNK:(h + 1) * CHUNK, :], dv)
            dvs.append(dv)
        dv = jnp.concatenate(dvs, axis=0)
        dvec_ref[4:5, :] += colsum(dv * vhat)
        dvec_ref[5:6, :] += colsum(dv)
        dp_ref[:, 7 * g:8 * g] = _layernorm_bwd(dv, vec[4:5, :], vhat, rv).astype(BF16)

        xhat, rx = _rmsnorm_stats(x_ref[...])
        dh = _nn(dp_ref[...], wi_ref[...].reshape(di, d))
        dx_ref[...] = dy_ref[...] + _rmsnorm_bwd(dh, g_ref[...], xhat, rx)
        dg_ref[0:1, :] += colsum(dh * xhat)

    halo_prev = pl.BlockSpec((HALO, di), lambda i: (jnp.maximum(i * per - 1, 0), 0))
    halo_next = lambda w: pl.BlockSpec((HALO, w), lambda i: (jnp.minimum((i + 1) * per, t // HALO - 1), 0))
    row = lambda w: pl.BlockSpec((tm, w), lambda i: (i, 0))
    outs = (jax.ShapeDtypeStruct((t, di), BF16), jax.ShapeDtypeStruct(conv_w.shape, F32),
            jax.ShapeDtypeStruct(vecs.shape, F32), jax.ShapeDtypeStruct(sconv_w.shape, F32),
            jax.ShapeDtypeStruct((g, g), F32), jax.ShapeDtypeStruct((nh * CHUNK, CHUNK), F32),
            jax.ShapeDtypeStruct((CHUNK, CHUNK), F32), jax.ShapeDtypeStruct((t, d), F32),
            jax.ShapeDtypeStruct((8, d), F32))
    return _call(
        body, name, nt, [p, p, p, dy, dy, x, gain.reshape(1, d), wbuf, wbuf, conv_w, vecs, sconv_w, pool_bd, wcat,
                         bst, wcat_t],
        [halo_prev, row(di), halo_next(di), row(d), halo_next(d), row(d), _const((1, d)),
         _wspec(g, d, lay.win(l)), _wspec(lay.wo, d, lay.wout(l)),
         _const(conv_w.shape), _const(vecs.shape), _const(sconv_w.shape), _const(pool_bd.shape),
         _const(wcat.shape), _const(bst.shape), _const(wcat_t.shape)],
        list(outs), [row(di)] + [_const(o.shape) for o in outs[1:7]] + [row(d), _const((8, d))],
        scratch=[pltpu.VMEM((HALO + tm + HALO, g), F32), pltpu.VMEM((HALO + tm, g), F32),
                 pltpu.VMEM((HALO + tm, g), F32), pltpu.VMEM((ext, g), F32), pltpu.VMEM((ext, g), F32),
                 pltpu.VMEM((ext, g), F32), pltpu.VMEM((7, HALO + tm + HALO - 8, g), F32),
                 pltpu.VMEM((7, ext - 8, g), F32), pltpu.VMEM((2, ext, g), F32)],
        riders=riders)


def _loss_head(x, target, gain, name):
    t, d = x.shape
    tm = _row_tile(t, 512, 8)

    def body(x_ref, t_ref, g_ref, dx_ref, loss_ref, dg_ref):
        @pl.when(pl.program_id(0) == 0)
        def _():
            loss_ref[...] = jnp.zeros_like(loss_ref)
            dg_ref[...] = jnp.zeros_like(dg_ref)

        g = g_ref[...]
        xhat, r = _rmsnorm_stats(x_ref[...])
        err = xhat * g - t_ref[...]
        loss_ref[...] += 0.5 * jnp.sum(jnp.mean(err * err, axis=-1, keepdims=True))
        dyv = err * (1.0 / d)
        dx_ref[...] = _rmsnorm_bwd(dyv, g, xhat, r)
        dg_ref[0:1, :] += jnp.sum(dyv * xhat, axis=0, keepdims=True)

    row = pl.BlockSpec((tm, d), lambda i: (i, 0))
    return pl.pallas_call(
        body, name=name, grid=(t // tm,),
        out_shape=(jax.ShapeDtypeStruct((t, d), F32), jax.ShapeDtypeStruct((8, 128), F32),
                   jax.ShapeDtypeStruct((8, d), F32)),
        in_specs=[row, row, _const((1, d))],
        out_specs=(row, _const((8, 128)), _const((8, d))),
        compiler_params=_params(("arbitrary",)),
    )(x, target, gain.reshape(1, d))


def _adam_step(w, gv, m, v):
    nm = ADAM_B1 * m + (1.0 - ADAM_B1) * gv
    nv = ADAM_B2 * v + (1.0 - ADAM_B2) * (gv * gv)
    m_hat = nm / (1.0 - ADAM_B1 ** ADAM_STEP)
    v_hat = nv / (1.0 - ADAM_B2 ** ADAM_STEP)
    return -ADAM_LR * (m_hat / (jnp.sqrt(v_hat) + ADAM_EPS) + ADAM_WD * w), nm, nv


def _reduce_adamw(slots, blks, transposed, w, m, v, name):
    depth, rows, cols = w.shape
    tr = _row_tile(rows, 512, 128) if transposed else _row_tile(rows, 256, 16)
    nt = rows // tr
    pad = -cols % 128

    def body(*refs):
        slot_refs = refs[:depth]
        w_ref, m_ref, v_ref, g_ref, d_ref, nm_ref, nv_ref = refs[depth:]
        layer = pl.program_id(0) // nt
        for l in range(depth):
            @pl.when(layer == l)
            def _(s_ref=slot_refs[l]):
                gv = s_ref[0].astype(F32)
                for j in range(1, N_DEV):
                    gv = gv + s_ref[j].astype(F32)
                if transposed:
                    if pad:
                        gv = jnp.concatenate([gv, jnp.zeros((pad, tr), F32)], axis=0)
                    gv = gv.T[:, :cols]
                g_ref[0] = gv
                d_ref[0], nm_ref[0], nv_ref[0] = _adam_step(w_ref[0], gv, m_ref[0], v_ref[0])

    def slot_spec(l):
        tile = lambda i: jnp.clip(i - l * nt, 0, nt - 1)
        if transposed:
            return pl.BlockSpec((N_DEV, cols, tr), lambda i: (0, blks[l], tile(i)))
        return pl.BlockSpec((N_DEV, tr, cols), lambda i: (0, blks[l] * nt + tile(i), 0))

    spec = pl.BlockSpec((1, tr, cols), lambda i: (i // nt, i % nt, 0))
    shape = jax.ShapeDtypeStruct(w.shape, F32)
    return pl.pallas_call(
        body, name=name, grid=(depth * nt,),
        out_shape=(shape,) * 4, in_specs=[slot_spec(l) for l in range(depth)] + [spec] * 3, out_specs=(spec,) * 4,
        compiler_params=_params(("arbitrary",)),
    )(*slots, w, m, v)


def _adamw(w, grad, m, v, name):
    shape = w.shape
    cols = shape[-1]
    rows = w.size // cols
    tr = _row_tile(rows, 512, 8) if rows > 1024 else rows

    def body(w_ref, g_ref, m_ref, v_ref, d_ref, nm_ref, nv_ref):
        d_ref[...], nm_ref[...], nv_ref[...] = _adam_step(w_ref[...], g_ref[...], m_ref[...], v_ref[...])

    spec = pl.BlockSpec((tr, cols), lambda i: (i, 0))
    flat = jax.ShapeDtypeStruct((rows, cols), F32)
    outs = pl.pallas_call(
        body, name=name, grid=(rows // tr,),
        out_shape=(flat, flat, flat), in_specs=[spec] * 4, out_specs=(spec,) * 3,
        compiler_params=_params(("parallel",)),
    )(*(a.reshape(rows, cols) for a in (w, grad, m, v)))
    return tuple(o.reshape(shape) for o in outs)


def _f32_rows(parts, d, rows):
    out, offs, at = [], [], 0
    for a in parts:
        n = -(-a.size // d)
        out.append(jnp.pad(a.reshape(-1), (0, n * d - a.size)).reshape(n, d))
        offs.append(at)
        at += n
    total = at if rows is None else rows
    assert at <= total
    if total > at:
        out.append(jnp.zeros((total - at, d), F32))
    return jnp.concatenate(out, axis=0), offs


def _take(buf, off, shape):
    n = 1
    for s in shape:
        n *= s
    d = buf.shape[1]
    return buf[off:off + -(-n // d)].reshape(-1)[:n].reshape(shape)


def _block_diag(blocks):
    n, k, _ = blocks.shape
    out = jnp.zeros((n * k, n * k), blocks.dtype)
    for i in range(n):
        out = lax.dynamic_update_slice(out, blocks[i], (i * k, i * k))
    return out


def kernel(x, ffn1_norm, ffn1_w1, ffn1_w3, ffn1_w2, mix_norm, w_in, conf_conv_w, conf_conv_b, conf_ln_g, conf_ln_b, sconv_w, pool_w, pool_scale, gmlp_ln_g, gmlp_ln_b, gmlp_w_s, gmlp_b_s, w_out, ffn2_norm, ffn2_w1, ffn2_w3, ffn2_w2, final_norm, loss_target, m_ffn1_norm, m_ffn1_w1, m_ffn1_w3, m_ffn1_w2, m_mix_norm, m_w_in, m_conf_conv_w, m_conf_conv_b, m_conf_ln_g, m_conf_ln_b, m_sconv_w, m_pool_w, m_pool_scale, m_gmlp_ln_g, m_gmlp_ln_b, m_gmlp_w_s, m_gmlp_b_s, m_w_out, m_ffn2_norm, m_ffn2_w1, m_ffn2_w3, m_ffn2_w2, m_final_norm, v_ffn1_norm, v_ffn1_w1, v_ffn1_w3, v_ffn1_w2, v_mix_norm, v_w_in, v_conf_conv_w, v_conf_conv_b, v_conf_ln_g, v_conf_ln_b, v_sconv_w, v_pool_w, v_pool_scale, v_gmlp_ln_g, v_gmlp_ln_b, v_gmlp_w_s, v_gmlp_b_s, v_w_out, v_ffn2_norm, v_ffn2_w1, v_ffn2_w3, v_ffn2_w2, v_final_norm):
    w = dict(zip(WEIGHTS, (ffn1_norm, ffn1_w1, ffn1_w3, ffn1_w2, mix_norm, w_in, conf_conv_w, conf_conv_b, conf_ln_g,
                           conf_ln_b, sconv_w, pool_w, pool_scale, gmlp_ln_g, gmlp_ln_b, gmlp_w_s, gmlp_b_s, w_out,
                           ffn2_norm, ffn2_w1, ffn2_w3, ffn2_w2, final_norm)))
    mom1 = dict(zip(WEIGHTS, (m_ffn1_norm, m_ffn1_w1, m_ffn1_w3, m_ffn1_w2, m_mix_norm, m_w_in, m_conf_conv_w,
                              m_conf_conv_b, m_conf_ln_g, m_conf_ln_b, m_sconv_w, m_pool_w, m_pool_scale, m_gmlp_ln_g,
                              m_gmlp_ln_b, m_gmlp_w_s, m_gmlp_b_s, m_w_out, m_ffn2_norm, m_ffn2_w1, m_ffn2_w3,
                              m_ffn2_w2, m_final_norm)))
    mom2 = dict(zip(WEIGHTS, (v_ffn1_norm, v_ffn1_w1, v_ffn1_w3, v_ffn1_w2, v_mix_norm, v_w_in, v_conf_conv_w,
                              v_conf_conv_b, v_conf_ln_g, v_conf_ln_b, v_sconv_w, v_pool_w, v_pool_scale, v_gmlp_ln_g,
                              v_gmlp_ln_b, v_gmlp_w_s, v_gmlp_b_s, v_w_out, v_ffn2_norm, v_ffn2_w1, v_ffn2_w3,
                              v_ffn2_w2, v_final_norm)))
    _, t, d = x.shape
    depth, _, fs = ffn1_w1.shape
    g = w_in.shape[2]
    wo = w_out.shape[1]
    nh = g // HEAD_DIM
    cs = conf_conv_w.shape[2]
    lay = _MixRows(g, wo)
    me = 4 * lax.axis_index("x") + 2 * lax.axis_index("y") + lax.axis_index("c")

    def rows_bf16(mats):
        return jnp.concatenate([m_.astype(BF16) for m_ in mats], axis=0)

    rows_of = {n: _hidden_major(w[n], f"rows_{n}") for n in SHARDED_COLS}
    set_f1 = [rows_bf16([rows_of['ffn1_w1'][l], rows_of['ffn1_w3'][l], ffn1_w2[l]]) for l in range(depth)]
    set_f2 = [rows_bf16([rows_of['ffn2_w1'][l], rows_of['ffn2_w3'][l], ffn2_w2[l]]) for l in range(depth)]
    set_mx = [rows_bf16([rows_of['w_in'][l], w_out[l]]) for l in range(depth)]
    convs = jnp.concatenate([conf_conv_w, sconv_w], axis=1)
    nconv = convs.size
    conv_bits = jnp.pad(convs.reshape(-1), (0, CONV_ROWS * d // 2 - nconv))
    conv_rows = lax.bitcast_convert_type(conv_bits, BF16).reshape(CONV_ROWS, d)
    w_f1 = [None] * depth
    w_f1[0] = _alone(_Gather(jnp.concatenate([set_f1[0], conv_rows], axis=0)), "gather_first")

    conv_all = lax.bitcast_convert_type(w_f1[0][:, 3 * fs:, :].reshape(N_DEV, CONV_ROWS * d // 2, 2), F32)
    conv_all = conv_all[:, :nconv].reshape(N_DEV, depth, CONF_KERNEL + SHORT_KERNEL, cs)
    conv_all = conv_all.transpose(1, 2, 0, 3).reshape(depth, CONF_KERNEL + SHORT_KERNEL, g)

    def small_inputs(l):
        cw = jnp.pad(conv_all[l, :CONF_KERNEL], ((0, 32 - CONF_KERNEL), (0, 0)))
        sw = jnp.pad(conv_all[l, CONF_KERNEL:], ((0, 8 - SHORT_KERNEL), (0, 0)))
        vecs = jnp.stack([conf_conv_b[l], conf_ln_g[l], conf_ln_b[l], pool_scale[l], gmlp_ln_g[l], gmlp_ln_b[l],
                          jnp.zeros((g,), F32), jnp.zeros((g,), F32)])
        pool_bd = _block_diag(pool_w[l]).astype(BF16)
        low = jnp.tril(jnp.ones((CHUNK, CHUNK), bool))
        ws = jnp.where(low[None], gmlp_w_s[l], 0.0).astype(BF16)
        wcat = ws.transpose(1, 0, 2).reshape(CHUNK, nh * CHUNK)
        wcat_t = ws.transpose(0, 2, 1).reshape(nh * CHUNK, CHUNK)
        bst = jnp.repeat(gmlp_b_s[l].T, HEAD_DIM, axis=1)
        return cw, vecs, sw, pool_bd, wcat, bst, wcat_t

    xs = x[0]
    saved = []
    for l in range(depth):
        sm = small_inputs(l)
        (x1, a1, b1), ((w_f2,), (w_mx,)) = _ffn_fwd(xs, ffn1_norm[l], w_f1[l], fs, f"ffn1_fwd_{l}",
                                                    riders=[_Gather(set_f2[l]), _Gather(set_mx[l])])
        x2, mix, h2, p = _mix_fwd(x1, mix_norm[l], w_mx, lay, l, sm[:6], f"mix_fwd_{l}")
        nxt = [_Gather(set_f1[l + 1])] if l + 1 < depth else []
        (x3, a2, b2), got = _ffn_fwd(x2, ffn2_norm[l], w_f2, fs, f"ffn2_fwd_{l}", riders=nxt)
        if nxt:
            w_f1[l + 1] = got[0][0]
        saved.append((xs, a1, b1, x1, h2, p, mix, x2, a2, b2, sm, w_f2, w_mx))
        xs = x3

    dx, loss_part, dgf = _loss_head(xs, loss_target[0], final_norm, "loss_head")
    loss = lax.psum(loss_part[0, 0], ("x", "y", "c"))

    def small_rows(parts):
        buf, at = _f32_rows(parts, d, None)
        return jnp.pad(buf, ((0, -buf.shape[0] % 16), (0, 0))), at

    small = [None] * depth
    big = ('ffn1_w1', 'ffn1_w3', 'ffn1_w2', 'ffn2_w1', 'ffn2_w3', 'ffn2_w2', 'w_in', 'w_out')
    slots = {n: [None] * depth for n in big}
    blk_of = {'ffn1_w1': 0, 'ffn1_w3': 1, 'ffn1_w2': 2, 'ffn2_w1': 0, 'ffn2_w3': 1, 'ffn2_w2': 2,
              'w_in': lay.win(0), 'w_out': lay.wout(0)}
    blks = {n: [blk_of[n]] * depth for n in big}

    def landed(names, l, got_):
        for n in names:
            slots[n][l] = got_

    def ffn_grads(da, db, u, h, dy, tag):
        gset = lax.empty((N_DEV, 3 * fs, d), BF16)
        gset = _wgrad(da, h, gset, fs, 0, 1.0, f"{tag}_dw1")
        gset = _wgrad(db, h, gset, fs, 1, 1.0, f"{tag}_dw3")
        return _wgrad(u, dy, gset, fs, 2, FFN_RESIDUAL, f"{tag}_dw2")

    pending = None
    early_all = None
    for l in reversed(range(depth)):
        x0, a1, b1, x1, h2, p, mix, x2, a2, b2, sm, w_f2, w_mx = saved[l]
        ride = [_Exchange(pending)] if pending is not None else []
        (dx2, da, db, u, h, dgn2), got = _ffn_bwd(x2, dx, a2, b2, ffn2_norm[l], w_f2, fs, f"ffn2_bwd_{l}", riders=ride)
        if ride:
            landed(big[0:3], l + 1, got[0][0])
        gset = lax.empty((N_DEV, 3 * fs, d), BF16)
        if l == 0 and depth > 1:
            early, early_offs = small_rows([a for k in range(1, depth) for a in small[k]] + [dgf[0]])
            gset, ((early_all,),) = _wgrad(da, h, gset, fs, 0, 1.0, f"ffn2_{l}_dw1", riders=[_Gather(early)])
        else:
            gset = _wgrad(da, h, gset, fs, 0, 1.0, f"ffn2_{l}_dw1")
        gset = _wgrad(db, h, gset, fs, 1, 1.0, f"ffn2_{l}_dw3")
        gset = _wgrad(u, dx, gset, fs, 2, FFN_RESIDUAL, f"ffn2_{l}_dw2")
        (dp, dcw, dvec, dsw, dpbd, dws, dbs, dx1, dgm), got = _mix_bwd(
            p, dx2, x1, mix_norm[l], w_mx, lay, l, sm, f"mix_bwd_{l}", riders=[_Exchange(gset)])
        landed(big[3:6], l, got[0][0])
        if l > 0:
            gmx = lax.empty((N_DEV, g + wo, d), BF16)
            gmx = _wgrad(mix, dx2, gmx, wo, lay.wout(l), 1.0, f"dw_out_{l}")
            gmx = _wgrad(dp, h2, gmx, g, lay.win(l), 1.0, f"dw_in_{l}")
            ride = [_Exchange(gmx)]
        else:
            g_wo = _wgrad(mix, dx2, lax.empty((N_DEV, wo, d), BF16), wo, 0, 1.0, f"dw_out_{l}")
            g_wi, ((slots['w_out'][0],),) = _wgrad(dp, h2, lax.empty((N_DEV, g, d), BF16), g, 0, 1.0, f"dw_in_{l}",
                                                   riders=[_Exchange(g_wo)])
            blks['w_out'][0] = blks['w_in'][0] = 0
            ride = []
        (dx, da, db, u, h, dgn1), got = _ffn_bwd(x0, dx1, a1, b1, ffn1_norm[l], w_f1[l], fs,
                                                 f"ffn1_bwd_{l}", riders=ride)
        if l > 0:
            landed(big[6:8], l, got[0][0])
        pg = g // len(POOL_WINDOWS)
        dpool = jnp.stack([dpbd[k * pg:(k + 1) * pg, k * pg:(k + 1) * pg] for k in range(len(POOL_WINDOWS))])
        small[l] = [dgn1[0], dgm[0], dgn2[0], dvec, dcw, dsw, dpool, dws, dbs[:, :nh].T]
        if l > 0:
            pending = ffn_grads(da, db, u, h, dx1, f"ffn1_{l}")
    grad_x = dx[None]
    late, late_offs = small_rows(small[0] if depth > 1 else small[0] + [dgf[0]])
    one = lambda: lax.empty((N_DEV, fs, d), BF16)
    g_a, ((slots['w_in'][0],),) = _wgrad(da, h, one(), fs, 0, 1.0, "ffn1_0_dw1", riders=[_Exchange(g_wi)])
    g_b, ((slots['ffn1_w1'][0],),) = _wgrad(db, h, one(), fs, 0, 1.0, "ffn1_0_dw3", riders=[_Exchange(g_a)])
    g_c, ((slots['ffn1_w3'][0],),) = _wgrad(u, dx1, one(), fs, 0, FFN_RESIDUAL, "ffn1_0_dw2",
                                            riders=[_Exchange(g_b)])
    (slots['ffn1_w2'][0],), (late_all,) = _call(lambda: None, "exchange_last", 1, [], [], [], [],
                                                riders=[_Exchange(g_c), _Gather(late)])[1]
    for n in big[0:3]:
        blks[n][0] = 0

    grads, deltas, new_m, new_v = {}, {}, {}, {}
    for n in big:
        grads[n], deltas[n], new_m[n], new_v[n] = _reduce_adamw(
            slots[n], blks[n], n in SHARDED_COLS, w[n], mom1[n], mom2[n], f"reduce_adamw_{n}")

    per_layer = len(small[0])
    shapes = [(d,), (d,), (d,), (8, g), (32, g), (8, g), (len(POOL_WINDOWS), g // len(POOL_WINDOWS),
              g // len(POOL_WINDOWS)), (nh * CHUNK, CHUNK), (nh, CHUNK)]
    late_sum = _sum_slots(late_all, "sum_small_late")
    early_sum = _sum_slots(early_all, "sum_small_early") if depth > 1 else None
    got = [[_take(late_sum, late_offs[k], shapes[k]) for k in range(per_layer)]]
    got += [[_take(early_sum, early_offs[(l - 1) * per_layer + k], shapes[k]) for k in range(per_layer)]
            for l in range(1, depth)]
    col = lambda k: jnp.stack([got[l][k] for l in range(depth)])
    grads['ffn1_norm'], grads['mix_norm'], grads['ffn2_norm'] = col(0), col(1), col(2)
    dvec_all = col(3)
    for k, n in enumerate(('conf_conv_b', 'conf_ln_g', 'conf_ln_b', 'pool_scale', 'gmlp_ln_g', 'gmlp_ln_b')):
        grads[n] = dvec_all[:, k]
    grads['conf_conv_w'] = lax.dynamic_slice_in_dim(col(4)[:, :CONF_KERNEL], me * cs, cs, axis=2)
    grads['sconv_w'] = lax.dynamic_slice_in_dim(col(5)[:, :SHORT_KERNEL], me * cs, cs, axis=2)
    grads['pool_w'] = col(6)
    grads['gmlp_w_s'] = col(7).reshape(depth, nh, CHUNK, CHUNK)
    grads['gmlp_b_s'] = col(8)
    grads['final_norm'] = (_take(early_sum, early_offs[-1], (d,)) if depth > 1
                           else _take(late_sum, late_offs[-1], (d,)))

    for n in WEIGHTS:
        if n not in big:
            deltas[n], new_m[n], new_v[n] = _adamw(w[n], grads[n], mom1[n], mom2[n], f"adamw_{n}")

    return (loss, grad_x, *[grads[n] for n in WEIGHTS], *[deltas[n] for n in WEIGHTS],
            *[new_m[n] for n in WEIGHTS], *[new_v[n] for n in WEIGHTS])
```

```python
import functools

import jax
import jax.numpy as jnp
from jax import lax
from jax.experimental import pallas as pl
from jax.experimental.pallas import tpu as pltpu

F32 = jnp.float32
BF16 = jnp.bfloat16
MESH = pl.DeviceIdType.MESH

N_DEV = 8
EPS = 1e-6
FFN_RESIDUAL = 0.5
CONF_KERNEL = 31
SHORT_KERNEL = 3
POOL_WINDOWS = (2, 4, 8, 16)
CHUNK = 128
HEAD_DIM = 64
N_IN_PIECES = 8
HALO = 32
CONV_ROWS = 16
ADAM_LR = 0.001
ADAM_B1 = 0.9
ADAM_B2 = 0.999
ADAM_EPS = 1e-08
ADAM_WD = 0.01
ADAM_STEP = 10
VMEM_LIMIT_V7X = 56 * 1024 * 1024

WEIGHTS = ['ffn1_norm', 'ffn1_w1', 'ffn1_w3', 'ffn1_w2', 'mix_norm', 'w_in', 'conf_conv_w', 'conf_conv_b',
           'conf_ln_g', 'conf_ln_b', 'sconv_w', 'pool_w', 'pool_scale', 'gmlp_ln_g', 'gmlp_ln_b', 'gmlp_w_s',
           'gmlp_b_s', 'w_out', 'ffn2_norm', 'ffn2_w1', 'ffn2_w3', 'ffn2_w2', 'final_norm']
SHARDED_ROWS = ('ffn1_w2', 'ffn2_w2', 'w_out')
SHARDED_COLS = ('ffn1_w1', 'ffn1_w3', 'ffn2_w1', 'ffn2_w3', 'w_in')
SHARDED_CHAN = ('conf_conv_w', 'sconv_w')


def _params(sem=None):
    return pltpu.CompilerParams(dimension_semantics=sem, vmem_limit_bytes=VMEM_LIMIT_V7X)


def _nn(a, b):
    return jnp.dot(a, b, preferred_element_type=F32)


def _nt(a, b):
    return lax.dot_general(a, b, (((1,), (1,)), ((), ())), preferred_element_type=F32)


def _tn(a, b):
    return lax.dot_general(a, b, (((0,), (0,)), ((), ())), preferred_element_type=F32)


def _sigmoid(x):
    return 0.5 * jnp.tanh(0.5 * x) + 0.5


def _row_tile(n, pref, mult=8):
    t = min(n, pref)
    while n % t or t % mult:
        t -= 1
    return t


def _chunks(n, size):
    out, s = [], 0
    while s < n:
        out.append((s, min(size, n - s)))
        s += size
    return out


def _const(shape):
    return pl.BlockSpec(shape, lambda i: (0,) * len(shape))


def _resident(shape, index):
    return pl.BlockSpec(shape, lambda i: index, pipeline_mode=pl.Buffered(1))


def _wspec(rows, d, blk):
    return _resident((N_DEV, rows, d), (0, blk, 0))


def _layernorm_stats(x):
    mu = jnp.mean(x, axis=-1, keepdims=True)
    xc = x - mu
    r = lax.rsqrt(jnp.mean(xc * xc, axis=-1, keepdims=True) + EPS)
    return xc * r, r


def _layernorm_bwd(dy, g, xhat, r):
    dxh = dy * g
    return r * (dxh - jnp.mean(dxh, axis=-1, keepdims=True) - xhat * jnp.mean(dxh * xhat, axis=-1, keepdims=True))


def _rmsnorm_stats(x):
    r = lax.rsqrt(jnp.mean(x * x, axis=-1, keepdims=True) + EPS)
    return x * r, r


def _rmsnorm_bwd(dh, g, xhat, r):
    dxh = dh * g
    return r * (dxh - xhat * jnp.mean(dxh * xhat, axis=-1, keepdims=True))


class _MixRows:
    def __init__(self, g, wo):
        assert g % wo == 0 and wo % 16 == 0
        self.g, self.wo = g, wo

    def win(self, l):
        return 0

    def wout(self, l):
        return self.g // self.wo


class _Gather:
    def __init__(self, shard):
        self.operands = [shard]
        self.out_shape = [jax.ShapeDtypeStruct((N_DEV,) + shard.shape, shard.dtype)]
        self.scratch = [pltpu.SemaphoreType.DMA((7,)), pltpu.SemaphoreType.DMA((7,)), pltpu.SemaphoreType.DMA]

    def _copies(self, ins, outs, scr, arrivals):
        (x_ref,), (out_ref,), (send_sems, recv_sems, local_sem) = ins, outs, scr
        x, y, c = lax.axis_index("x"), lax.axis_index("y"), lax.axis_index("c")
        me, sibling = (x, y, c), (x, y, 1 - c)
        chips = [(1 - x, y), (x, 1 - y), (1 - x, 1 - y)]

        def slot(px, py, pc):
            return out_ref.at[4 * px + 2 * py + pc]

        def copy(k, block, to, src=None):
            return pltpu.make_async_remote_copy(
                src_ref=slot(*block) if src is None else src, dst_ref=slot(*block),
                send_sem=send_sems.at[k], recv_sem=recv_sems.at[k], device_id=to, device_id_type=MESH)

        mine = pltpu.make_async_copy(x_ref, slot(*me), local_sem)
        first = [copy(0, me, sibling, src=x_ref)]
        first += [copy(1 + j, me, (*chip, c), src=x_ref) for j, chip in enumerate(chips)]
        passed = [copy(4 + j, (*chip, c), sibling) for j, chip in enumerate(chips)]
        if arrivals == "ici":
            landing = [copy(1 + j, (*chip, c), me) for j, chip in enumerate(chips)]
        else:
            landing = [copy(0, sibling, me)] + [copy(4 + j, (*chip, 1 - c), me) for j, chip in enumerate(chips)]
        return mine, first, passed, landing

    def begin(self, i, n, ins, outs, scr):
        mine, first, passed, over_ici = self._copies(ins, outs, scr, "ici")

        @pl.when(i == 0)
        def _():
            mine.start()
            for cp in first:
                cp.start()

        @pl.when(i == (n - 3 if n >= 6 else n // 2))
        def _():
            for landed, onward in zip(over_ici, passed):
                landed.wait_recv()
                onward.start()

    def end(self, i, n, ins, outs, scr):
        mine, first, passed, over_d2d = self._copies(ins, outs, scr, "d2d")

        @pl.when(i == n - 1)
        def _():
            for cp in over_d2d:
                cp.wait_recv()
            for cp in first + passed:
                cp.wait_send()
            mine.wait()


class _Exchange:
    def __init__(self, blocks):
        self.operands = [blocks]
        self.out_shape = [jax.ShapeDtypeStruct(blocks.shape, blocks.dtype)]
        self.scratch = [pltpu.SemaphoreType.DMA((7,)), pltpu.SemaphoreType.DMA((7,)), pltpu.SemaphoreType.DMA]

    def _copies(self, ins, outs, scr, with_arrivals):
        (s_ref,), (r_ref,), (send_sems, recv_sems, local_sem) = ins, outs, scr
        x, y, c = lax.axis_index("x"), lax.axis_index("y"), lax.axis_index("c")
        me = 4 * x + 2 * y + c
        mine = pltpu.make_async_copy(s_ref.at[me], r_ref.at[me], local_sem)
        sends, recvs = [], []
        for k in range(1, N_DEV):
            px = (1 - x) if (k >> 2) & 1 else x
            py = (1 - y) if (k >> 1) & 1 else y
            pc = (1 - c) if k & 1 else c
            peer = 4 * px + 2 * py + pc
            for dst, group in ((r_ref.at[me], sends), (r_ref.at[peer], recvs)):
                if group is sends or with_arrivals:
                    group.append(pltpu.make_async_remote_copy(
                        src_ref=s_ref.at[peer], dst_ref=dst, send_sem=send_sems.at[k - 1],
                        recv_sem=recv_sems.at[k - 1], device_id=(px, py, pc), device_id_type=MESH))
        return mine, sends, recvs

    def begin(self, i, n, ins, outs, scr):
        mine, sends, _ = self._copies(ins, outs, scr, False)

        @pl.when(i == 0)
        def _():
            mine.start()
            for cp in sends:
                cp.start()

    def end(self, i, n, ins, outs, scr):
        mine, sends, recvs = self._copies(ins, outs, scr, True)

        @pl.when(i == n - 1)
        def _():
            for cp in recvs:
                cp.wait_recv()
            for cp in sends:
                cp.wait_send()
            mine.wait()


_ANY = pl.BlockSpec(memory_space=pl.ANY)


def _call(body, name, steps, operands, in_specs, out_shape, out_specs, scratch=(), riders=(), aliases=None):
    n_in, n_out, n_scr = len(operands), len(out_shape), len(scratch)

    def full(*refs):
        pos = [0]

        def take(k):
            pos[0] += k
            return refs[pos[0] - k:pos[0]]

        h_in, r_in = take(n_in), [take(len(r.operands)) for r in riders]
        h_out, r_out = take(n_out), [take(len(r.out_shape)) for r in riders]
        h_scr, r_scr = take(n_scr), [take(len(r.scratch)) for r in riders]
        i = pl.program_id(0)
        for r, a, b, c in zip(riders, r_in, r_out, r_scr):
            r.begin(i, steps, a, b, c)
        body(*h_in, *h_out, *h_scr)
        for r, a, b, c in zip(riders, r_in, r_out, r_scr):
            r.end(i, steps, a, b, c)

    outs = pl.pallas_call(
        full, name=name, grid=(steps,),
        out_shape=tuple(out_shape) + tuple(s for r in riders for s in r.out_shape),
        in_specs=list(in_specs) + [_ANY for r in riders for _ in r.operands],
        out_specs=tuple(out_specs) + tuple(_ANY for r in riders for _ in r.out_shape),
        scratch_shapes=list(scratch) + [s for r in riders for s in r.scratch],
        input_output_aliases=aliases or {},
        compiler_params=_params(("arbitrary",)),
    )(*operands, *(o for r in riders for o in r.operands))
    host, rest = outs[:n_out], list(outs[n_out:])
    rides = []
    for r in riders:
        rides.append(rest[:len(r.out_shape)])
        rest = rest[len(r.out_shape):]
    return host, rides


def _alone(rider, name):
    return _call(lambda: None, name, 1, [], [], [], [], riders=[rider])[1][0][0]


def _sum_slots(slots, name):
    _, rows, cols = slots.shape
    tr = _row_tile(rows, 256, 16)

    def body(s_ref, o_ref):
        acc = s_ref[0].astype(F32)
        for j in range(1, N_DEV):
            acc = acc + s_ref[j].astype(F32)
        o_ref[...] = acc

    return pl.pallas_call(
        body, name=name, grid=(rows // tr,),
        out_shape=jax.ShapeDtypeStruct((rows, cols), F32),
        in_specs=[pl.BlockSpec((N_DEV, tr, cols), lambda i: (0, i, 0))],
        out_specs=pl.BlockSpec((tr, cols), lambda i: (i, 0)),
        compiler_params=_params(("parallel",)),
    )(slots)


def _ffn_fwd(x, gain, wset, fs, name, riders=()):
    t, d = x.shape
    f = N_DEV * fs
    tm = _row_tile(t, 512, 128)
    cols = _chunks(f, 1024)

    def body(x_ref, g_ref, w1_ref, w3_ref, w2_ref, y_ref, a_ref, b_ref, u_ref):
        xv = x_ref[...]
        xhat, _ = _rmsnorm_stats(xv)
        h = (xhat * g_ref[...]).astype(BF16)
        w1 = w1_ref[...].reshape(f, d)
        w3 = w3_ref[...].reshape(f, d)
        for s, n in cols:
            a = _nt(h, w1[s:s + n, :])
            b = _nt(h, w3[s:s + n, :])
            a_ref[:, s:s + n] = a.astype(BF16)
            b_ref[:, s:s + n] = b.astype(BF16)
            u_ref[:, s:s + n] = (a * _sigmoid(a) * b).astype(BF16)
        y_ref[...] = xv + FFN_RESIDUAL * _nn(u_ref[...], w2_ref[...].reshape(f, d))

    return _call(
        body, name, t // tm, [x, gain.reshape(1, d), wset, wset, wset],
        [pl.BlockSpec((tm, d), lambda i: (i, 0)), _const((1, d)),
         _wspec(fs, d, 0), _wspec(fs, d, 1), _wspec(fs, d, 2)],
        [jax.ShapeDtypeStruct((t, d), F32), jax.ShapeDtypeStruct((t, f), BF16), jax.ShapeDtypeStruct((t, f), BF16)],
        [pl.BlockSpec((tm, d), lambda i: (i, 0)), pl.BlockSpec((tm, f), lambda i: (i, 0)),
         pl.BlockSpec((tm, f), lambda i: (i, 0))],
        scratch=[pltpu.VMEM((tm, f), BF16)], riders=riders)


def _ffn_bwd(x, dy, a, b, gain, wset, fs, name, riders=()):
    t, d = x.shape
    f = N_DEV * fs
    tm = _row_tile(t, 256, 128)
    cols = _chunks(f, 1024)

    def body(x_ref, dy_ref, a_ref, b_ref, g_ref, w1_ref, w3_ref, w2_ref,
             dx_ref, da_ref, db_ref, u_ref, h_ref, dg_ref):
        @pl.when(pl.program_id(0) == 0)
        def _():
            dg_ref[...] = jnp.zeros_like(dg_ref)

        xv, dyv, g = x_ref[...], dy_ref[...], g_ref[...]
        xhat, r = _rmsnorm_stats(xv)
        h_ref[...] = (xhat * g).astype(BF16)
        dyb = (FFN_RESIDUAL * dyv).astype(BF16)
        w1 = w1_ref[...].reshape(f, d)
        w3 = w3_ref[...].reshape(f, d)
        w2 = w2_ref[...].reshape(f, d)
        dh = jnp.zeros((tm, d), F32)
        for s, n in cols:
            du = _nt(dyb, w2[s:s + n, :])
            av = a_ref[:, s:s + n].astype(F32)
            bv = b_ref[:, s:s + n].astype(F32)
            sig = _sigmoid(av)
            sa = av * sig
            u_ref[:, s:s + n] = (sa * bv).astype(BF16)
            da = (du * bv * (sig + sa * (1.0 - sig))).astype(BF16)
            db = (du * sa).astype(BF16)
            da_ref[:, s:s + n] = da
            db_ref[:, s:s + n] = db
            dh = dh + _nn(da, w1[s:s + n, :]) + _nn(db, w3[s:s + n, :])
        dx_ref[...] = dyv + _rmsnorm_bwd(dh, g, xhat, r)
        dg_ref[0:1, :] += jnp.sum(dh * xhat, axis=0, keepdims=True)

    big = jax.ShapeDtypeStruct((t, f), BF16)
    row = lambda w: pl.BlockSpec((tm, w), lambda i: (i, 0))
    return _call(
        body, name, t // tm, [x, dy, a, b, gain.reshape(1, d), wset, wset, wset],
        [row(d), row(d), row(f), row(f), _const((1, d)), _wspec(fs, d, 0), _wspec(fs, d, 1), _wspec(fs, d, 2)],
        [jax.ShapeDtypeStruct((t, d), F32), big, big, big, jax.ShapeDtypeStruct((t, d), BF16),
         jax.ShapeDtypeStruct((8, d), F32)],
        [row(d), row(f), row(f), row(f), row(d), _const((8, d))], riders=riders)


def _hidden_major(wcols, name):
    depth, d, c = wcols.shape

    def body(w_ref, o_ref):
        eye = lax.broadcasted_iota(jnp.int32, (d, d), 0) == lax.broadcasted_iota(jnp.int32, (d, d), 1)
        o_ref[0] = _tn(w_ref[0].astype(BF16), eye.astype(BF16)).astype(BF16)

    return pl.pallas_call(
        body, name=name, grid=(depth,),
        out_shape=jax.ShapeDtypeStruct((depth, c, d), BF16),
        in_specs=[pl.BlockSpec((1, d, c), lambda l: (l, 0, 0))],
        out_specs=pl.BlockSpec((1, c, d), lambda l: (l, 0, 0)),
        compiler_params=_params(("parallel",)),
    )(wcols)


def _wgrad(lhs, rhs, gbuf, rows, blk, scale, name, riders=()):
    t, m = lhs.shape
    d = rhs.shape[1]
    bt = _row_tile(t, 1024, 128)
    nk = t // bt
    assert m == N_DEV * rows

    def body(a_ref, b_ref, g_in, o_ref, acc_ref):
        del g_in
        k = pl.program_id(0)

        @pl.when(k == 0)
        def _():
            acc_ref[...] = jnp.zeros_like(acc_ref)

        acc_ref[...] += _tn(a_ref[...], b_ref[...].astype(BF16))

        @pl.when(k == nk - 1)
        def _():
            o_ref[...] = (scale * acc_ref[...]).astype(BF16).reshape(N_DEV, rows, d)

    (out,), rides = _call(
        body, name, nk, [lhs, rhs, gbuf],
        [pl.BlockSpec((bt, m), lambda k: (k, 0)), pl.BlockSpec((bt, d), lambda k: (k, 0)), _ANY],
        [jax.ShapeDtypeStruct(gbuf.shape, gbuf.dtype)], [pl.BlockSpec((N_DEV, rows, d), lambda k: (0, blk, 0))],
        scratch=[pltpu.VMEM((m, d), F32)], riders=riders, aliases={2: 0})
    return (out, rides) if riders else out


def _head_masks(rows, g):
    lane = lax.broadcasted_iota(jnp.int32, (rows, g), 1)
    return [(lane >= h * HEAD_DIM) & (lane < (h + 1) * HEAD_DIM) for h in range(g // HEAD_DIM)]


def _pool_windows(rows, g):
    lane = lax.broadcasted_iota(jnp.int32, (rows, g), 1)
    pg = g // len(POOL_WINDOWS)
    w = jnp.full((rows, g), float(POOL_WINDOWS[-1]), F32)
    for k in range(len(POOL_WINDOWS) - 2, -1, -1):
        w = jnp.where(lane < (k + 1) * pg, float(POOL_WINDOWS[k]), w)
    return w


def _window_sum(ref, tmp, first, rows, wl, direction):
    total = ref.shape[0]
    src, out, shift = ref, None, 1
    for k, w in enumerate(POOL_WINDOWS):
        assert w == 2 * shift and 8 * (k + 1) <= HALO
        if k + 1 < len(POOL_WINDOWS):
            n = total - 8 * (k + 1)
            lo = 8 * (k + 1) if direction < 0 else 0
            dst = tmp.at[k % 2]
            dst[pl.ds(lo, n), :] = src[pl.ds(lo, n), :] + src[pl.ds(lo + direction * shift, n), :]
            src = dst
            level = dst[pl.ds(first, rows), :]
        else:
            level = src[pl.ds(first, rows), :] + src[pl.ds(first + direction * shift, rows), :]
        out = level if out is None else jnp.where(wl >= float(w), level, out)
        shift = w
    return out


def _shifted_copies(src_ref, dst_ref):
    rows = dst_ref.shape[1]
    for b in range(1, 8):
        dst_ref[b - 1] = src_ref[pl.ds(b, rows), :]


def _window(src_ref, shifted_ref, off, rows):
    b = off % 8
    if b == 0:
        return src_ref[pl.ds(off, rows), :]
    return shifted_ref[b - 1, pl.ds(off - b, rows), :]


def _pool_divisor(first_row, rows, g):
    pos1 = (lax.broadcasted_iota(jnp.int32, (rows, g), 0) + first_row + 1).astype(F32)
    return jnp.minimum(pos1, _pool_windows(rows, g))


def _spatial_mix(vq, wcat_ref, masks):
    vstack = jnp.concatenate([jnp.where(m, vq, jnp.zeros_like(vq)) for m in masks], axis=0)
    return _nn(wcat_ref[...], vstack)


def _mix_fwd(x, gain, wbuf, lay, l, small, name):
    t, d = x.shape
    g = lay.g
    di = N_DEV * g
    tm = _row_tile(t, 256, CHUNK)
    conv_w, vecs, sconv_w, pool_bd, wcat, bst = small

    def body(x_ref, g_ref, wi_ref, wo_ref, cw_ref, vec_ref, sw_ref, pbd_ref, wcat_ref, bst_ref,
             y_ref, mix_ref, h_ref, pm_ref, pp_ref, ea_ref, eb_ref, ec_ref, sh_ref, tmp_ref):
        i = pl.program_id(0)

        @pl.when(i == 0)
        def _():
            pp_ref[...] = jnp.zeros_like(pp_ref)

        xhat, _ = _rmsnorm_stats(x_ref[...])
        h = (xhat * g_ref[...]).astype(BF16)
        h_ref[...] = h
        pm_ref[...] = _nt(h, wi_ref[...].reshape(di, d)).astype(BF16)

        def prev(j):
            return pp_ref[:, j * g:(j + 1) * g].astype(F32)

        def main(j):
            return pm_ref[:, j * g:(j + 1) * g].astype(F32)

        vec = vec_ref[...]
        ea_ref[0:HALO, :] = prev(0) * _sigmoid(prev(1))
        ea_ref[HALO:, :] = main(0) * _sigmoid(main(1))
        _shifted_copies(ea_ref, sh_ref)
        cw = cw_ref[...]
        c = jnp.zeros((tm, g), F32) + vec[0:1, :]
        for k in range(CONF_KERNEL):
            c = c + cw[k:k + 1, :] * _window(ea_ref, sh_ref, HALO - (CONF_KERNEL - 1) + k, tm)
        chat, _ = _layernorm_stats(c)
        ln = chat * vec[1:2, :] + vec[2:3, :]
        mix_ref[:, 0:g] = (ln * _sigmoid(ln)).astype(BF16)
        eb_ref[0:HALO, :] = prev(3) * prev(4)
        eb_ref[HALO:, :] = main(3) * main(4)
        sw = sw_ref[...]
        cz = jnp.zeros((tm, g), F32)
        for k in range(SHORT_KERNEL):
            cz = cz + sw[k:k + 1, :] * eb_ref[pl.ds(HALO - (SHORT_KERNEL - 1) + k, tm), :]
        mix_ref[:, g:2 * g] = (main(2) * cz).astype(BF16)
        ec_ref[0:HALO, :] = prev(5)
        xp = main(5)
        ec_ref[HALO:, :] = xp
        wl = _pool_windows(tm, g)
        mean = _window_sum(ec_ref, tmp_ref, HALO, tm, wl, -1) / _pool_divisor(i * tm, tm, g)
        yc = _nn((mean - xp).astype(BF16), pbd_ref[...])
        mix_ref[:, 2 * g:3 * g] = (yc * vec[3:4, :]).astype(BF16)
        vhat, _ = _layernorm_stats(main(7))
        v = (vhat * vec[4:5, :] + vec[5:6, :]).astype(BF16)
        gu = main(6)
        masks = _head_masks(CHUNK, g)
        for q in range(tm // CHUNK):
            rs = slice(q * CHUNK, (q + 1) * CHUNK)
            mixed = _spatial_mix(v[rs, :], wcat_ref, masks) + bst_ref[...]
            mix_ref[rs, 3 * g:4 * g] = (gu[rs, :] * mixed).astype(BF16)
        y_ref[...] = x_ref[...] + _nn(mix_ref[...], wo_ref[...].reshape(d, d))
        pp_ref[...] = pm_ref[tm - HALO:, :]

    row = lambda w: pl.BlockSpec((tm, w), lambda i: (i, 0))
    return pl.pallas_call(
        body, name=name, grid=(t // tm,),
        out_shape=(jax.ShapeDtypeStruct((t, d), F32), jax.ShapeDtypeStruct((t, 4 * g), BF16),
                   jax.ShapeDtypeStruct((t, d), BF16), jax.ShapeDtypeStruct((t, di), BF16)),
        in_specs=[row(d), _const((1, d)), _wspec(g, d, lay.win(l)), _wspec(lay.wo, d, lay.wout(l)),
                  _const(conv_w.shape), _const(vecs.shape), _const(sconv_w.shape), _const(pool_bd.shape),
                  _const(wcat.shape), _const(bst.shape)],
        out_specs=(row(d), row(4 * g), row(d), row(di)),
        scratch_shapes=[pltpu.VMEM((HALO, di), BF16)] + [pltpu.VMEM((HALO + tm, g), F32)] * 3
        + [pltpu.VMEM((7, HALO + tm - 8, g), F32), pltpu.VMEM((2, HALO + tm, g), F32)],
        compiler_params=_params(("arbitrary",)),
    )(x, gain.reshape(1, d), wbuf, wbuf, conv_w, vecs, sconv_w, pool_bd, wcat, bst)


def _mix_bwd(p, dy, x, gain, wbuf, lay, l, small, name, riders=()):
    t, d = dy.shape
    g = lay.g
    di = N_DEV * g
    tm = _row_tile(t, 256, CHUNK)
    nt = t // tm
    nh = g // HEAD_DIM
    per = tm // HALO
    ext = tm + HALO
    conv_w, vecs, sconv_w, pool_bd, wcat, bst, wcat_t = small

    def body(pp_ref, pm_ref, pn_ref, dy_ref, dyn_ref, x_ref, g_ref, wi_ref, wo_ref, cw_ref, vec_ref, sw_ref, pbd_ref,
             wcat_ref, bst_ref, wcatt_ref, dp_ref, dcw_ref, dvec_ref, dsw_ref, dpbd_ref, dws_ref, dbs_ref,
             dx_ref, dg_ref, ea_ref, eb_ref, ec_ref, fa_ref, fb_ref, fc_ref, sh_ref, sf_ref, tmp_ref):
        i = pl.program_id(0)
        seen = (i > 0).astype(F32)
        more = (i < nt - 1).astype(F32)

        @pl.when(i == 0)
        def _():
            for ref in (dcw_ref, dvec_ref, dsw_ref, dpbd_ref, dws_ref, dbs_ref, dg_ref):
                ref[...] = jnp.zeros_like(ref)

        def prev(j):
            return pp_ref[:, j * g:(j + 1) * g].astype(F32) * seen

        def main(j):
            return pm_ref[:, j * g:(j + 1) * g].astype(F32)

        def nxt(j):
            return pn_ref[:, j * g:(j + 1) * g].astype(F32)

        def colsum(v):
            return jnp.sum(v, axis=0, keepdims=True)

        vec = vec_ref[...]
        wo = wo_ref[...].reshape(d, d)
        dmix = jnp.concatenate([_nt(dy_ref[...].astype(BF16), wo),
                                _nt((dyn_ref[...] * more).astype(BF16), wo)], axis=0)

        val, gate = main(0), main(1)
        sg = _sigmoid(gate)
        ea_ref[0:HALO, :] = prev(0) * _sigmoid(prev(1))
        ea_ref[HALO:ext, :] = val * sg
        ea_ref[ext:, :] = nxt(0) * _sigmoid(nxt(1))
        _shifted_copies(ea_ref, sh_ref)
        cw = cw_ref[...]
        c = jnp.zeros((ext, g), F32) + vec[0:1, :]
        for k in range(CONF_KERNEL):
            c = c + cw[k:k + 1, :] * _window(ea_ref, sh_ref, HALO - (CONF_KERNEL - 1) + k, ext)
        chat, r = _layernorm_stats(c)
        ln = chat * vec[1:2, :] + vec[2:3, :]
        sl = _sigmoid(ln)
        dln = dmix[:, 0:g] * (sl * (1.0 + ln * (1.0 - sl)))
        dvec_ref[1:2, :] += colsum(dln[0:tm] * chat[0:tm])
        dvec_ref[2:3, :] += colsum(dln[0:tm])
        dc = _layernorm_bwd(dln, vec[1:2, :], chat, r)
        dvec_ref[0:1, :] += colsum(dc[0:tm])
        fa_ref[...] = dc
        _shifted_copies(fa_ref, sf_ref)
        dcm = dc[0:tm]
        dy0 = jnp.zeros((tm, g), F32)
        for k in range(CONF_KERNEL):
            dy0 = dy0 + cw[k:k + 1, :] * _window(fa_ref, sf_ref, CONF_KERNEL - 1 - k, tm)
            dcw_ref[k:k + 1, :] += colsum(dcm * _window(ea_ref, sh_ref, HALO - (CONF_KERNEL - 1) + k, tm))
        dp_ref[:, 0:g] = (dy0 * sg).astype(BF16)
        dp_ref[:, g:2 * g] = (dy0 * val * sg * (1.0 - sg)).astype(BF16)

        sb, sc, sx = main(2), main(3), main(4)
        eb_ref[0:HALO, :] = prev(3) * prev(4)
        eb_ref[HALO:ext, :] = sc * sx
        sw = sw_ref[...]
        cz = jnp.zeros((tm, g), F32)
        for k in range(SHORT_KERNEL):
            cz = cz + sw[k:k + 1, :] * eb_ref[pl.ds(HALO - (SHORT_KERNEL - 1) + k, tm), :]
        dob = dmix[:, g:2 * g]
        dp_ref[:, 2 * g:3 * g] = (dob[0:tm] * cz).astype(BF16)
        fb_ref[0:tm, :] = dob[0:tm] * sb
        fb_ref[tm:, :] = dob[tm:] * nxt(2)
        dczm = fb_ref[0:tm, :]
        dz = jnp.zeros((tm, g), F32)
        for k in range(SHORT_KERNEL):
            dz = dz + sw[k:k + 1, :] * fb_ref[pl.ds(SHORT_KERNEL - 1 - k, tm), :]
            dsw_ref[k:k + 1, :] += colsum(dczm * eb_ref[pl.ds(HALO - (SHORT_KERNEL - 1) + k, tm), :])
        dp_ref[:, 3 * g:4 * g] = (dz * sx).astype(BF16)
        dp_ref[:, 4 * g:5 * g] = (dz * sc).astype(BF16)

        xp = main(5)
        ec_ref[0:HALO, :] = prev(5)
        ec_ref[HALO:, :] = xp
        wl = _pool_windows(tm, g)
        dpool = ((_window_sum(ec_ref, tmp_ref, HALO, tm, wl, -1) / _pool_divisor(i * tm, tm, g)) - xp).astype(BF16)
        pbd = pbd_ref[...]
        yc = _nn(dpool, pbd)
        doc = dmix[:, 2 * g:3 * g]
        dvec_ref[3:4, :] += colsum(doc[0:tm] * yc)
        dyc = (doc * vec[3:4, :]).astype(BF16)
        dpbd_ref[...] += _tn(dpool, dyc[0:tm])
        dd = _nt(dyc, pbd)
        fc_ref[...] = dd / _pool_divisor(i * tm, ext, g)
        dp_ref[:, 5 * g:6 * g] = (_window_sum(fc_ref, tmp_ref, 0, tm, wl, 1) - dd[0:tm]).astype(BF16)

        gv = main(7)
        vhat, rv = _layernorm_stats(gv)
        v = (vhat * vec[4:5, :] + vec[5:6, :]).astype(BF16)
        gu = main(6)
        dod = dmix[0:tm, 3 * g:4 * g]
        masks = _head_masks(CHUNK, g)
        tril = (lax.broadcasted_iota(jnp.int32, (CHUNK, CHUNK), 1)
                <= lax.broadcasted_iota(jnp.int32, (CHUNK, CHUNK), 0)).astype(F32)
        tril = jnp.concatenate([tril] * nh, axis=0)
        head_col = lax.broadcasted_iota(jnp.int32, (CHUNK, CHUNK), 1)
        dvs = []
        for q in range(tm // CHUNK):
            rs = slice(q * CHUNK, (q + 1) * CHUNK)
            vq = v[rs, :]
            mixed = _spatial_mix(vq, wcat_ref, masks) + bst_ref[...]
            dp_ref[rs, 6 * g:7 * g] = (dod[rs, :] * mixed).astype(BF16)
            dmixed = dod[rs, :] * gu[rs, :]
            dmb = dmixed.astype(BF16)
            for h, m in enumerate(masks):
                hs = jnp.sum(jnp.where(m, dmixed, 0.0), axis=-1, keepdims=True)
                dbs_ref[...] += jnp.where(head_col == h, hs, 0.0)
            dmstack = jnp.concatenate([jnp.where(m, dmb, jnp.zeros_like(dmb)) for m in masks], axis=0)
            dws_ref[...] += _nt(dmstack, vq) * tril
            back = _nn(wcatt_ref[...], dmb)
            dv = jnp.zeros((CHUNK, g), F32)
            for h, m in enumerate(masks):
                dv = jnp.where(m, back[h * CHUNK:(h + 1) * CHUNK, :], dv)
            dvs.append(dv)
        dv = jnp.concatenate(dvs, axis=0)
        dvec_ref[4:5, :] += colsum(dv * vhat)
        dvec_ref[5:6, :] += colsum(dv)
        dp_ref[:, 7 * g:8 * g] = _layernorm_bwd(dv, vec[4:5, :], vhat, rv).astype(BF16)

        xhat, rx = _rmsnorm_stats(x_ref[...])
        dh = _nn(dp_ref[...], wi_ref[...].reshape(di, d))
        dx_ref[...] = dy_ref[...] + _rmsnorm_bwd(dh, g_ref[...], xhat, rx)
        dg_ref[0:1, :] += colsum(dh * xhat)

    halo_prev = pl.BlockSpec((HALO, di), lambda i: (jnp.maximum(i * per - 1, 0), 0))
    halo_next = lambda w: pl.BlockSpec((HALO, w), lambda i: (jnp.minimum((i + 1) * per, t // HALO - 1), 0))
    row = lambda w: pl.BlockSpec((tm, w), lambda i: (i, 0))
    outs = (jax.ShapeDtypeStruct((t, di), BF16), jax.ShapeDtypeStruct(conv_w.shape, F32),
            jax.ShapeDtypeStruct(vecs.shape, F32), jax.ShapeDtypeStruct(sconv_w.shape, F32),
            jax.ShapeDtypeStruct((g, g), F32), jax.ShapeDtypeStruct((nh * CHUNK, CHUNK), F32),
            jax.ShapeDtypeStruct((CHUNK, CHUNK), F32), jax.ShapeDtypeStruct((t, d), F32),
            jax.ShapeDtypeStruct((8, d), F32))
    return _call(
        body, name, nt, [p, p, p, dy, dy, x, gain.reshape(1, d), wbuf, wbuf, conv_w, vecs, sconv_w, pool_bd, wcat,
                         bst, wcat_t],
        [halo_prev, row(di), halo_next(di), row(d), halo_next(d), row(d), _const((1, d)),
         _wspec(g, d, lay.win(l)), _wspec(lay.wo, d, lay.wout(l)),
         _const(conv_w.shape), _const(vecs.shape), _const(sconv_w.shape), _const(pool_bd.shape),
         _const(wcat.shape), _const(bst.shape), _const(wcat_t.shape)],
        list(outs), [row(di)] + [_const(o.shape) for o in outs[1:7]] + [row(d), _const((8, d))],
        scratch=[pltpu.VMEM((HALO + tm + HALO, g), F32), pltpu.VMEM((HALO + tm, g), F32),
                 pltpu.VMEM((HALO + tm, g), F32), pltpu.VMEM((ext, g), F32), pltpu.VMEM((ext, g), F32),
                 pltpu.VMEM((ext, g), F32), pltpu.VMEM((7, HALO + tm + HALO - 8, g), F32),
                 pltpu.VMEM((7, ext - 8, g), F32), pltpu.VMEM((2, ext, g), F32)],
        riders=riders)


def _loss_head(x, target, gain, name):
    t, d = x.shape
    tm = _row_tile(t, 512, 8)

    def body(x_ref, t_ref, g_ref, dx_ref, loss_ref, dg_ref):
        @pl.when(pl.program_id(0) == 0)
        def _():
            loss_ref[...] = jnp.zeros_like(loss_ref)
            dg_ref[...] = jnp.zeros_like(dg_ref)

        g = g_ref[...]
        xhat, r = _rmsnorm_stats(x_ref[...])
        err = xhat * g - t_ref[...]
        loss_ref[...] += 0.5 * jnp.sum(jnp.mean(err * err, axis=-1, keepdims=True))
        dyv = err * (1.0 / d)
        dx_ref[...] = _rmsnorm_bwd(dyv, g, xhat, r)
        dg_ref[0:1, :] += jnp.sum(dyv * xhat, axis=0, keepdims=True)

    row = pl.BlockSpec((tm, d), lambda i: (i, 0))
    return pl.pallas_call(
        body, name=name, grid=(t // tm,),
        out_shape=(jax.ShapeDtypeStruct((t, d), F32), jax.ShapeDtypeStruct((8, 128), F32),
                   jax.ShapeDtypeStruct((8, d), F32)),
        in_specs=[row, row, _const((1, d))],
        out_specs=(row, _const((8, 128)), _const((8, d))),
        compiler_params=_params(("arbitrary",)),
    )(x, target, gain.reshape(1, d))


def _adam_step(w, gv, m, v):
    nm = ADAM_B1 * m + (1.0 - ADAM_B1) * gv
    nv = ADAM_B2 * v + (1.0 - ADAM_B2) * (gv * gv)
    m_hat = nm / (1.0 - ADAM_B1 ** ADAM_STEP)
    v_hat = nv / (1.0 - ADAM_B2 ** ADAM_STEP)
    return -ADAM_LR * (m_hat / (jnp.sqrt(v_hat) + ADAM_EPS) + ADAM_WD * w), nm, nv


def _reduce_adamw(slots, blks, transposed, w, m, v, name):
    depth, rows, cols = w.shape
    tr = _row_tile(rows, 512, 128) if transposed else _row_tile(rows, 256, 16)
    nt = rows // tr
    pad = -cols % 128

    def body(*refs):
        slot_refs = refs[:depth]
        w_ref, m_ref, v_ref, g_ref, d_ref, nm_ref, nv_ref = refs[depth:]
        layer = pl.program_id(0) // nt
        for l in range(depth):
            @pl.when(layer == l)
            def _(s_ref=slot_refs[l]):
                gv = s_ref[0].astype(F32)
                for j in range(1, N_DEV):
                    gv = gv + s_ref[j].astype(F32)
                if transposed:
                    if pad:
                        gv = jnp.concatenate([gv, jnp.zeros((pad, tr), F32)], axis=0)
                    gv = gv.T[:, :cols]
                g_ref[0] = gv
                d_ref[0], nm_ref[0], nv_ref[0] = _adam_step(w_ref[0], gv, m_ref[0], v_ref[0])

    def slot_spec(l):
        tile = lambda i: jnp.clip(i - l * nt, 0, nt - 1)
        if transposed:
            return pl.BlockSpec((N_DEV, cols, tr), lambda i: (0, blks[l], tile(i)))
        return pl.BlockSpec((N_DEV, tr, cols), lambda i: (0, blks[l] * nt + tile(i), 0))

    spec = pl.BlockSpec((1, tr, cols), lambda i: (i // nt, i % nt, 0))
    shape = jax.ShapeDtypeStruct(w.shape, F32)
    return pl.pallas_call(
        body, name=name, grid=(depth * nt,),
        out_shape=(shape,) * 4, in_specs=[slot_spec(l) for l in range(depth)] + [spec] * 3, out_specs=(spec,) * 4,
        compiler_params=_params(("arbitrary",)),
    )(*slots, w, m, v)


def _adamw(w, grad, m, v, name):
    shape = w.shape
    cols = shape[-1]
    rows = w.size // cols
    tr = _row_tile(rows, 512, 8) if rows > 1024 else rows

    def body(w_ref, g_ref, m_ref, v_ref, d_ref, nm_ref, nv_ref):
        d_ref[...], nm_ref[...], nv_ref[...] = _adam_step(w_ref[...], g_ref[...], m_ref[...], v_ref[...])

    spec = pl.BlockSpec((tr, cols), lambda i: (i, 0))
    flat = jax.ShapeDtypeStruct((rows, cols), F32)
    outs = pl.pallas_call(
        body, name=name, grid=(rows // tr,),
        out_shape=(flat, flat, flat), in_specs=[spec] * 4, out_specs=(spec,) * 3,
        compiler_params=_params(("parallel",)),
    )(*(a.reshape(rows, cols) for a in (w, grad, m, v)))
    return tuple(o.reshape(shape) for o in outs)


def _f32_rows(parts, d, rows):
    out, offs, at = [], [], 0
    for a in parts:
        n = -(-a.size // d)
        out.append(jnp.pad(a.reshape(-1), (0, n * d - a.size)).reshape(n, d))
        offs.append(at)
        at += n
    total = at if rows is None else rows
    assert at <= total
    if total > at:
        out.append(jnp.zeros((total - at, d), F32))
    return jnp.concatenate(out, axis=0), offs


def _take(buf, off, shape):
    n = 1
    for s in shape:
        n *= s
    d = buf.shape[1]
    return buf[off:off + -(-n // d)].reshape(-1)[:n].reshape(shape)


def _block_diag(blocks):
    n, k, _ = blocks.shape
    out = jnp.zeros((n * k, n * k), blocks.dtype)
    for i in range(n):
        out = lax.dynamic_update_slice(out, blocks[i], (i * k, i * k))
    return out


def kernel(x, ffn1_norm, ffn1_w1, ffn1_w3, ffn1_w2, mix_norm, w_in, conf_conv_w, conf_conv_b, conf_ln_g, conf_ln_b, sconv_w, pool_w, pool_scale, gmlp_ln_g, gmlp_ln_b, gmlp_w_s, gmlp_b_s, w_out, ffn2_norm, ffn2_w1, ffn2_w3, ffn2_w2, final_norm, loss_target, m_ffn1_norm, m_ffn1_w1, m_ffn1_w3, m_ffn1_w2, m_mix_norm, m_w_in, m_conf_conv_w, m_conf_conv_b, m_conf_ln_g, m_conf_ln_b, m_sconv_w, m_pool_w, m_pool_scale, m_gmlp_ln_g, m_gmlp_ln_b, m_gmlp_w_s, m_gmlp_b_s, m_w_out, m_ffn2_norm, m_ffn2_w1, m_ffn2_w3, m_ffn2_w2, m_final_norm, v_ffn1_norm, v_ffn1_w1, v_ffn1_w3, v_ffn1_w2, v_mix_norm, v_w_in, v_conf_conv_w, v_conf_conv_b, v_conf_ln_g, v_conf_ln_b, v_sconv_w, v_pool_w, v_pool_scale, v_gmlp_ln_g, v_gmlp_ln_b, v_gmlp_w_s, v_gmlp_b_s, v_w_out, v_ffn2_norm, v_ffn2_w1, v_ffn2_w3, v_ffn2_w2, v_final_norm):
    w = dict(zip(WEIGHTS, (ffn1_norm, ffn1_w1, ffn1_w3, ffn1_w2, mix_norm, w_in, conf_conv_w, conf_conv_b, conf_ln_g,
                           conf_ln_b, sconv_w, pool_w, pool_scale, gmlp_ln_g, gmlp_ln_b, gmlp_w_s, gmlp_b_s, w_out,
                           ffn2_norm, ffn2_w1, ffn2_w3, ffn2_w2, final_norm)))
    mom1 = dict(zip(WEIGHTS, (m_ffn1_norm, m_ffn1_w1, m_ffn1_w3, m_ffn1_w2, m_mix_norm, m_w_in, m_conf_conv_w,
                              m_conf_conv_b, m_conf_ln_g, m_conf_ln_b, m_sconv_w, m_pool_w, m_pool_scale, m_gmlp_ln_g,
                              m_gmlp_ln_b, m_gmlp_w_s, m_gmlp_b_s, m_w_out, m_ffn2_norm, m_ffn2_w1, m_ffn2_w3,
                              m_ffn2_w2, m_final_norm)))
    mom2 = dict(zip(WEIGHTS, (v_ffn1_norm, v_ffn1_w1, v_ffn1_w3, v_ffn1_w2, v_mix_norm, v_w_in, v_conf_conv_w,
                              v_conf_conv_b, v_conf_ln_g, v_conf_ln_b, v_sconv_w, v_pool_w, v_pool_scale, v_gmlp_ln_g,
                              v_gmlp_ln_b, v_gmlp_w_s, v_gmlp_b_s, v_w_out, v_ffn2_norm, v_ffn2_w1, v_ffn2_w3,
                              v_ffn2_w2, v_final_norm)))
    _, t, d = x.shape
    depth, _, fs = ffn1_w1.shape
    g = w_in.shape[2]
    wo = w_out.shape[1]
    nh = g // HEAD_DIM
    cs = conf_conv_w.shape[2]
    lay = _MixRows(g, wo)
    me = 4 * lax.axis_index("x") + 2 * lax.axis_index("y") + lax.axis_index("c")

    def rows_bf16(mats):
        return jnp.concatenate([m_.astype(BF16) for m_ in mats], axis=0)

    swapped = tuple(n for n in SHARDED_COLS if w[n].shape[2] % 128)
    rows_of = {n: (jnp.swapaxes(w[n], 1, 2) if n in swapped else _hidden_major(w[n], f"rows_{n}"))
               for n in SHARDED_COLS}
    set_f1 = [rows_bf16([rows_of['ffn1_w1'][l], rows_of['ffn1_w3'][l], ffn1_w2[l]]) for l in range(depth)]
    set_f2 = [rows_bf16([rows_of['ffn2_w1'][l], rows_of['ffn2_w3'][l], ffn2_w2[l]]) for l in range(depth)]
    set_mx = [rows_bf16([rows_of['w_in'][l], w_out[l]]) for l in range(depth)]
    convs = jnp.concatenate([conf_conv_w, sconv_w], axis=1)
    nconv = convs.size
    conv_bits = jnp.pad(convs.reshape(-1), (0, CONV_ROWS * d // 2 - nconv))
    conv_rows = lax.bitcast_convert_type(conv_bits, BF16).reshape(CONV_ROWS, d)
    w_f1 = [None] * depth
    w_f1[0] = _alone(_Gather(jnp.concatenate([set_f1[0], conv_rows], axis=0)), "gather_first")

    conv_all = lax.bitcast_convert_type(w_f1[0][:, 3 * fs:, :].reshape(N_DEV, CONV_ROWS * d // 2, 2), F32)
    conv_all = conv_all[:, :nconv].reshape(N_DEV, depth, CONF_KERNEL + SHORT_KERNEL, cs)
    conv_all = conv_all.transpose(1, 2, 0, 3).reshape(depth, CONF_KERNEL + SHORT_KERNEL, g)

    def small_inputs(l):
        cw = jnp.pad(conv_all[l, :CONF_KERNEL], ((0, 32 - CONF_KERNEL), (0, 0)))
        sw = jnp.pad(conv_all[l, CONF_KERNEL:], ((0, 8 - SHORT_KERNEL), (0, 0)))
        vecs = jnp.stack([conf_conv_b[l], conf_ln_g[l], conf_ln_b[l], pool_scale[l], gmlp_ln_g[l], gmlp_ln_b[l],
                          jnp.zeros((g,), F32), jnp.zeros((g,), F32)])
        pool_bd = _block_diag(pool_w[l]).astype(BF16)
        low = jnp.tril(jnp.ones((CHUNK, CHUNK), bool))
        ws = jnp.where(low[None], gmlp_w_s[l], 0.0).astype(BF16)
        wcat = ws.transpose(1, 0, 2).reshape(CHUNK, nh * CHUNK)
        wcat_t = ws.transpose(0, 2, 1).reshape(nh * CHUNK, CHUNK)
        bst = jnp.repeat(gmlp_b_s[l].T, HEAD_DIM, axis=1)
        return cw, vecs, sw, pool_bd, wcat, bst, wcat_t

    xs = x[0]
    saved = []
    for l in range(depth):
        sm = small_inputs(l)
        (x1, a1, b1), ((w_f2,), (w_mx,)) = _ffn_fwd(xs, ffn1_norm[l], w_f1[l], fs, f"ffn1_fwd_{l}",
                                                    riders=[_Gather(set_f2[l]), _Gather(set_mx[l])])
        x2, mix, h2, p = _mix_fwd(x1, mix_norm[l], w_mx, lay, l, sm[:6], f"mix_fwd_{l}")
        nxt = [_Gather(set_f1[l + 1])] if l + 1 < depth else []
        (x3, a2, b2), got = _ffn_fwd(x2, ffn2_norm[l], w_f2, fs, f"ffn2_fwd_{l}", riders=nxt)
        if nxt:
            w_f1[l + 1] = got[0][0]
        saved.append((xs, a1, b1, x1, h2, p, mix, x2, a2, b2, sm, w_f2, w_mx))
        xs = x3

    dx, loss_part, dgf = _loss_head(xs, loss_target[0], final_norm, "loss_head")
    loss = lax.psum(loss_part[0, 0], ("x", "y", "c"))

    def small_rows(parts):
        buf, at = _f32_rows(parts, d, None)
        return jnp.pad(buf, ((0, -buf.shape[0] % 16), (0, 0))), at

    small = [None] * depth
    big = ('ffn1_w1', 'ffn1_w3', 'ffn1_w2', 'ffn2_w1', 'ffn2_w3', 'ffn2_w2', 'w_in', 'w_out')
    slots = {n: [None] * depth for n in big}
    blk_of = {'ffn1_w1': 0, 'ffn1_w3': 1, 'ffn1_w2': 2, 'ffn2_w1': 0, 'ffn2_w3': 1, 'ffn2_w2': 2,
              'w_in': lay.win(0), 'w_out': lay.wout(0)}
    blks = {n: [blk_of[n]] * depth for n in big}

    def landed(names, l, got_):
        for n in names:
            slots[n][l] = got_

    def ffn_grads(da, db, u, h, dy, tag):
        gset = lax.empty((N_DEV, 3 * fs, d), BF16)
        gset = _wgrad(da, h, gset, fs, 0, 1.0, f"{tag}_dw1")
        gset = _wgrad(db, h, gset, fs, 1, 1.0, f"{tag}_dw3")
        return _wgrad(u, dy, gset, fs, 2, FFN_RESIDUAL, f"{tag}_dw2")

    pending = None
    early_all = None
    for l in reversed(range(depth)):
        x0, a1, b1, x1, h2, p, mix, x2, a2, b2, sm, w_f2, w_mx = saved[l]
        ride = [_Exchange(pending)] if pending is not None else []
        (dx2, da, db, u, h, dgn2), got = _ffn_bwd(x2, dx, a2, b2, ffn2_norm[l], w_f2, fs, f"ffn2_bwd_{l}", riders=ride)
        if ride:
            landed(big[0:3], l + 1, got[0][0])
        gset = lax.empty((N_DEV, 3 * fs, d), BF16)
        if l == 0 and depth > 1:
            early, early_offs = small_rows([a for k in range(1, depth) for a in small[k]] + [dgf[0]])
            gset, ((early_all,),) = _wgrad(da, h, gset, fs, 0, 1.0, f"ffn2_{l}_dw1", riders=[_Gather(early)])
        else:
            gset = _wgrad(da, h, gset, fs, 0, 1.0, f"ffn2_{l}_dw1")
        gset = _wgrad(db, h, gset, fs, 1, 1.0, f"ffn2_{l}_dw3")
        gset = _wgrad(u, dx, gset, fs, 2, FFN_RESIDUAL, f"ffn2_{l}_dw2")
        (dp, dcw, dvec, dsw, dpbd, dws, dbs, dx1, dgm), got = _mix_bwd(
            p, dx2, x1, mix_norm[l], w_mx, lay, l, sm, f"mix_bwd_{l}", riders=[_Exchange(gset)])
        landed(big[3:6], l, got[0][0])
        if l > 0:
            gmx = lax.empty((N_DEV, g + wo, d), BF16)
            gmx = _wgrad(mix, dx2, gmx, wo, lay.wout(l), 1.0, f"dw_out_{l}")
            gmx = _wgrad(dp, h2, gmx, g, lay.win(l), 1.0, f"dw_in_{l}")
            ride = [_Exchange(gmx)]
        else:
            g_wo = _wgrad(mix, dx2, lax.empty((N_DEV, wo, d), BF16), wo, 0, 1.0, f"dw_out_{l}")
            g_wi, ((slots['w_out'][0],),) = _wgrad(dp, h2, lax.empty((N_DEV, g, d), BF16), g, 0, 1.0, f"dw_in_{l}",
                                                   riders=[_Exchange(g_wo)])
            blks['w_out'][0] = blks['w_in'][0] = 0
            ride = []
        (dx, da, db, u, h, dgn1), got = _ffn_bwd(x0, dx1, a1, b1, ffn1_norm[l], w_f1[l], fs,
                                                 f"ffn1_bwd_{l}", riders=ride)
        if l > 0:
            landed(big[6:8], l, got[0][0])
        pg = g // len(POOL_WINDOWS)
        dpool = jnp.stack([dpbd[k * pg:(k + 1) * pg, k * pg:(k + 1) * pg] for k in range(len(POOL_WINDOWS))])
        small[l] = [dgn1[0], dgm[0], dgn2[0], dvec, dcw, dsw, dpool, dws, dbs[:, :nh].T]
        if l > 0:
            pending = ffn_grads(da, db, u, h, dx1, f"ffn1_{l}")
    grad_x = dx[None]
    late, late_offs = small_rows(small[0] if depth > 1 else small[0] + [dgf[0]])
    one = lambda: lax.empty((N_DEV, fs, d), BF16)
    g_a, ((slots['w_in'][0],),) = _wgrad(da, h, one(), fs, 0, 1.0, "ffn1_0_dw1", riders=[_Exchange(g_wi)])
    g_b, ((slots['ffn1_w1'][0],),) = _wgrad(db, h, one(), fs, 0, 1.0, "ffn1_0_dw3", riders=[_Exchange(g_a)])
    g_c, ((slots['ffn1_w3'][0],),) = _wgrad(u, dx1, one(), fs, 0, FFN_RESIDUAL, "ffn1_0_dw2",
                                            riders=[_Exchange(g_b)])
    (slots['ffn1_w2'][0],), (late_all,) = _call(lambda: None, "exchange_last", 1, [], [], [], [],
                                                riders=[_Exchange(g_c), _Gather(late)])[1]
    for n in big[0:3]:
        blks[n][0] = 0

    grads, deltas, new_m, new_v = {}, {}, {}, {}
    for n in big:
        if n in swapped:
            outs = _reduce_adamw(slots[n], blks[n], False, *(jnp.swapaxes(a_, 1, 2) for a_ in (w[n], mom1[n], mom2[n])),
                                 f"reduce_adamw_{n}")
            grads[n], deltas[n], new_m[n], new_v[n] = (jnp.swapaxes(o_, 1, 2) for o_ in outs)
        else:
            grads[n], deltas[n], new_m[n], new_v[n] = _reduce_adamw(
                slots[n], blks[n], n in SHARDED_COLS, w[n], mom1[n], mom2[n], f"reduce_adamw_{n}")

    per_layer = len(small[0])
    shapes = [(d,), (d,), (d,), (8, g), (32, g), (8, g), (len(POOL_WINDOWS), g // len(POOL_WINDOWS),
              g // len(POOL_WINDOWS)), (nh * CHUNK, CHUNK), (nh, CHUNK)]
    late_sum = _sum_slots(late_all, "sum_small_late")
    early_sum = _sum_slots(early_all, "sum_small_early") if depth > 1 else None
    got = [[_take(late_sum, late_offs[k], shapes[k]) for k in range(per_layer)]]
    got += [[_take(early_sum, early_offs[(l - 1) * per_layer + k], shapes[k]) for k in range(per_layer)]
            for l in range(1, depth)]
    col = lambda k: jnp.stack([got[l][k] for l in range(depth)])
    grads['ffn1_norm'], grads['mix_norm'], grads['ffn2_norm'] = col(0), col(1), col(2)
    dvec_all = col(3)
    for k, n in enumerate(('conf_conv_b', 'conf_ln_g', 'conf_ln_b', 'pool_scale', 'gmlp_ln_g', 'gmlp_ln_b')):
        grads[n] = dvec_all[:, k]
    grads['conf_conv_w'] = lax.dynamic_slice_in_dim(col(4)[:, :CONF_KERNEL], me * cs, cs, axis=2)
    grads['sconv_w'] = lax.dynamic_slice_in_dim(col(5)[:, :SHORT_KERNEL], me * cs, cs, axis=2)
    grads['pool_w'] = col(6)
    grads['gmlp_w_s'] = col(7).reshape(depth, nh, CHUNK, CHUNK)
    grads['gmlp_b_s'] = col(8)
    grads['final_norm'] = (_take(early_sum, early_offs[-1], (d,)) if depth > 1
                           else _take(late_sum, late_offs[-1], (d,)))

    for n in WEIGHTS:
        if n not in big:
            deltas[n], new_m[n], new_v[n] = _adamw(w[n], grads[n], mom1[n], mom2[n], f"adamw_{n}")

    return (loss, grad_x, *[grads[n] for n in WEIGHTS], *[deltas[n] for n in WEIGHTS],
            *[new_m[n] for n in WEIGHTS], *[new_v[n] for n in WEIGHTS])
```

```python
import functools

import jax
import jax.numpy as jnp
from jax import lax
from jax.experimental import pallas as pl
from jax.experimental.pallas import tpu as pltpu

F32 = jnp.float32
BF16 = jnp.bfloat16
MESH = pl.DeviceIdType.MESH

N_DEV = 8
EPS = 1e-6
FFN_RESIDUAL = 0.5
CONF_KERNEL = 31
SHORT_KERNEL = 3
POOL_WINDOWS = (2, 4, 8, 16)
CHUNK = 128
HEAD_DIM = 64
N_IN_PIECES = 8
HALO = 32
CONV_ROWS = 16
ADAM_LR = 0.001
ADAM_B1 = 0.9
ADAM_B2 = 0.999
ADAM_EPS = 1e-08
ADAM_WD = 0.01
ADAM_STEP = 10
VMEM_LIMIT_V7X = 56 * 1024 * 1024

WEIGHTS = ['ffn1_norm', 'ffn1_w1', 'ffn1_w3', 'ffn1_w2', 'mix_norm', 'w_in', 'conf_conv_w', 'conf_conv_b',
           'conf_ln_g', 'conf_ln_b', 'sconv_w', 'pool_w', 'pool_scale', 'gmlp_ln_g', 'gmlp_ln_b', 'gmlp_w_s',
           'gmlp_b_s', 'w_out', 'ffn2_norm', 'ffn2_w1', 'ffn2_w3', 'ffn2_w2', 'final_norm']
SHARDED_ROWS = ('ffn1_w2', 'ffn2_w2', 'w_out')
SHARDED_COLS = ('ffn1_w1', 'ffn1_w3', 'ffn2_w1', 'ffn2_w3', 'w_in')
SHARDED_CHAN = ('conf_conv_w', 'sconv_w')


def _params(sem=None):
    return pltpu.CompilerParams(dimension_semantics=sem, vmem_limit_bytes=VMEM_LIMIT_V7X)


def _nn(a, b):
    return jnp.dot(a, b, preferred_element_type=F32)


def _nt(a, b):
    return lax.dot_general(a, b, (((1,), (1,)), ((), ())), preferred_element_type=F32)


def _tn(a, b):
    return lax.dot_general(a, b, (((0,), (0,)), ((), ())), preferred_element_type=F32)


def _sigmoid(x):
    return 0.5 * jnp.tanh(0.5 * x) + 0.5


def _row_tile(n, pref, mult=8):
    t = min(n, pref)
    while n % t or t % mult:
        t -= 1
    return t


def _chunks(n, size):
    out, s = [], 0
    while s < n:
        out.append((s, min(size, n - s)))
        s += size
    return out


def _const(shape):
    return pl.BlockSpec(shape, lambda i: (0,) * len(shape))


def _resident(shape, index):
    return pl.BlockSpec(shape, lambda i: index, pipeline_mode=pl.Buffered(1))


def _wspec(rows, d, blk):
    return _resident((N_DEV, rows, d), (0, blk, 0))


def _layernorm_stats(x):
    mu = jnp.mean(x, axis=-1, keepdims=True)
    xc = x - mu
    r = lax.rsqrt(jnp.mean(xc * xc, axis=-1, keepdims=True) + EPS)
    return xc * r, r


def _layernorm_bwd(dy, g, xhat, r):
    dxh = dy * g
    return r * (dxh - jnp.mean(dxh, axis=-1, keepdims=True) - xhat * jnp.mean(dxh * xhat, axis=-1, keepdims=True))


def _rmsnorm_stats(x):
    r = lax.rsqrt(jnp.mean(x * x, axis=-1, keepdims=True) + EPS)
    return x * r, r


def _rmsnorm_bwd(dh, g, xhat, r):
    dxh = dh * g
    return r * (dxh - xhat * jnp.mean(dxh * xhat, axis=-1, keepdims=True))


class _MixRows:
    def __init__(self, g, wo):
        assert g % wo == 0 and wo % 16 == 0
        self.g, self.wo = g, wo

    def win(self, l):
        return 0

    def wout(self, l):
        return self.g // self.wo


class _Gather:
    def __init__(self, shard):
        self.operands = [shard]
        self.out_shape = [jax.ShapeDtypeStruct((N_DEV,) + shard.shape, shard.dtype)]
        self.scratch = [pltpu.SemaphoreType.DMA((7,)), pltpu.SemaphoreType.DMA((7,)), pltpu.SemaphoreType.DMA]

    def _copies(self, ins, outs, scr, arrivals):
        (x_ref,), (out_ref,), (send_sems, recv_sems, local_sem) = ins, outs, scr
        x, y, c = lax.axis_index("x"), lax.axis_index("y"), lax.axis_index("c")
        me, sibling = (x, y, c), (x, y, 1 - c)
        chips = [(1 - x, y), (x, 1 - y), (1 - x, 1 - y)]

        def slot(px, py, pc):
            return out_ref.at[4 * px + 2 * py + pc]

        def copy(k, block, to, src=None):
            return pltpu.make_async_remote_copy(
                src_ref=slot(*block) if src is None else src, dst_ref=slot(*block),
                send_sem=send_sems.at[k], recv_sem=recv_sems.at[k], device_id=to, device_id_type=MESH)

        mine = pltpu.make_async_copy(x_ref, slot(*me), local_sem)
        first = [copy(0, me, sibling, src=x_ref)]
        first += [copy(1 + j, me, (*chip, c), src=x_ref) for j, chip in enumerate(chips)]
        passed = [copy(4 + j, (*chip, c), sibling) for j, chip in enumerate(chips)]
        if arrivals == "ici":
            landing = [copy(1 + j, (*chip, c), me) for j, chip in enumerate(chips)]
        else:
            landing = [copy(0, sibling, me)] + [copy(4 + j, (*chip, 1 - c), me) for j, chip in enumerate(chips)]
        return mine, first, passed, landing

    def begin(self, i, n, ins, outs, scr):
        mine, first, passed, over_ici = self._copies(ins, outs, scr, "ici")

        @pl.when(i == 0)
        def _():
            mine.start()
            for cp in first:
                cp.start()

        @pl.when(i == (n - 3 if n >= 6 else n // 2))
        def _():
            for landed, onward in zip(over_ici, passed):
                landed.wait_recv()
                onward.start()

    def end(self, i, n, ins, outs, scr):
        mine, first, passed, over_d2d = self._copies(ins, outs, scr, "d2d")

        @pl.when(i == n - 1)
        def _():
            for cp in over_d2d:
                cp.wait_recv()
            for cp in first + passed:
                cp.wait_send()
            mine.wait()


class _Exchange:
    def __init__(self, blocks):
        self.operands = [blocks]
        self.out_shape = [jax.ShapeDtypeStruct(blocks.shape, blocks.dtype)]
        self.scratch = [pltpu.SemaphoreType.DMA((7,)), pltpu.SemaphoreType.DMA((7,)), pltpu.SemaphoreType.DMA]

    def _copies(self, ins, outs, scr, with_arrivals):
        (s_ref,), (r_ref,), (send_sems, recv_sems, local_sem) = ins, outs, scr
        x, y, c = lax.axis_index("x"), lax.axis_index("y"), lax.axis_index("c")
        me = 4 * x + 2 * y + c
        mine = pltpu.make_async_copy(s_ref.at[me], r_ref.at[me], local_sem)
        sends, recvs = [], []
        for k in range(1, N_DEV):
            px = (1 - x) if (k >> 2) & 1 else x
            py = (1 - y) if (k >> 1) & 1 else y
            pc = (1 - c) if k & 1 else c
            peer = 4 * px + 2 * py + pc
            for dst, group in ((r_ref.at[me], sends), (r_ref.at[peer], recvs)):
                if group is sends or with_arrivals:
                    group.append(pltpu.make_async_remote_copy(
                        src_ref=s_ref.at[peer], dst_ref=dst, send_sem=send_sems.at[k - 1],
                        recv_sem=recv_sems.at[k - 1], device_id=(px, py, pc), device_id_type=MESH))
        return mine, sends, recvs

    def begin(self, i, n, ins, outs, scr):
        mine, sends, _ = self._copies(ins, outs, scr, False)

        @pl.when(i == 0)
        def _():
            mine.start()
            for cp in sends:
                cp.start()

    def end(self, i, n, ins, outs, scr):
        mine, sends, recvs = self._copies(ins, outs, scr, True)

        @pl.when(i == n - 1)
        def _():
            for cp in recvs:
                cp.wait_recv()
            for cp in sends:
                cp.wait_send()
            mine.wait()


_ANY = pl.BlockSpec(memory_space=pl.ANY)


def _call(body, name, steps, operands, in_specs, out_shape, out_specs, scratch=(), riders=(), aliases=None):
    n_in, n_out, n_scr = len(operands), len(out_shape), len(scratch)

    def full(*refs):
        pos = [0]

        def take(k):
            pos[0] += k
            return refs[pos[0] - k:pos[0]]

        h_in, r_in = take(n_in), [take(len(r.operands)) for r in riders]
        h_out, r_out = take(n_out), [take(len(r.out_shape)) for r in riders]
        h_scr, r_scr = take(n_scr), [take(len(r.scratch)) for r in riders]
        i = pl.program_id(0)
        for r, a, b, c in zip(riders, r_in, r_out, r_scr):
            r.begin(i, steps, a, b, c)
        body(*h_in, *h_out, *h_scr)
        for r, a, b, c in zip(riders, r_in, r_out, r_scr):
            r.end(i, steps, a, b, c)

    outs = pl.pallas_call(
        full, name=name, grid=(steps,),
        out_shape=tuple(out_shape) + tuple(s for r in riders for s in r.out_shape),
        in_specs=list(in_specs) + [_ANY for r in riders for _ in r.operands],
        out_specs=tuple(out_specs) + tuple(_ANY for r in riders for _ in r.out_shape),
        scratch_shapes=list(scratch) + [s for r in riders for s in r.scratch],
        input_output_aliases=aliases or {},
        compiler_params=_params(("arbitrary",)),
    )(*operands, *(o for r in riders for o in r.operands))
    host, rest = outs[:n_out], list(outs[n_out:])
    rides = []
    for r in riders:
        rides.append(rest[:len(r.out_shape)])
        rest = rest[len(r.out_shape):]
    return host, rides


def _alone(rider, name):
    return _call(lambda: None, name, 1, [], [], [], [], riders=[rider])[1][0][0]


def _sum_slots(slots, name):
    _, rows, cols = slots.shape
    tr = _row_tile(rows, 256, 16)

    def body(s_ref, o_ref):
        acc = s_ref[0].astype(F32)
        for j in range(1, N_DEV):
            acc = acc + s_ref[j].astype(F32)
        o_ref[...] = acc

    return pl.pallas_call(
        body, name=name, grid=(rows // tr,),
        out_shape=jax.ShapeDtypeStruct((rows, cols), F32),
        in_specs=[pl.BlockSpec((N_DEV, tr, cols), lambda i: (0, i, 0))],
        out_specs=pl.BlockSpec((tr, cols), lambda i: (i, 0)),
        compiler_params=_params(("parallel",)),
    )(slots)


def _ffn_fwd(x, gain, wset, fs, name, riders=()):
    t, d = x.shape
    f = N_DEV * fs
    tm = _row_tile(t, 512, 128)
    cols = _chunks(f, 1024)

    def body(x_ref, g_ref, w1_ref, w3_ref, w2_ref, y_ref, a_ref, b_ref, u_ref):
        xv = x_ref[...]
        xhat, _ = _rmsnorm_stats(xv)
        h = (xhat * g_ref[...]).astype(BF16)
        w1 = w1_ref[...].reshape(f, d)
        w3 = w3_ref[...].reshape(f, d)
        for s, n in cols:
            a = _nt(h, w1[s:s + n, :])
            b = _nt(h, w3[s:s + n, :])
            a_ref[:, s:s + n] = a.astype(BF16)
            b_ref[:, s:s + n] = b.astype(BF16)
            u_ref[:, s:s + n] = (a * _sigmoid(a) * b).astype(BF16)
        y_ref[...] = xv + FFN_RESIDUAL * _nn(u_ref[...], w2_ref[...].reshape(f, d))

    return _call(
        body, name, t // tm, [x, gain.reshape(1, d), wset, wset, wset],
        [pl.BlockSpec((tm, d), lambda i: (i, 0)), _const((1, d)),
         _wspec(fs, d, 0), _wspec(fs, d, 1), _wspec(fs, d, 2)],
        [jax.ShapeDtypeStruct((t, d), F32), jax.ShapeDtypeStruct((t, f), BF16), jax.ShapeDtypeStruct((t, f), BF16)],
        [pl.BlockSpec((tm, d), lambda i: (i, 0)), pl.BlockSpec((tm, f), lambda i: (i, 0)),
         pl.BlockSpec((tm, f), lambda i: (i, 0))],
        scratch=[pltpu.VMEM((tm, f), BF16)], riders=riders)


def _ffn_bwd(x, dy, a, b, gain, wset, fs, name, riders=()):
    t, d = x.shape
    f = N_DEV * fs
    tm = _row_tile(t, 256, 128)
    cols = _chunks(f, 1024)

    def body(x_ref, dy_ref, a_ref, b_ref, g_ref, w1_ref, w3_ref, w2_ref,
             dx_ref, da_ref, db_ref, u_ref, h_ref, dg_ref):
        @pl.when(pl.program_id(0) == 0)
        def _():
            dg_ref[...] = jnp.zeros_like(dg_ref)

        xv, dyv, g = x_ref[...], dy_ref[...], g_ref[...]
        xhat, r = _rmsnorm_stats(xv)
        h_ref[...] = (xhat * g).astype(BF16)
        dyb = (FFN_RESIDUAL * dyv).astype(BF16)
        w1 = w1_ref[...].reshape(f, d)
        w3 = w3_ref[...].reshape(f, d)
        w2 = w2_ref[...].reshape(f, d)
        dh = jnp.zeros((tm, d), F32)
        for s, n in cols:
            du = _nt(dyb, w2[s:s + n, :])
            av = a_ref[:, s:s + n].astype(F32)
            bv = b_ref[:, s:s + n].astype(F32)
            sig = _sigmoid(av)
            sa = av * sig
            u_ref[:, s:s + n] = (sa * bv).astype(BF16)
            da = (du * bv * (sig + sa * (1.0 - sig))).astype(BF16)
            db = (du * sa).astype(BF16)
            da_ref[:, s:s + n] = da
            db_ref[:, s:s + n] = db
            dh = dh + _nn(da, w1[s:s + n, :]) + _nn(db, w3[s:s + n, :])
        dx_ref[...] = dyv + _rmsnorm_bwd(dh, g, xhat, r)
        dg_ref[0:1, :] += jnp.sum(dh * xhat, axis=0, keepdims=True)

    big = jax.ShapeDtypeStruct((t, f), BF16)
    row = lambda w: pl.BlockSpec((tm, w), lambda i: (i, 0))
    return _call(
        body, name, t // tm, [x, dy, a, b, gain.reshape(1, d), wset, wset, wset],
        [row(d), row(d), row(f), row(f), _const((1, d)), _wspec(fs, d, 0), _wspec(fs, d, 1), _wspec(fs, d, 2)],
        [jax.ShapeDtypeStruct((t, d), F32), big, big, big, jax.ShapeDtypeStruct((t, d), BF16),
         jax.ShapeDtypeStruct((8, d), F32)],
        [row(d), row(f), row(f), row(f), row(d), _const((8, d))], riders=riders)


def _hidden_major(wcols, name):
    depth, d, c = wcols.shape

    def body(w_ref, o_ref):
        eye = lax.broadcasted_iota(jnp.int32, (d, d), 0) == lax.broadcasted_iota(jnp.int32, (d, d), 1)
        o_ref[0] = _tn(w_ref[0].astype(BF16), eye.astype(BF16)).astype(BF16)

    return pl.pallas_call(
        body, name=name, grid=(depth,),
        out_shape=jax.ShapeDtypeStruct((depth, c, d), BF16),
        in_specs=[pl.BlockSpec((1, d, c), lambda l: (l, 0, 0))],
        out_specs=pl.BlockSpec((1, c, d), lambda l: (l, 0, 0)),
        compiler_params=_params(("parallel",)),
    )(wcols)


def _wgrad(lhs, rhs, gbuf, rows, blk, scale, name, riders=()):
    t, m = lhs.shape
    d = rhs.shape[1]
    bt = _row_tile(t, 1024, 128)
    nk = t // bt
    assert m == N_DEV * rows

    def body(a_ref, b_ref, g_in, o_ref, acc_ref):
        del g_in
        k = pl.program_id(0)

        @pl.when(k == 0)
        def _():
            acc_ref[...] = jnp.zeros_like(acc_ref)

        acc_ref[...] += _tn(a_ref[...], b_ref[...].astype(BF16))

        @pl.when(k == nk - 1)
        def _():
            o_ref[...] = (scale * acc_ref[...]).astype(BF16).reshape(N_DEV, rows, d)

    (out,), rides = _call(
        body, name, nk, [lhs, rhs, gbuf],
        [pl.BlockSpec((bt, m), lambda k: (k, 0)), pl.BlockSpec((bt, d), lambda k: (k, 0)), _ANY],
        [jax.ShapeDtypeStruct(gbuf.shape, gbuf.dtype)], [pl.BlockSpec((N_DEV, rows, d), lambda k: (0, blk, 0))],
        scratch=[pltpu.VMEM((m, d), F32)], riders=riders, aliases={2: 0})
    return (out, rides) if riders else out


def _head_masks(rows, g):
    lane = lax.broadcasted_iota(jnp.int32, (rows, g), 1)
    return [(lane >= h * HEAD_DIM) & (lane < (h + 1) * HEAD_DIM) for h in range(g // HEAD_DIM)]


def _pool_windows(rows, g):
    lane = lax.broadcasted_iota(jnp.int32, (rows, g), 1)
    pg = g // len(POOL_WINDOWS)
    w = jnp.full((rows, g), float(POOL_WINDOWS[-1]), F32)
    for k in range(len(POOL_WINDOWS) - 2, -1, -1):
        w = jnp.where(lane < (k + 1) * pg, float(POOL_WINDOWS[k]), w)
    return w


def _window_sum(ref, tmp, first, rows, wl, direction):
    total = ref.shape[0]
    src, out, shift = ref, None, 1
    for k, w in enumerate(POOL_WINDOWS):
        assert w == 2 * shift and 8 * (k + 1) <= HALO
        if k + 1 < len(POOL_WINDOWS):
            n = total - 8 * (k + 1)
            lo = 8 * (k + 1) if direction < 0 else 0
            dst = tmp.at[k % 2]
            dst[pl.ds(lo, n), :] = src[pl.ds(lo, n), :] + src[pl.ds(lo + direction * shift, n), :]
            src = dst
            level = dst[pl.ds(first, rows), :]
        else:
            level = src[pl.ds(first, rows), :] + src[pl.ds(first + direction * shift, rows), :]
        out = level if out is None else jnp.where(wl >= float(w), level, out)
        shift = w
    return out


def _shifted_copies(src_ref, dst_ref):
    rows = dst_ref.shape[1]
    for b in range(1, 8):
        dst_ref[b - 1] = src_ref[pl.ds(b, rows), :]


def _window(src_ref, shifted_ref, off, rows):
    b = off % 8
    if b == 0:
        return src_ref[pl.ds(off, rows), :]
    return shifted_ref[b - 1, pl.ds(off - b, rows), :]


def _pool_divisor(first_row, rows, g):
    pos1 = (lax.broadcasted_iota(jnp.int32, (rows, g), 0) + first_row + 1).astype(F32)
    return jnp.minimum(pos1, _pool_windows(rows, g))


def _spatial_mix(vq, wcat_ref, masks):
    vstack = jnp.concatenate([jnp.where(m, vq, jnp.zeros_like(vq)) for m in masks], axis=0)
    return _nn(wcat_ref[...], vstack)


def _mix_fwd(x, gain, wbuf, lay, l, small, name):
    t, d = x.shape
    g = lay.g
    di = N_DEV * g
    tm = _row_tile(t, 256, CHUNK)
    conv_w, vecs, sconv_w, pool_bd, wcat, bst = small

    def body(x_ref, g_ref, wi_ref, wo_ref, cw_ref, vec_ref, sw_ref, pbd_ref, wcat_ref, bst_ref,
             y_ref, mix_ref, h_ref, pm_ref, c_ref, pp_ref, ea_ref, eb_ref, ec_ref, sh_ref, tmp_ref):
        i = pl.program_id(0)

        @pl.when(i == 0)
        def _():
            pp_ref[...] = jnp.zeros_like(pp_ref)

        xhat, _ = _rmsnorm_stats(x_ref[...])
        h = (xhat * g_ref[...]).astype(BF16)
        h_ref[...] = h
        pm_ref[...] = _nt(h, wi_ref[...].reshape(di, d)).astype(BF16)

        def prev(j):
            return pp_ref[:, j * g:(j + 1) * g].astype(F32)

        def main(j):
            return pm_ref[:, j * g:(j + 1) * g].astype(F32)

        vec = vec_ref[...]
        ea_ref[0:HALO, :] = prev(0) * _sigmoid(prev(1))
        ea_ref[HALO:, :] = main(0) * _sigmoid(main(1))
        _shifted_copies(ea_ref, sh_ref)
        cw = cw_ref[...]
        c = jnp.zeros((tm, g), F32) + vec[0:1, :]
        for k in range(CONF_KERNEL):
            c = c + cw[k:k + 1, :] * _window(ea_ref, sh_ref, HALO - (CONF_KERNEL - 1) + k, tm)
        c_ref[...] = c
        chat, _ = _layernorm_stats(c)
        ln = chat * vec[1:2, :] + vec[2:3, :]
        mix_ref[:, 0:g] = (ln * _sigmoid(ln)).astype(BF16)
        eb_ref[0:HALO, :] = prev(3) * prev(4)
        eb_ref[HALO:, :] = main(3) * main(4)
        sw = sw_ref[...]
        cz = jnp.zeros((tm, g), F32)
        for k in range(SHORT_KERNEL):
            cz = cz + sw[k:k + 1, :] * eb_ref[pl.ds(HALO - (SHORT_KERNEL - 1) + k, tm), :]
        mix_ref[:, g:2 * g] = (main(2) * cz).astype(BF16)
        ec_ref[0:HALO, :] = prev(5)
        xp = main(5)
        ec_ref[HALO:, :] = xp
        wl = _pool_windows(tm, g)
        mean = _window_sum(ec_ref, tmp_ref, HALO, tm, wl, -1) / _pool_divisor(i * tm, tm, g)
        yc = _nn((mean - xp).astype(BF16), pbd_ref[...])
        mix_ref[:, 2 * g:3 * g] = (yc * vec[3:4, :]).astype(BF16)
        vhat, _ = _layernorm_stats(main(7))
        v = (vhat * vec[4:5, :] + vec[5:6, :]).astype(BF16)
        gu = main(6)
        masks = _head_masks(CHUNK, g)
        for q in range(tm // CHUNK):
            rs = slice(q * CHUNK, (q + 1) * CHUNK)
            mixed = _spatial_mix(v[rs, :], wcat_ref, masks) + bst_ref[...]
            mix_ref[rs, 3 * g:4 * g] = (gu[rs, :] * mixed).astype(BF16)
        y_ref[...] = x_ref[...] + _nn(mix_ref[...], wo_ref[...].reshape(d, d))
        pp_ref[...] = pm_ref[tm - HALO:, :]

    row = lambda w: pl.BlockSpec((tm, w), lambda i: (i, 0))
    return pl.pallas_call(
        body, name=name, grid=(t // tm,),
        out_shape=(jax.ShapeDtypeStruct((t, d), F32), jax.ShapeDtypeStruct((t, 4 * g), BF16),
                   jax.ShapeDtypeStruct((t, d), BF16), jax.ShapeDtypeStruct((t, di), BF16),
                   jax.ShapeDtypeStruct((t, g), F32)),
        in_specs=[row(d), _const((1, d)), _wspec(g, d, lay.win(l)), _wspec(lay.wo, d, lay.wout(l)),
                  _const(conv_w.shape), _const(vecs.shape), _const(sconv_w.shape), _const(pool_bd.shape),
                  _const(wcat.shape), _const(bst.shape)],
        out_specs=(row(d), row(4 * g), row(d), row(di), row(g)),
        scratch_shapes=[pltpu.VMEM((HALO, di), BF16)] + [pltpu.VMEM((HALO + tm, g), F32)] * 3
        + [pltpu.VMEM((7, HALO + tm - 8, g), F32), pltpu.VMEM((2, HALO + tm, g), F32)],
        compiler_params=_params(("arbitrary",)),
    )(x, gain.reshape(1, d), wbuf, wbuf, conv_w, vecs, sconv_w, pool_bd, wcat, bst)


def _mix_bwd(p, dy, conv_out, x, gain, wbuf, lay, l, small, name, riders=()):
    t, d = dy.shape
    g = lay.g
    di = N_DEV * g
    tm = _row_tile(t, 256, CHUNK)
    nt = t // tm
    nh = g // HEAD_DIM
    per = tm // HALO
    ext = tm + HALO
    conv_w, vecs, sconv_w, pool_bd, wcat, bst, wcat_t = small

    def body(pp_ref, pm_ref, pn_ref, dy_ref, dyn_ref, c_ref, cn_ref, x_ref, g_ref, wi_ref, wo_ref, cw_ref, vec_ref,
             sw_ref, pbd_ref, wcat_ref, bst_ref, wcatt_ref, dp_ref, dcw_ref, dvec_ref, dsw_ref, dpbd_ref, dws_ref, dbs_ref,
             dx_ref, dg_ref, ea_ref, eb_ref, ec_ref, fa_ref, fb_ref, fc_ref, sh_ref, sf_ref, tmp_ref):
        i = pl.program_id(0)
        seen = (i > 0).astype(F32)
        more = (i < nt - 1).astype(F32)

        @pl.when(i == 0)
        def _():
            for ref in (dcw_ref, dvec_ref, dsw_ref, dpbd_ref, dws_ref, dbs_ref, dg_ref):
                ref[...] = jnp.zeros_like(ref)

        def prev(j):
            return pp_ref[:, j * g:(j + 1) * g].astype(F32) * seen

        def main(j):
            return pm_ref[:, j * g:(j + 1) * g].astype(F32)

        def nxt(j):
            return pn_ref[:, j * g:(j + 1) * g].astype(F32)

        def colsum(v):
            return jnp.sum(v, axis=0, keepdims=True)

        vec = vec_ref[...]
        wo = wo_ref[...].reshape(d, d)
        dmix = jnp.concatenate([_nt(dy_ref[...].astype(BF16), wo),
                                _nt((dyn_ref[...] * more).astype(BF16), wo)], axis=0)

        val, gate = main(0), main(1)
        sg = _sigmoid(gate)
        ea_ref[0:HALO, :] = prev(0) * _sigmoid(prev(1))
        ea_ref[HALO:, :] = val * sg
        _shifted_copies(ea_ref, sh_ref)
        cw = cw_ref[...]
        c = jnp.concatenate([c_ref[...], cn_ref[...]], axis=0)
        chat, r = _layernorm_stats(c)
        ln = chat * vec[1:2, :] + vec[2:3, :]
        sl = _sigmoid(ln)
        dln = dmix[:, 0:g] * (sl * (1.0 + ln * (1.0 - sl)))
        dvec_ref[1:2, :] += colsum(dln[0:tm] * chat[0:tm])
        dvec_ref[2:3, :] += colsum(dln[0:tm])
        dc = _layernorm_bwd(dln, vec[1:2, :], chat, r)
        dvec_ref[0:1, :] += colsum(dc[0:tm])
        fa_ref[...] = dc
        _shifted_copies(fa_ref, sf_ref)
        dcm = dc[0:tm]
        dy0 = jnp.zeros((tm, g), F32)
        for k in range(CONF_KERNEL):
            dy0 = dy0 + cw[k:k + 1, :] * _window(fa_ref, sf_ref, CONF_KERNEL - 1 - k, tm)
            dcw_ref[k:k + 1, :] += colsum(dcm * _window(ea_ref, sh_ref, HALO - (CONF_KERNEL - 1) + k, tm))
        dp_ref[:, 0:g] = (dy0 * sg).astype(BF16)
        dp_ref[:, g:2 * g] = (dy0 * val * sg * (1.0 - sg)).astype(BF16)

        sb, sc, sx = main(2), main(3), main(4)
        eb_ref[0:HALO, :] = prev(3) * prev(4)
        eb_ref[HALO:ext, :] = sc * sx
        sw = sw_ref[...]
        cz = jnp.zeros((tm, g), F32)
        for k in range(SHORT_KERNEL):
            cz = cz + sw[k:k + 1, :] * eb_ref[pl.ds(HALO - (SHORT_KERNEL - 1) + k, tm), :]
        dob = dmix[:, g:2 * g]
        dp_ref[:, 2 * g:3 * g] = (dob[0:tm] * cz).astype(BF16)
        fb_ref[0:tm, :] = dob[0:tm] * sb
        fb_ref[tm:, :] = dob[tm:] * nxt(2)
        dczm = fb_ref[0:tm, :]
        dz = jnp.zeros((tm, g), F32)
        for k in range(SHORT_KERNEL):
            dz = dz + sw[k:k + 1, :] * fb_ref[pl.ds(SHORT_KERNEL - 1 - k, tm), :]
            dsw_ref[k:k + 1, :] += colsum(dczm * eb_ref[pl.ds(HALO - (SHORT_KERNEL - 1) + k, tm), :])
        dp_ref[:, 3 * g:4 * g] = (dz * sx).astype(BF16)
        dp_ref[:, 4 * g:5 * g] = (dz * sc).astype(BF16)

        xp = main(5)
        ec_ref[0:HALO, :] = prev(5)
        ec_ref[HALO:, :] = xp
        wl = _pool_windows(tm, g)
        dpool = ((_window_sum(ec_ref, tmp_ref, HALO, tm, wl, -1) / _pool_divisor(i * tm, tm, g)) - xp).astype(BF16)
        pbd = pbd_ref[...]
        yc = _nn(dpool, pbd)
        doc = dmix[:, 2 * g:3 * g]
        dvec_ref[3:4, :] += colsum(doc[0:tm] * yc)
        dyc = (doc * vec[3:4, :]).astype(BF16)
        dpbd_ref[...] += _tn(dpool, dyc[0:tm])
        dd = _nt(dyc, pbd)
        fc_ref[...] = dd / _pool_divisor(i * tm, ext, g)
        dp_ref[:, 5 * g:6 * g] = (_window_sum(fc_ref, tmp_ref, 0, tm, wl, 1) - dd[0:tm]).astype(BF16)

        gv = main(7)
        vhat, rv = _layernorm_stats(gv)
        v = (vhat * vec[4:5, :] + vec[5:6, :]).astype(BF16)
        gu = main(6)
        dod = dmix[0:tm, 3 * g:4 * g]
        masks = _head_masks(CHUNK, g)
        tril = (lax.broadcasted_iota(jnp.int32, (CHUNK, CHUNK), 1)
                <= lax.broadcasted_iota(jnp.int32, (CHUNK, CHUNK), 0)).astype(F32)
        tril = jnp.concatenate([tril] * nh, axis=0)
        head_col = lax.broadcasted_iota(jnp.int32, (CHUNK, CHUNK), 1)
        dvs = []
        for q in range(tm // CHUNK):
            rs = slice(q * CHUNK, (q + 1) * CHUNK)
            vq = v[rs, :]
            mixed = _spatial_mix(vq, wcat_ref, masks) + bst_ref[...]
            dp_ref[rs, 6 * g:7 * g] = (dod[rs, :] * mixed).astype(BF16)
            dmixed = dod[rs, :] * gu[rs, :]
            dmb = dmixed.astype(BF16)
            for h, m in enumerate(masks):
                hs = jnp.sum(jnp.where(m, dmixed, 0.0), axis=-1, keepdims=True)
                dbs_ref[...] += jnp.where(head_col == h, hs, 0.0)
            dmstack = jnp.concatenate([jnp.where(m, dmb, jnp.zeros_like(dmb)) for m in masks], axis=0)
            dws_ref[...] += _nt(dmstack, vq) * tril
            back = _nn(wcatt_ref[...], dmb)
            dv = jnp.zeros((CHUNK, g), F32)
            for h, m in enumerate(masks):
                dv = jnp.where(m, back[h * CHUNK:(h + 1) * CHUNK, :], dv)
            dvs.append(dv)
        dv = jnp.concatenate(dvs, axis=0)
        dvec_ref[4:5, :] += colsum(dv * vhat)
        dvec_ref[5:6, :] += colsum(dv)
        dp_ref[:, 7 * g:8 * g] = _layernorm_bwd(dv, vec[4:5, :], vhat, rv).astype(BF16)

        xhat, rx = _rmsnorm_stats(x_ref[...])
        dh = _nn(dp_ref[...], wi_ref[...].reshape(di, d))
        dx_ref[...] = dy_ref[...] + _rmsnorm_bwd(dh, g_ref[...], xhat, rx)
        dg_ref[0:1, :] += colsum(dh * xhat)

    halo_prev = pl.BlockSpec((HALO, di), lambda i: (jnp.maximum(i * per - 1, 0), 0))
    halo_next = lambda w: pl.BlockSpec((HALO, w), lambda i: (jnp.minimum((i + 1) * per, t // HALO - 1), 0))
    row = lambda w: pl.BlockSpec((tm, w), lambda i: (i, 0))
    outs = (jax.ShapeDtypeStruct((t, di), BF16), jax.ShapeDtypeStruct(conv_w.shape, F32),
            jax.ShapeDtypeStruct(vecs.shape, F32), jax.ShapeDtypeStruct(sconv_w.shape, F32),
            jax.ShapeDtypeStruct((g, g), F32), jax.ShapeDtypeStruct((nh * CHUNK, CHUNK), F32),
            jax.ShapeDtypeStruct((CHUNK, CHUNK), F32), jax.ShapeDtypeStruct((t, d), F32),
            jax.ShapeDtypeStruct((8, d), F32))
    return _call(
        body, name, nt, [p, p, p, dy, dy, conv_out, conv_out, x, gain.reshape(1, d), wbuf, wbuf, conv_w, vecs,
                         sconv_w, pool_bd, wcat, bst, wcat_t],
        [halo_prev, row(di), halo_next(di), row(d), halo_next(d), row(g), halo_next(g), row(d), _const((1, d)),
         _wspec(g, d, lay.win(l)), _wspec(lay.wo, d, lay.wout(l)),
         _const(conv_w.shape), _const(vecs.shape), _const(sconv_w.shape), _const(pool_bd.shape),
         _const(wcat.shape), _const(bst.shape), _const(wcat_t.shape)],
        list(outs), [row(di)] + [_const(o.shape) for o in outs[1:7]] + [row(d), _const((8, d))],
        scratch=[pltpu.VMEM((HALO + tm, g), F32), pltpu.VMEM((HALO + tm, g), F32),
                 pltpu.VMEM((HALO + tm, g), F32), pltpu.VMEM((ext, g), F32), pltpu.VMEM((ext, g), F32),
                 pltpu.VMEM((ext, g), F32), pltpu.VMEM((7, HALO + tm - 8, g), F32),
                 pltpu.VMEM((7, ext - 8, g), F32), pltpu.VMEM((2, ext, g), F32)],
        riders=riders)


def _loss_head(x, target, gain, name):
    t, d = x.shape
    tm = _row_tile(t, 512, 8)

    def body(x_ref, t_ref, g_ref, dx_ref, loss_ref, dg_ref):
        @pl.when(pl.program_id(0) == 0)
        def _():
            loss_ref[...] = jnp.zeros_like(loss_ref)
            dg_ref[...] = jnp.zeros_like(dg_ref)

        g = g_ref[...]
        xhat, r = _rmsnorm_stats(x_ref[...])
        err = xhat * g - t_ref[...]
        loss_ref[...] += 0.5 * jnp.sum(jnp.mean(err * err, axis=-1, keepdims=True))
        dyv = err * (1.0 / d)
        dx_ref[...] = _rmsnorm_bwd(dyv, g, xhat, r)
        dg_ref[0:1, :] += jnp.sum(dyv * xhat, axis=0, keepdims=True)

    row = pl.BlockSpec((tm, d), lambda i: (i, 0))
    return pl.pallas_call(
        body, name=name, grid=(t // tm,),
        out_shape=(jax.ShapeDtypeStruct((t, d), F32), jax.ShapeDtypeStruct((8, 128), F32),
                   jax.ShapeDtypeStruct((8, d), F32)),
        in_specs=[row, row, _const((1, d))],
        out_specs=(row, _const((8, 128)), _const((8, d))),
        compiler_params=_params(("arbitrary",)),
    )(x, target, gain.reshape(1, d))


def _adam_step(w, gv, m, v):
    nm = ADAM_B1 * m + (1.0 - ADAM_B1) * gv
    nv = ADAM_B2 * v + (1.0 - ADAM_B2) * (gv * gv)
    m_hat = nm / (1.0 - ADAM_B1 ** ADAM_STEP)
    v_hat = nv / (1.0 - ADAM_B2 ** ADAM_STEP)
    return -ADAM_LR * (m_hat / (jnp.sqrt(v_hat) + ADAM_EPS) + ADAM_WD * w), nm, nv


def _reduce_adamw(slots, blks, transposed, w, m, v, name):
    depth, rows, cols = w.shape
    tr = _row_tile(rows, 512, 128) if transposed else _row_tile(rows, 256, 16)
    nt = rows // tr
    pad = -cols % 128

    def body(*refs):
        slot_refs = refs[:depth]
        w_ref, m_ref, v_ref, g_ref, d_ref, nm_ref, nv_ref = refs[depth:]
        layer = pl.program_id(0) // nt
        for l in range(depth):
            @pl.when(layer == l)
            def _(s_ref=slot_refs[l]):
                gv = s_ref[0].astype(F32)
                for j in range(1, N_DEV):
                    gv = gv + s_ref[j].astype(F32)
                if transposed:
                    if pad:
                        gv = jnp.concatenate([gv, jnp.zeros((pad, tr), F32)], axis=0)
                    gv = gv.T[:, :cols]
                g_ref[0] = gv
                d_ref[0], nm_ref[0], nv_ref[0] = _adam_step(w_ref[0], gv, m_ref[0], v_ref[0])

    def slot_spec(l):
        tile = lambda i: jnp.clip(i - l * nt, 0, nt - 1)
        if transposed:
            return pl.BlockSpec((N_DEV, cols, tr), lambda i: (0, blks[l], tile(i)))
        return pl.BlockSpec((N_DEV, tr, cols), lambda i: (0, blks[l] * nt + tile(i), 0))

    spec = pl.BlockSpec((1, tr, cols), lambda i: (i // nt, i % nt, 0))
    shape = jax.ShapeDtypeStruct(w.shape, F32)
    return pl.pallas_call(
        body, name=name, grid=(depth * nt,),
        out_shape=(shape,) * 4, in_specs=[slot_spec(l) for l in range(depth)] + [spec] * 3, out_specs=(spec,) * 4,
        compiler_params=_params(("arbitrary",)),
    )(*slots, w, m, v)


def _adamw(w, grad, m, v, name):
    shape = w.shape
    cols = shape[-1]
    rows = w.size // cols
    tr = _row_tile(rows, 512, 8) if rows > 1024 else rows

    def body(w_ref, g_ref, m_ref, v_ref, d_ref, nm_ref, nv_ref):
        d_ref[...], nm_ref[...], nv_ref[...] = _adam_step(w_ref[...], g_ref[...], m_ref[...], v_ref[...])

    spec = pl.BlockSpec((tr, cols), lambda i: (i, 0))
    flat = jax.ShapeDtypeStruct((rows, cols), F32)
    outs = pl.pallas_call(
        body, name=name, grid=(rows // tr,),
        out_shape=(flat, flat, flat), in_specs=[spec] * 4, out_specs=(spec,) * 3,
        compiler_params=_params(("parallel",)),
    )(*(a.reshape(rows, cols) for a in (w, grad, m, v)))
    return tuple(o.reshape(shape) for o in outs)


def _f32_rows(parts, d, rows):
    out, offs, at = [], [], 0
    for a in parts:
        n = -(-a.size // d)
        out.append(jnp.pad(a.reshape(-1), (0, n * d - a.size)).reshape(n, d))
        offs.append(at)
        at += n
    total = at if rows is None else rows
    assert at <= total
    if total > at:
        out.append(jnp.zeros((total - at, d), F32))
    return jnp.concatenate(out, axis=0), offs


def _take(buf, off, shape):
    n = 1
    for s in shape:
        n *= s
    d = buf.shape[1]
    return buf[off:off + -(-n // d)].reshape(-1)[:n].reshape(shape)


def _block_diag(blocks):
    n, k, _ = blocks.shape
    out = jnp.zeros((n * k, n * k), blocks.dtype)
    for i in range(n):
        out = lax.dynamic_update_slice(out, blocks[i], (i * k, i * k))
    return out


def kernel(x, ffn1_norm, ffn1_w1, ffn1_w3, ffn1_w2, mix_norm, w_in, conf_conv_w, conf_conv_b, conf_ln_g, conf_ln_b, sconv_w, pool_w, pool_scale, gmlp_ln_g, gmlp_ln_b, gmlp_w_s, gmlp_b_s, w_out, ffn2_norm, ffn2_w1, ffn2_w3, ffn2_w2, final_norm, loss_target, m_ffn1_norm, m_ffn1_w1, m_ffn1_w3, m_ffn1_w2, m_mix_norm, m_w_in, m_conf_conv_w, m_conf_conv_b, m_conf_ln_g, m_conf_ln_b, m_sconv_w, m_pool_w, m_pool_scale, m_gmlp_ln_g, m_gmlp_ln_b, m_gmlp_w_s, m_gmlp_b_s, m_w_out, m_ffn2_norm, m_ffn2_w1, m_ffn2_w3, m_ffn2_w2, m_final_norm, v_ffn1_norm, v_ffn1_w1, v_ffn1_w3, v_ffn1_w2, v_mix_norm, v_w_in, v_conf_conv_w, v_conf_conv_b, v_conf_ln_g, v_conf_ln_b, v_sconv_w, v_pool_w, v_pool_scale, v_gmlp_ln_g, v_gmlp_ln_b, v_gmlp_w_s, v_gmlp_b_s, v_w_out, v_ffn2_norm, v_ffn2_w1, v_ffn2_w3, v_ffn2_w2, v_final_norm):
    w = dict(zip(WEIGHTS, (ffn1_norm, ffn1_w1, ffn1_w3, ffn1_w2, mix_norm, w_in, conf_conv_w, conf_conv_b, conf_ln_g,
                           conf_ln_b, sconv_w, pool_w, pool_scale, gmlp_ln_g, gmlp_ln_b, gmlp_w_s, gmlp_b_s, w_out,
                           ffn2_norm, ffn2_w1, ffn2_w3, ffn2_w2, final_norm)))
    mom1 = dict(zip(WEIGHTS, (m_ffn1_norm, m_ffn1_w1, m_ffn1_w3, m_ffn1_w2, m_mix_norm, m_w_in, m_conf_conv_w,
                              m_conf_conv_b, m_conf_ln_g, m_conf_ln_b, m_sconv_w, m_pool_w, m_pool_scale, m_gmlp_ln_g,
                              m_gmlp_ln_b, m_gmlp_w_s, m_gmlp_b_s, m_w_out, m_ffn2_norm, m_ffn2_w1, m_ffn2_w3,
                              m_ffn2_w2, m_final_norm)))
    mom2 = dict(zip(WEIGHTS, (v_ffn1_norm, v_ffn1_w1, v_ffn1_w3, v_ffn1_w2, v_mix_norm, v_w_in, v_conf_conv_w,
                              v_conf_conv_b, v_conf_ln_g, v_conf_ln_b, v_sconv_w, v_pool_w, v_pool_scale, v_gmlp_ln_g,
                              v_gmlp_ln_b, v_gmlp_w_s, v_gmlp_b_s, v_w_out, v_ffn2_norm, v_ffn2_w1, v_ffn2_w3,
                              v_ffn2_w2, v_final_norm)))
    _, t, d = x.shape
    depth, _, fs = ffn1_w1.shape
    g = w_in.shape[2]
    wo = w_out.shape[1]
    nh = g // HEAD_DIM
    cs = conf_conv_w.shape[2]
    lay = _MixRows(g, wo)
    me = 4 * lax.axis_index("x") + 2 * lax.axis_index("y") + lax.axis_index("c")

    def rows_bf16(mats):
        return jnp.concatenate([m_.astype(BF16) for m_ in mats], axis=0)

    swapped = tuple(n for n in SHARDED_COLS if w[n].shape[2] % 128)
    rows_of = {n: (jnp.swapaxes(w[n], 1, 2) if n in swapped else _hidden_major(w[n], f"rows_{n}"))
               for n in SHARDED_COLS}
    set_f1 = [rows_bf16([rows_of['ffn1_w1'][l], rows_of['ffn1_w3'][l], ffn1_w2[l]]) for l in range(depth)]
    set_f2 = [rows_bf16([rows_of['ffn2_w1'][l], rows_of['ffn2_w3'][l], ffn2_w2[l]]) for l in range(depth)]
    set_mx = [rows_bf16([rows_of['w_in'][l], w_out[l]]) for l in range(depth)]
    convs = jnp.concatenate([conf_conv_w, sconv_w], axis=1)
    nconv = convs.size
    conv_bits = jnp.pad(convs.reshape(-1), (0, CONV_ROWS * d // 2 - nconv))
    conv_rows = lax.bitcast_convert_type(conv_bits, BF16).reshape(CONV_ROWS, d)
    w_f1 = [None] * depth
    w_f1[0] = _alone(_Gather(jnp.concatenate([set_f1[0], conv_rows], axis=0)), "gather_first")

    conv_all = lax.bitcast_convert_type(w_f1[0][:, 3 * fs:, :].reshape(N_DEV, CONV_ROWS * d // 2, 2), F32)
    conv_all = conv_all[:, :nconv].reshape(N_DEV, depth, CONF_KERNEL + SHORT_KERNEL, cs)
    conv_all = conv_all.transpose(1, 2, 0, 3).reshape(depth, CONF_KERNEL + SHORT_KERNEL, g)

    def small_inputs(l):
        cw = jnp.pad(conv_all[l, :CONF_KERNEL], ((0, 32 - CONF_KERNEL), (0, 0)))
        sw = jnp.pad(conv_all[l, CONF_KERNEL:], ((0, 8 - SHORT_KERNEL), (0, 0)))
        vecs = jnp.stack([conf_conv_b[l], conf_ln_g[l], conf_ln_b[l], pool_scale[l], gmlp_ln_g[l], gmlp_ln_b[l],
                          jnp.zeros((g,), F32), jnp.zeros((g,), F32)])
        pool_bd = _block_diag(pool_w[l]).astype(BF16)
        low = jnp.tril(jnp.ones((CHUNK, CHUNK), bool))
        ws = jnp.where(low[None], gmlp_w_s[l], 0.0).astype(BF16)
        wcat = ws.transpose(1, 0, 2).reshape(CHUNK, nh * CHUNK)
        wcat_t = ws.transpose(0, 2, 1).reshape(nh * CHUNK, CHUNK)
        bst = jnp.repeat(gmlp_b_s[l].T, HEAD_DIM, axis=1)
        return cw, vecs, sw, pool_bd, wcat, bst, wcat_t

    xs = x[0]
    saved = []
    for l in range(depth):
        sm = small_inputs(l)
        (x1, a1, b1), ((w_f2,), (w_mx,)) = _ffn_fwd(xs, ffn1_norm[l], w_f1[l], fs, f"ffn1_fwd_{l}",
                                                    riders=[_Gather(set_f2[l]), _Gather(set_mx[l])])
        x2, mix, h2, p, conv_out = _mix_fwd(x1, mix_norm[l], w_mx, lay, l, sm[:6], f"mix_fwd_{l}")
        nxt = [_Gather(set_f1[l + 1])] if l + 1 < depth else []
        (x3, a2, b2), got = _ffn_fwd(x2, ffn2_norm[l], w_f2, fs, f"ffn2_fwd_{l}", riders=nxt)
        if nxt:
            w_f1[l + 1] = got[0][0]
        saved.append((xs, a1, b1, x1, h2, p, mix, x2, a2, b2, sm, w_f2, w_mx, conv_out))
        xs = x3

    dx, loss_part, dgf = _loss_head(xs, loss_target[0], final_norm, "loss_head")
    loss = lax.psum(loss_part[0, 0], ("x", "y", "c"))

    def small_rows(parts):
        buf, at = _f32_rows(parts, d, None)
        return jnp.pad(buf, ((0, -buf.shape[0] % 16), (0, 0))), at

    small = [None] * depth
    big = ('ffn1_w1', 'ffn1_w3', 'ffn1_w2', 'ffn2_w1', 'ffn2_w3', 'ffn2_w2', 'w_in', 'w_out')
    slots = {n: [None] * depth for n in big}
    blk_of = {'ffn1_w1': 0, 'ffn1_w3': 1, 'ffn1_w2': 2, 'ffn2_w1': 0, 'ffn2_w3': 1, 'ffn2_w2': 2,
              'w_in': lay.win(0), 'w_out': lay.wout(0)}
    blks = {n: [blk_of[n]] * depth for n in big}

    def landed(names, l, got_):
        for n in names:
            slots[n][l] = got_

    def ffn_grads(da, db, u, h, dy, tag):
        gset = lax.empty((N_DEV, 3 * fs, d), BF16)
        gset = _wgrad(da, h, gset, fs, 0, 1.0, f"{tag}_dw1")
        gset = _wgrad(db, h, gset, fs, 1, 1.0, f"{tag}_dw3")
        return _wgrad(u, dy, gset, fs, 2, FFN_RESIDUAL, f"{tag}_dw2")

    pending = None
    early_all = None
    for l in reversed(range(depth)):
        x0, a1, b1, x1, h2, p, mix, x2, a2, b2, sm, w_f2, w_mx, conv_out = saved[l]
        ride = [_Exchange(pending)] if pending is not None else []
        (dx2, da, db, u, h, dgn2), got = _ffn_bwd(x2, dx, a2, b2, ffn2_norm[l], w_f2, fs, f"ffn2_bwd_{l}", riders=ride)
        if ride:
            landed(big[0:3], l + 1, got[0][0])
        gset = lax.empty((N_DEV, 3 * fs, d), BF16)
        if l == 0 and depth > 1:
            early, early_offs = small_rows([a for k in range(1, depth) for a in small[k]] + [dgf[0]])
            gset, ((early_all,),) = _wgrad(da, h, gset, fs, 0, 1.0, f"ffn2_{l}_dw1", riders=[_Gather(early)])
        else:
            gset = _wgrad(da, h, gset, fs, 0, 1.0, f"ffn2_{l}_dw1")
        gset = _wgrad(db, h, gset, fs, 1, 1.0, f"ffn2_{l}_dw3")
        gset = _wgrad(u, dx, gset, fs, 2, FFN_RESIDUAL, f"ffn2_{l}_dw2")
        (dp, dcw, dvec, dsw, dpbd, dws, dbs, dx1, dgm), got = _mix_bwd(
            p, dx2, conv_out, x1, mix_norm[l], w_mx, lay, l, sm, f"mix_bwd_{l}", riders=[_Exchange(gset)])
        landed(big[3:6], l, got[0][0])
        if l > 0:
            gmx = lax.empty((N_DEV, g + wo, d), BF16)
            gmx = _wgrad(mix, dx2, gmx, wo, lay.wout(l), 1.0, f"dw_out_{l}")
            gmx = _wgrad(dp, h2, gmx, g, lay.win(l), 1.0, f"dw_in_{l}")
            ride = [_Exchange(gmx)]
        else:
            g_wo = _wgrad(mix, dx2, lax.empty((N_DEV, wo, d), BF16), wo, 0, 1.0, f"dw_out_{l}")
            g_wi, ((slots['w_out'][0],),) = _wgrad(dp, h2, lax.empty((N_DEV, g, d), BF16), g, 0, 1.0, f"dw_in_{l}",
                                                   riders=[_Exchange(g_wo)])
            blks['w_out'][0] = blks['w_in'][0] = 0
            ride = []
        (dx, da, db, u, h, dgn1), got = _ffn_bwd(x0, dx1, a1, b1, ffn1_norm[l], w_f1[l], fs,
                                                 f"ffn1_bwd_{l}", riders=ride)
        if l > 0:
            landed(big[6:8], l, got[0][0])
        pg = g // len(POOL_WINDOWS)
        dpool = jnp.stack([dpbd[k * pg:(k + 1) * pg, k * pg:(k + 1) * pg] for k in range(len(POOL_WINDOWS))])
        small[l] = [dgn1[0], dgm[0], dgn2[0], dvec, dcw, dsw, dpool, dws, dbs[:, :nh].T]
        if l > 0:
            pending = ffn_grads(da, db, u, h, dx1, f"ffn1_{l}")
    grad_x = dx[None]
    late, late_offs = small_rows(small[0] if depth > 1 else small[0] + [dgf[0]])
    one = lambda: lax.empty((N_DEV, fs, d), BF16)
    g_a, ((slots['w_in'][0],),) = _wgrad(da, h, one(), fs, 0, 1.0, "ffn1_0_dw1", riders=[_Exchange(g_wi)])
    g_b, ((slots['ffn1_w1'][0],),) = _wgrad(db, h, one(), fs, 0, 1.0, "ffn1_0_dw3", riders=[_Exchange(g_a)])
    g_c, ((slots['ffn1_w3'][0],),) = _wgrad(u, dx1, one(), fs, 0, FFN_RESIDUAL, "ffn1_0_dw2",
                                            riders=[_Exchange(g_b)])
    (slots['ffn1_w2'][0],), (late_all,) = _call(lambda: None, "exchange_last", 1, [], [], [], [],
                                                riders=[_Exchange(g_c), _Gather(late)])[1]
    for n in big[0:3]:
        blks[n][0] = 0

    grads, deltas, new_m, new_v = {}, {}, {}, {}
    for n in big:
        if n in swapped:
            outs = _reduce_adamw(slots[n], blks[n], False, *(jnp.swapaxes(a_, 1, 2) for a_ in (w[n], mom1[n], mom2[n])),
                                 f"reduce_adamw_{n}")
            grads[n], deltas[n], new_m[n], new_v[n] = (jnp.swapaxes(o_, 1, 2) for o_ in outs)
        else:
            grads[n], deltas[n], new_m[n], new_v[n] = _reduce_adamw(
                slots[n], blks[n], n in SHARDED_COLS, w[n], mom1[n], mom2[n], f"reduce_adamw_{n}")

    per_layer = len(small[0])
    shapes = [(d,), (d,), (d,), (8, g), (32, g), (8, g), (len(POOL_WINDOWS), g // len(POOL_WINDOWS),
              g // len(POOL_WINDOWS)), (nh * CHUNK, CHUNK), (nh, CHUNK)]
    late_sum = _sum_slots(late_all, "sum_small_late")
    early_sum = _sum_slots(early_all, "sum_small_early") if depth > 1 else None
    got = [[_take(late_sum, late_offs[k], shapes[k]) for k in range(per_layer)]]
    got += [[_take(early_sum, early_offs[(l - 1) * per_layer + k], shapes[k]) for k in range(per_layer)]
            for l in range(1, depth)]
    col = lambda k: jnp.stack([got[l][k] for l in range(depth)])
    grads['ffn1_norm'], grads['mix_norm'], grads['ffn2_norm'] = col(0), col(1), col(2)
    dvec_all = col(3)
    for k, n in enumerate(('conf_conv_b', 'conf_ln_g', 'conf_ln_b', 'pool_scale', 'gmlp_ln_g', 'gmlp_ln_b')):
        grads[n] = dvec_all[:, k]
    grads['conf_conv_w'] = lax.dynamic_slice_in_dim(col(4)[:, :CONF_KERNEL], me * cs, cs, axis=2)
    grads['sconv_w'] = lax.dynamic_slice_in_dim(col(5)[:, :SHORT_KERNEL], me * cs, cs, axis=2)
    grads['pool_w'] = col(6)
    grads['gmlp_w_s'] = col(7).reshape(depth, nh, CHUNK, CHUNK)
    grads['gmlp_b_s'] = col(8)
    grads['final_norm'] = (_take(early_sum, early_offs[-1], (d,)) if depth > 1
                           else _take(late_sum, late_offs[-1], (d,)))

    for n in WEIGHTS:
        if n not in big:
            deltas[n], new_m[n], new_v[n] = _adamw(w[n], grads[n], mom1[n], mom2[n], f"adamw_{n}")

    return (loss, grad_x, *[grads[n] for n in WEIGHTS], *[deltas[n] for n in WEIGHTS],
            *[new_m[n] for n in WEIGHTS], *[new_v[n] for n in WEIGHTS])
```

```python
import functools

import jax
import jax.numpy as jnp
from jax import lax
from jax.experimental import pallas as pl
from jax.experimental.pallas import tpu as pltpu

F32 = jnp.float32
BF16 = jnp.bfloat16
MESH = pl.DeviceIdType.MESH

N_DEV = 8
EPS = 1e-6
FFN_RESIDUAL = 0.5
CONF_KERNEL = 31
SHORT_KERNEL = 3
POOL_WINDOWS = (2, 4, 8, 16)
CHUNK = 128
HEAD_DIM = 64
N_IN_PIECES = 8
HALO = 32
CONV_ROWS = 16
ADAM_LR = 0.001
ADAM_B1 = 0.9
ADAM_B2 = 0.999
ADAM_EPS = 1e-08
ADAM_WD = 0.01
ADAM_STEP = 10
VMEM_LIMIT_V7X = 56 * 1024 * 1024

WEIGHTS = ['ffn1_norm', 'ffn1_w1', 'ffn1_w3', 'ffn1_w2', 'mix_norm', 'w_in', 'conf_conv_w', 'conf_conv_b',
           'conf_ln_g', 'conf_ln_b', 'sconv_w', 'pool_w', 'pool_scale', 'gmlp_ln_g', 'gmlp_ln_b', 'gmlp_w_s',
           'gmlp_b_s', 'w_out', 'ffn2_norm', 'ffn2_w1', 'ffn2_w3', 'ffn2_w2', 'final_norm']
SHARDED_ROWS = ('ffn1_w2', 'ffn2_w2', 'w_out')
SHARDED_COLS = ('ffn1_w1', 'ffn1_w3', 'ffn2_w1', 'ffn2_w3', 'w_in')
SHARDED_CHAN = ('conf_conv_w', 'sconv_w')


def _params(sem=None):
    return pltpu.CompilerParams(dimension_semantics=sem, vmem_limit_bytes=VMEM_LIMIT_V7X)


def _nn(a, b):
    return jnp.dot(a, b, preferred_element_type=F32)


def _nt(a, b):
    return lax.dot_general(a, b, (((1,), (1,)), ((), ())), preferred_element_type=F32)


def _tn(a, b):
    return lax.dot_general(a, b, (((0,), (0,)), ((), ())), preferred_element_type=F32)


def _sigmoid(x):
    return 0.5 * jnp.tanh(0.5 * x) + 0.5


def _row_tile(n, pref, mult=8):
    t = min(n, pref)
    while n % t or t % mult:
        t -= 1
    return t


def _chunks(n, size):
    out, s = [], 0
    while s < n:
        out.append((s, min(size, n - s)))
        s += size
    return out


def _const(shape):
    return pl.BlockSpec(shape, lambda i: (0,) * len(shape))


def _resident(shape, index):
    return pl.BlockSpec(shape, lambda i: index, pipeline_mode=pl.Buffered(1))


def _wspec(rows, d, blk):
    return _resident((N_DEV, rows, d), (0, blk, 0))


def _layernorm_stats(x):
    mu = jnp.mean(x, axis=-1, keepdims=True)
    xc = x - mu
    r = lax.rsqrt(jnp.mean(xc * xc, axis=-1, keepdims=True) + EPS)
    return xc * r, r


def _layernorm_bwd(dy, g, xhat, r):
    dxh = dy * g
    return r * (dxh - jnp.mean(dxh, axis=-1, keepdims=True) - xhat * jnp.mean(dxh * xhat, axis=-1, keepdims=True))


def _rmsnorm_stats(x):
    r = lax.rsqrt(jnp.mean(x * x, axis=-1, keepdims=True) + EPS)
    return x * r, r


def _rmsnorm_bwd(dh, g, xhat, r):
    dxh = dh * g
    return r * (dxh - xhat * jnp.mean(dxh * xhat, axis=-1, keepdims=True))


class _MixRows:
    def __init__(self, g, wo):
        assert g % wo == 0 and wo % 16 == 0
        self.g, self.wo = g, wo

    def win(self, l):
        return 0

    def wout(self, l):
        return self.g // self.wo


class _Gather:
    def __init__(self, shard):
        self.operands = [shard]
        self.out_shape = [jax.ShapeDtypeStruct((N_DEV,) + shard.shape, shard.dtype)]
        self.scratch = [pltpu.SemaphoreType.DMA((7,)), pltpu.SemaphoreType.DMA((7,)), pltpu.SemaphoreType.DMA]

    def _copies(self, ins, outs, scr, arrivals):
        (x_ref,), (out_ref,), (send_sems, recv_sems, local_sem) = ins, outs, scr
        x, y, c = lax.axis_index("x"), lax.axis_index("y"), lax.axis_index("c")
        me, sibling = (x, y, c), (x, y, 1 - c)
        chips = [(1 - x, y), (x, 1 - y), (1 - x, 1 - y)]

        def slot(px, py, pc):
            return out_ref.at[4 * px + 2 * py + pc]

        def copy(k, block, to, src=None):
            return pltpu.make_async_remote_copy(
                src_ref=slot(*block) if src is None else src, dst_ref=slot(*block),
                send_sem=send_sems.at[k], recv_sem=recv_sems.at[k], device_id=to, device_id_type=MESH)

        mine = pltpu.make_async_copy(x_ref, slot(*me), local_sem)
        first = [copy(0, me, sibling, src=x_ref)]
        first += [copy(1 + j, me, (*chip, c), src=x_ref) for j, chip in enumerate(chips)]
        passed = [copy(4 + j, (*chip, c), sibling) for j, chip in enumerate(chips)]
        if arrivals == "ici":
            landing = [copy(1 + j, (*chip, c), me) for j, chip in enumerate(chips)]
        else:
            landing = [copy(0, sibling, me)] + [copy(4 + j, (*chip, 1 - c), me) for j, chip in enumerate(chips)]
        return mine, first, passed, landing

    def begin(self, i, n, ins, outs, scr):
        mine, first, passed, over_ici = self._copies(ins, outs, scr, "ici")

        @pl.when(i == 0)
        def _():
            mine.start()
            for cp in first:
                cp.start()

        @pl.when(i == (n - 3 if n >= 6 else n // 2))
        def _():
            for landed, onward in zip(over_ici, passed):
                landed.wait_recv()
                onward.start()

    def end(self, i, n, ins, outs, scr):
        mine, first, passed, over_d2d = self._copies(ins, outs, scr, "d2d")

        @pl.when(i == n - 1)
        def _():
            for cp in over_d2d:
                cp.wait_recv()
            for cp in first + passed:
                cp.wait_send()
            mine.wait()


class _Exchange:
    def __init__(self, blocks):
        self.operands = [blocks]
        self.out_shape = [jax.ShapeDtypeStruct(blocks.shape, blocks.dtype)]
        self.scratch = [pltpu.SemaphoreType.DMA((7,)), pltpu.SemaphoreType.DMA((7,)), pltpu.SemaphoreType.DMA]

    def _copies(self, ins, outs, scr, with_arrivals):
        (s_ref,), (r_ref,), (send_sems, recv_sems, local_sem) = ins, outs, scr
        x, y, c = lax.axis_index("x"), lax.axis_index("y"), lax.axis_index("c")
        me = 4 * x + 2 * y + c
        mine = pltpu.make_async_copy(s_ref.at[me], r_ref.at[me], local_sem)
        sends, recvs = [], []
        for k in range(1, N_DEV):
            px = (1 - x) if (k >> 2) & 1 else x
            py = (1 - y) if (k >> 1) & 1 else y
            pc = (1 - c) if k & 1 else c
            peer = 4 * px + 2 * py + pc
            for dst, group in ((r_ref.at[me], sends), (r_ref.at[peer], recvs)):
                if group is sends or with_arrivals:
                    group.append(pltpu.make_async_remote_copy(
                        src_ref=s_ref.at[peer], dst_ref=dst, send_sem=send_sems.at[k - 1],
                        recv_sem=recv_sems.at[k - 1], device_id=(px, py, pc), device_id_type=MESH))
        return mine, sends, recvs

    def begin(self, i, n, ins, outs, scr):
        mine, sends, _ = self._copies(ins, outs, scr, False)

        @pl.when(i == 0)
        def _():
            mine.start()
            for cp in sends:
                cp.start()

    def end(self, i, n, ins, outs, scr):
        mine, sends, recvs = self._copies(ins, outs, scr, True)

        @pl.when(i == n - 1)
        def _():
            for cp in recvs:
                cp.wait_recv()
            for cp in sends:
                cp.wait_send()
            mine.wait()


_ANY = pl.BlockSpec(memory_space=pl.ANY)


def _call(body, name, steps, operands, in_specs, out_shape, out_specs, scratch=(), riders=(), aliases=None):
    n_in, n_out, n_scr = len(operands), len(out_shape), len(scratch)

    def full(*refs):
        pos = [0]

        def take(k):
            pos[0] += k
            return refs[pos[0] - k:pos[0]]

        h_in, r_in = take(n_in), [take(len(r.operands)) for r in riders]
        h_out, r_out = take(n_out), [take(len(r.out_shape)) for r in riders]
        h_scr, r_scr = take(n_scr), [take(len(r.scratch)) for r in riders]
        i = pl.program_id(0)
        for r, a, b, c in zip(riders, r_in, r_out, r_scr):
            r.begin(i, steps, a, b, c)
        body(*h_in, *h_out, *h_scr)
        for r, a, b, c in zip(riders, r_in, r_out, r_scr):
            r.end(i, steps, a, b, c)

    outs = pl.pallas_call(
        full, name=name, grid=(steps,),
        out_shape=tuple(out_shape) + tuple(s for r in riders for s in r.out_shape),
        in_specs=list(in_specs) + [_ANY for r in riders for _ in r.operands],
        out_specs=tuple(out_specs) + tuple(_ANY for r in riders for _ in r.out_shape),
        scratch_shapes=list(scratch) + [s for r in riders for s in r.scratch],
        input_output_aliases=aliases or {},
        compiler_params=_params(("arbitrary",)),
    )(*operands, *(o for r in riders for o in r.operands))
    host, rest = outs[:n_out], list(outs[n_out:])
    rides = []
    for r in riders:
        rides.append(rest[:len(r.out_shape)])
        rest = rest[len(r.out_shape):]
    return host, rides


def _alone(rider, name):
    return _call(lambda: None, name, 1, [], [], [], [], riders=[rider])[1][0][0]


def _sum_slots(slots, name):
    _, rows, cols = slots.shape
    tr = _row_tile(rows, 256, 16)

    def body(s_ref, o_ref):
        acc = s_ref[0].astype(F32)
        for j in range(1, N_DEV):
            acc = acc + s_ref[j].astype(F32)
        o_ref[...] = acc

    return pl.pallas_call(
        body, name=name, grid=(rows // tr,),
        out_shape=jax.ShapeDtypeStruct((rows, cols), F32),
        in_specs=[pl.BlockSpec((N_DEV, tr, cols), lambda i: (0, i, 0))],
        out_specs=pl.BlockSpec((tr, cols), lambda i: (i, 0)),
        compiler_params=_params(("parallel",)),
    )(slots)


def _ffn_fwd(x, gain, wset, fs, name, riders=()):
    t, d = x.shape
    f = N_DEV * fs
    tm = _row_tile(t, 512, 128)
    cols = _chunks(f, 1024)

    def body(x_ref, g_ref, w1_ref, w3_ref, w2_ref, y_ref, a_ref, b_ref, u_ref):
        xv = x_ref[...]
        xhat, _ = _rmsnorm_stats(xv)
        h = (xhat * g_ref[...]).astype(BF16)
        w1 = w1_ref[...].reshape(f, d)
        w3 = w3_ref[...].reshape(f, d)
        for s, n in cols:
            a = _nt(h, w1[s:s + n, :])
            b = _nt(h, w3[s:s + n, :])
            a_ref[:, s:s + n] = a.astype(BF16)
            b_ref[:, s:s + n] = b.astype(BF16)
            u_ref[:, s:s + n] = (a * _sigmoid(a) * b).astype(BF16)
        y_ref[...] = xv + FFN_RESIDUAL * _nn(u_ref[...], w2_ref[...].reshape(f, d))

    return _call(
        body, name, t // tm, [x, gain.reshape(1, d), wset, wset, wset],
        [pl.BlockSpec((tm, d), lambda i: (i, 0)), _const((1, d)),
         _wspec(fs, d, 0), _wspec(fs, d, 1), _wspec(fs, d, 2)],
        [jax.ShapeDtypeStruct((t, d), F32), jax.ShapeDtypeStruct((t, f), BF16), jax.ShapeDtypeStruct((t, f), BF16)],
        [pl.BlockSpec((tm, d), lambda i: (i, 0)), pl.BlockSpec((tm, f), lambda i: (i, 0)),
         pl.BlockSpec((tm, f), lambda i: (i, 0))],
        scratch=[pltpu.VMEM((tm, f), BF16)], riders=riders)


def _ffn_bwd(x, dy, a, b, gain, wset, fs, name, riders=()):
    t, d = x.shape
    f = N_DEV * fs
    tm = _row_tile(t, 256, 128)
    cols = _chunks(f, 1024)

    def body(x_ref, dy_ref, a_ref, b_ref, g_ref, w1_ref, w3_ref, w2_ref,
             dx_ref, da_ref, db_ref, u_ref, h_ref, dg_ref):
        @pl.when(pl.program_id(0) == 0)
        def _():
            dg_ref[...] = jnp.zeros_like(dg_ref)

        xv, dyv, g = x_ref[...], dy_ref[...], g_ref[...]
        xhat, r = _rmsnorm_stats(xv)
        h_ref[...] = (xhat * g).astype(BF16)
        dyb = (FFN_RESIDUAL * dyv).astype(BF16)
        w1 = w1_ref[...].reshape(f, d)
        w3 = w3_ref[...].reshape(f, d)
        w2 = w2_ref[...].reshape(f, d)
        dh = jnp.zeros((tm, d), F32)
        for s, n in cols:
            du = _nt(dyb, w2[s:s + n, :])
            av = a_ref[:, s:s + n].astype(F32)
            bv = b_ref[:, s:s + n].astype(F32)
            sig = _sigmoid(av)
            sa = av * sig
            u_ref[:, s:s + n] = (sa * bv).astype(BF16)
            da = (du * bv * (sig + sa * (1.0 - sig))).astype(BF16)
            db = (du * sa).astype(BF16)
            da_ref[:, s:s + n] = da
            db_ref[:, s:s + n] = db
            dh = dh + _nn(da, w1[s:s + n, :]) + _nn(db, w3[s:s + n, :])
        dx_ref[...] = dyv + _rmsnorm_bwd(dh, g, xhat, r)
        dg_ref[0:1, :] += jnp.sum(dh * xhat, axis=0, keepdims=True)

    big = jax.ShapeDtypeStruct((t, f), BF16)
    row = lambda w: pl.BlockSpec((tm, w), lambda i: (i, 0))
    return _call(
        body, name, t // tm, [x, dy, a, b, gain.reshape(1, d), wset, wset, wset],
        [row(d), row(d), row(f), row(f), _const((1, d)), _wspec(fs, d, 0), _wspec(fs, d, 1), _wspec(fs, d, 2)],
        [jax.ShapeDtypeStruct((t, d), F32), big, big, big, jax.ShapeDtypeStruct((t, d), BF16),
         jax.ShapeDtypeStruct((8, d), F32)],
        [row(d), row(f), row(f), row(f), row(d), _const((8, d))], riders=riders)


def _hidden_major(wcols, name):
    depth, d, c = wcols.shape

    def body(w_ref, o_ref):
        eye = lax.broadcasted_iota(jnp.int32, (d, d), 0) == lax.broadcasted_iota(jnp.int32, (d, d), 1)
        o_ref[0] = _tn(w_ref[0].astype(BF16), eye.astype(BF16)).astype(BF16)

    return pl.pallas_call(
        body, name=name, grid=(depth,),
        out_shape=jax.ShapeDtypeStruct((depth, c, d), BF16),
        in_specs=[pl.BlockSpec((1, d, c), lambda l: (l, 0, 0))],
        out_specs=pl.BlockSpec((1, c, d), lambda l: (l, 0, 0)),
        compiler_params=_params(("parallel",)),
    )(wcols)


def _wgrad(lhs, rhs, gbuf, rows, blk, scale, name, riders=()):
    t, m = lhs.shape
    d = rhs.shape[1]
    bt = _row_tile(t, 1024, 128)
    nk = t // bt
    assert m == N_DEV * rows

    def body(a_ref, b_ref, g_in, o_ref, acc_ref):
        del g_in
        k = pl.program_id(0)

        @pl.when(k == 0)
        def _():
            acc_ref[...] = jnp.zeros_like(acc_ref)

        acc_ref[...] += _tn(a_ref[...], b_ref[...].astype(BF16))

        @pl.when(k == nk - 1)
        def _():
            o_ref[...] = (scale * acc_ref[...]).astype(BF16).reshape(N_DEV, rows, d)

    (out,), rides = _call(
        body, name, nk, [lhs, rhs, gbuf],
        [pl.BlockSpec((bt, m), lambda k: (k, 0)), pl.BlockSpec((bt, d), lambda k: (k, 0)), _ANY],
        [jax.ShapeDtypeStruct(gbuf.shape, gbuf.dtype)], [pl.BlockSpec((N_DEV, rows, d), lambda k: (0, blk, 0))],
        scratch=[pltpu.VMEM((m, d), F32)], riders=riders, aliases={2: 0})
    return (out, rides) if riders else out


def _head_masks(rows, g):
    lane = lax.broadcasted_iota(jnp.int32, (rows, g), 1)
    return [(lane >= h * HEAD_DIM) & (lane < (h + 1) * HEAD_DIM) for h in range(g // HEAD_DIM)]


def _pool_windows(rows, g):
    lane = lax.broadcasted_iota(jnp.int32, (rows, g), 1)
    pg = g // len(POOL_WINDOWS)
    w = jnp.full((rows, g), float(POOL_WINDOWS[-1]), F32)
    for k in range(len(POOL_WINDOWS) - 2, -1, -1):
        w = jnp.where(lane < (k + 1) * pg, float(POOL_WINDOWS[k]), w)
    return w


def _window_sum(ref, tmp, first, rows, wl, direction):
    total = ref.shape[0]
    src, out, shift = ref, None, 1
    for k, w in enumerate(POOL_WINDOWS):
        assert w == 2 * shift and 8 * (k + 1) <= HALO
        if k + 1 < len(POOL_WINDOWS):
            n = total - 8 * (k + 1)
            lo = 8 * (k + 1) if direction < 0 else 0
            dst = tmp.at[k % 2]
            dst[pl.ds(lo, n), :] = src[pl.ds(lo, n), :] + src[pl.ds(lo + direction * shift, n), :]
            src = dst
            level = dst[pl.ds(first, rows), :]
        else:
            level = src[pl.ds(first, rows), :] + src[pl.ds(first + direction * shift, rows), :]
        out = level if out is None else jnp.where(wl >= float(w), level, out)
        shift = w
    return out


def _shifted_copies(src_ref, dst_ref):
    rows = dst_ref.shape[1]
    for b in range(1, 8):
        dst_ref[b - 1] = src_ref[pl.ds(b, rows), :]


def _window(src_ref, shifted_ref, off, rows):
    b = off % 8
    if b == 0:
        return src_ref[pl.ds(off, rows), :]
    return shifted_ref[b - 1, pl.ds(off - b, rows), :]


def _pool_divisor(first_row, rows, g):
    pos1 = (lax.broadcasted_iota(jnp.int32, (rows, g), 0) + first_row + 1).astype(F32)
    return jnp.minimum(pos1, _pool_windows(rows, g))


def _spatial_mix(vq, wcat_ref, masks):
    vstack = jnp.concatenate([jnp.where(m, vq, jnp.zeros_like(vq)) for m in masks], axis=0)
    return _nn(wcat_ref[...], vstack)


def _mix_fwd(x, gain, wbuf, lay, l, small, name):
    t, d = x.shape
    g = lay.g
    di = N_DEV * g
    tm = _row_tile(t, 256, CHUNK)
    conv_w, vecs, sconv_w, pool_bd, wcat, bst = small

    def body(x_ref, g_ref, wi_ref, wo_ref, cw_ref, vec_ref, sw_ref, pbd_ref, wcat_ref, bst_ref,
             y_ref, mix_ref, h_ref, pm_ref, c_ref, pd_ref, pp_ref, ea_ref, eb_ref, ec_ref, sh_ref, tmp_ref):
        i = pl.program_id(0)

        @pl.when(i == 0)
        def _():
            pp_ref[...] = jnp.zeros_like(pp_ref)

        xhat, _ = _rmsnorm_stats(x_ref[...])
        h = (xhat * g_ref[...]).astype(BF16)
        h_ref[...] = h
        pm_ref[...] = _nt(h, wi_ref[...].reshape(di, d)).astype(BF16)

        def prev(j):
            return pp_ref[:, j * g:(j + 1) * g].astype(F32)

        def main(j):
            return pm_ref[:, j * g:(j + 1) * g].astype(F32)

        vec = vec_ref[...]
        ea_ref[0:HALO, :] = prev(0) * _sigmoid(prev(1))
        ea_ref[HALO:, :] = main(0) * _sigmoid(main(1))
        _shifted_copies(ea_ref, sh_ref)
        cw = cw_ref[...]
        c = jnp.zeros((tm, g), F32) + vec[0:1, :]
        for k in range(CONF_KERNEL):
            c = c + cw[k:k + 1, :] * _window(ea_ref, sh_ref, HALO - (CONF_KERNEL - 1) + k, tm)
        c_ref[...] = c
        chat, _ = _layernorm_stats(c)
        ln = chat * vec[1:2, :] + vec[2:3, :]
        mix_ref[:, 0:g] = (ln * _sigmoid(ln)).astype(BF16)
        eb_ref[0:HALO, :] = prev(3) * prev(4)
        eb_ref[HALO:, :] = main(3) * main(4)
        sw = sw_ref[...]
        cz = jnp.zeros((tm, g), F32)
        for k in range(SHORT_KERNEL):
            cz = cz + sw[k:k + 1, :] * eb_ref[pl.ds(HALO - (SHORT_KERNEL - 1) + k, tm), :]
        mix_ref[:, g:2 * g] = (main(2) * cz).astype(BF16)
        ec_ref[0:HALO, :] = prev(5)
        xp = main(5)
        ec_ref[HALO:, :] = xp
        wl = _pool_windows(tm, g)
        mean = _window_sum(ec_ref, tmp_ref, HALO, tm, wl, -1) / _pool_divisor(i * tm, tm, g)
        pd_ref[...] = (mean - xp).astype(BF16)
        yc = _nn(pd_ref[...], pbd_ref[...])
        mix_ref[:, 2 * g:3 * g] = (yc * vec[3:4, :]).astype(BF16)
        vhat, _ = _layernorm_stats(main(7))
        v = (vhat * vec[4:5, :] + vec[5:6, :]).astype(BF16)
        gu = main(6)
        masks = _head_masks(CHUNK, g)
        for q in range(tm // CHUNK):
            rs = slice(q * CHUNK, (q + 1) * CHUNK)
            mixed = _spatial_mix(v[rs, :], wcat_ref, masks) + bst_ref[...]
            mix_ref[rs, 3 * g:4 * g] = (gu[rs, :] * mixed).astype(BF16)
        y_ref[...] = x_ref[...] + _nn(mix_ref[...], wo_ref[...].reshape(d, d))
        pp_ref[...] = pm_ref[tm - HALO:, :]

    row = lambda w: pl.BlockSpec((tm, w), lambda i: (i, 0))
    return pl.pallas_call(
        body, name=name, grid=(t // tm,),
        out_shape=(jax.ShapeDtypeStruct((t, d), F32), jax.ShapeDtypeStruct((t, 4 * g), BF16),
                   jax.ShapeDtypeStruct((t, d), BF16), jax.ShapeDtypeStruct((t, di), BF16),
                   jax.ShapeDtypeStruct((t, g), F32), jax.ShapeDtypeStruct((t, g), BF16)),
        in_specs=[row(d), _const((1, d)), _wspec(g, d, lay.win(l)), _wspec(lay.wo, d, lay.wout(l)),
                  _const(conv_w.shape), _const(vecs.shape), _const(sconv_w.shape), _const(pool_bd.shape),
                  _const(wcat.shape), _const(bst.shape)],
        out_specs=(row(d), row(4 * g), row(d), row(di), row(g), row(g)),
        scratch_shapes=[pltpu.VMEM((HALO, di), BF16)] + [pltpu.VMEM((HALO + tm, g), F32)] * 3
        + [pltpu.VMEM((7, HALO + tm - 8, g), F32), pltpu.VMEM((2, HALO + tm, g), F32)],
        compiler_params=_params(("arbitrary",)),
    )(x, gain.reshape(1, d), wbuf, wbuf, conv_w, vecs, sconv_w, pool_bd, wcat, bst)


def _mix_bwd(p, dy, conv_out, pool_dif, x, gain, wbuf, lay, l, small, name, riders=()):
    t, d = dy.shape
    g = lay.g
    di = N_DEV * g
    tm = _row_tile(t, 256, CHUNK)
    nt = t // tm
    nh = g // HEAD_DIM
    per = tm // HALO
    ext = tm + HALO
    conv_w, vecs, sconv_w, pool_bd, wcat, bst, wcat_t = small

    def body(pp_ref, pm_ref, pn_ref, dy_ref, dyn_ref, c_ref, cn_ref, pd_ref, x_ref, g_ref, wi_ref, wo_ref, cw_ref, vec_ref,
             sw_ref, pbd_ref, wcat_ref, bst_ref, wcatt_ref, dp_ref, dcw_ref, dvec_ref, dsw_ref, dpbd_ref, dws_ref, dbs_ref,
             dx_ref, dg_ref, ea_ref, eb_ref, ec_ref, fa_ref, fb_ref, fc_ref, sh_ref, sf_ref, tmp_ref):
        i = pl.program_id(0)
        seen = (i > 0).astype(F32)
        more = (i < nt - 1).astype(F32)

        @pl.when(i == 0)
        def _():
            for ref in (dcw_ref, dvec_ref, dsw_ref, dpbd_ref, dws_ref, dbs_ref, dg_ref):
                ref[...] = jnp.zeros_like(ref)

        def prev(j):
            return pp_ref[:, j * g:(j + 1) * g].astype(F32) * seen

        def main(j):
            return pm_ref[:, j * g:(j + 1) * g].astype(F32)

        def nxt(j):
            return pn_ref[:, j * g:(j + 1) * g].astype(F32)

        def colsum(v):
            return jnp.sum(v, axis=0, keepdims=True)

        vec = vec_ref[...]
        wo = wo_ref[...].reshape(d, d)
        dmix = jnp.concatenate([_nt(dy_ref[...].astype(BF16), wo),
                                _nt((dyn_ref[...] * more).astype(BF16), wo)], axis=0)

        val, gate = main(0), main(1)
        sg = _sigmoid(gate)
        ea_ref[0:HALO, :] = prev(0) * _sigmoid(prev(1))
        ea_ref[HALO:, :] = val * sg
        _shifted_copies(ea_ref, sh_ref)
        cw = cw_ref[...]
        c = jnp.concatenate([c_ref[...], cn_ref[...]], axis=0)
        chat, r = _layernorm_stats(c)
        ln = chat * vec[1:2, :] + vec[2:3, :]
        sl = _sigmoid(ln)
        dln = dmix[:, 0:g] * (sl * (1.0 + ln * (1.0 - sl)))
        dvec_ref[1:2, :] += colsum(dln[0:tm] * chat[0:tm])
        dvec_ref[2:3, :] += colsum(dln[0:tm])
        dc = _layernorm_bwd(dln, vec[1:2, :], chat, r)
        dvec_ref[0:1, :] += colsum(dc[0:tm])
        fa_ref[...] = dc
        _shifted_copies(fa_ref, sf_ref)
        dcm = dc[0:tm]
        dy0 = jnp.zeros((tm, g), F32)
        for k in range(CONF_KERNEL):
            dy0 = dy0 + cw[k:k + 1, :] * _window(fa_ref, sf_ref, CONF_KERNEL - 1 - k, tm)
            dcw_ref[k:k + 1, :] += colsum(dcm * _window(ea_ref, sh_ref, HALO - (CONF_KERNEL - 1) + k, tm))
        dp_ref[:, 0:g] = (dy0 * sg).astype(BF16)
        dp_ref[:, g:2 * g] = (dy0 * val * sg * (1.0 - sg)).astype(BF16)

        sb, sc, sx = main(2), main(3), main(4)
        eb_ref[0:HALO, :] = prev(3) * prev(4)
        eb_ref[HALO:ext, :] = sc * sx
        sw = sw_ref[...]
        cz = jnp.zeros((tm, g), F32)
        for k in range(SHORT_KERNEL):
            cz = cz + sw[k:k + 1, :] * eb_ref[pl.ds(HALO - (SHORT_KERNEL - 1) + k, tm), :]
        dob = dmix[:, g:2 * g]
        dp_ref[:, 2 * g:3 * g] = (dob[0:tm] * cz).astype(BF16)
        fb_ref[0:tm, :] = dob[0:tm] * sb
        fb_ref[tm:, :] = dob[tm:] * nxt(2)
        dczm = fb_ref[0:tm, :]
        dz = jnp.zeros((tm, g), F32)
        for k in range(SHORT_KERNEL):
            dz = dz + sw[k:k + 1, :] * fb_ref[pl.ds(SHORT_KERNEL - 1 - k, tm), :]
            dsw_ref[k:k + 1, :] += colsum(dczm * eb_ref[pl.ds(HALO - (SHORT_KERNEL - 1) + k, tm), :])
        dp_ref[:, 3 * g:4 * g] = (dz * sx).astype(BF16)
        dp_ref[:, 4 * g:5 * g] = (dz * sc).astype(BF16)

        wl = _pool_windows(tm, g)
        dpool = pd_ref[...]
        pbd = pbd_ref[...]
        yc = _nn(dpool, pbd)
        doc = dmix[:, 2 * g:3 * g]
        dvec_ref[3:4, :] += colsum(doc[0:tm] * yc)
        dyc = (doc * vec[3:4, :]).astype(BF16)
        dpbd_ref[...] += _tn(dpool, dyc[0:tm])
        dd = _nt(dyc, pbd)
        fc_ref[...] = dd / _pool_divisor(i * tm, ext, g)
        dp_ref[:, 5 * g:6 * g] = (_window_sum(fc_ref, tmp_ref, 0, tm, wl, 1) - dd[0:tm]).astype(BF16)

        gv = main(7)
        vhat, rv = _layernorm_stats(gv)
        v = (vhat * vec[4:5, :] + vec[5:6, :]).astype(BF16)
        gu = main(6)
        dod = dmix[0:tm, 3 * g:4 * g]
        masks = _head_masks(CHUNK, g)
        tril = (lax.broadcasted_iota(jnp.int32, (CHUNK, CHUNK), 1)
                <= lax.broadcasted_iota(jnp.int32, (CHUNK, CHUNK), 0)).astype(F32)
        tril = jnp.concatenate([tril] * nh, axis=0)
        head_col = lax.broadcasted_iota(jnp.int32, (CHUNK, CHUNK), 1)
        dvs = []
        for q in range(tm // CHUNK):
            rs = slice(q * CHUNK, (q + 1) * CHUNK)
            vq = v[rs, :]
            mixed = _spatial_mix(vq, wcat_ref, masks) + bst_ref[...]
            dp_ref[rs, 6 * g:7 * g] = (dod[rs, :] * mixed).astype(BF16)
            dmixed = dod[rs, :] * gu[rs, :]
            dmb = dmixed.astype(BF16)
            for h, m in enumerate(masks):
                hs = jnp.sum(jnp.where(m, dmixed, 0.0), axis=-1, keepdims=True)
                dbs_ref[...] += jnp.where(head_col == h, hs, 0.0)
            dmstack = jnp.concatenate([jnp.where(m, dmb, jnp.zeros_like(dmb)) for m in masks], axis=0)
            dws_ref[...] += _nt(dmstack, vq) * tril
            back = _nn(wcatt_ref[...], dmb)
            dv = jnp.zeros((CHUNK, g), F32)
            for h, m in enumerate(masks):
                dv = jnp.where(m, back[h * CHUNK:(h + 1) * CHUNK, :], dv)
            dvs.append(dv)
        dv = jnp.concatenate(dvs, axis=0)
        dvec_ref[4:5, :] += colsum(dv * vhat)
        dvec_ref[5:6, :] += colsum(dv)
        dp_ref[:, 7 * g:8 * g] = _layernorm_bwd(dv, vec[4:5, :], vhat, rv).astype(BF16)

        xhat, rx = _rmsnorm_stats(x_ref[...])
        dh = _nn(dp_ref[...], wi_ref[...].reshape(di, d))
        dx_ref[...] = dy_ref[...] + _rmsnorm_bwd(dh, g_ref[...], xhat, rx)
        dg_ref[0:1, :] += colsum(dh * xhat)

    halo_prev = pl.BlockSpec((HALO, di), lambda i: (jnp.maximum(i * per - 1, 0), 0))
    halo_next = lambda w: pl.BlockSpec((HALO, w), lambda i: (jnp.minimum((i + 1) * per, t // HALO - 1), 0))
    row = lambda w: pl.BlockSpec((tm, w), lambda i: (i, 0))
    outs = (jax.ShapeDtypeStruct((t, di), BF16), jax.ShapeDtypeStruct(conv_w.shape, F32),
            jax.ShapeDtypeStruct(vecs.shape, F32), jax.ShapeDtypeStruct(sconv_w.shape, F32),
            jax.ShapeDtypeStruct((g, g), F32), jax.ShapeDtypeStruct((nh * CHUNK, CHUNK), F32),
            jax.ShapeDtypeStruct((CHUNK, CHUNK), F32), jax.ShapeDtypeStruct((t, d), F32),
            jax.ShapeDtypeStruct((8, d), F32))
    return _call(
        body, name, nt, [p, p, p, dy, dy, conv_out, conv_out, pool_dif, x, gain.reshape(1, d), wbuf, wbuf, conv_w, vecs,
                         sconv_w, pool_bd, wcat, bst, wcat_t],
        [halo_prev, row(di), halo_next(di), row(d), halo_next(d), row(g), halo_next(g), row(g), row(d), _const((1, d)),
         _wspec(g, d, lay.win(l)), _wspec(lay.wo, d, lay.wout(l)),
         _const(conv_w.shape), _const(vecs.shape), _const(sconv_w.shape), _const(pool_bd.shape),
         _const(wcat.shape), _const(bst.shape), _const(wcat_t.shape)],
        list(outs), [row(di)] + [_const(o.shape) for o in outs[1:7]] + [row(d), _const((8, d))],
        scratch=[pltpu.VMEM((HALO + tm, g), F32), pltpu.VMEM((HALO + tm, g), F32),
                 pltpu.VMEM((HALO + tm, g), F32), pltpu.VMEM((ext, g), F32), pltpu.VMEM((ext, g), F32),
                 pltpu.VMEM((ext, g), F32), pltpu.VMEM((7, HALO + tm - 8, g), F32),
                 pltpu.VMEM((7, ext - 8, g), F32), pltpu.VMEM((2, ext, g), F32)],
        riders=riders)


def _loss_head(x, target, gain, name):
    t, d = x.shape
    tm = _row_tile(t, 512, 8)

    def body(x_ref, t_ref, g_ref, dx_ref, loss_ref, dg_ref):
        @pl.when(pl.program_id(0) == 0)
        def _():
            loss_ref[...] = jnp.zeros_like(loss_ref)
            dg_ref[...] = jnp.zeros_like(dg_ref)

        g = g_ref[...]
        xhat, r = _rmsnorm_stats(x_ref[...])
        err = xhat * g - t_ref[...]
        loss_ref[...] += 0.5 * jnp.sum(jnp.mean(err * err, axis=-1, keepdims=True))
        dyv = err * (1.0 / d)
        dx_ref[...] = _rmsnorm_bwd(dyv, g, xhat, r)
        dg_ref[0:1, :] += jnp.sum(dyv * xhat, axis=0, keepdims=True)

    row = pl.BlockSpec((tm, d), lambda i: (i, 0))
    return pl.pallas_call(
        body, name=name, grid=(t // tm,),
        out_shape=(jax.ShapeDtypeStruct((t, d), F32), jax.ShapeDtypeStruct((8, 128), F32),
                   jax.ShapeDtypeStruct((8, d), F32)),
        in_specs=[row, row, _const((1, d))],
        out_specs=(row, _const((8, 128)), _const((8, d))),
        compiler_params=_params(("arbitrary",)),
    )(x, target, gain.reshape(1, d))


def _adam_step(w, gv, m, v):
    nm = ADAM_B1 * m + (1.0 - ADAM_B1) * gv
    nv = ADAM_B2 * v + (1.0 - ADAM_B2) * (gv * gv)
    m_hat = nm / (1.0 - ADAM_B1 ** ADAM_STEP)
    v_hat = nv / (1.0 - ADAM_B2 ** ADAM_STEP)
    return -ADAM_LR * (m_hat / (jnp.sqrt(v_hat) + ADAM_EPS) + ADAM_WD * w), nm, nv


def _reduce_adamw(slots, blks, transposed, w, m, v, name):
    depth, rows, cols = w.shape
    tr = _row_tile(rows, 512, 128) if transposed else _row_tile(rows, 256, 16)
    nt = rows // tr
    pad = -cols % 128

    def body(*refs):
        slot_refs = refs[:depth]
        w_ref, m_ref, v_ref, g_ref, d_ref, nm_ref, nv_ref = refs[depth:]
        layer = pl.program_id(0) // nt
        for l in range(depth):
            @pl.when(layer == l)
            def _(s_ref=slot_refs[l]):
                gv = s_ref[0].astype(F32)
                for j in range(1, N_DEV):
                    gv = gv + s_ref[j].astype(F32)
                if transposed:
                    if pad:
                        gv = jnp.concatenate([gv, jnp.zeros((pad, tr), F32)], axis=0)
                    gv = gv.T[:, :cols]
                g_ref[0] = gv
                d_ref[0], nm_ref[0], nv_ref[0] = _adam_step(w_ref[0], gv, m_ref[0], v_ref[0])

    def slot_spec(l):
        tile = lambda i: jnp.clip(i - l * nt, 0, nt - 1)
        if transposed:
            return pl.BlockSpec((N_DEV, cols, tr), lambda i: (0, blks[l], tile(i)))
        return pl.BlockSpec((N_DEV, tr, cols), lambda i: (0, blks[l] * nt + tile(i), 0))

    spec = pl.BlockSpec((1, tr, cols), lambda i: (i // nt, i % nt, 0))
    shape = jax.ShapeDtypeStruct(w.shape, F32)
    return pl.pallas_call(
        body, name=name, grid=(depth * nt,),
        out_shape=(shape,) * 4, in_specs=[slot_spec(l) for l in range(depth)] + [spec] * 3, out_specs=(spec,) * 4,
        compiler_params=_params(("arbitrary",)),
    )(*slots, w, m, v)


def _adamw(w, grad, m, v, name):
    shape = w.shape
    cols = shape[-1]
    rows = w.size // cols
    tr = _row_tile(rows, 512, 8) if rows > 1024 else rows

    def body(w_ref, g_ref, m_ref, v_ref, d_ref, nm_ref, nv_ref):
        d_ref[...], nm_ref[...], nv_ref[...] = _adam_step(w_ref[...], g_ref[...], m_ref[...], v_ref[...])

    spec = pl.BlockSpec((tr, cols), lambda i: (i, 0))
    flat = jax.ShapeDtypeStruct((rows, cols), F32)
    outs = pl.pallas_call(
        body, name=name, grid=(rows // tr,),
        out_shape=(flat, flat, flat), in_specs=[spec] * 4, out_specs=(spec,) * 3,
        compiler_params=_params(("parallel",)),
    )(*(a.reshape(rows, cols) for a in (w, grad, m, v)))
    return tuple(o.reshape(shape) for o in outs)


def _f32_rows(parts, d, rows):
    out, offs, at = [], [], 0
    for a in parts:
        n = -(-a.size // d)
        out.append(jnp.pad(a.reshape(-1), (0, n * d - a.size)).reshape(n, d))
        offs.append(at)
        at += n
    total = at if rows is None else rows
    assert at <= total
    if total > at:
        out.append(jnp.zeros((total - at, d), F32))
    return jnp.concatenate(out, axis=0), offs


def _take(buf, off, shape):
    n = 1
    for s in shape:
        n *= s
    d = buf.shape[1]
    return buf[off:off + -(-n // d)].reshape(-1)[:n].reshape(shape)


def _block_diag(blocks):
    n, k, _ = blocks.shape
    out = jnp.zeros((n * k, n * k), blocks.dtype)
    for i in range(n):
        out = lax.dynamic_update_slice(out, blocks[i], (i * k, i * k))
    return out


def kernel(x, ffn1_norm, ffn1_w1, ffn1_w3, ffn1_w2, mix_norm, w_in, conf_conv_w, conf_conv_b, conf_ln_g, conf_ln_b, sconv_w, pool_w, pool_scale, gmlp_ln_g, gmlp_ln_b, gmlp_w_s, gmlp_b_s, w_out, ffn2_norm, ffn2_w1, ffn2_w3, ffn2_w2, final_norm, loss_target, m_ffn1_norm, m_ffn1_w1, m_ffn1_w3, m_ffn1_w2, m_mix_norm, m_w_in, m_conf_conv_w, m_conf_conv_b, m_conf_ln_g, m_conf_ln_b, m_sconv_w, m_pool_w, m_pool_scale, m_gmlp_ln_g, m_gmlp_ln_b, m_gmlp_w_s, m_gmlp_b_s, m_w_out, m_ffn2_norm, m_ffn2_w1, m_ffn2_w3, m_ffn2_w2, m_final_norm, v_ffn1_norm, v_ffn1_w1, v_ffn1_w3, v_ffn1_w2, v_mix_norm, v_w_in, v_conf_conv_w, v_conf_conv_b, v_conf_ln_g, v_conf_ln_b, v_sconv_w, v_pool_w, v_pool_scale, v_gmlp_ln_g, v_gmlp_ln_b, v_gmlp_w_s, v_gmlp_b_s, v_w_out, v_ffn2_norm, v_ffn2_w1, v_ffn2_w3, v_ffn2_w2, v_final_norm):
    w = dict(zip(WEIGHTS, (ffn1_norm, ffn1_w1, ffn1_w3, ffn1_w2, mix_norm, w_in, conf_conv_w, conf_conv_b, conf_ln_g,
                           conf_ln_b, sconv_w, pool_w, pool_scale, gmlp_ln_g, gmlp_ln_b, gmlp_w_s, gmlp_b_s, w_out,
                           ffn2_norm, ffn2_w1, ffn2_w3, ffn2_w2, final_norm)))
    mom1 = dict(zip(WEIGHTS, (m_ffn1_norm, m_ffn1_w1, m_ffn1_w3, m_ffn1_w2, m_mix_norm, m_w_in, m_conf_conv_w,
                              m_conf_conv_b, m_conf_ln_g, m_conf_ln_b, m_sconv_w, m_pool_w, m_pool_scale, m_gmlp_ln_g,
                              m_gmlp_ln_b, m_gmlp_w_s, m_gmlp_b_s, m_w_out, m_ffn2_norm, m_ffn2_w1, m_ffn2_w3,
                              m_ffn2_w2, m_final_norm)))
    mom2 = dict(zip(WEIGHTS, (v_ffn1_norm, v_ffn1_w1, v_ffn1_w3, v_ffn1_w2, v_mix_norm, v_w_in, v_conf_conv_w,
                              v_conf_conv_b, v_conf_ln_g, v_conf_ln_b, v_sconv_w, v_pool_w, v_pool_scale, v_gmlp_ln_g,
                              v_gmlp_ln_b, v_gmlp_w_s, v_gmlp_b_s, v_w_out, v_ffn2_norm, v_ffn2_w1, v_ffn2_w3,
                              v_ffn2_w2, v_final_norm)))
    _, t, d = x.shape
    depth, _, fs = ffn1_w1.shape
    g = w_in.shape[2]
    wo = w_out.shape[1]
    nh = g // HEAD_DIM
    cs = conf_conv_w.shape[2]
    lay = _MixRows(g, wo)
    me = 4 * lax.axis_index("x") + 2 * lax.axis_index("y") + lax.axis_index("c")

    def rows_bf16(mats):
        return jnp.concatenate([m_.astype(BF16) for m_ in mats], axis=0)

    swapped = tuple(n for n in SHARDED_COLS if w[n].shape[2] % 128)
    rows_of = {n: (jnp.swapaxes(w[n], 1, 2) if n in swapped else _hidden_major(w[n], f"rows_{n}"))
               for n in SHARDED_COLS}
    set_f1 = [rows_bf16([rows_of['ffn1_w1'][l], rows_of['ffn1_w3'][l], ffn1_w2[l]]) for l in range(depth)]
    set_f2 = [rows_bf16([rows_of['ffn2_w1'][l], rows_of['ffn2_w3'][l], ffn2_w2[l]]) for l in range(depth)]
    set_mx = [rows_bf16([rows_of['w_in'][l], w_out[l]]) for l in range(depth)]
    convs = jnp.concatenate([conf_conv_w, sconv_w], axis=1)
    nconv = convs.size
    conv_bits = jnp.pad(convs.reshape(-1), (0, CONV_ROWS * d // 2 - nconv))
    conv_rows = lax.bitcast_convert_type(conv_bits, BF16).reshape(CONV_ROWS, d)
    w_f1 = [None] * depth
    w_f1[0] = _alone(_Gather(jnp.concatenate([set_f1[0], conv_rows], axis=0)), "gather_first")

    conv_all = lax.bitcast_convert_type(w_f1[0][:, 3 * fs:, :].reshape(N_DEV, CONV_ROWS * d // 2, 2), F32)
    conv_all = conv_all[:, :nconv].reshape(N_DEV, depth, CONF_KERNEL + SHORT_KERNEL, cs)
    conv_all = conv_all.transpose(1, 2, 0, 3).reshape(depth, CONF_KERNEL + SHORT_KERNEL, g)

    def small_inputs(l):
        cw = jnp.pad(conv_all[l, :CONF_KERNEL], ((0, 32 - CONF_KERNEL), (0, 0)))
        sw = jnp.pad(conv_all[l, CONF_KERNEL:], ((0, 8 - SHORT_KERNEL), (0, 0)))
        vecs = jnp.stack([conf_conv_b[l], conf_ln_g[l], conf_ln_b[l], pool_scale[l], gmlp_ln_g[l], gmlp_ln_b[l],
                          jnp.zeros((g,), F32), jnp.zeros((g,), F32)])
        pool_bd = _block_diag(pool_w[l]).astype(BF16)
        low = jnp.tril(jnp.ones((CHUNK, CHUNK), bool))
        ws = jnp.where(low[None], gmlp_w_s[l], 0.0).astype(BF16)
        wcat = ws.transpose(1, 0, 2).reshape(CHUNK, nh * CHUNK)
        wcat_t = ws.transpose(0, 2, 1).reshape(nh * CHUNK, CHUNK)
        bst = jnp.repeat(gmlp_b_s[l].T, HEAD_DIM, axis=1)
        return cw, vecs, sw, pool_bd, wcat, bst, wcat_t

    xs = x[0]
    saved = []
    for l in range(depth):
        sm = small_inputs(l)
        (x1, a1, b1), ((w_f2,), (w_mx,)) = _ffn_fwd(xs, ffn1_norm[l], w_f1[l], fs, f"ffn1_fwd_{l}",
                                                    riders=[_Gather(set_f2[l]), _Gather(set_mx[l])])
        x2, mix, h2, p, *kept = _mix_fwd(x1, mix_norm[l], w_mx, lay, l, sm[:6], f"mix_fwd_{l}")
        nxt = [_Gather(set_f1[l + 1])] if l + 1 < depth else []
        (x3, a2, b2), got = _ffn_fwd(x2, ffn2_norm[l], w_f2, fs, f"ffn2_fwd_{l}", riders=nxt)
        if nxt:
            w_f1[l + 1] = got[0][0]
        saved.append((xs, a1, b1, x1, h2, p, mix, x2, a2, b2, sm, w_f2, w_mx, kept))
        xs = x3

    dx, loss_part, dgf = _loss_head(xs, loss_target[0], final_norm, "loss_head")
    loss = lax.psum(loss_part[0, 0], ("x", "y", "c"))

    def small_rows(parts):
        buf, at = _f32_rows(parts, d, None)
        return jnp.pad(buf, ((0, -buf.shape[0] % 16), (0, 0))), at

    small = [None] * depth
    big = ('ffn1_w1', 'ffn1_w3', 'ffn1_w2', 'ffn2_w1', 'ffn2_w3', 'ffn2_w2', 'w_in', 'w_out')
    slots = {n: [None] * depth for n in big}
    blk_of = {'ffn1_w1': 0, 'ffn1_w3': 1, 'ffn1_w2': 2, 'ffn2_w1': 0, 'ffn2_w3': 1, 'ffn2_w2': 2,
              'w_in': lay.win(0), 'w_out': lay.wout(0)}
    blks = {n: [blk_of[n]] * depth for n in big}

    def landed(names, l, got_):
        for n in names:
            slots[n][l] = got_

    def ffn_grads(da, db, u, h, dy, tag):
        gset = lax.empty((N_DEV, 3 * fs, d), BF16)
        gset = _wgrad(da, h, gset, fs, 0, 1.0, f"{tag}_dw1")
        gset = _wgrad(db, h, gset, fs, 1, 1.0, f"{tag}_dw3")
        return _wgrad(u, dy, gset, fs, 2, FFN_RESIDUAL, f"{tag}_dw2")

    pending = None
    early_all = None
    for l in reversed(range(depth)):
        x0, a1, b1, x1, h2, p, mix, x2, a2, b2, sm, w_f2, w_mx, kept = saved[l]
        ride = [_Exchange(pending)] if pending is not None else []
        (dx2, da, db, u, h, dgn2), got = _ffn_bwd(x2, dx, a2, b2, ffn2_norm[l], w_f2, fs, f"ffn2_bwd_{l}", riders=ride)
        if ride:
            landed(big[0:3], l + 1, got[0][0])
        gset = lax.empty((N_DEV, 3 * fs, d), BF16)
        if l == 0 and depth > 1:
            early, early_offs = small_rows([a for k in range(1, depth) for a in small[k]] + [dgf[0]])
            cut = early.shape[0] // 3 // 16 * 16
            gset, ((e0,),) = _wgrad(da, h, gset, fs, 0, 1.0, f"ffn2_{l}_dw1", riders=[_Gather(early[:cut])])
            gset, ((e1,),) = _wgrad(db, h, gset, fs, 1, 1.0, f"ffn2_{l}_dw3", riders=[_Gather(early[cut:2 * cut])])
            gset, ((e2,),) = _wgrad(u, dx, gset, fs, 2, FFN_RESIDUAL, f"ffn2_{l}_dw2",
                                    riders=[_Gather(early[2 * cut:])])
            early_all = jnp.concatenate([e0, e1, e2], axis=1)
        else:
            gset = _wgrad(da, h, gset, fs, 0, 1.0, f"ffn2_{l}_dw1")
            gset = _wgrad(db, h, gset, fs, 1, 1.0, f"ffn2_{l}_dw3")
            gset = _wgrad(u, dx, gset, fs, 2, FFN_RESIDUAL, f"ffn2_{l}_dw2")
        (dp, dcw, dvec, dsw, dpbd, dws, dbs, dx1, dgm), got = _mix_bwd(
            p, dx2, *kept, x1, mix_norm[l], w_mx, lay, l, sm, f"mix_bwd_{l}", riders=[_Exchange(gset)])
        landed(big[3:6], l, got[0][0])
        if l > 0:
            gmx = lax.empty((N_DEV, g + wo, d), BF16)
            gmx = _wgrad(mix, dx2, gmx, wo, lay.wout(l), 1.0, f"dw_out_{l}")
            gmx = _wgrad(dp, h2, gmx, g, lay.win(l), 1.0, f"dw_in_{l}")
            ride = [_Exchange(gmx)]
        else:
            g_wo = _wgrad(mix, dx2, lax.empty((N_DEV, wo, d), BF16), wo, 0, 1.0, f"dw_out_{l}")
            g_wi, ((slots['w_out'][0],),) = _wgrad(dp, h2, lax.empty((N_DEV, g, d), BF16), g, 0, 1.0, f"dw_in_{l}",
                                                   riders=[_Exchange(g_wo)])
            blks['w_out'][0] = blks['w_in'][0] = 0
            ride = []
        (dx, da, db, u, h, dgn1), got = _ffn_bwd(x0, dx1, a1, b1, ffn1_norm[l], w_f1[l], fs,
                                                 f"ffn1_bwd_{l}", riders=ride)
        if l > 0:
            landed(big[6:8], l, got[0][0])
        pg = g // len(POOL_WINDOWS)
        dpool = jnp.stack([dpbd[k * pg:(k + 1) * pg, k * pg:(k + 1) * pg] for k in range(len(POOL_WINDOWS))])
        small[l] = [dgn1[0], dgm[0], dgn2[0], dvec, dcw, dsw, dpool, dws, dbs[:, :nh].T]
        if l > 0:
            pending = ffn_grads(da, db, u, h, dx1, f"ffn1_{l}")
    grad_x = dx[None]
    late, late_offs = small_rows(small[0] if depth > 1 else small[0] + [dgf[0]])
    one = lambda: lax.empty((N_DEV, fs, d), BF16)
    g_a, ((slots['w_in'][0],),) = _wgrad(da, h, one(), fs, 0, 1.0, "ffn1_0_dw1", riders=[_Exchange(g_wi)])
    g_b, ((slots['ffn1_w1'][0],),) = _wgrad(db, h, one(), fs, 0, 1.0, "ffn1_0_dw3", riders=[_Exchange(g_a)])
    g_c, ((slots['ffn1_w3'][0],),) = _wgrad(u, dx1, one(), fs, 0, FFN_RESIDUAL, "ffn1_0_dw2",
                                            riders=[_Exchange(g_b)])
    (slots['ffn1_w2'][0],), (late_all,) = _call(lambda: None, "exchange_last", 1, [], [], [], [],
                                                riders=[_Exchange(g_c), _Gather(late)])[1]
    for n in big[0:3]:
        blks[n][0] = 0

    grads, deltas, new_m, new_v = {}, {}, {}, {}
    for n in big:
        if n in swapped:
            outs = _reduce_adamw(slots[n], blks[n], False, *(jnp.swapaxes(a_, 1, 2) for a_ in (w[n], mom1[n], mom2[n])),
                                 f"reduce_adamw_{n}")
            grads[n], deltas[n], new_m[n], new_v[n] = (jnp.swapaxes(o_, 1, 2) for o_ in outs)
        else:
            grads[n], deltas[n], new_m[n], new_v[n] = _reduce_adamw(
                slots[n], blks[n], n in SHARDED_COLS, w[n], mom1[n], mom2[n], f"reduce_adamw_{n}")

    per_layer = len(small[0])
    shapes = [(d,), (d,), (d,), (8, g), (32, g), (8, g), (len(POOL_WINDOWS), g // len(POOL_WINDOWS),
              g // len(POOL_WINDOWS)), (nh * CHUNK, CHUNK), (nh, CHUNK)]
    late_sum = _sum_slots(late_all, "sum_small_late")
    early_sum = _sum_slots(early_all, "sum_small_early") if depth > 1 else None
    got = [[_take(late_sum, late_offs[k], shapes[k]) for k in range(per_layer)]]
    got += [[_take(early_sum, early_offs[(l - 1) * per_layer + k], shapes[k]) for k in range(per_layer)]
            for l in range(1, depth)]
    col = lambda k: jnp.stack([got[l][k] for l in range(depth)])
    grads['ffn1_norm'], grads['mix_norm'], grads['ffn2_norm'] = col(0), col(1), col(2)
    dvec_all = col(3)
    for k, n in enumerate(('conf_conv_b', 'conf_ln_g', 'conf_ln_b', 'pool_scale', 'gmlp_ln_g', 'gmlp_ln_b')):
        grads[n] = dvec_all[:, k]
    grads['conf_conv_w'] = lax.dynamic_slice_in_dim(col(4)[:, :CONF_KERNEL], me * cs, cs, axis=2)
    grads['sconv_w'] = lax.dynamic_slice_in_dim(col(5)[:, :SHORT_KERNEL], me * cs, cs, axis=2)
    grads['pool_w'] = col(6)
    grads['gmlp_w_s'] = col(7).reshape(depth, nh, CHUNK, CHUNK)
    grads['gmlp_b_s'] = col(8)
    grads['final_norm'] = (_take(early_sum, early_offs[-1], (d,)) if depth > 1
                           else _take(late_sum, late_offs[-1], (d,)))

    for n in WEIGHTS:
        if n not in big:
            deltas[n], new_m[n], new_v[n] = _adamw(w[n], grads[n], mom1[n], mom2[n], f"adamw_{n}")

    return (loss, grad_x, *[grads[n] for n in WEIGHTS], *[deltas[n] for n in WEIGHTS],
            *[new_m[n] for n in WEIGHTS], *[new_v[n] for n in WEIGHTS])
```

```python
import functools

import jax
import jax.numpy as jnp
from jax import lax
from jax.experimental import pallas as pl
from jax.experimental.pallas import tpu as pltpu

F32 = jnp.float32
BF16 = jnp.bfloat16
MESH = pl.DeviceIdType.MESH

N_DEV = 8
EPS = 1e-6
FFN_RESIDUAL = 0.5
CONF_KERNEL = 31
SHORT_KERNEL = 3
POOL_WINDOWS = (2, 4, 8, 16)
CHUNK = 128
HEAD_DIM = 64
N_IN_PIECES = 8
HALO = 32
CONV_ROWS = 16
ADAM_LR = 0.001
ADAM_B1 = 0.9
ADAM_B2 = 0.999
ADAM_EPS = 1e-08
ADAM_WD = 0.01
ADAM_STEP = 10
VMEM_LIMIT_V7X = 56 * 1024 * 1024

WEIGHTS = ['ffn1_norm', 'ffn1_w1', 'ffn1_w3', 'ffn1_w2', 'mix_norm', 'w_in', 'conf_conv_w', 'conf_conv_b',
           'conf_ln_g', 'conf_ln_b', 'sconv_w', 'pool_w', 'pool_scale', 'gmlp_ln_g', 'gmlp_ln_b', 'gmlp_w_s',
           'gmlp_b_s', 'w_out', 'ffn2_norm', 'ffn2_w1', 'ffn2_w3', 'ffn2_w2', 'final_norm']
SHARDED_ROWS = ('ffn1_w2', 'ffn2_w2', 'w_out')
SHARDED_COLS = ('ffn1_w1', 'ffn1_w3', 'ffn2_w1', 'ffn2_w3', 'w_in')
SHARDED_CHAN = ('conf_conv_w', 'sconv_w')


def _params(sem=None):
    return pltpu.CompilerParams(dimension_semantics=sem, vmem_limit_bytes=VMEM_LIMIT_V7X)


def _nn(a, b):
    return jnp.dot(a, b, preferred_element_type=F32)


def _nt(a, b):
    return lax.dot_general(a, b, (((1,), (1,)), ((), ())), preferred_element_type=F32)


def _tn(a, b):
    return lax.dot_general(a, b, (((0,), (0,)), ((), ())), preferred_element_type=F32)


def _sigmoid(x):
    return 0.5 * jnp.tanh(0.5 * x) + 0.5


def _row_tile(n, pref, mult=8):
    t = min(n, pref)
    while n % t or t % mult:
        t -= 1
    return t


def _chunks(n, size):
    out, s = [], 0
    while s < n:
        out.append((s, min(size, n - s)))
        s += size
    return out


def _const(shape):
    return pl.BlockSpec(shape, lambda i: (0,) * len(shape))


def _resident(shape, index):
    return pl.BlockSpec(shape, lambda i: index, pipeline_mode=pl.Buffered(1))


def _wspec(rows, d, blk):
    return _resident((N_DEV, rows, d), (0, blk, 0))


def _layernorm_stats(x):
    mu = jnp.mean(x, axis=-1, keepdims=True)
    xc = x - mu
    r = lax.rsqrt(jnp.mean(xc * xc, axis=-1, keepdims=True) + EPS)
    return xc * r, r


def _layernorm_bwd(dy, g, xhat, r):
    dxh = dy * g
    return r * (dxh - jnp.mean(dxh, axis=-1, keepdims=True) - xhat * jnp.mean(dxh * xhat, axis=-1, keepdims=True))


def _rmsnorm_stats(x):
    r = lax.rsqrt(jnp.mean(x * x, axis=-1, keepdims=True) + EPS)
    return x * r, r


def _rmsnorm_bwd(dh, g, xhat, r):
    dxh = dh * g
    return r * (dxh - xhat * jnp.mean(dxh * xhat, axis=-1, keepdims=True))


class _MixRows:
    def __init__(self, g, wo):
        assert g % wo == 0 and wo % 16 == 0
        self.g, self.wo = g, wo

    def win(self, l):
        return 0

    def wout(self, l):
        return self.g // self.wo


class _Gather:
    def __init__(self, shard):
        self.operands = [shard]
        self.out_shape = [jax.ShapeDtypeStruct((N_DEV,) + shard.shape, shard.dtype)]
        self.scratch = [pltpu.SemaphoreType.DMA((7,)), pltpu.SemaphoreType.DMA((7,)), pltpu.SemaphoreType.DMA]

    def _copies(self, ins, outs, scr, arrivals):
        (x_ref,), (out_ref,), (send_sems, recv_sems, local_sem) = ins, outs, scr
        x, y, c = lax.axis_index("x"), lax.axis_index("y"), lax.axis_index("c")
        me, sibling = (x, y, c), (x, y, 1 - c)
        chips = [(1 - x, y), (x, 1 - y), (1 - x, 1 - y)]

        def slot(px, py, pc):
            return out_ref.at[4 * px + 2 * py + pc]

        def copy(k, block, to, src=None):
            return pltpu.make_async_remote_copy(
                src_ref=slot(*block) if src is None else src, dst_ref=slot(*block),
                send_sem=send_sems.at[k], recv_sem=recv_sems.at[k], device_id=to, device_id_type=MESH)

        mine = pltpu.make_async_copy(x_ref, slot(*me), local_sem)
        first = [copy(0, me, sibling, src=x_ref)]
        first += [copy(1 + j, me, (*chip, c), src=x_ref) for j, chip in enumerate(chips)]
        passed = [copy(4 + j, (*chip, c), sibling) for j, chip in enumerate(chips)]
        if arrivals == "ici":
            landing = [copy(1 + j, (*chip, c), me) for j, chip in enumerate(chips)]
        else:
            landing = [copy(0, sibling, me)] + [copy(4 + j, (*chip, 1 - c), me) for j, chip in enumerate(chips)]
        return mine, first, passed, landing

    def begin(self, i, n, ins, outs, scr):
        mine, first, passed, over_ici = self._copies(ins, outs, scr, "ici")

        @pl.when(i == 0)
        def _():
            mine.start()
            for cp in first:
                cp.start()

        @pl.when(i == (n - 3 if n >= 6 else n // 2))
        def _():
            for landed, onward in zip(over_ici, passed):
                landed.wait_recv()
                onward.start()

    def end(self, i, n, ins, outs, scr):
        mine, first, passed, over_d2d = self._copies(ins, outs, scr, "d2d")

        @pl.when(i == n - 1)
        def _():
            for cp in over_d2d:
                cp.wait_recv()
            for cp in first + passed:
                cp.wait_send()
            mine.wait()


class _Exchange:
    def __init__(self, blocks):
        self.operands = [blocks]
        self.out_shape = [jax.ShapeDtypeStruct(blocks.shape, blocks.dtype)]
        self.scratch = [pltpu.SemaphoreType.DMA((7,)), pltpu.SemaphoreType.DMA((7,)), pltpu.SemaphoreType.DMA]

    def _copies(self, ins, outs, scr, with_arrivals):
        (s_ref,), (r_ref,), (send_sems, recv_sems, local_sem) = ins, outs, scr
        x, y, c = lax.axis_index("x"), lax.axis_index("y"), lax.axis_index("c")
        me = 4 * x + 2 * y + c
        mine = pltpu.make_async_copy(s_ref.at[me], r_ref.at[me], local_sem)
        sends, recvs = [], []
        for k in range(1, N_DEV):
            px = (1 - x) if (k >> 2) & 1 else x
            py = (1 - y) if (k >> 1) & 1 else y
            pc = (1 - c) if k & 1 else c
            peer = 4 * px + 2 * py + pc
            for dst, group in ((r_ref.at[me], sends), (r_ref.at[peer], recvs)):
                if group is sends or with_arrivals:
                    group.append(pltpu.make_async_remote_copy(
                        src_ref=s_ref.at[peer], dst_ref=dst, send_sem=send_sems.at[k - 1],
                        recv_sem=recv_sems.at[k - 1], device_id=(px, py, pc), device_id_type=MESH))
        return mine, sends, recvs

    def begin(self, i, n, ins, outs, scr):
        mine, sends, _ = self._copies(ins, outs, scr, False)

        @pl.when(i == 0)
        def _():
            mine.start()
            for cp in sends:
                cp.start()

    def end(self, i, n, ins, outs, scr):
        mine, sends, recvs = self._copies(ins, outs, scr, True)

        @pl.when(i == n - 1)
        def _():
            for cp in recvs:
                cp.wait_recv()
            for cp in sends:
                cp.wait_send()
            mine.wait()


_ANY = pl.BlockSpec(memory_space=pl.ANY)


def _call(body, name, steps, operands, in_specs, out_shape, out_specs, scratch=(), riders=(), aliases=None):
    n_in, n_out, n_scr = len(operands), len(out_shape), len(scratch)

    def full(*refs):
        pos = [0]

        def take(k):
            pos[0] += k
            return refs[pos[0] - k:pos[0]]

        h_in, r_in = take(n_in), [take(len(r.operands)) for r in riders]
        h_out, r_out = take(n_out), [take(len(r.out_shape)) for r in riders]
        h_scr, r_scr = take(n_scr), [take(len(r.scratch)) for r in riders]
        i = pl.program_id(0)
        for r, a, b, c in zip(riders, r_in, r_out, r_scr):
            r.begin(i, steps, a, b, c)
        body(*h_in, *h_out, *h_scr)
        for r, a, b, c in zip(riders, r_in, r_out, r_scr):
            r.end(i, steps, a, b, c)

    outs = pl.pallas_call(
        full, name=name, grid=(steps,),
        out_shape=tuple(out_shape) + tuple(s for r in riders for s in r.out_shape),
        in_specs=list(in_specs) + [_ANY for r in riders for _ in r.operands],
        out_specs=tuple(out_specs) + tuple(_ANY for r in riders for _ in r.out_shape),
        scratch_shapes=list(scratch) + [s for r in riders for s in r.scratch],
        input_output_aliases=aliases or {},
        compiler_params=_params(("arbitrary",)),
    )(*operands, *(o for r in riders for o in r.operands))
    host, rest = outs[:n_out], list(outs[n_out:])
    rides = []
    for r in riders:
        rides.append(rest[:len(r.out_shape)])
        rest = rest[len(r.out_shape):]
    return host, rides


def _alone(rider, name):
    return _call(lambda: None, name, 1, [], [], [], [], riders=[rider])[1][0][0]


def _sum_slots(slots, name):
    _, rows, cols = slots.shape
    tr = _row_tile(rows, 256, 16)

    def body(s_ref, o_ref):
        acc = s_ref[0].astype(F32)
        for j in range(1, N_DEV):
            acc = acc + s_ref[j].astype(F32)
        o_ref[...] = acc

    return pl.pallas_call(
        body, name=name, grid=(rows // tr,),
        out_shape=jax.ShapeDtypeStruct((rows, cols), F32),
        in_specs=[pl.BlockSpec((N_DEV, tr, cols), lambda i: (0, i, 0))],
        out_specs=pl.BlockSpec((tr, cols), lambda i: (i, 0)),
        compiler_params=_params(("parallel",)),
    )(slots)


def _ffn_fwd(x, gain, wset, fs, name, riders=()):
    t, d = x.shape
    f = N_DEV * fs
    tm = _row_tile(t, 512, 128)
    cols = _chunks(f, 1024)

    def body(x_ref, g_ref, w1_ref, w3_ref, w2_ref, y_ref, a_ref, b_ref, u_ref):
        xv = x_ref[...]
        xhat, _ = _rmsnorm_stats(xv)
        h = (xhat * g_ref[...]).astype(BF16)
        w1 = w1_ref[...].reshape(f, d)
        w3 = w3_ref[...].reshape(f, d)
        for s, n in cols:
            a = _nt(h, w1[s:s + n, :])
            b = _nt(h, w3[s:s + n, :])
            a_ref[:, s:s + n] = a.astype(BF16)
            b_ref[:, s:s + n] = b.astype(BF16)
            u_ref[:, s:s + n] = (a * _sigmoid(a) * b).astype(BF16)
        y_ref[...] = xv + FFN_RESIDUAL * _nn(u_ref[...], w2_ref[...].reshape(f, d))

    return _call(
        body, name, t // tm, [x, gain.reshape(1, d), wset, wset, wset],
        [pl.BlockSpec((tm, d), lambda i: (i, 0)), _const((1, d)),
         _wspec(fs, d, 0), _wspec(fs, d, 1), _wspec(fs, d, 2)],
        [jax.ShapeDtypeStruct((t, d), F32), jax.ShapeDtypeStruct((t, f), BF16), jax.ShapeDtypeStruct((t, f), BF16)],
        [pl.BlockSpec((tm, d), lambda i: (i, 0)), pl.BlockSpec((tm, f), lambda i: (i, 0)),
         pl.BlockSpec((tm, f), lambda i: (i, 0))],
        scratch=[pltpu.VMEM((tm, f), BF16)], riders=riders)


def _ffn_fwd_head(x, gain, wset, fs, target, final_gain, name):
    t, d = x.shape
    f = N_DEV * fs
    tm = _row_tile(t, 512, 128)
    cols = _chunks(f, 1024)

    def body(x_ref, g_ref, w1_ref, w3_ref, w2_ref, t_ref, gf_ref, dx_ref, a_ref, b_ref, loss_ref, dg_ref, u_ref):
        @pl.when(pl.program_id(0) == 0)
        def _():
            loss_ref[...] = jnp.zeros_like(loss_ref)
            dg_ref[...] = jnp.zeros_like(dg_ref)

        xv = x_ref[...]
        xhat, _ = _rmsnorm_stats(xv)
        h = (xhat * g_ref[...]).astype(BF16)
        w1 = w1_ref[...].reshape(f, d)
        w3 = w3_ref[...].reshape(f, d)
        for s, n in cols:
            a = _nt(h, w1[s:s + n, :])
            b = _nt(h, w3[s:s + n, :])
            a_ref[:, s:s + n] = a.astype(BF16)
            b_ref[:, s:s + n] = b.astype(BF16)
            u_ref[:, s:s + n] = (a * _sigmoid(a) * b).astype(BF16)
        y = xv + FFN_RESIDUAL * _nn(u_ref[...], w2_ref[...].reshape(f, d))
        gf = gf_ref[...]
        yhat, r = _rmsnorm_stats(y)
        err = yhat * gf - t_ref[...]
        loss_ref[...] += 0.5 * jnp.sum(jnp.mean(err * err, axis=-1, keepdims=True))
        dyv = err * (1.0 / d)
        dx_ref[...] = _rmsnorm_bwd(dyv, gf, yhat, r)
        dg_ref[0:1, :] += jnp.sum(dyv * yhat, axis=0, keepdims=True)

    row = lambda w: pl.BlockSpec((tm, w), lambda i: (i, 0))
    return _call(
        body, name, t // tm, [x, gain.reshape(1, d), wset, wset, wset, target, final_gain.reshape(1, d)],
        [row(d), _const((1, d)), _wspec(fs, d, 0), _wspec(fs, d, 1), _wspec(fs, d, 2), row(d), _const((1, d))],
        [jax.ShapeDtypeStruct((t, d), F32), jax.ShapeDtypeStruct((t, f), BF16), jax.ShapeDtypeStruct((t, f), BF16),
         jax.ShapeDtypeStruct((8, 128), F32), jax.ShapeDtypeStruct((8, d), F32)],
        [row(d), row(f), row(f), _const((8, 128)), _const((8, d))],
        scratch=[pltpu.VMEM((tm, f), BF16)])[0]


def _ffn_bwd(x, dy, a, b, gain, wset, fs, name, riders=()):
    t, d = x.shape
    f = N_DEV * fs
    tm = _row_tile(t, 256, 128)
    cols = _chunks(f, 1024)

    def body(x_ref, dy_ref, a_ref, b_ref, g_ref, w1_ref, w3_ref, w2_ref,
             dx_ref, da_ref, db_ref, u_ref, h_ref, dg_ref):
        @pl.when(pl.program_id(0) == 0)
        def _():
            dg_ref[...] = jnp.zeros_like(dg_ref)

        xv, dyv, g = x_ref[...], dy_ref[...], g_ref[...]
        xhat, r = _rmsnorm_stats(xv)
        h_ref[...] = (xhat * g).astype(BF16)
        dyb = (FFN_RESIDUAL * dyv).astype(BF16)
        w1 = w1_ref[...].reshape(f, d)
        w3 = w3_ref[...].reshape(f, d)
        w2 = w2_ref[...].reshape(f, d)
        dh = jnp.zeros((tm, d), F32)
        for s, n in cols:
            du = _nt(dyb, w2[s:s + n, :])
            av = a_ref[:, s:s + n].astype(F32)
            bv = b_ref[:, s:s + n].astype(F32)
            sig = _sigmoid(av)
            sa = av * sig
            u_ref[:, s:s + n] = (sa * bv).astype(BF16)
            da = (du * bv * (sig + sa * (1.0 - sig))).astype(BF16)
            db = (du * sa).astype(BF16)
            da_ref[:, s:s + n] = da
            db_ref[:, s:s + n] = db
            dh = dh + _nn(da, w1[s:s + n, :]) + _nn(db, w3[s:s + n, :])
        dx_ref[...] = dyv + _rmsnorm_bwd(dh, g, xhat, r)
        dg_ref[0:1, :] += jnp.sum(dh * xhat, axis=0, keepdims=True)

    big = jax.ShapeDtypeStruct((t, f), BF16)
    row = lambda w: pl.BlockSpec((tm, w), lambda i: (i, 0))
    return _call(
        body, name, t // tm, [x, dy, a, b, gain.reshape(1, d), wset, wset, wset],
        [row(d), row(d), row(f), row(f), _const((1, d)), _wspec(fs, d, 0), _wspec(fs, d, 1), _wspec(fs, d, 2)],
        [jax.ShapeDtypeStruct((t, d), F32), big, big, big, jax.ShapeDtypeStruct((t, d), BF16),
         jax.ShapeDtypeStruct((8, d), F32)],
        [row(d), row(f), row(f), row(f), row(d), _const((8, d))], riders=riders)


def _hidden_major(wcols, name):
    depth, d, c = wcols.shape

    def body(w_ref, o_ref):
        eye = lax.broadcasted_iota(jnp.int32, (d, d), 0) == lax.broadcasted_iota(jnp.int32, (d, d), 1)
        o_ref[0] = _tn(w_ref[0].astype(BF16), eye.astype(BF16)).astype(BF16)

    return pl.pallas_call(
        body, name=name, grid=(depth,),
        out_shape=jax.ShapeDtypeStruct((depth, c, d), BF16),
        in_specs=[pl.BlockSpec((1, d, c), lambda l: (l, 0, 0))],
        out_specs=pl.BlockSpec((1, c, d), lambda l: (l, 0, 0)),
        compiler_params=_params(("parallel",)),
    )(wcols)


def _wgrad(lhs, rhs, gbuf, rows, blk, scale, name, riders=()):
    t, m = lhs.shape
    d = rhs.shape[1]
    bt = _row_tile(t, 1024, 128)
    nk = t // bt
    assert m == N_DEV * rows

    def body(a_ref, b_ref, g_in, o_ref, acc_ref):
        del g_in
        k = pl.program_id(0)

        @pl.when(k == 0)
        def _():
            acc_ref[...] = jnp.zeros_like(acc_ref)

        acc_ref[...] += _tn(a_ref[...], b_ref[...].astype(BF16))

        @pl.when(k == nk - 1)
        def _():
            o_ref[...] = (scale * acc_ref[...]).astype(BF16).reshape(N_DEV, rows, d)

    (out,), rides = _call(
        body, name, nk, [lhs, rhs, gbuf],
        [pl.BlockSpec((bt, m), lambda k: (k, 0)), pl.BlockSpec((bt, d), lambda k: (k, 0)), _ANY],
        [jax.ShapeDtypeStruct(gbuf.shape, gbuf.dtype)], [pl.BlockSpec((N_DEV, rows, d), lambda k: (0, blk, 0))],
        scratch=[pltpu.VMEM((m, d), F32)], riders=riders, aliases={2: 0})
    return (out, rides) if riders else out


def _head_masks(rows, g):
    lane = lax.broadcasted_iota(jnp.int32, (rows, g), 1)
    return [(lane >= h * HEAD_DIM) & (lane < (h + 1) * HEAD_DIM) for h in range(g // HEAD_DIM)]


def _pool_windows(rows, g):
    lane = lax.broadcasted_iota(jnp.int32, (rows, g), 1)
    pg = g // len(POOL_WINDOWS)
    w = jnp.full((rows, g), float(POOL_WINDOWS[-1]), F32)
    for k in range(len(POOL_WINDOWS) - 2, -1, -1):
        w = jnp.where(lane < (k + 1) * pg, float(POOL_WINDOWS[k]), w)
    return w


def _window_sum(ref, tmp, first, rows, wl, direction):
    total = ref.shape[0]
    src, out, shift = ref, None, 1
    for k, w in enumerate(POOL_WINDOWS):
        assert w == 2 * shift and 8 * (k + 1) <= HALO
        if k + 1 < len(POOL_WINDOWS):
            n = total - 8 * (k + 1)
            lo = 8 * (k + 1) if direction < 0 else 0
            dst = tmp.at[k % 2]
            dst[pl.ds(lo, n), :] = src[pl.ds(lo, n), :] + src[pl.ds(lo + direction * shift, n), :]
            src = dst
            level = dst[pl.ds(first, rows), :]
        else:
            level = src[pl.ds(first, rows), :] + src[pl.ds(first + direction * shift, rows), :]
        out = level if out is None else jnp.where(wl >= float(w), level, out)
        shift = w
    return out


def _shifted_copies(src_ref, dst_ref):
    rows = dst_ref.shape[1]
    for b in range(1, 8):
        dst_ref[b - 1] = src_ref[pl.ds(b, rows), :]


def _window(src_ref, shifted_ref, off, rows):
    b = off % 8
    if b == 0:
        return src_ref[pl.ds(off, rows), :]
    return shifted_ref[b - 1, pl.ds(off - b, rows), :]


def _pool_divisor(first_row, rows, g):
    pos1 = (lax.broadcasted_iota(jnp.int32, (rows, g), 0) + first_row + 1).astype(F32)
    return jnp.minimum(pos1, _pool_windows(rows, g))


def _spatial_mix(vq, wcat_ref, masks):
    vstack = jnp.concatenate([jnp.where(m, vq, jnp.zeros_like(vq)) for m in masks], axis=0)
    return _nn(wcat_ref[...], vstack)


def _mix_fwd(x, gain, wbuf, lay, l, small, name):
    t, d = x.shape
    g = lay.g
    di = N_DEV * g
    tm = _row_tile(t, 256, CHUNK)
    conv_w, vecs, sconv_w, pool_bd, wcat, bst = small

    def body(x_ref, g_ref, wi_ref, wo_ref, cw_ref, vec_ref, sw_ref, pbd_ref, wcat_ref, bst_ref,
             y_ref, mix_ref, h_ref, pm_ref, c_ref, pp_ref, ea_ref, eb_ref, ec_ref, sh_ref, tmp_ref):
        i = pl.program_id(0)

        @pl.when(i == 0)
        def _():
            pp_ref[...] = jnp.zeros_like(pp_ref)

        xhat, _ = _rmsnorm_stats(x_ref[...])
        h = (xhat * g_ref[...]).astype(BF16)
        h_ref[...] = h
        pm_ref[...] = _nt(h, wi_ref[...].reshape(di, d)).astype(BF16)

        def prev(j):
            return pp_ref[:, j * g:(j + 1) * g].astype(F32)

        def main(j):
            return pm_ref[:, j * g:(j + 1) * g].astype(F32)

        vec = vec_ref[...]
        ea_ref[0:HALO, :] = prev(0) * _sigmoid(prev(1))
        ea_ref[HALO:, :] = main(0) * _sigmoid(main(1))
        _shifted_copies(ea_ref, sh_ref)
        cw = cw_ref[...]
        c = jnp.zeros((tm, g), F32) + vec[0:1, :]
        for k in range(CONF_KERNEL):
            c = c + cw[k:k + 1, :] * _window(ea_ref, sh_ref, HALO - (CONF_KERNEL - 1) + k, tm)
        c_ref[...] = c
        chat, _ = _layernorm_stats(c)
        ln = chat * vec[1:2, :] + vec[2:3, :]
        mix_ref[:, 0:g] = (ln * _sigmoid(ln)).astype(BF16)
        eb_ref[0:HALO, :] = prev(3) * prev(4)
        eb_ref[HALO:, :] = main(3) * main(4)
        sw = sw_ref[...]
        cz = jnp.zeros((tm, g), F32)
        for k in range(SHORT_KERNEL):
            cz = cz + sw[k:k + 1, :] * eb_ref[pl.ds(HALO - (SHORT_KERNEL - 1) + k, tm), :]
        mix_ref[:, g:2 * g] = (main(2) * cz).astype(BF16)
        ec_ref[0:HALO, :] = prev(5)
        xp = main(5)
        ec_ref[HALO:, :] = xp
        wl = _pool_windows(tm, g)
        mean = _window_sum(ec_ref, tmp_ref, HALO, tm, wl, -1) / _pool_divisor(i * tm, tm, g)
        yc = _nn((mean - xp).astype(BF16), pbd_ref[...])
        mix_ref[:, 2 * g:3 * g] = (yc * vec[3:4, :]).astype(BF16)
        vhat, _ = _layernorm_stats(main(7))
        v = (vhat * vec[4:5, :] + vec[5:6, :]).astype(BF16)
        gu = main(6)
        masks = _head_masks(CHUNK, g)
        for q in range(tm // CHUNK):
            rs = slice(q * CHUNK, (q + 1) * CHUNK)
            mixed = _spatial_mix(v[rs, :], wcat_ref, masks) + bst_ref[...]
            mix_ref[rs, 3 * g:4 * g] = (gu[rs, :] * mixed).astype(BF16)
        y_ref[...] = x_ref[...] + _nn(mix_ref[...], wo_ref[...].reshape(d, d))
        pp_ref[...] = pm_ref[tm - HALO:, :]

    row = lambda w: pl.BlockSpec((tm, w), lambda i: (i, 0))
    return pl.pallas_call(
        body, name=name, grid=(t // tm,),
        out_shape=(jax.ShapeDtypeStruct((t, d), F32), jax.ShapeDtypeStruct((t, 4 * g), BF16),
                   jax.ShapeDtypeStruct((t, d), BF16), jax.ShapeDtypeStruct((t, di), BF16),
                   jax.ShapeDtypeStruct((t, g), F32)),
        in_specs=[row(d), _const((1, d)), _wspec(g, d, lay.win(l)), _wspec(lay.wo, d, lay.wout(l)),
                  _const(conv_w.shape), _const(vecs.shape), _const(sconv_w.shape), _const(pool_bd.shape),
                  _const(wcat.shape), _const(bst.shape)],
        out_specs=(row(d), row(4 * g), row(d), row(di), row(g)),
        scratch_shapes=[pltpu.VMEM((HALO, di), BF16)] + [pltpu.VMEM((HALO + tm, g), F32)] * 3
        + [pltpu.VMEM((7, HALO + tm - 8, g), F32), pltpu.VMEM((2, HALO + tm, g), F32)],
        compiler_params=_params(("arbitrary",)),
    )(x, gain.reshape(1, d), wbuf, wbuf, conv_w, vecs, sconv_w, pool_bd, wcat, bst)


def _mix_bwd(p, dy, conv_out, x, gain, wbuf, lay, l, small, name, riders=()):
    t, d = dy.shape
    g = lay.g
    di = N_DEV * g
    tm = _row_tile(t, 256, CHUNK)
    nt = t // tm
    nh = g // HEAD_DIM
    per = tm // HALO
    ext = tm + HALO
    conv_w, vecs, sconv_w, pool_bd, wcat, bst, wcat_t = small

    def body(pp_ref, pm_ref, pn_ref, dy_ref, dyn_ref, c_ref, cn_ref, x_ref, g_ref, wi_ref, wo_ref, cw_ref, vec_ref,
             sw_ref, pbd_ref, wcat_ref, bst_ref, wcatt_ref, dp_ref, dcw_ref, dvec_ref, dsw_ref, dpbd_ref, dws_ref, dbs_ref,
             dx_ref, dg_ref, ea_ref, eb_ref, ec_ref, fa_ref, fb_ref, fc_ref, sh_ref, sf_ref, tmp_ref):
        i = pl.program_id(0)
        seen = (i > 0).astype(F32)
        more = (i < nt - 1).astype(F32)

        @pl.when(i == 0)
        def _():
            for ref in (dcw_ref, dvec_ref, dsw_ref, dpbd_ref, dws_ref, dbs_ref, dg_ref):
                ref[...] = jnp.zeros_like(ref)

        def prev(j):
            return pp_ref[:, j * g:(j + 1) * g].astype(F32) * seen

        def main(j):
            return pm_ref[:, j * g:(j + 1) * g].astype(F32)

        def nxt(j):
            return pn_ref[:, j * g:(j + 1) * g].astype(F32)

        def colsum(v):
            return jnp.sum(v, axis=0, keepdims=True)

        vec = vec_ref[...]
        wo = wo_ref[...].reshape(d, d)
        dmix = jnp.concatenate([_nt(dy_ref[...].astype(BF16), wo),
                                _nt((dyn_ref[...] * more).astype(BF16), wo)], axis=0)

        val, gate = main(0), main(1)
        sg = _sigmoid(gate)
        ea_ref[0:HALO, :] = prev(0) * _sigmoid(prev(1))
        ea_ref[HALO:, :] = val * sg
        _shifted_copies(ea_ref, sh_ref)
        cw = cw_ref[...]
        c = jnp.concatenate([c_ref[...], cn_ref[...]], axis=0)
        chat, r = _layernorm_stats(c)
        ln = chat * vec[1:2, :] + vec[2:3, :]
        sl = _sigmoid(ln)
        dln = dmix[:, 0:g] * (sl * (1.0 + ln * (1.0 - sl)))
        dvec_ref[1:2, :] += colsum(dln[0:tm] * chat[0:tm])
        dvec_ref[2:3, :] += colsum(dln[0:tm])
        dc = _layernorm_bwd(dln, vec[1:2, :], chat, r)
        dvec_ref[0:1, :] += colsum(dc[0:tm])
        fa_ref[...] = dc
        _shifted_copies(fa_ref, sf_ref)
        dcm = dc[0:tm]
        dy0 = jnp.zeros((tm, g), F32)
        for k in range(CONF_KERNEL):
            dy0 = dy0 + cw[k:k + 1, :] * _window(fa_ref, sf_ref, CONF_KERNEL - 1 - k, tm)
            dcw_ref[k:k + 1, :] += colsum(dcm * _window(ea_ref, sh_ref, HALO - (CONF_KERNEL - 1) + k, tm))
        dp_ref[:, 0:g] = (dy0 * sg).astype(BF16)
        dp_ref[:, g:2 * g] = (dy0 * val * sg * (1.0 - sg)).astype(BF16)

        sb, sc, sx = main(2), main(3), main(4)
        eb_ref[0:HALO, :] = prev(3) * prev(4)
        eb_ref[HALO:ext, :] = sc * sx
        sw = sw_ref[...]
        cz = jnp.zeros((tm, g), F32)
        for k in range(SHORT_KERNEL):
            cz = cz + sw[k:k + 1, :] * eb_ref[pl.ds(HALO - (SHORT_KERNEL - 1) + k, tm), :]
        dob = dmix[:, g:2 * g]
        dp_ref[:, 2 * g:3 * g] = (dob[0:tm] * cz).astype(BF16)
        fb_ref[0:tm, :] = dob[0:tm] * sb
        fb_ref[tm:, :] = dob[tm:] * nxt(2)
        dczm = fb_ref[0:tm, :]
        dz = jnp.zeros((tm, g), F32)
        for k in range(SHORT_KERNEL):
            dz = dz + sw[k:k + 1, :] * fb_ref[pl.ds(SHORT_KERNEL - 1 - k, tm), :]
            dsw_ref[k:k + 1, :] += colsum(dczm * eb_ref[pl.ds(HALO - (SHORT_KERNEL - 1) + k, tm), :])
        dp_ref[:, 3 * g:4 * g] = (dz * sx).astype(BF16)
        dp_ref[:, 4 * g:5 * g] = (dz * sc).astype(BF16)

        xp = main(5)
        ec_ref[0:HALO, :] = prev(5)
        ec_ref[HALO:, :] = xp
        wl = _pool_windows(tm, g)
        dpool = ((_window_sum(ec_ref, tmp_ref, HALO, tm, wl, -1) / _pool_divisor(i * tm, tm, g)) - xp).astype(BF16)
        pbd = pbd_ref[...]
        yc = _nn(dpool, pbd)
        doc = dmix[:, 2 * g:3 * g]
        dvec_ref[3:4, :] += colsum(doc[0:tm] * yc)
        dyc = (doc * vec[3:4, :]).astype(BF16)
        dpbd_ref[...] += _tn(dpool, dyc[0:tm])
        dd = _nt(dyc, pbd)
        fc_ref[...] = dd / _pool_divisor(i * tm, ext, g)
        dp_ref[:, 5 * g:6 * g] = (_window_sum(fc_ref, tmp_ref, 0, tm, wl, 1) - dd[0:tm]).astype(BF16)

        gv = main(7)
        vhat, rv = _layernorm_stats(gv)
        v = (vhat * vec[4:5, :] + vec[5:6, :]).astype(BF16)
        gu = main(6)
        dod = dmix[0:tm, 3 * g:4 * g]
        masks = _head_masks(CHUNK, g)
        tril = (lax.broadcasted_iota(jnp.int32, (CHUNK, CHUNK), 1)
                <= lax.broadcasted_iota(jnp.int32, (CHUNK, CHUNK), 0)).astype(F32)
        tril = jnp.concatenate([tril] * nh, axis=0)
        head_col = lax.broadcasted_iota(jnp.int32, (CHUNK, CHUNK), 1)
        dvs = []
        for q in range(tm // CHUNK):
            rs = slice(q * CHUNK, (q + 1) * CHUNK)
            vq = v[rs, :]
            mixed = _spatial_mix(vq, wcat_ref, masks) + bst_ref[...]
            dp_ref[rs, 6 * g:7 * g] = (dod[rs, :] * mixed).astype(BF16)
            dmixed = dod[rs, :] * gu[rs, :]
            dmb = dmixed.astype(BF16)
            for h, m in enumerate(masks):
                hs = jnp.sum(jnp.where(m, dmixed, 0.0), axis=-1, keepdims=True)
                dbs_ref[...] += jnp.where(head_col == h, hs, 0.0)
            dmstack = jnp.concatenate([jnp.where(m, dmb, jnp.zeros_like(dmb)) for m in masks], axis=0)
            dws_ref[...] += _nt(dmstack, vq) * tril
            back = _nn(wcatt_ref[...], dmb)
            dv = jnp.zeros((CHUNK, g), F32)
            for h, m in enumerate(masks):
                dv = jnp.where(m, back[h * CHUNK:(h + 1) * CHUNK, :], dv)
            dvs.append(dv)
        dv = jnp.concatenate(dvs, axis=0)
        dvec_ref[4:5, :] += colsum(dv * vhat)
        dvec_ref[5:6, :] += colsum(dv)
        dp_ref[:, 7 * g:8 * g] = _layernorm_bwd(dv, vec[4:5, :], vhat, rv).astype(BF16)

        xhat, rx = _rmsnorm_stats(x_ref[...])
        dh = _nn(dp_ref[...], wi_ref[...].reshape(di, d))
        dx_ref[...] = dy_ref[...] + _rmsnorm_bwd(dh, g_ref[...], xhat, rx)
        dg_ref[0:1, :] += colsum(dh * xhat)

    halo_prev = pl.BlockSpec((HALO, di), lambda i: (jnp.maximum(i * per - 1, 0), 0))
    halo_next = lambda w: pl.BlockSpec((HALO, w), lambda i: (jnp.minimum((i + 1) * per, t // HALO - 1), 0))
    row = lambda w: pl.BlockSpec((tm, w), lambda i: (i, 0))
    outs = (jax.ShapeDtypeStruct((t, di), BF16), jax.ShapeDtypeStruct(conv_w.shape, F32),
            jax.ShapeDtypeStruct(vecs.shape, F32), jax.ShapeDtypeStruct(sconv_w.shape, F32),
            jax.ShapeDtypeStruct((g, g), F32), jax.ShapeDtypeStruct((nh * CHUNK, CHUNK), F32),
            jax.ShapeDtypeStruct((CHUNK, CHUNK), F32), jax.ShapeDtypeStruct((t, d), F32),
            jax.ShapeDtypeStruct((8, d), F32))
    return _call(
        body, name, nt, [p, p, p, dy, dy, conv_out, conv_out, x, gain.reshape(1, d), wbuf, wbuf, conv_w, vecs,
                         sconv_w, pool_bd, wcat, bst, wcat_t],
        [halo_prev, row(di), halo_next(di), row(d), halo_next(d), row(g), halo_next(g), row(d), _const((1, d)),
         _wspec(g, d, lay.win(l)), _wspec(lay.wo, d, lay.wout(l)),
         _const(conv_w.shape), _const(vecs.shape), _const(sconv_w.shape), _const(pool_bd.shape),
         _const(wcat.shape), _const(bst.shape), _const(wcat_t.shape)],
        list(outs), [row(di)] + [_const(o.shape) for o in outs[1:7]] + [row(d), _const((8, d))],
        scratch=[pltpu.VMEM((HALO + tm, g), F32), pltpu.VMEM((HALO + tm, g), F32),
                 pltpu.VMEM((HALO + tm, g), F32), pltpu.VMEM((ext, g), F32), pltpu.VMEM((ext, g), F32),
                 pltpu.VMEM((ext, g), F32), pltpu.VMEM((7, HALO + tm - 8, g), F32),
                 pltpu.VMEM((7, ext - 8, g), F32), pltpu.VMEM((2, ext, g), F32)],
        riders=riders)


def _loss_head(x, target, gain, name):
    t, d = x.shape
    tm = _row_tile(t, 512, 8)

    def body(x_ref, t_ref, g_ref, dx_ref, loss_ref, dg_ref):
        @pl.when(pl.program_id(0) == 0)
        def _():
            loss_ref[...] = jnp.zeros_like(loss_ref)
            dg_ref[...] = jnp.zeros_like(dg_ref)

        g = g_ref[...]
        xhat, r = _rmsnorm_stats(x_ref[...])
        err = xhat * g - t_ref[...]
        loss_ref[...] += 0.5 * jnp.sum(jnp.mean(err * err, axis=-1, keepdims=True))
        dyv = err * (1.0 / d)
        dx_ref[...] = _rmsnorm_bwd(dyv, g, xhat, r)
        dg_ref[0:1, :] += jnp.sum(dyv * xhat, axis=0, keepdims=True)

    row = pl.BlockSpec((tm, d), lambda i: (i, 0))
    return pl.pallas_call(
        body, name=name, grid=(t // tm,),
        out_shape=(jax.ShapeDtypeStruct((t, d), F32), jax.ShapeDtypeStruct((8, 128), F32),
                   jax.ShapeDtypeStruct((8, d), F32)),
        in_specs=[row, row, _const((1, d))],
        out_specs=(row, _const((8, 128)), _const((8, d))),
        compiler_params=_params(("arbitrary",)),
    )(x, target, gain.reshape(1, d))


def _adam_step(w, gv, m, v):
    nm = ADAM_B1 * m + (1.0 - ADAM_B1) * gv
    nv = ADAM_B2 * v + (1.0 - ADAM_B2) * (gv * gv)
    m_hat = nm / (1.0 - ADAM_B1 ** ADAM_STEP)
    v_hat = nv / (1.0 - ADAM_B2 ** ADAM_STEP)
    return -ADAM_LR * (m_hat / (jnp.sqrt(v_hat) + ADAM_EPS) + ADAM_WD * w), nm, nv


def _reduce_adamw(slots, blks, transposed, w, m, v, name):
    depth, rows, cols = w.shape
    tr = _row_tile(rows, 512, 128) if transposed else _row_tile(rows, 256, 16)
    nt = rows // tr
    pad = -cols % 128

    def body(*refs):
        slot_refs = refs[:depth]
        w_ref, m_ref, v_ref, g_ref, d_ref, nm_ref, nv_ref = refs[depth:]
        layer = pl.program_id(0) // nt
        for l in range(depth):
            @pl.when(layer == l)
            def _(s_ref=slot_refs[l]):
                gv = s_ref[0].astype(F32)
                for j in range(1, N_DEV):
                    gv = gv + s_ref[j].astype(F32)
                if transposed:
                    if pad:
                        gv = jnp.concatenate([gv, jnp.zeros((pad, tr), F32)], axis=0)
                    gv = gv.T[:, :cols]
                g_ref[0] = gv
                d_ref[0], nm_ref[0], nv_ref[0] = _adam_step(w_ref[0], gv, m_ref[0], v_ref[0])

    def slot_spec(l):
        tile = lambda i: jnp.clip(i - l * nt, 0, nt - 1)
        if transposed:
            return pl.BlockSpec((N_DEV, cols, tr), lambda i: (0, blks[l], tile(i)))
        return pl.BlockSpec((N_DEV, tr, cols), lambda i: (0, blks[l] * nt + tile(i), 0))

    spec = pl.BlockSpec((1, tr, cols), lambda i: (i // nt, i % nt, 0))
    shape = jax.ShapeDtypeStruct(w.shape, F32)
    return pl.pallas_call(
        body, name=name, grid=(depth * nt,),
        out_shape=(shape,) * 4, in_specs=[slot_spec(l) for l in range(depth)] + [spec] * 3, out_specs=(spec,) * 4,
        compiler_params=_params(("arbitrary",)),
    )(*slots, w, m, v)


def _adamw(w, grad, m, v, name):
    shape = w.shape
    cols = shape[-1]
    rows = w.size // cols
    tr = _row_tile(rows, 512, 8) if rows > 1024 else rows

    def body(w_ref, g_ref, m_ref, v_ref, d_ref, nm_ref, nv_ref):
        d_ref[...], nm_ref[...], nv_ref[...] = _adam_step(w_ref[...], g_ref[...], m_ref[...], v_ref[...])

    spec = pl.BlockSpec((tr, cols), lambda i: (i, 0))
    flat = jax.ShapeDtypeStruct((rows, cols), F32)
    outs = pl.pallas_call(
        body, name=name, grid=(rows // tr,),
        out_shape=(flat, flat, flat), in_specs=[spec] * 4, out_specs=(spec,) * 3,
        compiler_params=_params(("parallel",)),
    )(*(a.reshape(rows, cols) for a in (w, grad, m, v)))
    return tuple(o.reshape(shape) for o in outs)


def _f32_rows(parts, d, rows):
    out, offs, at = [], [], 0
    for a in parts:
        n = -(-a.size // d)
        out.append(jnp.pad(a.reshape(-1), (0, n * d - a.size)).reshape(n, d))
        offs.append(at)
        at += n
    total = at if rows is None else rows
    assert at <= total
    if total > at:
        out.append(jnp.zeros((total - at, d), F32))
    return jnp.concatenate(out, axis=0), offs


def _take(buf, off, shape):
    n = 1
    for s in shape:
        n *= s
    d = buf.shape[1]
    return buf[off:off + -(-n // d)].reshape(-1)[:n].reshape(shape)


def _block_diag(blocks):
    n, k, _ = blocks.shape
    out = jnp.zeros((n * k, n * k), blocks.dtype)
    for i in range(n):
        out = lax.dynamic_update_slice(out, blocks[i], (i * k, i * k))
    return out


def kernel(x, ffn1_norm, ffn1_w1, ffn1_w3, ffn1_w2, mix_norm, w_in, conf_conv_w, conf_conv_b, conf_ln_g, conf_ln_b, sconv_w, pool_w, pool_scale, gmlp_ln_g, gmlp_ln_b, gmlp_w_s, gmlp_b_s, w_out, ffn2_norm, ffn2_w1, ffn2_w3, ffn2_w2, final_norm, loss_target, m_ffn1_norm, m_ffn1_w1, m_ffn1_w3, m_ffn1_w2, m_mix_norm, m_w_in, m_conf_conv_w, m_conf_conv_b, m_conf_ln_g, m_conf_ln_b, m_sconv_w, m_pool_w, m_pool_scale, m_gmlp_ln_g, m_gmlp_ln_b, m_gmlp_w_s, m_gmlp_b_s, m_w_out, m_ffn2_norm, m_ffn2_w1, m_ffn2_w3, m_ffn2_w2, m_final_norm, v_ffn1_norm, v_ffn1_w1, v_ffn1_w3, v_ffn1_w2, v_mix_norm, v_w_in, v_conf_conv_w, v_conf_conv_b, v_conf_ln_g, v_conf_ln_b, v_sconv_w, v_pool_w, v_pool_scale, v_gmlp_ln_g, v_gmlp_ln_b, v_gmlp_w_s, v_gmlp_b_s, v_w_out, v_ffn2_norm, v_ffn2_w1, v_ffn2_w3, v_ffn2_w2, v_final_norm):
    w = dict(zip(WEIGHTS, (ffn1_norm, ffn1_w1, ffn1_w3, ffn1_w2, mix_norm, w_in, conf_conv_w, conf_conv_b, conf_ln_g,
                           conf_ln_b, sconv_w, pool_w, pool_scale, gmlp_ln_g, gmlp_ln_b, gmlp_w_s, gmlp_b_s, w_out,
                           ffn2_norm, ffn2_w1, ffn2_w3, ffn2_w2, final_norm)))
    mom1 = dict(zip(WEIGHTS, (m_ffn1_norm, m_ffn1_w1, m_ffn1_w3, m_ffn1_w2, m_mix_norm, m_w_in, m_conf_conv_w,
                              m_conf_conv_b, m_conf_ln_g, m_conf_ln_b, m_sconv_w, m_pool_w, m_pool_scale, m_gmlp_ln_g,
                              m_gmlp_ln_b, m_gmlp_w_s, m_gmlp_b_s, m_w_out, m_ffn2_norm, m_ffn2_w1, m_ffn2_w3,
                              m_ffn2_w2, m_final_norm)))
    mom2 = dict(zip(WEIGHTS, (v_ffn1_norm, v_ffn1_w1, v_ffn1_w3, v_ffn1_w2, v_mix_norm, v_w_in, v_conf_conv_w,
                              v_conf_conv_b, v_conf_ln_g, v_conf_ln_b, v_sconv_w, v_pool_w, v_pool_scale, v_gmlp_ln_g,
                              v_gmlp_ln_b, v_gmlp_w_s, v_gmlp_b_s, v_w_out, v_ffn2_norm, v_ffn2_w1, v_ffn2_w3,
                              v_ffn2_w2, v_final_norm)))
    _, t, d = x.shape
    depth, _, fs = ffn1_w1.shape
    g = w_in.shape[2]
    wo = w_out.shape[1]
    nh = g // HEAD_DIM
    cs = conf_conv_w.shape[2]
    lay = _MixRows(g, wo)
    me = 4 * lax.axis_index("x") + 2 * lax.axis_index("y") + lax.axis_index("c")

    def rows_bf16(mats):
        return jnp.concatenate([m_.astype(BF16) for m_ in mats], axis=0)

    swapped = tuple(n for n in SHARDED_COLS if w[n].shape[2] % 128)
    rows_of = {n: (jnp.swapaxes(w[n], 1, 2) if n in swapped else _hidden_major(w[n], f"rows_{n}"))
               for n in SHARDED_COLS}
    set_f1 = [rows_bf16([rows_of['ffn1_w1'][l], rows_of['ffn1_w3'][l], ffn1_w2[l]]) for l in range(depth)]
    set_f2 = [rows_bf16([rows_of['ffn2_w1'][l], rows_of['ffn2_w3'][l], ffn2_w2[l]]) for l in range(depth)]
    set_mx = [rows_bf16([rows_of['w_in'][l], w_out[l]]) for l in range(depth)]
    convs = jnp.concatenate([conf_conv_w, sconv_w], axis=1)
    nconv = convs.size
    conv_bits = jnp.pad(convs.reshape(-1), (0, CONV_ROWS * d // 2 - nconv))
    conv_rows = lax.bitcast_convert_type(conv_bits, BF16).reshape(CONV_ROWS, d)
    w_f1 = [None] * depth
    w_f1[0] = _alone(_Gather(jnp.concatenate([set_f1[0], conv_rows], axis=0)), "gather_first")

    conv_all = lax.bitcast_convert_type(w_f1[0][:, 3 * fs:, :].reshape(N_DEV, CONV_ROWS * d // 2, 2), F32)
    conv_all = conv_all[:, :nconv].reshape(N_DEV, depth, CONF_KERNEL + SHORT_KERNEL, cs)
    conv_all = conv_all.transpose(1, 2, 0, 3).reshape(depth, CONF_KERNEL + SHORT_KERNEL, g)

    def small_inputs(l):
        cw = jnp.pad(conv_all[l, :CONF_KERNEL], ((0, 32 - CONF_KERNEL), (0, 0)))
        sw = jnp.pad(conv_all[l, CONF_KERNEL:], ((0, 8 - SHORT_KERNEL), (0, 0)))
        vecs = jnp.stack([conf_conv_b[l], conf_ln_g[l], conf_ln_b[l], pool_scale[l], gmlp_ln_g[l], gmlp_ln_b[l],
                          jnp.zeros((g,), F32), jnp.zeros((g,), F32)])
        pool_bd = _block_diag(pool_w[l]).astype(BF16)
        low = jnp.tril(jnp.ones((CHUNK, CHUNK), bool))
        ws = jnp.where(low[None], gmlp_w_s[l], 0.0).astype(BF16)
        wcat = ws.transpose(1, 0, 2).reshape(CHUNK, nh * CHUNK)
        wcat_t = ws.transpose(0, 2, 1).reshape(nh * CHUNK, CHUNK)
        bst = jnp.repeat(gmlp_b_s[l].T, HEAD_DIM, axis=1)
        return cw, vecs, sw, pool_bd, wcat, bst, wcat_t

    xs = x[0]
    saved = []
    for l in range(depth):
        sm = small_inputs(l)
        (x1, a1, b1), ((w_f2,), (w_mx,)) = _ffn_fwd(xs, ffn1_norm[l], w_f1[l], fs, f"ffn1_fwd_{l}",
                                                    riders=[_Gather(set_f2[l]), _Gather(set_mx[l])])
        x2, mix, h2, p, conv_out = _mix_fwd(x1, mix_norm[l], w_mx, lay, l, sm[:6], f"mix_fwd_{l}")
        if l + 1 < depth:
            (x3, a2, b2), ((w_f1[l + 1],),) = _ffn_fwd(x2, ffn2_norm[l], w_f2, fs, f"ffn2_fwd_{l}",
                                                       riders=[_Gather(set_f1[l + 1])])
        else:
            dx, a2, b2, loss_part, dgf = _ffn_fwd_head(x2, ffn2_norm[l], w_f2, fs, loss_target[0], final_norm,
                                                       f"ffn2_fwd_{l}")
        saved.append((xs, a1, b1, x1, h2, p, mix, x2, a2, b2, sm, w_f2, w_mx, conv_out))
        xs = x3 if l + 1 < depth else None

    loss = lax.psum(loss_part[0, 0], ("x", "y", "c"))

    def small_rows(parts):
        buf, at = _f32_rows(parts, d, None)
        return jnp.pad(buf, ((0, -buf.shape[0] % 16), (0, 0))), at

    small = [None] * depth
    big = ('ffn1_w1', 'ffn1_w3', 'ffn1_w2', 'ffn2_w1', 'ffn2_w3', 'ffn2_w2', 'w_in', 'w_out')
    slots = {n: [None] * depth for n in big}
    blk_of = {'ffn1_w1': 0, 'ffn1_w3': 1, 'ffn1_w2': 2, 'ffn2_w1': 0, 'ffn2_w3': 1, 'ffn2_w2': 2,
              'w_in': lay.win(0), 'w_out': lay.wout(0)}
    blks = {n: [blk_of[n]] * depth for n in big}

    def landed(names, l, got_):
        for n in names:
            slots[n][l] = got_

    def ffn_grads(da, db, u, h, dy, tag):
        gset = lax.empty((N_DEV, 3 * fs, d), BF16)
        gset = _wgrad(da, h, gset, fs, 0, 1.0, f"{tag}_dw1")
        gset = _wgrad(db, h, gset, fs, 1, 1.0, f"{tag}_dw3")
        return _wgrad(u, dy, gset, fs, 2, FFN_RESIDUAL, f"{tag}_dw2")

    pending = None
    early_all = None
    for l in reversed(range(depth)):
        x0, a1, b1, x1, h2, p, mix, x2, a2, b2, sm, w_f2, w_mx, conv_out = saved[l]
        ride = [_Exchange(pending)] if pending is not None else []
        (dx2, da, db, u, h, dgn2), got = _ffn_bwd(x2, dx, a2, b2, ffn2_norm[l], w_f2, fs, f"ffn2_bwd_{l}", riders=ride)
        if ride:
            landed(big[0:3], l + 1, got[0][0])
        gset = lax.empty((N_DEV, 3 * fs, d), BF16)
        if l == 0 and depth > 1:
            early, early_offs = small_rows([a for k in range(1, depth) for a in small[k]] + [dgf[0]])
            gset, ((early_all,),) = _wgrad(da, h, gset, fs, 0, 1.0, f"ffn2_{l}_dw1", riders=[_Gather(early)])
        else:
            gset = _wgrad(da, h, gset, fs, 0, 1.0, f"ffn2_{l}_dw1")
        gset = _wgrad(db, h, gset, fs, 1, 1.0, f"ffn2_{l}_dw3")
        gset = _wgrad(u, dx, gset, fs, 2, FFN_RESIDUAL, f"ffn2_{l}_dw2")
        (dp, dcw, dvec, dsw, dpbd, dws, dbs, dx1, dgm), got = _mix_bwd(
            p, dx2, conv_out, x1, mix_norm[l], w_mx, lay, l, sm, f"mix_bwd_{l}", riders=[_Exchange(gset)])
        landed(big[3:6], l, got[0][0])
        if l > 0:
            gmx = lax.empty((N_DEV, g + wo, d), BF16)
            gmx = _wgrad(mix, dx2, gmx, wo, lay.wout(l), 1.0, f"dw_out_{l}")
            gmx = _wgrad(dp, h2, gmx, g, lay.win(l), 1.0, f"dw_in_{l}")
            ride = [_Exchange(gmx)]
        else:
            g_wo = _wgrad(mix, dx2, lax.empty((N_DEV, wo, d), BF16), wo, 0, 1.0, f"dw_out_{l}")
            g_wi, ((slots['w_out'][0],),) = _wgrad(dp, h2, lax.empty((N_DEV, g, d), BF16), g, 0, 1.0, f"dw_in_{l}",
                                                   riders=[_Exchange(g_wo)])
            blks['w_out'][0] = blks['w_in'][0] = 0
            ride = []
        (dx, da, db, u, h, dgn1), got = _ffn_bwd(x0, dx1, a1, b1, ffn1_norm[l], w_f1[l], fs,
                                                 f"ffn1_bwd_{l}", riders=ride)
        if l > 0:
            landed(big[6:8], l, got[0][0])
        pg = g // len(POOL_WINDOWS)
        dpool = jnp.stack([dpbd[k * pg:(k + 1) * pg, k * pg:(k + 1) * pg] for k in range(len(POOL_WINDOWS))])
        small[l] = [dgn1[0], dgm[0], dgn2[0], dvec, dcw, dsw, dpool, dws, dbs[:, :nh].T]
        if l > 0:
            pending = ffn_grads(da, db, u, h, dx1, f"ffn1_{l}")
    grad_x = dx[None]
    late, late_offs = small_rows(small[0] if depth > 1 else small[0] + [dgf[0]])
    one = lambda: lax.empty((N_DEV, fs, d), BF16)
    g_a, ((slots['w_in'][0],),) = _wgrad(da, h, one(), fs, 0, 1.0, "ffn1_0_dw1", riders=[_Exchange(g_wi)])
    g_b, ((slots['ffn1_w1'][0],),) = _wgrad(db, h, one(), fs, 0, 1.0, "ffn1_0_dw3", riders=[_Exchange(g_a)])
    g_c, ((slots['ffn1_w3'][0],),) = _wgrad(u, dx1, one(), fs, 0, FFN_RESIDUAL, "ffn1_0_dw2",
                                            riders=[_Exchange(g_b)])
    (slots['ffn1_w2'][0],), (late_all,) = _call(lambda: None, "exchange_last", 1, [], [], [], [],
                                                riders=[_Exchange(g_c), _Gather(late)])[1]
    for n in big[0:3]:
        blks[n][0] = 0

    grads, deltas, new_m, new_v = {}, {}, {}, {}
    for n in big:
        if n in swapped:
            outs = _reduce_adamw(slots[n], blks[n], False, *(jnp.swapaxes(a_, 1, 2) for a_ in (w[n], mom1[n], mom2[n])),
                                 f"reduce_adamw_{n}")
            grads[n], deltas[n], new_m[n], new_v[n] = (jnp.swapaxes(o_, 1, 2) for o_ in outs)
        else:
            grads[n], deltas[n], new_m[n], new_v[n] = _reduce_adamw(
                slots[n], blks[n], n in SHARDED_COLS, w[n], mom1[n], mom2[n], f"reduce_adamw_{n}")

    per_layer = len(small[0])
    shapes = [(d,), (d,), (d,), (8, g), (32, g), (8, g), (len(POOL_WINDOWS), g // len(POOL_WINDOWS),
              g // len(POOL_WINDOWS)), (nh * CHUNK, CHUNK), (nh, CHUNK)]
    late_sum = _sum_slots(late_all, "sum_small_late")
    early_sum = _sum_slots(early_all, "sum_small_early") if depth > 1 else None
    got = [[_take(late_sum, late_offs[k], shapes[k]) for k in range(per_layer)]]
    got += [[_take(early_sum, early_offs[(l - 1) * per_layer + k], shapes[k]) for k in range(per_layer)]
            for l in range(1, depth)]
    col = lambda k: jnp.stack([got[l][k] for l in range(depth)])
    grads['ffn1_norm'], grads['mix_norm'], grads['ffn2_norm'] = col(0), col(1), col(2)
    dvec_all = col(3)
    for k, n in enumerate(('conf_conv_b', 'conf_ln_g', 'conf_ln_b', 'pool_scale', 'gmlp_ln_g', 'gmlp_ln_b')):
        grads[n] = dvec_all[:, k]
    grads['conf_conv_w'] = lax.dynamic_slice_in_dim(col(4)[:, :CONF_KERNEL], me * cs, cs, axis=2)
    grads['sconv_w'] = lax.dynamic_slice_in_dim(col(5)[:, :SHORT_KERNEL], me * cs, cs, axis=2)
    grads['pool_w'] = col(6)
    grads['gmlp_w_s'] = col(7).reshape(depth, nh, CHUNK, CHUNK)
    grads['gmlp_b_s'] = col(8)
    grads['final_norm'] = (_take(early_sum, early_offs[-1], (d,)) if depth > 1
                           else _take(late_sum, late_offs[-1], (d,)))

    for n in WEIGHTS:
        if n not in big:
            deltas[n], new_m[n], new_v[n] = _adamw(w[n], grads[n], mom1[n], mom2[n], f"adamw_{n}")

    return (loss, grad_x, *[grads[n] for n in WEIGHTS], *[deltas[n] for n in WEIGHTS],
            *[new_m[n] for n in WEIGHTS], *[new_v[n] for n in WEIGHTS])
```
